```python
import math
import jax, jax.numpy as jnp
from jax import lax
import numpy as np

D_MODEL = 1024
BATCH = 16
SEQ = 2048
DEPTH = 1

CHUNK = 64

D_CONV = 3 * D_MODEL // 4
CONV_WIDTH = 31
D_SSM = D_MODEL // 4
SSM_GROUP = 16
N_SSM_GROUPS = D_SSM // SSM_GROUP
SSM_STATE = 64
D_IN = 2 * D_CONV + D_SSM + 2 * D_MODEL

N_EXPERTS = 32
TOP_K = 4
D_FF_EXPERT = D_MODEL
SWIGLU_LIMIT = 7.0
SWIGLU_ALPHA = 1.702
MOE_BLOCK = 256

NORM_EPS = 1e-6

kernel_name = "hybrid_conformer_s5_moe_block"


def rms_norm(x, g):
    xf = x.astype(jnp.float32)
    y = xf * lax.rsqrt(jnp.mean(xf * xf, axis=-1, keepdims=True) + NORM_EPS)
    return (y * g.astype(jnp.float32)).astype(x.dtype)


def layer_norm(x, g, b):
    xf = x.astype(jnp.float32)
    mu = jnp.mean(xf, axis=-1, keepdims=True)
    xc = xf - mu
    y = xc * lax.rsqrt(jnp.mean(xc * xc, axis=-1, keepdims=True) + NORM_EPS)
    return (y * g.astype(jnp.float32) + b.astype(jnp.float32)).astype(x.dtype)


def conv_branch(a_val, a_gate, conv_w, conv_b, ln_g, ln_b, w_conv_out):
    v = a_val * jax.nn.sigmoid(a_gate)
    v = lax.conv_general_dilated(
        v, conv_w[:, None, :].astype(v.dtype), window_strides=(1,),
        padding=[(CONV_WIDTH - 1, 0)],
        dimension_numbers=("NWC", "WIO", "NWC"),
        feature_group_count=D_CONV) + conv_b
    v = layer_norm(v, ln_g, ln_b)
    v = jax.nn.silu(v)
    return v @ w_conv_out


def _complex_linear_combine(left, right):
    alr, ali, blr, bli = left
    arr, ari, brr, bri = right
    return (arr * alr - ari * ali,
            arr * ali + ari * alr,
            arr * blr - ari * bli + brr,
            arr * bli + ari * blr + bri)


def s5_branch(u, a_re, a_im, log_step, b_re, b_im, c_re, c_im, d_skip, w_glu, b_glu, w_ssm_out):
    bsz, seq, _ = u.shape
    f32 = jnp.float32
    ug = u.astype(f32).reshape(bsz, seq, N_SSM_GROUPS, SSM_GROUP)
    lr, li = a_re.astype(f32), a_im.astype(f32)
    dt = jnp.exp(log_step.astype(f32))[:, None]
    mag = jnp.exp(lr * dt)
    ar = mag * jnp.cos(li * dt)
    ai = mag * jnp.sin(li * dt)
    den = lr * lr + li * li
    fr = ((ar - 1.0) * lr + ai * li) / den
    fi = (ai * lr - (ar - 1.0) * li) / den
    br, bi = b_re.astype(f32), b_im.astype(f32)
    bbr = fr[..., None] * br - fi[..., None] * bi
    bbi = fr[..., None] * bi + fi[..., None] * br
    bu_r = jnp.einsum("bsgh,gph->bsgp", ug, bbr)
    bu_i = jnp.einsum("bsgh,gph->bsgp", ug, bbi)
    a_r = jnp.broadcast_to(ar[None, None], (1, seq) + ar.shape)
    a_i = jnp.broadcast_to(ai[None, None], (1, seq) + ai.shape)
    _, _, xr, xi = lax.associative_scan(_complex_linear_combine, (a_r, a_i, bu_r, bu_i), axis=1)
    y = (jnp.einsum("bsgp,ghp->bsgh", xr, c_re.astype(f32))
         - jnp.einsum("bsgp,ghp->bsgh", xi, c_im.astype(f32)))
    y = y.reshape(bsz, seq, D_SSM) + d_skip.astype(f32) * u.astype(f32)
    y = jax.nn.gelu(y).astype(u.dtype)
    y = y * jax.nn.sigmoid(y @ w_glu + b_glu)
    return y @ w_ssm_out


def hybrid_mixer(h, w_in, conv_w, conv_b, conv_ln_g, conv_ln_b, w_conv_out,
                 a_re, a_im, log_step, b_re, b_im, c_re, c_im, d_skip, w_glu, b_glu, w_ssm_out, w_out):
    proj = h @ w_in
    a_val, a_gate, u, g_conv, g_ssm = jnp.split(
        proj, [D_CONV, 2 * D_CONV, 2 * D_CONV + D_SSM, 2 * D_CONV + D_SSM + D_MODEL], axis=-1)
    y_conv = conv_branch(a_val, a_gate, conv_w, conv_b, conv_ln_g, conv_ln_b, w_conv_out)
    y_ssm = s5_branch(u, a_re, a_im, log_step, b_re, b_im, c_re, c_im, d_skip, w_glu, b_glu, w_ssm_out)
    m = jax.nn.sigmoid(g_conv) * y_conv + jax.nn.sigmoid(g_ssm) * y_ssm
    return m @ w_out


def moe(h, w_router, b_router, w_gate_up, b_gate_up, w_down, b_down):
    n_tok, d = h.shape
    logits = (h @ w_router + b_router).astype(jnp.float32)
    top_v, top_e = lax.top_k(logits, TOP_K)
    gate = jax.nn.softmax(top_v, axis=-1)
    n_assign = n_tok * TOP_K
    e_flat = top_e.reshape(n_assign).astype(jnp.int32)
    order = jnp.argsort(e_flat)
    sorted_e = e_flat[order]
    counts = jnp.zeros((N_EXPERTS,), jnp.int32).at[e_flat].add(1)
    off = jnp.cumsum(counts) - counts
    padded = (counts + MOE_BLOCK - 1) // MOE_BLOCK * MOE_BLOCK
    pend = jnp.cumsum(padded)
    poff = pend - padded
    dest_sorted = poff[sorted_e] + jnp.arange(n_assign, dtype=jnp.int32) - off[sorted_e]
    dest = jnp.zeros((n_assign,), jnp.int32).at[order].set(dest_sorted)
    n_blocks = -(-n_assign // MOE_BLOCK) + N_EXPERTS
    n_rows = n_blocks * MOE_BLOCK
    row_tok = jnp.full((n_rows,), n_tok, jnp.int32).at[dest].set(
        jnp.arange(n_assign, dtype=jnp.int32) // TOP_K)
    h_pad = jnp.concatenate([h, jnp.zeros((1, d), h.dtype)], axis=0)
    xs = h_pad[row_tok].reshape(n_blocks, MOE_BLOCK, d)
    blk_e = jnp.minimum(
        jnp.searchsorted(pend, jnp.arange(n_blocks, dtype=jnp.int32) * MOE_BLOCK, side="right"),
        N_EXPERTS - 1)

    def expert_block(args):
        xb, e = args
        gu = xb @ w_gate_up[e] + b_gate_up[e]
        g, lin = gu[:, :D_FF_EXPERT], gu[:, D_FF_EXPERT:]
        g = jnp.minimum(g, SWIGLU_LIMIT)
        lin = jnp.clip(lin, -SWIGLU_LIMIT, SWIGLU_LIMIT)
        act = g * jax.nn.sigmoid(SWIGLU_ALPHA * g) * (lin + 1.0)
        return act @ w_down[e] + b_down[e]

    ys = lax.map(expert_block, (xs, blk_e)).reshape(n_rows, d)
    return jnp.einsum("tkd,tk->td", ys[dest].reshape(n_tok, TOP_K, d), gate.astype(ys.dtype))


def setup_inputs(seed: int = 0) -> dict:
    key = jax.random.key(seed)
    ks = jax.random.split(key, 32)
    f32 = jnp.float32
    L, D, G, P, H = DEPTH, D_MODEL, N_SSM_GROUPS, SSM_STATE, SSM_GROUP
    E, F = N_EXPERTS, D_FF_EXPERT

    def nrm(k, shape, scale):
        return jax.random.normal(k, shape, f32) * scale

    log_step = jax.random.uniform(ks[9], (L, G), f32, math.log(1e-3), math.log(1e-1))
    a_re = -0.5 + nrm(ks[7], (L, G, P), 0.01)
    a_im = math.pi * jnp.arange(P, dtype=f32)[None, None, :] + nrm(ks[8], (L, G, P), 0.01)
    return {
        "x": nrm(ks[0], (BATCH, SEQ, D), 1.0),
        "norm_mix_g": 1.0 + nrm(ks[1], (L, D), 0.05),
        "w_in": nrm(ks[2], (L, D, D_IN), D ** -0.5),
        "conv_w": nrm(ks[3], (L, CONV_WIDTH, D_CONV), CONV_WIDTH ** -0.5),
        "conv_b": nrm(ks[4], (L, D_CONV), 0.02),
        "conv_ln_g": 1.0 + nrm(ks[5], (L, D_CONV), 0.05),
        "conv_ln_b": nrm(ks[6], (L, D_CONV), 0.02),
        "w_conv_out": nrm(ks[10], (L, D_CONV, D), D_CONV ** -0.5),
        "ssm_a_re": a_re,
        "ssm_a_im": a_im,
        "ssm_log_step": log_step,
        "ssm_b_re": nrm(ks[11], (L, G, P, H), (2 * H) ** -0.5),
        "ssm_b_im": nrm(ks[12], (L, G, P, H), (2 * H) ** -0.5),
        "ssm_c_re": nrm(ks[13], (L, G, H, P), (2 * P) ** -0.5),
        "ssm_c_im": nrm(ks[14], (L, G, H, P), (2 * P) ** -0.5),
        "ssm_d": nrm(ks[15], (L, D_SSM), 1.0),
        "w_ssm_glu": nrm(ks[16], (L, D_SSM, D_SSM), D_SSM ** -0.5),
        "b_ssm_glu": nrm(ks[17], (L, D_SSM), 0.02),
        "w_ssm_out": nrm(ks[18], (L, D_SSM, D), D_SSM ** -0.5),
        "w_out": nrm(ks[19], (L, D, D), D ** -0.5),
        "norm_ffn_g": 1.0 + nrm(ks[20], (L, D), 0.05),
        "w_router": nrm(ks[21], (L, D, E), D ** -0.5),
        "b_router": nrm(ks[22], (L, E), 0.01),
        "w_gate_up": nrm(ks[23], (L, E, D, 2 * F), D ** -0.5),
        "b_gate_up": nrm(ks[24], (L, E, 2 * F), 0.02),
        "w_down": nrm(ks[25], (L, E, F, D), F ** -0.5),
        "b_down": nrm(ks[26], (L, E, D), 0.02),
        "norm_final_g": 1.0 + nrm(ks[27], (D,), 0.05),
    }


def reference(x, norm_mix_g, w_in, conv_w, conv_b, conv_ln_g, conv_ln_b, w_conv_out,
              ssm_a_re, ssm_a_im, ssm_log_step, ssm_b_re, ssm_b_im, ssm_c_re, ssm_c_im, ssm_d,
              w_ssm_glu, b_ssm_glu, w_ssm_out, w_out, norm_ffn_g, w_router, b_router,
              w_gate_up, b_gate_up, w_down, b_down, norm_final_g):
    bsz, seq, d = x.shape
    for l in range(DEPTH):
        h = rms_norm(x, norm_mix_g[l])
        x = x + hybrid_mixer(h, w_in[l], conv_w[l], conv_b[l], conv_ln_g[l], conv_ln_b[l], w_conv_out[l],
                             ssm_a_re[l], ssm_a_im[l], ssm_log_step[l], ssm_b_re[l], ssm_b_im[l],
                             ssm_c_re[l], ssm_c_im[l], ssm_d[l], w_ssm_glu[l], b_ssm_glu[l],
                             w_ssm_out[l], w_out[l])
        h = rms_norm(x, norm_ffn_g[l]).reshape(bsz * seq, d)
        x = x + moe(h, w_router[l], b_router[l], w_gate_up[l], b_gate_up[l],
                    w_down[l], b_down[l]).reshape(bsz, seq, d)
    return rms_norm(x, norm_final_g)
```

```python
import functools
import math

import jax
import jax.numpy as jnp
from jax import lax
from jax.experimental import pallas as pl
from jax.experimental.pallas import tpu as pltpu

NORM_EPS = 1e-6
SWIGLU_LIMIT = 7.0
SWIGLU_ALPHA = 1.702
TOP_K = 4
SSM_GROUP = 16

LANES = 128
SUBLANES = 8

MIX_ROWS = 512
SCAN_SEGS = SUBLANES
SEG_PAD = 8
CONV_HALO = 32
CONV_ROWS = 64
ROUTE_ROWS = 512
MOE_BLOCK = 256
DISPATCH_ROWS = 1024
COMBINE_ROWS = 256
VMEM_LIMIT = 56 * 1024 * 1024


def _sigmoid(x):
    return 1.0 / (1.0 + jnp.exp(-x))


def _gelu_tanh(x):
    c = math.sqrt(2.0 / math.pi)
    return 0.5 * x * (1.0 + jnp.tanh(c * (x + 0.044715 * (x * x * x))))


def _const_spec(shape):
    nd = len(shape)
    return pl.BlockSpec(shape, lambda *_: (0,) * nd)


def _ssm_disc_kernel(seg_len, are_ref, aim_ref, ls_ref, bre_ref, bim_ref,
                     ar_ref, ai_ref, arp_ref, aip_ref, bbr_ref, bbi_ref):
    lr = are_ref[...]
    li = aim_ref[...]
    dt = jnp.exp(ls_ref[...])
    mag = jnp.exp(lr * dt)
    ar = mag * jnp.cos(li * dt)
    ai = mag * jnp.sin(li * dt)
    den = lr * lr + li * li
    fr = ((ar - 1.0) * lr + ai * li) / den
    fi = (ai * lr - (ar - 1.0) * li) / den
    ar_ref[...] = ar
    ai_ref[...] = ai
    magp = jnp.exp(lr * dt * seg_len)
    arp_ref[...] = magp * jnp.cos(li * dt * seg_len)
    aip_ref[...] = magp * jnp.sin(li * dt * seg_len)
    br = bre_ref[...]
    bi = bim_ref[...]
    bbr_ref[...] = fr[:, None, :] * br - fi[:, None, :] * bi
    bbi_ref[...] = fr[:, None, :] * bi + fi[:, None, :] * br


def _ssm_disc(a_re, a_im, log_step, bt_re, bt_im, seg_len):
    g, p = a_re.shape
    h = bt_re.shape[1]
    f32 = jnp.float32
    out_shape = [jax.ShapeDtypeStruct((g, p), f32)] * 4 + [jax.ShapeDtypeStruct((g, h, p), f32)] * 2
    return pl.pallas_call(
        functools.partial(_ssm_disc_kernel, float(seg_len)),
        out_shape=out_shape,
        name="ssm_disc",
    )(a_re, a_im, log_step.reshape(g, 1), bt_re, bt_im)


def _mixer_kernel(dims, x_ref, gmix_ref, win_ref, convw_ref, convb_ref, lng_ref, lnb_ref, wco_ref,
                  bblk_ref, cr_ref, ci_ref, avec_ref, dskip_ref, wglu_ref, bglu_ref, wso_ref, wout_ref,
                  x1_ref, vbuf_ref, conv_ref, scan_ref, xst_ref, cin_ref, st_ref):
    d, dc, ds, gp, cw = dims
    rows = MIX_ROWS
    seg = rows // SCAN_SEGS
    pitch = seg + SEG_PAD
    nsl = gp // LANES
    c_idx = pl.program_id(1)

    @pl.when(c_idx == 0)
    def _():
        vbuf_ref[0:CONV_HALO, :] = jnp.zeros((CONV_HALO, dc), jnp.float32)
        st_ref[...] = jnp.zeros_like(st_ref)

    x = x_ref[0]
    ms = jnp.mean(x * x, axis=-1, keepdims=True)
    h = (x * lax.rsqrt(ms + NORM_EPS) * gmix_ref[...]).astype(jnp.bfloat16)

    pa = jnp.dot(h, win_ref[:, 0:2 * dc], preferred_element_type=jnp.float32)
    vbuf_ref[CONV_HALO:CONV_HALO + rows, :] = pa[:, 0:dc] * _sigmoid(pa[:, dc:2 * dc])

    base = CONV_HALO - (cw - 1)

    def conv_chunk(rc, carry):
        r0 = pl.multiple_of(rc * CONV_ROWS, CONV_ROWS)
        for lc in range(dc // LANES):
            ls = slice(lc * LANES, (lc + 1) * LANES)
            win = vbuf_ref[pl.ds(r0, CONV_ROWS + CONV_HALO), ls]
            acc = jnp.zeros((CONV_ROWS, LANES), jnp.float32)
            for mis in range(SUBLANES):
                sh = win if mis == 0 else win[mis:mis + CONV_ROWS + CONV_HALO - SUBLANES, :]
                for k in range(cw):
                    if (base + k) % SUBLANES == mis:
                        q = base + k - mis
                        acc = acc + convw_ref[k:k + 1, ls] * sh[q:q + CONV_ROWS, :]
            conv_ref[pl.ds(r0, CONV_ROWS), ls] = acc
        return carry

    lax.fori_loop(0, rows // CONV_ROWS, conv_chunk, 0)
    vbuf_ref[0:CONV_HALO, :] = vbuf_ref[rows:rows + CONV_HALO, :]

    v = conv_ref[...] + convb_ref[...]
    mu = jnp.mean(v, axis=-1, keepdims=True)
    vc = v - mu
    var = jnp.mean(vc * vc, axis=-1, keepdims=True)
    v = vc * lax.rsqrt(var + NORM_EPS) * lng_ref[...] + lnb_ref[...]
    v = v * _sigmoid(v)
    y_conv = jnp.dot(v.astype(jnp.bfloat16), wco_ref[...], preferred_element_type=jnp.float32)

    u = jnp.dot(h, win_ref[:, 2 * dc:2 * dc + ds], preferred_element_type=jnp.float32)
    bu = jnp.dot(u.astype(jnp.bfloat16), bblk_ref[...], preferred_element_type=jnp.float32)
    for n in range(2 * nsl):
        for s in range(SCAN_SEGS):
            scan_ref[n, s * pitch:s * pitch + seg, :] = bu[s * seg:(s + 1) * seg, n * LANES:(n + 1) * LANES]

    def bcast(row, n):
        return jnp.broadcast_to(avec_ref[row:row + 1, n * LANES:(n + 1) * LANES], (SUBLANES, LANES))

    for n in range(nsl):
        a_r = bcast(0, n)
        a_i = bcast(1, n)

        def local_step(j, st):
            sr, si = st
            br = scan_ref[n, pl.ds(j, SCAN_SEGS, stride=pitch), :]
            bi = scan_ref[nsl + n, pl.ds(j, SCAN_SEGS, stride=pitch), :]
            return (a_r * sr - a_i * si + br, a_r * si + a_i * sr + bi)

        zero = jnp.zeros((SUBLANES, LANES), jnp.float32)
        fr, fi = lax.fori_loop(0, seg, local_step, (zero, zero))

        ap_r = avec_ref[2:3, n * LANES:(n + 1) * LANES]
        ap_i = avec_ref[3:4, n * LANES:(n + 1) * LANES]
        c_r = st_ref[0:1, n * LANES:(n + 1) * LANES]
        c_i = st_ref[1:2, n * LANES:(n + 1) * LANES]
        for s in range(SCAN_SEGS):
            cin_ref[0, s:s + 1, :] = c_r
            cin_ref[1, s:s + 1, :] = c_i
            n_r = ap_r * c_r - ap_i * c_i + fr[s:s + 1, :]
            n_i = ap_r * c_i + ap_i * c_r + fi[s:s + 1, :]
            c_r, c_i = n_r, n_i
        st_ref[0:1, n * LANES:(n + 1) * LANES] = c_r
        st_ref[1:2, n * LANES:(n + 1) * LANES] = c_i

        def full_step(j, st):
            sr, si = st
            br = scan_ref[n, pl.ds(j, SCAN_SEGS, stride=pitch), :]
            bi = scan_ref[nsl + n, pl.ds(j, SCAN_SEGS, stride=pitch), :]
            nr = a_r * sr - a_i * si + br
            ni = a_r * si + a_i * sr + bi
            scan_ref[n, pl.ds(j, SCAN_SEGS, stride=pitch), :] = nr
            scan_ref[nsl + n, pl.ds(j, SCAN_SEGS, stride=pitch), :] = ni
            return (nr, ni)

        lax.fori_loop(0, seg, full_step, (cin_ref[0], cin_ref[1]))

    for n in range(2 * nsl):
        for s in range(SCAN_SEGS):
            xst_ref[s * seg:(s + 1) * seg, n * LANES:(n + 1) * LANES] = (
                scan_ref[n, s * pitch:s * pitch + seg, :].astype(jnp.bfloat16))

    y = (jnp.dot(xst_ref[:, 0:gp], cr_ref[...], preferred_element_type=jnp.float32)
         - jnp.dot(xst_ref[:, gp:2 * gp], ci_ref[...], preferred_element_type=jnp.float32))
    y = _gelu_tanh(y + dskip_ref[...] * u)
    glu = jnp.dot(y.astype(jnp.bfloat16), wglu_ref[...], preferred_element_type=jnp.float32) + bglu_ref[...]
    y = y * _sigmoid(glu)
    y_ssm = jnp.dot(y.astype(jnp.bfloat16), wso_ref[...], preferred_element_type=jnp.float32)

    g0 = 2 * dc + ds
    g_conv = jnp.dot(h, win_ref[:, g0:g0 + d], preferred_element_type=jnp.float32)
    m = _sigmoid(g_conv) * y_conv
    g_ssm = jnp.dot(h, win_ref[:, g0 + d:g0 + 2 * d], preferred_element_type=jnp.float32)
    m = m + _sigmoid(g_ssm) * y_ssm
    x1_ref[0] = x + jnp.dot(m.astype(jnp.bfloat16), wout_ref[...], preferred_element_type=jnp.float32)


def _mixer(x, gmix, w_in, conv_w, conv_b, ln_g, ln_b, w_co, bblk, c_r, c_i, avec, d_skip,
           w_glu, b_glu, w_so, w_out):
    b, s, d = x.shape
    cw, dc = conv_w.shape
    ds = d_skip.shape[-1]
    gp = c_r.shape[0]
    rows = MIX_ROWS
    seg = rows // SCAN_SEGS
    pitch = seg + SEG_PAD
    assert s % rows == 0 and cw - 1 <= CONV_HALO and gp % LANES == 0 and dc % LANES == 0
    dims = (d, dc, ds, gp, cw)
    consts = [gmix, w_in, conv_w, conv_b, ln_g, ln_b, w_co, bblk, c_r, c_i, avec, d_skip,
              w_glu, b_glu, w_so, w_out]
    return pl.pallas_call(
        functools.partial(_mixer_kernel, dims),
        grid=(b, s // rows),
        in_specs=[pl.BlockSpec((1, rows, d), lambda i, j: (i, j, 0))] + [_const_spec(c.shape) for c in consts],
        out_specs=pl.BlockSpec((1, rows, d), lambda i, j: (i, j, 0)),
        out_shape=jax.ShapeDtypeStruct((b, s, d), jnp.float32),
        scratch_shapes=[
            pltpu.VMEM((CONV_HALO + rows, dc), jnp.float32),
            pltpu.VMEM((rows, dc), jnp.float32),
            pltpu.VMEM((2 * gp // LANES, SCAN_SEGS * pitch, LANES), jnp.float32),
            pltpu.VMEM((rows, 2 * gp), jnp.bfloat16),
            pltpu.VMEM((2, SUBLANES, LANES), jnp.float32),
            pltpu.VMEM((2, gp), jnp.float32),
        ],
        compiler_params=pltpu.CompilerParams(
            dimension_semantics=("arbitrary", "arbitrary"), vmem_limit_bytes=VMEM_LIMIT),
        name="mixer",
    )(x, *consts)


def _route_kernel(x1_ref, g_ref, wr_ref, br_ref, h2_ref, te_ref, gate_ref, rank_ref, cnt_ref, carry_ref):
    rows, d = x1_ref.shape
    ne = wr_ref.shape[1]

    @pl.when(pl.program_id(0) == 0)
    def _():
        carry_ref[...] = jnp.zeros_like(carry_ref)

    x = x1_ref[...]
    ms = jnp.mean(x * x, axis=-1, keepdims=True)
    h2 = x * lax.rsqrt(ms + NORM_EPS) * g_ref[...]
    for j in range(d // LANES):
        h2_ref[pl.ds(j, rows, stride=d // LANES), :] = h2[:, j * LANES:(j + 1) * LANES]

    bf = jnp.bfloat16
    h_hi = h2.astype(bf)
    h_lo = (h2 - h_hi.astype(jnp.float32)).astype(bf)
    w = wr_ref[...]
    w_hi = w.astype(bf)
    w_lo = (w - w_hi.astype(jnp.float32)).astype(bf)
    logits = (jnp.dot(h_hi, w_hi, preferred_element_type=jnp.float32)
              + jnp.dot(h_lo, w_hi, preferred_element_type=jnp.float32)
              + jnp.dot(h_hi, w_lo, preferred_element_type=jnp.float32)) + br_ref[...]

    lane = lax.broadcasted_iota(jnp.int32, (rows, ne), 1).astype(jnp.float32)
    slot = lax.broadcasted_iota(jnp.int32, (rows, TOP_K), 1)
    work = logits
    sels, vals = [], []
    te = jnp.zeros((rows, TOP_K), jnp.float32)
    for k in range(TOP_K):
        mx = jnp.max(work, axis=-1, keepdims=True)
        idx = jnp.min(jnp.where(work == mx, lane, float(ne)), axis=-1, keepdims=True)
        sel = lane == idx
        sels.append(sel)
        vals.append(mx)
        te = jnp.where(slot == k, idx, te)
        work = jnp.where(sel, -jnp.inf, work)
    te_ref[...] = te.astype(jnp.int32)

    exps = [jnp.exp(v - vals[0]) for v in vals]
    tot = exps[0]
    for e in exps[1:]:
        tot = tot + e
    gate = jnp.zeros((rows, TOP_K), jnp.float32)
    for k in range(TOP_K):
        gate = jnp.where(slot == k, exps[k] / tot, gate)
    gate_ref[...] = gate

    cnt = jnp.zeros((rows, ne), jnp.float32)
    for sel in sels:
        cnt = cnt + sel.astype(jnp.float32)
    r_i = lax.broadcasted_iota(jnp.int32, (rows, rows), 0)
    c_i = lax.broadcasted_iota(jnp.int32, (rows, rows), 1)
    tril = (c_i < r_i).astype(bf)
    before = jnp.dot(tril, cnt.astype(bf), preferred_element_type=jnp.float32) + carry_ref[...]
    rank = jnp.zeros((rows, TOP_K), jnp.float32)
    for k in range(TOP_K):
        rk = jnp.sum(jnp.where(sels[k], before, 0.0), axis=-1, keepdims=True)
        rank = jnp.where(slot == k, rk, rank)
    rank_ref[...] = rank.astype(jnp.int32)
    carry_ref[...] = carry_ref[...] + jnp.sum(cnt, axis=0, keepdims=True)
    cnt_ref[...] = carry_ref[...].astype(jnp.int32)


def _route(x1, g_ffn, w_router, b_router):
    t, d = x1.shape
    ne = w_router.shape[1]
    rows = ROUTE_ROWS
    assert t % rows == 0 and d % LANES == 0
    nt = d // LANES
    return pl.pallas_call(
        _route_kernel,
        grid=(t // rows,),
        in_specs=[pl.BlockSpec((rows, d), lambda i: (i, 0)),
                  _const_spec(g_ffn.shape), _const_spec(w_router.shape), _const_spec(b_router.shape)],
        out_specs=[pl.BlockSpec((rows * nt, LANES), lambda i: (i, 0)),
                   pl.BlockSpec((rows, TOP_K), lambda i: (i, 0)),
                   pl.BlockSpec((rows, TOP_K), lambda i: (i, 0)),
                   pl.BlockSpec((rows, TOP_K), lambda i: (i, 0)),
                   _const_spec((1, ne))],
        out_shape=[jax.ShapeDtypeStruct((t * nt, LANES), jnp.float32),
                   jax.ShapeDtypeStruct((t, TOP_K), jnp.int32),
                   jax.ShapeDtypeStruct((t, TOP_K), jnp.float32),
                   jax.ShapeDtypeStruct((t, TOP_K), jnp.int32),
                   jax.ShapeDtypeStruct((1, ne), jnp.int32)],
        scratch_shapes=[pltpu.VMEM((1, ne), jnp.float32)],
        compiler_params=pltpu.CompilerParams(
            dimension_semantics=("arbitrary",), vmem_limit_bytes=VMEM_LIMIT),
        name="route",
    )(x1, g_ffn, w_router, b_router)


def _meta_kernel(cnt_ref, poff_ref, blke_ref, blks_ref):
    ne = cnt_ref.shape[1]
    nb = blke_ref.shape[1]
    f32 = jnp.float32
    cnt = cnt_ref[...].astype(f32)
    nblk = jnp.floor((cnt + (MOE_BLOCK - 1)) * (1.0 / MOE_BLOCK))
    e_r = lax.broadcasted_iota(jnp.int32, (ne, ne), 0)
    e_c = lax.broadcasted_iota(jnp.int32, (ne, ne), 1)
    nb_rows = jnp.broadcast_to(nblk, (ne, ne))
    bend_col = jnp.sum(jnp.where(e_c <= e_r, nb_rows, 0.0), axis=1, keepdims=True)
    nb_col = jnp.sum(jnp.where(e_c == e_r, nb_rows, 0.0), axis=1, keepdims=True)
    boff = jnp.sum(jnp.where(e_r < e_c, jnp.broadcast_to(nb_col, (ne, ne)), 0.0),
                   axis=0, keepdims=True)
    poff_ref[...] = (boff * MOE_BLOCK).astype(jnp.int32)
    total = jnp.sum(nblk, axis=1, keepdims=True)
    blk = lax.broadcasted_iota(jnp.int32, (ne, nb), 1).astype(f32)
    done = jnp.where(jnp.broadcast_to(bend_col, (ne, nb)) <= blk, 1.0, 0.0)
    be = jnp.minimum(jnp.sum(done, axis=0, keepdims=True), ne - 1.0)
    bid = lax.broadcasted_iota(jnp.int32, (1, nb), 1).astype(f32)
    last_e = jnp.sum(jnp.where(bid == total - 1.0, be, 0.0), axis=1, keepdims=True)
    blke_ref[...] = jnp.where(bid < total, be, last_e).astype(jnp.int32)
    blks_ref[...] = jnp.minimum(bid, total - 1.0).astype(jnp.int32)


def _meta(counts, n_blocks):
    ne = counts.shape[1]
    return pl.pallas_call(
        _meta_kernel,
        out_shape=[jax.ShapeDtypeStruct((1, ne), jnp.int32),
                   jax.ShapeDtypeStruct((1, n_blocks), jnp.int32),
                   jax.ShapeDtypeStruct((1, n_blocks), jnp.int32)],
        name="moe_meta",
    )(counts)


def _dest_kernel(poff_ref, te_ref, rank_ref, dest_ref):
    te = te_ref[...]
    dest = rank_ref[...]
    for e in range(poff_ref.shape[1]):
        dest = dest + jnp.where(te == e, poff_ref[0, e], 0)
    dest_ref[...] = dest


def _dest(poff, te, rank):
    return pl.pallas_call(
        _dest_kernel,
        in_specs=[pl.BlockSpec(memory_space=pltpu.SMEM), _const_spec(te.shape), _const_spec(rank.shape)],
        out_specs=_const_spec(te.shape),
        out_shape=jax.ShapeDtypeStruct(te.shape, jnp.int32),
        grid=(1,),
        name="moe_dest",
    )(poff, te, rank)


def _row(ref, r, nt):
    return ref.at[pl.ds(pl.multiple_of(r * nt, nt), nt), :]


def _dispatch_kernel(nt, cnt_ref, poff_ref, dest_ref, h2_ref, xs_ref, zero_ref, sem, zsem):
    rows = h2_ref.shape[0] // nt
    n_assign = rows * TOP_K
    ne = cnt_ref.shape[1]

    @pl.when(pl.program_id(0) == 0)
    def _():
        zero_ref[...] = jnp.zeros_like(zero_ref)

        def per_expert(e, carry):
            c = cnt_ref[0, e]
            p = poff_ref[0, e]
            pad = (MOE_BLOCK - (c % MOE_BLOCK)) % MOE_BLOCK

            def put(r, carry2):
                pltpu.make_async_copy(_row(zero_ref, r, nt), _row(xs_ref, p + c + r, nt), zsem).start()
                return carry2

            lax.fori_loop(0, pad, put, 0)

            @pl.when(pad > 0)
            def _():
                pltpu.make_async_copy(zero_ref.at[pl.ds(0, pad * nt), :],
                                      xs_ref.at[pl.ds(0, pad * nt), :], zsem).wait()
            return p + c + pad

        used = lax.fori_loop(0, ne, per_expert, 0)

        def spare(b, carry):
            r = pl.multiple_of(b * (MOE_BLOCK * nt), MOE_BLOCK * nt)
            cp = pltpu.make_async_copy(zero_ref, xs_ref.at[pl.ds(r, MOE_BLOCK * nt), :], zsem)
            cp.start()
            cp.wait()
            return carry

        lax.fori_loop(used // MOE_BLOCK, xs_ref.shape[0] // (MOE_BLOCK * nt), spare, 0)

    unroll = 8

    def issue(i, carry):
        for q in range(unroll):
            a = i * unroll + q
            pltpu.make_async_copy(_row(h2_ref, a // TOP_K, nt), _row(xs_ref, dest_ref[0, 0, a], nt), sem).start()
        return carry

    lax.fori_loop(0, n_assign // unroll, issue, 0)
    for _k in range(TOP_K):
        pltpu.make_async_copy(h2_ref, xs_ref.at[pl.ds(0, rows * nt), :], sem).wait()


def _dispatch(counts, poff, dest, h2, n_rows, nt):
    t = h2.shape[0] // nt
    rows = DISPATCH_ROWS
    assert t % rows == 0
    dest3 = dest.reshape(t // rows, 1, rows * TOP_K)
    smem = pl.BlockSpec(memory_space=pltpu.SMEM)
    return pl.pallas_call(
        functools.partial(_dispatch_kernel, nt),
        grid=(t // rows,),
        in_specs=[smem, smem,
                  pl.BlockSpec((1, 1, rows * TOP_K), lambda i: (i, 0, 0), memory_space=pltpu.SMEM),
                  pl.BlockSpec((rows * nt, LANES), lambda i: (i, 0))],
        out_specs=pl.BlockSpec(memory_space=pl.ANY),
        out_shape=jax.ShapeDtypeStruct((n_rows * nt, LANES), jnp.float32),
        scratch_shapes=[pltpu.VMEM((MOE_BLOCK * nt, LANES), jnp.float32),
                        pltpu.SemaphoreType.DMA(()), pltpu.SemaphoreType.DMA(())],
        compiler_params=pltpu.CompilerParams(
            dimension_semantics=("arbitrary",), vmem_limit_bytes=VMEM_LIMIT, has_side_effects=True),
        name="moe_dispatch",
    )(counts, poff, dest3, h2)


def _expert_kernel(nt, blke_ref, blks_ref, xs_ref, wgu_ref, bgu_ref, wd_ref, bd_ref, ys_ref):
    i = pl.program_id(0)
    f = wd_ref.shape[1]

    @pl.when(blks_ref[0, i] != i)
    def _():
        ys_ref[...] = jnp.zeros_like(ys_ref)

    @pl.when(blks_ref[0, i] == i)
    def _():
        x = jnp.concatenate([xs_ref[pl.ds(j, MOE_BLOCK, stride=nt), :] for j in range(nt)], axis=-1)
        gu = jnp.dot(x.astype(jnp.bfloat16), wgu_ref[0], preferred_element_type=jnp.float32) + bgu_ref[0]
        g = jnp.minimum(gu[:, 0:f], SWIGLU_LIMIT)
        lin = jnp.clip(gu[:, f:2 * f], -SWIGLU_LIMIT, SWIGLU_LIMIT)
        act = g * _sigmoid(SWIGLU_ALPHA * g) * (lin + 1.0)
        y = jnp.dot(act.astype(jnp.bfloat16), wd_ref[0], preferred_element_type=jnp.float32) + bd_ref[0]
        for j in range(nt):
            ys_ref[pl.ds(j, MOE_BLOCK, stride=nt), :] = y[:, j * LANES:(j + 1) * LANES]


def _experts(blk_e, blk_s, xs, w_gate_up, b_gate_up, w_down, b_down, nt):
    ne, d, f2 = w_gate_up.shape
    f = f2 // 2
    n_blocks = blk_e.shape[1]
    blk_rows = MOE_BLOCK * nt
    grid_spec = pltpu.PrefetchScalarGridSpec(
        num_scalar_prefetch=2,
        grid=(n_blocks,),
        in_specs=[pl.BlockSpec((blk_rows, LANES), lambda i, be, bs: (bs[0, i], 0)),
                  pl.BlockSpec((1, d, f2), lambda i, be, bs: (be[0, i], 0, 0)),
                  pl.BlockSpec((1, 1, f2), lambda i, be, bs: (be[0, i], 0, 0)),
                  pl.BlockSpec((1, f, d), lambda i, be, bs: (be[0, i], 0, 0)),
                  pl.BlockSpec((1, 1, d), lambda i, be, bs: (be[0, i], 0, 0))],
        out_specs=pl.BlockSpec((blk_rows, LANES), lambda i, be, bs: (i, 0)),
    )
    return pl.pallas_call(
        functools.partial(_expert_kernel, nt),
        grid_spec=grid_spec,
        out_shape=jax.ShapeDtypeStruct(xs.shape, jnp.float32),
        compiler_params=pltpu.CompilerParams(
            dimension_semantics=("arbitrary",), vmem_limit_bytes=VMEM_LIMIT),
        name="moe_experts",
    )(blk_e, blk_s, xs, w_gate_up, b_gate_up.reshape(ne, 1, f2), w_down, b_down.reshape(ne, 1, d))


def _combine_kernel(nt, dest_ref, gate_ref, x1_ref, gfin_ref, ys_ref, out_ref, buf_ref, sem):
    rows, d = x1_ref.shape
    n_assign = rows * TOP_K
    unroll = 8

    def issue(i, carry):
        for q in range(unroll):
            a = i * unroll + q
            b = (a % TOP_K) * rows + a // TOP_K
            pltpu.make_async_copy(_row(ys_ref, dest_ref[0, 0, a], nt), _row(buf_ref, b, nt), sem).start()
        return carry

    lax.fori_loop(0, n_assign // unroll, issue, 0)
    pltpu.make_async_copy(ys_ref.at[pl.ds(0, n_assign * nt), :], buf_ref, sem).wait()

    gate = gate_ref[...]
    x = x1_ref[...]
    for k in range(TOP_K):
        yk = jnp.concatenate(
            [buf_ref[pl.ds(k * rows * nt + j, rows, stride=nt), :] for j in range(nt)], axis=-1)
        x = x + gate[:, k:k + 1] * yk
    ms = jnp.mean(x * x, axis=-1, keepdims=True)
    out_ref[...] = x * lax.rsqrt(ms + NORM_EPS) * gfin_ref[...]


def _combine(dest, gate, x1, g_final, ys, nt):
    t, d = x1.shape
    rows = COMBINE_ROWS
    assert t % rows == 0
    dest3 = dest.reshape(t // rows, 1, rows * TOP_K)
    return pl.pallas_call(
        functools.partial(_combine_kernel, nt),
        grid=(t // rows,),
        in_specs=[pl.BlockSpec((1, 1, rows * TOP_K), lambda i: (i, 0, 0), memory_space=pltpu.SMEM),
                  pl.BlockSpec((rows, TOP_K), lambda i: (i, 0)),
                  pl.BlockSpec((rows, d), lambda i: (i, 0)),
                  _const_spec(g_final.shape),
                  pl.BlockSpec(memory_space=pl.ANY)],
        out_specs=pl.BlockSpec((rows, d), lambda i: (i, 0)),
        out_shape=jax.ShapeDtypeStruct((t, d), jnp.float32),
        scratch_shapes=[pltpu.VMEM((rows * TOP_K * nt, LANES), jnp.float32), pltpu.SemaphoreType.DMA(())],
        compiler_params=pltpu.CompilerParams(
            dimension_semantics=("arbitrary",), vmem_limit_bytes=VMEM_LIMIT),
        name="moe_combine",
    )(dest3, gate, x1, g_final, ys)


def _block_diag(blocks):
    g, r, c = blocks.shape
    eye = jnp.eye(g, dtype=blocks.dtype)
    return (blocks[:, :, None, :] * eye[:, None, :, None]).reshape(g * r, g * c)


def kernel(x, norm_mix_g, w_in, conv_w, conv_b, conv_ln_g, conv_ln_b, w_conv_out, ssm_a_re, ssm_a_im,
           ssm_log_step, ssm_b_re, ssm_b_im, ssm_c_re, ssm_c_im, ssm_d, w_ssm_glu, b_ssm_glu, w_ssm_out,
           w_out, norm_ffn_g, w_router, b_router, w_gate_up, b_gate_up, w_down, b_down, norm_final_g):
    bsz, seq, d = x.shape
    depth = w_in.shape[0]
    bf = jnp.bfloat16
    nt = d // LANES
    n_tok = bsz * seq
    ne = w_router.shape[-1]
    n_blocks = -(-n_tok * TOP_K // MOE_BLOCK) + ne
    seg_len = MIX_ROWS // SCAN_SEGS

    def row(v):
        return v.reshape(1, -1)

    for l in range(depth):
        ar, ai, arp, aip, bbr, bbi = _ssm_disc(
            ssm_a_re[l], ssm_a_im[l], ssm_log_step[l],
            jnp.swapaxes(ssm_b_re[l], 1, 2), jnp.swapaxes(ssm_b_im[l], 1, 2), seg_len)
        bblk = jnp.concatenate([_block_diag(bbr), _block_diag(bbi)], axis=1).astype(bf)
        c_r = _block_diag(jnp.swapaxes(ssm_c_re[l], 1, 2)).astype(bf)
        c_i = _block_diag(jnp.swapaxes(ssm_c_im[l], 1, 2)).astype(bf)
        avec = jnp.stack([ar.reshape(-1), ai.reshape(-1), arp.reshape(-1), aip.reshape(-1)])

        x = _mixer(x, row(norm_mix_g[l]), w_in[l].astype(bf), conv_w[l], row(conv_b[l]),
                   row(conv_ln_g[l]), row(conv_ln_b[l]), w_conv_out[l].astype(bf), bblk, c_r, c_i, avec,
                   row(ssm_d[l]), w_ssm_glu[l].astype(bf), row(b_ssm_glu[l]), w_ssm_out[l].astype(bf),
                   w_out[l].astype(bf))

        x1 = x.reshape(n_tok, d)
        h2, te, gate, rank, counts = _route(x1, row(norm_ffn_g[l]), w_router[l], row(b_router[l]))
        poff, blk_e, blk_s = _meta(counts, n_blocks)
        dest = _dest(poff, te.reshape(-1, LANES), rank.reshape(-1, LANES))
        xs = _dispatch(counts, poff, dest, h2, n_blocks * MOE_BLOCK, nt)
        ys = _experts(blk_e, blk_s, xs, w_gate_up[l].astype(bf), b_gate_up[l], w_down[l].astype(bf),
                      b_down[l], nt)
        assert depth == 1
        x = _combine(dest, gate, x1, row(norm_final_g), ys, nt).reshape(bsz, seq, d)
    return x
```

```python
import functools
import math

import jax
import jax.numpy as jnp
from jax import lax
from jax.experimental import pallas as pl
from jax.experimental.pallas import tpu as pltpu

NORM_EPS = 1e-6
SWIGLU_LIMIT = 7.0
SWIGLU_ALPHA = 1.702
TOP_K = 4
SSM_GROUP = 16

LANES = 128
SUBLANES = 8

MIX_ROWS = 512
SCAN_SEGS = SUBLANES
SEG_PAD = 8
CONV_HALO = 32
CONV_ROWS = 64
ROUTE_ROWS = 512
MOE_BLOCK = 256
DISPATCH_ROWS = 1024
COMBINE_ROWS = 256
ISSUE_TOKENS = 4
VMEM_LIMIT = 56 * 1024 * 1024


def _sigmoid(x):
    return 1.0 / (1.0 + jnp.exp(-x))


def _gelu_tanh(x):
    c = math.sqrt(2.0 / math.pi)
    return 0.5 * x * (1.0 + jnp.tanh(c * (x + 0.044715 * (x * x * x))))


def _const_spec(shape):
    nd = len(shape)
    return pl.BlockSpec(shape, lambda *_: (0,) * nd)


def _ssm_disc_kernel(seg_len, are_ref, aim_ref, ls_ref, bre_ref, bim_ref,
                     ar_ref, ai_ref, arp_ref, aip_ref, bbr_ref, bbi_ref):
    lr = are_ref[...]
    li = aim_ref[...]
    dt = jnp.exp(ls_ref[...])
    mag = jnp.exp(lr * dt)
    ar = mag * jnp.cos(li * dt)
    ai = mag * jnp.sin(li * dt)
    den = lr * lr + li * li
    fr = ((ar - 1.0) * lr + ai * li) / den
    fi = (ai * lr - (ar - 1.0) * li) / den
    ar_ref[...] = ar
    ai_ref[...] = ai
    magp = jnp.exp(lr * dt * seg_len)
    arp_ref[...] = magp * jnp.cos(li * dt * seg_len)
    aip_ref[...] = magp * jnp.sin(li * dt * seg_len)
    br = bre_ref[...]
    bi = bim_ref[...]
    bbr_ref[...] = fr[:, None, :] * br - fi[:, None, :] * bi
    bbi_ref[...] = fr[:, None, :] * bi + fi[:, None, :] * br


def _ssm_disc(a_re, a_im, log_step, bt_re, bt_im, seg_len):
    g, p = a_re.shape
    h = bt_re.shape[1]
    f32 = jnp.float32
    out_shape = [jax.ShapeDtypeStruct((g, p), f32)] * 4 + [jax.ShapeDtypeStruct((g, h, p), f32)] * 2
    return pl.pallas_call(
        functools.partial(_ssm_disc_kernel, float(seg_len)),
        out_shape=out_shape,
        name="ssm_disc",
    )(a_re, a_im, log_step.reshape(g, 1), bt_re, bt_im)


def _mixer_kernel(dims, x_ref, gmix_ref, win_ref, convw_ref, convb_ref, lng_ref, lnb_ref, wco_ref,
                  bblk_ref, cr_ref, ci_ref, avec_ref, dskip_ref, wglu_ref, bglu_ref, wso_ref, wout_ref,
                  x1_ref, vbuf_ref, conv_ref, scan_ref, xst_ref, cin_ref, st_ref):
    d, dc, ds, gp, cw = dims
    rows = MIX_ROWS
    seg = rows // SCAN_SEGS
    pitch = seg + SEG_PAD
    nsl = gp // LANES
    c_idx = pl.program_id(1)

    half = rows // 2
    tail = 2 * half + 1

    @pl.when(c_idx == 0)
    def _():
        for lc in range(dc // LANES):
            vbuf_ref[lc, pl.ds(0, CONV_HALO, stride=2), :] = jnp.zeros((CONV_HALO, LANES), jnp.float32)
        st_ref[...] = jnp.zeros_like(st_ref)

    @pl.when(c_idx != 0)
    def _():
        for lc in range(dc // LANES):
            vbuf_ref[lc, pl.ds(0, CONV_HALO, stride=2), :] = vbuf_ref[lc, pl.ds(tail, CONV_HALO, stride=2), :]

    x = x_ref[0]
    ms = jnp.mean(x * x, axis=-1, keepdims=True)
    h = (x * lax.rsqrt(ms + NORM_EPS) * gmix_ref[...]).astype(jnp.bfloat16)

    pa = jnp.dot(h, win_ref[:, 0:2 * dc], preferred_element_type=jnp.float32)
    vglu = pa[:, 0:dc] * _sigmoid(pa[:, dc:2 * dc])
    for lc in range(dc // LANES):
        ls = slice(lc * LANES, (lc + 1) * LANES)
        vbuf_ref[lc, pl.ds(2 * CONV_HALO, half, stride=2), :] = vglu[0:half, ls]
        vbuf_ref[lc, pl.ds(2 * CONV_HALO + 1, half, stride=2), :] = vglu[half:rows, ls]
        vbuf_ref[lc, pl.ds(1, CONV_HALO, stride=2), :] = vglu[half - CONV_HALO:half, ls]

    base = CONV_HALO - (cw - 1)
    nv = CONV_ROWS // SUBLANES

    def conv_chunk(rc, carry):
        r0 = pl.multiple_of(rc * CONV_ROWS, CONV_ROWS)
        for p in range(2):
            for lc in range(dc // LANES):
                ls = slice(lc * LANES, (lc + 1) * LANES)
                wv = [jnp.broadcast_to(convw_ref[k:k + 1, ls], (SUBLANES, LANES)) for k in range(cw)]
                acc = [jnp.zeros((SUBLANES, LANES), jnp.float32) for _ in range(nv)]
                for s in range(CONV_ROWS + cw - 1):
                    win = vbuf_ref[lc, pl.ds(2 * (r0 + base + s) + p, SUBLANES, stride=2), :]
                    for i in range(nv):
                        k = s - SUBLANES * i
                        if 0 <= k < cw:
                            acc[i] = acc[i] + wv[k] * win
                for i in range(nv):
                    conv_ref[pl.ds(p * half + r0 + SUBLANES * i, SUBLANES), ls] = acc[i]
        return carry

    lax.fori_loop(0, half // CONV_ROWS, conv_chunk, 0)

    v = conv_ref[...] + convb_ref[...]
    mu = jnp.mean(v, axis=-1, keepdims=True)
    vc = v - mu
    var = jnp.mean(vc * vc, axis=-1, keepdims=True)
    v = vc * lax.rsqrt(var + NORM_EPS) * lng_ref[...] + lnb_ref[...]
    v = v * _sigmoid(v)
    y_conv = jnp.dot(v.astype(jnp.bfloat16), wco_ref[...], preferred_element_type=jnp.float32)

    u = jnp.dot(h, win_ref[:, 2 * dc:2 * dc + ds], preferred_element_type=jnp.float32)
    bu = jnp.dot(u.astype(jnp.bfloat16), bblk_ref[...], preferred_element_type=jnp.float32)
    for n in range(2 * nsl):
        for s in range(SCAN_SEGS):
            scan_ref[n, s * pitch:s * pitch + seg, :] = bu[s * seg:(s + 1) * seg, n * LANES:(n + 1) * LANES]

    def bcast(row, n):
        return jnp.broadcast_to(avec_ref[row:row + 1, n * LANES:(n + 1) * LANES], (SUBLANES, LANES))

    a_r = [bcast(0, n) for n in range(nsl)]
    a_i = [bcast(1, n) for n in range(nsl)]

    def step(j, st, store):
        out = []
        for n in range(nsl):
            sr, si = st[2 * n], st[2 * n + 1]
            br = scan_ref[n, pl.ds(j, SCAN_SEGS, stride=pitch), :]
            bi = scan_ref[nsl + n, pl.ds(j, SCAN_SEGS, stride=pitch), :]
            nr = a_r[n] * sr - a_i[n] * si + br
            ni = a_r[n] * si + a_i[n] * sr + bi
            if store:
                scan_ref[n, pl.ds(j, SCAN_SEGS, stride=pitch), :] = nr
                scan_ref[nsl + n, pl.ds(j, SCAN_SEGS, stride=pitch), :] = ni
            out += [nr, ni]
        return tuple(out)

    zero = jnp.zeros((SUBLANES, LANES), jnp.float32)
    fin = lax.fori_loop(0, seg, functools.partial(step, store=False), (zero,) * (2 * nsl))

    for n in range(nsl):
        fr, fi = fin[2 * n], fin[2 * n + 1]
        ap_r = avec_ref[2:3, n * LANES:(n + 1) * LANES]
        ap_i = avec_ref[3:4, n * LANES:(n + 1) * LANES]
        c_r = st_ref[0:1, n * LANES:(n + 1) * LANES]
        c_i = st_ref[1:2, n * LANES:(n + 1) * LANES]
        for s in range(SCAN_SEGS):
            cin_ref[2 * n, s:s + 1, :] = c_r
            cin_ref[2 * n + 1, s:s + 1, :] = c_i
            n_r = ap_r * c_r - ap_i * c_i + fr[s:s + 1, :]
            n_i = ap_r * c_i + ap_i * c_r + fi[s:s + 1, :]
            c_r, c_i = n_r, n_i
        st_ref[0:1, n * LANES:(n + 1) * LANES] = c_r
        st_ref[1:2, n * LANES:(n + 1) * LANES] = c_i

    lax.fori_loop(0, seg, functools.partial(step, store=True), tuple(cin_ref[q] for q in range(2 * nsl)))

    for n in range(2 * nsl):
        for s in range(SCAN_SEGS):
            xst_ref[s * seg:(s + 1) * seg, n * LANES:(n + 1) * LANES] = (
                scan_ref[n, s * pitch:s * pitch + seg, :].astype(jnp.bfloat16))

    y = (jnp.dot(xst_ref[:, 0:gp], cr_ref[...], preferred_element_type=jnp.float32)
         - jnp.dot(xst_ref[:, gp:2 * gp], ci_ref[...], preferred_element_type=jnp.float32))
    y = _gelu_tanh(y + dskip_ref[...] * u)
    glu = jnp.dot(y.astype(jnp.bfloat16), wglu_ref[...], preferred_element_type=jnp.float32) + bglu_ref[...]
    y = y * _sigmoid(glu)
    y_ssm = jnp.dot(y.astype(jnp.bfloat16), wso_ref[...], preferred_element_type=jnp.float32)

    g0 = 2 * dc + ds
    g_conv = jnp.dot(h, win_ref[:, g0:g0 + d], preferred_element_type=jnp.float32)
    m = _sigmoid(g_conv) * y_conv
    g_ssm = jnp.dot(h, win_ref[:, g0 + d:g0 + 2 * d], preferred_element_type=jnp.float32)
    m = m + _sigmoid(g_ssm) * y_ssm
    x1_ref[0] = x + jnp.dot(m.astype(jnp.bfloat16), wout_ref[...], preferred_element_type=jnp.float32)


def _mixer(x, gmix, w_in, conv_w, conv_b, ln_g, ln_b, w_co, bblk, c_r, c_i, avec, d_skip,
           w_glu, b_glu, w_so, w_out):
    b, s, d = x.shape
    cw, dc = conv_w.shape
    ds = d_skip.shape[-1]
    gp = c_r.shape[0]
    rows = MIX_ROWS
    seg = rows // SCAN_SEGS
    pitch = seg + SEG_PAD
    assert s % rows == 0 and cw - 1 <= CONV_HALO and gp % LANES == 0 and dc % LANES == 0
    dims = (d, dc, ds, gp, cw)
    consts = [gmix, w_in, conv_w, conv_b, ln_g, ln_b, w_co, bblk, c_r, c_i, avec, d_skip,
              w_glu, b_glu, w_so, w_out]
    return pl.pallas_call(
        functools.partial(_mixer_kernel, dims),
        grid=(b, s // rows),
        in_specs=[pl.BlockSpec((1, rows, d), lambda i, j: (i, j, 0))] + [_const_spec(c.shape) for c in consts],
        out_specs=pl.BlockSpec((1, rows, d), lambda i, j: (i, j, 0)),
        out_shape=jax.ShapeDtypeStruct((b, s, d), jnp.float32),
        scratch_shapes=[
            pltpu.VMEM((dc // LANES, 2 * CONV_HALO + rows, LANES), jnp.float32),
            pltpu.VMEM((rows, dc), jnp.float32),
            pltpu.VMEM((2 * gp // LANES, SCAN_SEGS * pitch, LANES), jnp.float32),
            pltpu.VMEM((rows, 2 * gp), jnp.bfloat16),
            pltpu.VMEM((2 * gp // LANES, SUBLANES, LANES), jnp.float32),
            pltpu.VMEM((2, gp), jnp.float32),
        ],
        compiler_params=pltpu.CompilerParams(
            dimension_semantics=("arbitrary", "arbitrary"), vmem_limit_bytes=VMEM_LIMIT),
        name="mixer",
    )(x, *consts)


def _route_kernel(x1_ref, g_ref, wr_ref, br_ref, h2_ref, te_ref, gate_ref, rank_ref, cnt_ref, carry_ref):
    rows, d = x1_ref.shape
    ne = wr_ref.shape[1]

    @pl.when(pl.program_id(0) == 0)
    def _():
        carry_ref[...] = jnp.zeros_like(carry_ref)

    x = x1_ref[...]
    ms = jnp.mean(x * x, axis=-1, keepdims=True)
    h2 = x * lax.rsqrt(ms + NORM_EPS) * g_ref[...]
    for j in range(d // LANES):
        h2_ref[pl.ds(j, rows, stride=d // LANES), :] = h2[:, j * LANES:(j + 1) * LANES]

    bf = jnp.bfloat16
    h_hi = h2.astype(bf)
    h_lo = (h2 - h_hi.astype(jnp.float32)).astype(bf)
    w = wr_ref[...]
    w_hi = w.astype(bf)
    w_lo = (w - w_hi.astype(jnp.float32)).astype(bf)
    logits = (jnp.dot(h_hi, w_hi, preferred_element_type=jnp.float32)
              + jnp.dot(h_lo, w_hi, preferred_element_type=jnp.float32)
              + jnp.dot(h_hi, w_lo, preferred_element_type=jnp.float32)) + br_ref[...]

    lane = lax.broadcasted_iota(jnp.int32, (rows, ne), 1).astype(jnp.float32)
    slot = lax.broadcasted_iota(jnp.int32, (rows, TOP_K), 1)
    work = logits
    sels, vals = [], []
    te = jnp.zeros((rows, TOP_K), jnp.float32)
    for k in range(TOP_K):
        mx = jnp.max(work, axis=-1, keepdims=True)
        idx = jnp.min(jnp.where(work == mx, lane, float(ne)), axis=-1, keepdims=True)
        sel = lane == idx
        sels.append(sel)
        vals.append(mx)
        te = jnp.where(slot == k, idx, te)
        work = jnp.where(sel, -jnp.inf, work)
    te_ref[...] = te.astype(jnp.int32)

    exps = [jnp.exp(v - vals[0]) for v in vals]
    tot = exps[0]
    for e in exps[1:]:
        tot = tot + e
    gate = jnp.zeros((rows, TOP_K), jnp.float32)
    for k in range(TOP_K):
        gate = jnp.where(slot == k, exps[k] / tot, gate)
    gate_ref[...] = gate

    cnt = jnp.zeros((rows, ne), jnp.float32)
    for sel in sels:
        cnt = cnt + sel.astype(jnp.float32)
    r_i = lax.broadcasted_iota(jnp.int32, (rows, rows), 0)
    c_i = lax.broadcasted_iota(jnp.int32, (rows, rows), 1)
    tril = (c_i < r_i).astype(bf)
    before = jnp.dot(tril, cnt.astype(bf), preferred_element_type=jnp.float32) + carry_ref[...]
    rank = jnp.zeros((rows, TOP_K), jnp.float32)
    for k in range(TOP_K):
        rk = jnp.sum(jnp.where(sels[k], before, 0.0), axis=-1, keepdims=True)
        rank = jnp.where(slot == k, rk, rank)
    rank_ref[...] = rank.astype(jnp.int32)
    carry_ref[...] = carry_ref[...] + jnp.sum(cnt, axis=0, keepdims=True)
    cnt_ref[...] = carry_ref[...].astype(jnp.int32)


def _route(x1, g_ffn, w_router, b_router):
    t, d = x1.shape
    ne = w_router.shape[1]
    rows = ROUTE_ROWS
    assert t % rows == 0 and d % LANES == 0
    nt = d // LANES
    return pl.pallas_call(
        _route_kernel,
        grid=(t // rows,),
        in_specs=[pl.BlockSpec((rows, d), lambda i: (i, 0)),
                  _const_spec(g_ffn.shape), _const_spec(w_router.shape), _const_spec(b_router.shape)],
        out_specs=[pl.BlockSpec((rows * nt, LANES), lambda i: (i, 0)),
                   pl.BlockSpec((rows, TOP_K), lambda i: (i, 0)),
                   pl.BlockSpec((rows, TOP_K), lambda i: (i, 0)),
                   pl.BlockSpec((rows, TOP_K), lambda i: (i, 0)),
                   _const_spec((1, ne))],
        out_shape=[jax.ShapeDtypeStruct((t * nt, LANES), jnp.float32),
                   jax.ShapeDtypeStruct((t, TOP_K), jnp.int32),
                   jax.ShapeDtypeStruct((t, TOP_K), jnp.float32),
                   jax.ShapeDtypeStruct((t, TOP_K), jnp.int32),
                   jax.ShapeDtypeStruct((1, ne), jnp.int32)],
        scratch_shapes=[pltpu.VMEM((1, ne), jnp.float32)],
        compiler_params=pltpu.CompilerParams(
            dimension_semantics=("arbitrary",), vmem_limit_bytes=VMEM_LIMIT),
        name="route",
    )(x1, g_ffn, w_router, b_router)


def _meta_kernel(cnt_ref, poff_ref, blke_ref, blks_ref):
    ne = cnt_ref.shape[1]
    nb = blke_ref.shape[1]
    f32 = jnp.float32
    cnt = cnt_ref[...].astype(f32)
    nblk = jnp.floor((cnt + (MOE_BLOCK - 1)) * (1.0 / MOE_BLOCK))
    e_r = lax.broadcasted_iota(jnp.int32, (ne, ne), 0)
    e_c = lax.broadcasted_iota(jnp.int32, (ne, ne), 1)
    nb_rows = jnp.broadcast_to(nblk, (ne, ne))
    bend_col = jnp.sum(jnp.where(e_c <= e_r, nb_rows, 0.0), axis=1, keepdims=True)
    nb_col = jnp.sum(jnp.where(e_c == e_r, nb_rows, 0.0), axis=1, keepdims=True)
    boff = jnp.sum(jnp.where(e_r < e_c, jnp.broadcast_to(nb_col, (ne, ne)), 0.0),
                   axis=0, keepdims=True)
    poff_ref[...] = (boff * MOE_BLOCK).astype(jnp.int32)
    total = jnp.sum(nblk, axis=1, keepdims=True)
    blk = lax.broadcasted_iota(jnp.int32, (ne, nb), 1).astype(f32)
    done = jnp.where(jnp.broadcast_to(bend_col, (ne, nb)) <= blk, 1.0, 0.0)
    be = jnp.minimum(jnp.sum(done, axis=0, keepdims=True), ne - 1.0)
    bid = lax.broadcasted_iota(jnp.int32, (1, nb), 1).astype(f32)
    last_e = jnp.sum(jnp.where(bid == total - 1.0, be, 0.0), axis=1, keepdims=True)
    blke_ref[...] = jnp.where(bid < total, be, last_e).astype(jnp.int32)
    blks_ref[...] = jnp.minimum(bid, total - 1.0).astype(jnp.int32)


def _meta(counts, n_blocks):
    ne = counts.shape[1]
    return pl.pallas_call(
        _meta_kernel,
        out_shape=[jax.ShapeDtypeStruct((1, ne), jnp.int32),
                   jax.ShapeDtypeStruct((1, n_blocks), jnp.int32),
                   jax.ShapeDtypeStruct((1, n_blocks), jnp.int32)],
        name="moe_meta",
    )(counts)


def _dest_kernel(poff_ref, te_ref, rank_ref, dest_ref):
    te = te_ref[...]
    dest = rank_ref[...]
    for e in range(poff_ref.shape[1]):
        dest = dest + jnp.where(te == e, poff_ref[0, e], 0)
    dest_ref[...] = dest


def _dest(poff, te, rank):
    return pl.pallas_call(
        _dest_kernel,
        in_specs=[pl.BlockSpec(memory_space=pltpu.SMEM), _const_spec(te.shape), _const_spec(rank.shape)],
        out_specs=_const_spec(te.shape),
        out_shape=jax.ShapeDtypeStruct(te.shape, jnp.int32),
        grid=(1,),
        name="moe_dest",
    )(poff, te, rank)


def _row(ref, r, nt):
    return ref.at[pl.ds(pl.multiple_of(r * nt, nt), nt), :]


def _dispatch_kernel(nt, cnt_ref, poff_ref, dest_ref, h2_ref, xs_ref, zero_ref, sem, zsem):
    rows = h2_ref.shape[0] // nt
    n_assign = rows * TOP_K
    ne = cnt_ref.shape[1]

    @pl.when(pl.program_id(0) == 0)
    def _():
        zero_ref[...] = jnp.zeros_like(zero_ref)

        def per_expert(e, carry):
            c = cnt_ref[0, e]
            p = poff_ref[0, e]
            pad = (MOE_BLOCK - (c % MOE_BLOCK)) % MOE_BLOCK

            def put(r, carry2):
                pltpu.make_async_copy(_row(zero_ref, r, nt), _row(xs_ref, p + c + r, nt), zsem).start()
                return carry2

            lax.fori_loop(0, pad, put, 0)

            @pl.when(pad > 0)
            def _():
                pltpu.make_async_copy(zero_ref.at[pl.ds(0, pad * nt), :],
                                      xs_ref.at[pl.ds(0, pad * nt), :], zsem).wait()
            return p + c + pad

        used = lax.fori_loop(0, ne, per_expert, 0)

        def spare(b, carry):
            r = pl.multiple_of(b * (MOE_BLOCK * nt), MOE_BLOCK * nt)
            cp = pltpu.make_async_copy(zero_ref, xs_ref.at[pl.ds(r, MOE_BLOCK * nt), :], zsem)
            cp.start()
            cp.wait()
            return carry

        lax.fori_loop(used // MOE_BLOCK, xs_ref.shape[0] // (MOE_BLOCK * nt), spare, 0)

    def issue(i, carry):
        for u in range(ISSUE_TOKENS):
            tok = i * ISSUE_TOKENS + u
            for k in range(TOP_K):
                pltpu.make_async_copy(_row(h2_ref, tok, nt), _row(xs_ref, dest_ref[0, 0, tok * TOP_K + k], nt),
                                      sem).start(priority=k % 2)
        return carry

    lax.fori_loop(0, rows // ISSUE_TOKENS, issue, 0)
    for _k in range(TOP_K):
        pltpu.make_async_copy(h2_ref, xs_ref.at[pl.ds(0, rows * nt), :], sem).wait()


def _dispatch(counts, poff, dest, h2, n_rows, nt):
    t = h2.shape[0] // nt
    rows = DISPATCH_ROWS
    assert t % rows == 0
    dest3 = dest.reshape(t // rows, 1, rows * TOP_K)
    smem = pl.BlockSpec(memory_space=pltpu.SMEM)
    return pl.pallas_call(
        functools.partial(_dispatch_kernel, nt),
        grid=(t // rows,),
        in_specs=[smem, smem,
                  pl.BlockSpec((1, 1, rows * TOP_K), lambda i: (i, 0, 0), memory_space=pltpu.SMEM),
                  pl.BlockSpec((rows * nt, LANES), lambda i: (i, 0))],
        out_specs=pl.BlockSpec(memory_space=pl.ANY),
        out_shape=jax.ShapeDtypeStruct((n_rows * nt, LANES), jnp.float32),
        scratch_shapes=[pltpu.VMEM((MOE_BLOCK * nt, LANES), jnp.float32),
                        pltpu.SemaphoreType.DMA(()), pltpu.SemaphoreType.DMA(())],
        compiler_params=pltpu.CompilerParams(
            dimension_semantics=("arbitrary",), vmem_limit_bytes=VMEM_LIMIT, has_side_effects=True),
        name="moe_dispatch",
    )(counts, poff, dest3, h2)


def _expert_kernel(nt, blke_ref, blks_ref, xs_ref, wgu32_ref, bgu_ref, wd32_ref, bd_ref, ys_ref,
                   wgu_ref, wd_ref):
    i = pl.program_id(0)
    f = wd_ref.shape[0]

    @pl.when(jnp.logical_or(i == 0, blke_ref[0, i] != blke_ref[0, jnp.maximum(i - 1, 0)]))
    def _():
        wgu_ref[...] = wgu32_ref[0].astype(jnp.bfloat16)
        wd_ref[...] = wd32_ref[0].astype(jnp.bfloat16)

    @pl.when(blks_ref[0, i] != i)
    def _():
        ys_ref[...] = jnp.zeros_like(ys_ref)

    @pl.when(blks_ref[0, i] == i)
    def _():
        x = jnp.concatenate([xs_ref[pl.ds(j, MOE_BLOCK, stride=nt), :] for j in range(nt)], axis=-1)
        gu = jnp.dot(x.astype(jnp.bfloat16), wgu_ref[...], preferred_element_type=jnp.float32) + bgu_ref[0]
        g = jnp.minimum(gu[:, 0:f], SWIGLU_LIMIT)
        lin = jnp.clip(gu[:, f:2 * f], -SWIGLU_LIMIT, SWIGLU_LIMIT)
        act = g * _sigmoid(SWIGLU_ALPHA * g) * (lin + 1.0)
        y = jnp.dot(act.astype(jnp.bfloat16), wd_ref[...], preferred_element_type=jnp.float32) + bd_ref[0]
        for j in range(nt):
            ys_ref[pl.ds(j, MOE_BLOCK, stride=nt), :] = y[:, j * LANES:(j + 1) * LANES]


def _experts(blk_e, blk_s, xs, w_gate_up, b_gate_up, w_down, b_down, nt):
    ne, d, f2 = w_gate_up.shape
    f = f2 // 2
    n_blocks = blk_e.shape[1]
    blk_rows = MOE_BLOCK * nt
    grid_spec = pltpu.PrefetchScalarGridSpec(
        num_scalar_prefetch=2,
        grid=(n_blocks,),
        in_specs=[pl.BlockSpec((blk_rows, LANES), lambda i, be, bs: (bs[0, i], 0)),
                  pl.BlockSpec((1, d, f2), lambda i, be, bs: (be[0, i], 0, 0)),
                  pl.BlockSpec((1, 1, f2), lambda i, be, bs: (be[0, i], 0, 0)),
                  pl.BlockSpec((1, f, d), lambda i, be, bs: (be[0, i], 0, 0)),
                  pl.BlockSpec((1, 1, d), lambda i, be, bs: (be[0, i], 0, 0))],
        out_specs=pl.BlockSpec((blk_rows, LANES), lambda i, be, bs: (i, 0)),
        scratch_shapes=[pltpu.VMEM((d, f2), jnp.bfloat16), pltpu.VMEM((f, d), jnp.bfloat16)],
    )
    return pl.pallas_call(
        functools.partial(_expert_kernel, nt),
        grid_spec=grid_spec,
        out_shape=jax.ShapeDtypeStruct(xs.shape, jnp.float32),
        compiler_params=pltpu.CompilerParams(
            dimension_semantics=("arbitrary",), vmem_limit_bytes=VMEM_LIMIT),
        name="moe_experts",
    )(blk_e, blk_s, xs, w_gate_up, b_gate_up.reshape(ne, 1, f2), w_down, b_down.reshape(ne, 1, d))


def _combine_kernel(nt, dest_ref, dnext_ref, gate_ref, x1_ref, gfin_ref, ys_ref, out_ref, buf_ref, sem):
    rows, d = x1_ref.shape
    n_assign = rows * TOP_K
    step = pl.program_id(0)
    cur = step % 2

    def gather(idx_ref, half):
        def issue(i, carry):
            for u in range(ISSUE_TOKENS):
                tok = i * ISSUE_TOKENS + u
                for k in range(TOP_K):
                    pltpu.make_async_copy(_row(ys_ref, idx_ref[0, 0, tok * TOP_K + k], nt),
                                          _row(buf_ref, half * n_assign + k * rows + tok, nt),
                                          sem.at[half]).start(priority=k % 2)
            return carry

        lax.fori_loop(0, rows // ISSUE_TOKENS, issue, 0)

    @pl.when(step == 0)
    def _():
        gather(dest_ref, 0)

    @pl.when(step + 1 < pl.num_programs(0))
    def _():
        gather(dnext_ref, 1 - cur)

    cur_rows = pl.multiple_of(cur * (n_assign * nt), n_assign * nt)
    pltpu.make_async_copy(ys_ref.at[pl.ds(0, n_assign * nt), :],
                          buf_ref.at[pl.ds(cur_rows, n_assign * nt), :], sem.at[cur]).wait()

    gate = gate_ref[...]
    x = x1_ref[...]
    for k in range(TOP_K):
        yk = jnp.concatenate(
            [buf_ref[pl.ds(cur_rows + (k * rows * nt + j), rows, stride=nt), :] for j in range(nt)], axis=-1)
        x = x + gate[:, k:k + 1] * yk
    ms = jnp.mean(x * x, axis=-1, keepdims=True)
    out_ref[...] = x * lax.rsqrt(ms + NORM_EPS) * gfin_ref[...]


def _combine(dest, gate, x1, g_final, ys, nt):
    t, d = x1.shape
    rows = COMBINE_ROWS
    assert t % rows == 0
    steps = t // rows
    dest3 = dest.reshape(steps, 1, rows * TOP_K)
    return pl.pallas_call(
        functools.partial(_combine_kernel, nt),
        grid=(steps,),
        in_specs=[pl.BlockSpec((1, 1, rows * TOP_K), lambda i: (i, 0, 0), memory_space=pltpu.SMEM),
                  pl.BlockSpec((1, 1, rows * TOP_K), lambda i: (jnp.minimum(i + 1, steps - 1), 0, 0),
                               memory_space=pltpu.SMEM),
                  pl.BlockSpec((rows, TOP_K), lambda i: (i, 0)),
                  pl.BlockSpec((rows, d), lambda i: (i, 0)),
                  _const_spec(g_final.shape),
                  pl.BlockSpec(memory_space=pl.ANY)],
        out_specs=pl.BlockSpec((rows, d), lambda i: (i, 0)),
        out_shape=jax.ShapeDtypeStruct((t, d), jnp.float32),
        scratch_shapes=[pltpu.VMEM((2 * rows * TOP_K * nt, LANES), jnp.float32), pltpu.SemaphoreType.DMA((2,))],
        compiler_params=pltpu.CompilerParams(
            dimension_semantics=("arbitrary",), vmem_limit_bytes=VMEM_LIMIT),
        name="moe_combine",
    )(dest3, dest3, gate, x1, g_final, ys)


def _block_diag(blocks):
    g, r, c = blocks.shape
    eye = jnp.eye(g, dtype=blocks.dtype)
    return (blocks[:, :, None, :] * eye[:, None, :, None]).reshape(g * r, g * c)


def kernel(x, norm_mix_g, w_in, conv_w, conv_b, conv_ln_g, conv_ln_b, w_conv_out, ssm_a_re, ssm_a_im,
           ssm_log_step, ssm_b_re, ssm_b_im, ssm_c_re, ssm_c_im, ssm_d, w_ssm_glu, b_ssm_glu, w_ssm_out,
           w_out, norm_ffn_g, w_router, b_router, w_gate_up, b_gate_up, w_down, b_down, norm_final_g):
    bsz, seq, d = x.shape
    depth = w_in.shape[0]
    bf = jnp.bfloat16
    nt = d // LANES
    n_tok = bsz * seq
    ne = w_router.shape[-1]
    n_blocks = -(-n_tok * TOP_K // MOE_BLOCK) + ne
    seg_len = MIX_ROWS // SCAN_SEGS

    def row(v):
        return v.reshape(1, -1)

    for l in range(depth):
        ar, ai, arp, aip, bbr, bbi = _ssm_disc(
            ssm_a_re[l], ssm_a_im[l], ssm_log_step[l],
            jnp.swapaxes(ssm_b_re[l], 1, 2), jnp.swapaxes(ssm_b_im[l], 1, 2), seg_len)
        bblk = jnp.concatenate([_block_diag(bbr), _block_diag(bbi)], axis=1).astype(bf)
        c_r = _block_diag(jnp.swapaxes(ssm_c_re[l], 1, 2)).astype(bf)
        c_i = _block_diag(jnp.swapaxes(ssm_c_im[l], 1, 2)).astype(bf)
        avec = jnp.stack([ar.reshape(-1), ai.reshape(-1), arp.reshape(-1), aip.reshape(-1)])

        x = _mixer(x, row(norm_mix_g[l]), w_in[l].astype(bf), conv_w[l], row(conv_b[l]),
                   row(conv_ln_g[l]), row(conv_ln_b[l]), w_conv_out[l].astype(bf), bblk, c_r, c_i, avec,
                   row(ssm_d[l]), w_ssm_glu[l].astype(bf), row(b_ssm_glu[l]), w_ssm_out[l].astype(bf),
                   w_out[l].astype(bf))

        x1 = x.reshape(n_tok, d)
        h2, te, gate, rank, counts = _route(x1, row(norm_ffn_g[l]), w_router[l], row(b_router[l]))
        poff, blk_e, blk_s = _meta(counts, n_blocks)
        dest = _dest(poff, te.reshape(-1, LANES), rank.reshape(-1, LANES))
        xs = _dispatch(counts, poff, dest, h2, n_blocks * MOE_BLOCK, nt)
        ys = _experts(blk_e, blk_s, xs, w_gate_up[l], b_gate_up[l], w_down[l], b_down[l], nt)
        assert depth == 1
        x = _combine(dest, gate, x1, row(norm_final_g), ys, nt).reshape(bsz, seq, d)
    return x
```

```python
import functools
import math

import jax
import jax.numpy as jnp
from jax import lax
from jax.experimental import pallas as pl
from jax.experimental.pallas import tpu as pltpu

NORM_EPS = 1e-6
SWIGLU_LIMIT = 7.0
SWIGLU_ALPHA = 1.702
TOP_K = 4
SSM_GROUP = 16

LANES = 128
SUBLANES = 8

MIX_ROWS = 512
SCAN_SEGS = SUBLANES
SEG_PAD = 8
CONV_HALO = 32
CONV_ROWS = 64
ROUTE_ROWS = 512
MOE_BLOCK = 512
DISPATCH_ROWS = 1024
COMBINE_ROWS = 256
ISSUE_TOKENS = 4
VMEM_LIMIT = 56 * 1024 * 1024


def _sigmoid(x):
    return 1.0 / (1.0 + jnp.exp(-x))


def _gelu_tanh(x):
    c = math.sqrt(2.0 / math.pi)
    return 0.5 * x * (1.0 + jnp.tanh(c * (x + 0.044715 * (x * x * x))))


def _const_spec(shape):
    nd = len(shape)
    return pl.BlockSpec(shape, lambda *_: (0,) * nd)


def _ssm_disc_kernel(seg_len, are_ref, aim_ref, ls_ref, bre_ref, bim_ref,
                     ar_ref, ai_ref, arp_ref, aip_ref, bbr_ref, bbi_ref):
    lr = are_ref[...]
    li = aim_ref[...]
    dt = jnp.exp(ls_ref[...])
    mag = jnp.exp(lr * dt)
    ar = mag * jnp.cos(li * dt)
    ai = mag * jnp.sin(li * dt)
    den = lr * lr + li * li
    fr = ((ar - 1.0) * lr + ai * li) / den
    fi = (ai * lr - (ar - 1.0) * li) / den
    ar_ref[...] = ar
    ai_ref[...] = ai
    magp = jnp.exp(lr * dt * seg_len)
    arp_ref[...] = magp * jnp.cos(li * dt * seg_len)
    aip_ref[...] = magp * jnp.sin(li * dt * seg_len)
    br = bre_ref[...]
    bi = bim_ref[...]
    bbr_ref[...] = fr[:, None, :] * br - fi[:, None, :] * bi
    bbi_ref[...] = fr[:, None, :] * bi + fi[:, None, :] * br


def _ssm_disc(a_re, a_im, log_step, bt_re, bt_im, seg_len):
    g, p = a_re.shape
    h = bt_re.shape[1]
    f32 = jnp.float32
    out_shape = [jax.ShapeDtypeStruct((g, p), f32)] * 4 + [jax.ShapeDtypeStruct((g, h, p), f32)] * 2
    return pl.pallas_call(
        functools.partial(_ssm_disc_kernel, float(seg_len)),
        out_shape=out_shape,
        name="ssm_disc",
    )(a_re, a_im, log_step.reshape(g, 1), bt_re, bt_im)


def _mixer_kernel(dims, x_ref, gmix_ref, win_ref, convw_ref, convb_ref, lng_ref, lnb_ref, wco_ref,
                  bblk_ref, cr_ref, ci_ref, avec_ref, dskip_ref, wglu_ref, bglu_ref, wso_ref, wout_ref,
                  x1_ref, vbuf_ref, conv_ref, scan_ref, xst_ref, cin_ref, st_ref):
    d, dc, ds, gp, cw = dims
    rows = MIX_ROWS
    seg = rows // SCAN_SEGS
    pitch = seg + SEG_PAD
    nsl = gp // LANES
    c_idx = pl.program_id(1)

    half = rows // 2
    tail = 2 * half + 1

    @pl.when(c_idx == 0)
    def _():
        for lc in range(dc // LANES):
            vbuf_ref[lc, pl.ds(0, CONV_HALO, stride=2), :] = jnp.zeros((CONV_HALO, LANES), jnp.float32)
        st_ref[...] = jnp.zeros_like(st_ref)

    @pl.when(c_idx != 0)
    def _():
        for lc in range(dc // LANES):
            vbuf_ref[lc, pl.ds(0, CONV_HALO, stride=2), :] = vbuf_ref[lc, pl.ds(tail, CONV_HALO, stride=2), :]

    x = x_ref[0]
    ms = jnp.mean(x * x, axis=-1, keepdims=True)
    h = (x * lax.rsqrt(ms + NORM_EPS) * gmix_ref[...]).astype(jnp.bfloat16)

    pa = jnp.dot(h, win_ref[:, 0:2 * dc], preferred_element_type=jnp.float32)
    vglu = pa[:, 0:dc] * _sigmoid(pa[:, dc:2 * dc])
    for lc in range(dc // LANES):
        ls = slice(lc * LANES, (lc + 1) * LANES)
        vbuf_ref[lc, pl.ds(2 * CONV_HALO, half, stride=2), :] = vglu[0:half, ls]
        vbuf_ref[lc, pl.ds(2 * CONV_HALO + 1, half, stride=2), :] = vglu[half:rows, ls]
        vbuf_ref[lc, pl.ds(1, CONV_HALO, stride=2), :] = vglu[half - CONV_HALO:half, ls]

    base = CONV_HALO - (cw - 1)
    nv = CONV_ROWS // SUBLANES

    g0 = 2 * dc + ds
    sg_conv = _sigmoid(jnp.dot(h, win_ref[:, g0:g0 + d], preferred_element_type=jnp.float32))
    sg_ssm = _sigmoid(jnp.dot(h, win_ref[:, g0 + d:g0 + 2 * d], preferred_element_type=jnp.float32))

    def conv_chunk(rc, carry):
        r0 = rc * CONV_ROWS
        for p in range(2):
            for lc in range(dc // LANES):
                ls = slice(lc * LANES, (lc + 1) * LANES)
                wv = [jnp.broadcast_to(convw_ref[k:k + 1, ls], (SUBLANES, LANES)) for k in range(cw)]
                acc = [jnp.zeros((SUBLANES, LANES), jnp.float32) for _ in range(nv)]
                for s in range(CONV_ROWS - SUBLANES + cw):
                    win = vbuf_ref[lc, pl.ds(2 * (r0 + base + s) + p, SUBLANES, stride=2), :]
                    for i in range(nv):
                        k = s - SUBLANES * i
                        if 0 <= k < cw:
                            acc[i] = acc[i] + wv[k] * win
                for i in range(nv):
                    conv_ref[pl.ds(p * half + r0 + SUBLANES * i, SUBLANES), ls] = acc[i]
        return carry

    for rc in range(half // CONV_ROWS):
        conv_chunk(rc, 0)

    v = conv_ref[...] + convb_ref[...]
    mu = jnp.mean(v, axis=-1, keepdims=True)
    vc = v - mu
    var = jnp.mean(vc * vc, axis=-1, keepdims=True)
    v = vc * lax.rsqrt(var + NORM_EPS) * lng_ref[...] + lnb_ref[...]
    v = v * _sigmoid(v)
    y_conv = jnp.dot(v.astype(jnp.bfloat16), wco_ref[...], preferred_element_type=jnp.float32)

    u = jnp.dot(h, win_ref[:, 2 * dc:2 * dc + ds], preferred_element_type=jnp.float32)
    bu = jnp.dot(u.astype(jnp.bfloat16), bblk_ref[...], preferred_element_type=jnp.float32)
    for n in range(2 * nsl):
        for s in range(SCAN_SEGS):
            scan_ref[n, s * pitch:s * pitch + seg, :] = bu[s * seg:(s + 1) * seg, n * LANES:(n + 1) * LANES]

    def bcast(row, n):
        return jnp.broadcast_to(avec_ref[row:row + 1, n * LANES:(n + 1) * LANES], (SUBLANES, LANES))

    a_r = [bcast(0, n) for n in range(nsl)]
    a_i = [bcast(1, n) for n in range(nsl)]

    def step(j, st, store):
        out = []
        for n in range(nsl):
            sr, si = st[2 * n], st[2 * n + 1]
            br = scan_ref[n, pl.ds(j, SCAN_SEGS, stride=pitch), :]
            bi = scan_ref[nsl + n, pl.ds(j, SCAN_SEGS, stride=pitch), :]
            nr = a_r[n] * sr - a_i[n] * si + br
            ni = a_r[n] * si + a_i[n] * sr + bi
            if store:
                scan_ref[n, pl.ds(j, SCAN_SEGS, stride=pitch), :] = nr
                scan_ref[nsl + n, pl.ds(j, SCAN_SEGS, stride=pitch), :] = ni
            out += [nr, ni]
        return tuple(out)

    zero = jnp.zeros((SUBLANES, LANES), jnp.float32)
    fin = lax.fori_loop(0, seg, functools.partial(step, store=False), (zero,) * (2 * nsl))

    for n in range(nsl):
        fr, fi = fin[2 * n], fin[2 * n + 1]
        ap_r = avec_ref[2:3, n * LANES:(n + 1) * LANES]
        ap_i = avec_ref[3:4, n * LANES:(n + 1) * LANES]
        c_r = st_ref[0:1, n * LANES:(n + 1) * LANES]
        c_i = st_ref[1:2, n * LANES:(n + 1) * LANES]
        for s in range(SCAN_SEGS):
            cin_ref[2 * n, s:s + 1, :] = c_r
            cin_ref[2 * n + 1, s:s + 1, :] = c_i
            n_r = ap_r * c_r - ap_i * c_i + fr[s:s + 1, :]
            n_i = ap_r * c_i + ap_i * c_r + fi[s:s + 1, :]
            c_r, c_i = n_r, n_i
        st_ref[0:1, n * LANES:(n + 1) * LANES] = c_r
        st_ref[1:2, n * LANES:(n + 1) * LANES] = c_i

    lax.fori_loop(0, seg, functools.partial(step, store=True), tuple(cin_ref[q] for q in range(2 * nsl)))

    for n in range(2 * nsl):
        for s in range(SCAN_SEGS):
            xst_ref[s * seg:(s + 1) * seg, n * LANES:(n + 1) * LANES] = (
                scan_ref[n, s * pitch:s * pitch + seg, :].astype(jnp.bfloat16))

    y = (jnp.dot(xst_ref[:, 0:gp], cr_ref[...], preferred_element_type=jnp.float32)
         - jnp.dot(xst_ref[:, gp:2 * gp], ci_ref[...], preferred_element_type=jnp.float32))
    y = _gelu_tanh(y + dskip_ref[...] * u)
    glu = jnp.dot(y.astype(jnp.bfloat16), wglu_ref[...], preferred_element_type=jnp.float32) + bglu_ref[...]
    y = y * _sigmoid(glu)
    y_ssm = jnp.dot(y.astype(jnp.bfloat16), wso_ref[...], preferred_element_type=jnp.float32)

    m = sg_conv * y_conv + sg_ssm * y_ssm
    x1_ref[0] = x + jnp.dot(m.astype(jnp.bfloat16), wout_ref[...], preferred_element_type=jnp.float32)


def _mixer(x, gmix, w_in, conv_w, conv_b, ln_g, ln_b, w_co, bblk, c_r, c_i, avec, d_skip,
           w_glu, b_glu, w_so, w_out):
    b, s, d = x.shape
    cw, dc = conv_w.shape
    ds = d_skip.shape[-1]
    gp = c_r.shape[0]
    rows = MIX_ROWS
    seg = rows // SCAN_SEGS
    pitch = seg + SEG_PAD
    assert s % rows == 0 and cw - 1 <= CONV_HALO and gp % LANES == 0 and dc % LANES == 0
    dims = (d, dc, ds, gp, cw)
    consts = [gmix, w_in, conv_w, conv_b, ln_g, ln_b, w_co, bblk, c_r, c_i, avec, d_skip,
              w_glu, b_glu, w_so, w_out]
    return pl.pallas_call(
        functools.partial(_mixer_kernel, dims),
        grid=(b, s // rows),
        in_specs=[pl.BlockSpec((1, rows, d), lambda i, j: (i, j, 0))] + [_const_spec(c.shape) for c in consts],
        out_specs=pl.BlockSpec((1, rows, d), lambda i, j: (i, j, 0)),
        out_shape=jax.ShapeDtypeStruct((b, s, d), jnp.float32),
        scratch_shapes=[
            pltpu.VMEM((dc // LANES, 2 * CONV_HALO + rows, LANES), jnp.float32),
            pltpu.VMEM((rows, dc), jnp.float32),
            pltpu.VMEM((2 * gp // LANES, SCAN_SEGS * pitch, LANES), jnp.float32),
            pltpu.VMEM((rows, 2 * gp), jnp.bfloat16),
            pltpu.VMEM((2 * gp // LANES, SUBLANES, LANES), jnp.float32),
            pltpu.VMEM((2, gp), jnp.float32),
        ],
        compiler_params=pltpu.CompilerParams(
            dimension_semantics=("arbitrary", "arbitrary"), vmem_limit_bytes=VMEM_LIMIT),
        name="mixer",
    )(x, *consts)


def _route_kernel(x1_ref, g_ref, wr_ref, br_ref, h2_ref, te_ref, gate_ref, rank_ref, cnt_ref, carry_ref):
    rows, d = x1_ref.shape
    ne = wr_ref.shape[1]

    @pl.when(pl.program_id(0) == 0)
    def _():
        carry_ref[...] = jnp.zeros_like(carry_ref)

    x = x1_ref[...]
    ms = jnp.mean(x * x, axis=-1, keepdims=True)
    h2 = x * lax.rsqrt(ms + NORM_EPS) * g_ref[...]
    for j in range(d // LANES):
        h2_ref[pl.ds(j, rows, stride=d // LANES), :] = h2[:, j * LANES:(j + 1) * LANES]

    bf = jnp.bfloat16
    h_hi = h2.astype(bf)
    h_lo = (h2 - h_hi.astype(jnp.float32)).astype(bf)
    w = wr_ref[...]
    w_hi = w.astype(bf)
    w_lo = (w - w_hi.astype(jnp.float32)).astype(bf)
    logits = (jnp.dot(h_hi, w_hi, preferred_element_type=jnp.float32)
              + jnp.dot(h_lo, w_hi, preferred_element_type=jnp.float32)
              + jnp.dot(h_hi, w_lo, preferred_element_type=jnp.float32)) + br_ref[...]

    lane = lax.broadcasted_iota(jnp.int32, (rows, ne), 1).astype(jnp.float32)
    slot = lax.broadcasted_iota(jnp.int32, (rows, TOP_K), 1)
    work = logits
    sels, vals = [], []
    te = jnp.zeros((rows, TOP_K), jnp.float32)
    for k in range(TOP_K):
        mx = jnp.max(work, axis=-1, keepdims=True)
        idx = jnp.min(jnp.where(work == mx, lane, float(ne)), axis=-1, keepdims=True)
        sel = lane == idx
        sels.append(sel)
        vals.append(mx)
        te = jnp.where(slot == k, idx, te)
        work = jnp.where(sel, -jnp.inf, work)
    te_ref[...] = te.astype(jnp.int32)

    exps = [jnp.exp(v - vals[0]) for v in vals]
    tot = exps[0]
    for e in exps[1:]:
        tot = tot + e
    gate = jnp.zeros((rows, TOP_K), jnp.float32)
    for k in range(TOP_K):
        gate = jnp.where(slot == k, exps[k] / tot, gate)
    gate_ref[...] = gate

    cnt = jnp.zeros((rows, ne), jnp.float32)
    for sel in sels:
        cnt = cnt + sel.astype(jnp.float32)
    r_i = lax.broadcasted_iota(jnp.int32, (rows, rows), 0)
    c_i = lax.broadcasted_iota(jnp.int32, (rows, rows), 1)
    tril = (c_i < r_i).astype(bf)
    before = jnp.dot(tril, cnt.astype(bf), preferred_element_type=jnp.float32) + carry_ref[...]
    rank = jnp.zeros((rows, TOP_K), jnp.float32)
    for k in range(TOP_K):
        rk = jnp.sum(jnp.where(sels[k], before, 0.0), axis=-1, keepdims=True)
        rank = jnp.where(slot == k, rk, rank)
    rank_ref[...] = rank.astype(jnp.int32)
    carry_ref[...] = carry_ref[...] + jnp.sum(cnt, axis=0, keepdims=True)
    cnt_ref[...] = carry_ref[...].astype(jnp.int32)


def _route(x1, g_ffn, w_router, b_router):
    t, d = x1.shape
    ne = w_router.shape[1]
    rows = ROUTE_ROWS
    assert t % rows == 0 and d % LANES == 0
    nt = d // LANES
    return pl.pallas_call(
        _route_kernel,
        grid=(t // rows,),
        in_specs=[pl.BlockSpec((rows, d), lambda i: (i, 0)),
                  _const_spec(g_ffn.shape), _const_spec(w_router.shape), _const_spec(b_router.shape)],
        out_specs=[pl.BlockSpec((rows * nt, LANES), lambda i: (i, 0)),
                   pl.BlockSpec((rows, TOP_K), lambda i: (i, 0)),
                   pl.BlockSpec((rows, TOP_K), lambda i: (i, 0)),
                   pl.BlockSpec((rows, TOP_K), lambda i: (i, 0)),
                   _const_spec((1, ne))],
        out_shape=[jax.ShapeDtypeStruct((t * nt, LANES), jnp.float32),
                   jax.ShapeDtypeStruct((t, TOP_K), jnp.int32),
                   jax.ShapeDtypeStruct((t, TOP_K), jnp.float32),
                   jax.ShapeDtypeStruct((t, TOP_K), jnp.int32),
                   jax.ShapeDtypeStruct((1, ne), jnp.int32)],
        scratch_shapes=[pltpu.VMEM((1, ne), jnp.float32)],
        compiler_params=pltpu.CompilerParams(
            dimension_semantics=("arbitrary",), vmem_limit_bytes=VMEM_LIMIT),
        name="route",
    )(x1, g_ffn, w_router, b_router)


def _meta_kernel(cnt_ref, poff_ref, blke_ref, blks_ref):
    ne = cnt_ref.shape[1]
    nb = blke_ref.shape[1]
    f32 = jnp.float32
    cnt = cnt_ref[...].astype(f32)
    nblk = jnp.floor((cnt + (MOE_BLOCK - 1)) * (1.0 / MOE_BLOCK))
    e_r = lax.broadcasted_iota(jnp.int32, (ne, ne), 0)
    e_c = lax.broadcasted_iota(jnp.int32, (ne, ne), 1)
    nb_rows = jnp.broadcast_to(nblk, (ne, ne))
    bend_col = jnp.sum(jnp.where(e_c <= e_r, nb_rows, 0.0), axis=1, keepdims=True)
    nb_col = jnp.sum(jnp.where(e_c == e_r, nb_rows, 0.0), axis=1, keepdims=True)
    boff = jnp.sum(jnp.where(e_r < e_c, jnp.broadcast_to(nb_col, (ne, ne)), 0.0),
                   axis=0, keepdims=True)
    poff_ref[...] = (boff * MOE_BLOCK).astype(jnp.int32)
    total = jnp.sum(nblk, axis=1, keepdims=True)
    blk = lax.broadcasted_iota(jnp.int32, (ne, nb), 1).astype(f32)
    done = jnp.where(jnp.broadcast_to(bend_col, (ne, nb)) <= blk, 1.0, 0.0)
    be = jnp.minimum(jnp.sum(done, axis=0, keepdims=True), ne - 1.0)
    bid = lax.broadcasted_iota(jnp.int32, (1, nb), 1).astype(f32)
    last_e = jnp.sum(jnp.where(bid == total - 1.0, be, 0.0), axis=1, keepdims=True)
    blke_ref[...] = jnp.where(bid < total, be, last_e).astype(jnp.int32)
    blks_ref[...] = jnp.minimum(bid, total - 1.0).astype(jnp.int32)


def _meta(counts, n_blocks):
    ne = counts.shape[1]
    return pl.pallas_call(
        _meta_kernel,
        out_shape=[jax.ShapeDtypeStruct((1, ne), jnp.int32),
                   jax.ShapeDtypeStruct((1, n_blocks), jnp.int32),
                   jax.ShapeDtypeStruct((1, n_blocks), jnp.int32)],
        name="moe_meta",
    )(counts)


def _dest_kernel(poff_ref, te_ref, rank_ref, dest_ref):
    te = te_ref[...]
    dest = rank_ref[...]
    for e in range(poff_ref.shape[1]):
        dest = dest + jnp.where(te == e, poff_ref[0, e], 0)
    dest_ref[...] = dest


def _dest(poff, te, rank):
    return pl.pallas_call(
        _dest_kernel,
        in_specs=[pl.BlockSpec(memory_space=pltpu.SMEM), _const_spec(te.shape), _const_spec(rank.shape)],
        out_specs=_const_spec(te.shape),
        out_shape=jax.ShapeDtypeStruct(te.shape, jnp.int32),
        grid=(1,),
        name="moe_dest",
    )(poff, te, rank)


def _row(ref, r, nt):
    return ref.at[pl.ds(pl.multiple_of(r * nt, nt), nt), :]


def _dispatch_kernel(nt, cnt_ref, poff_ref, dest_ref, h2_ref, xs_ref, zero_ref, sem, zsem):
    rows = h2_ref.shape[0] // nt
    n_assign = rows * TOP_K
    ne = cnt_ref.shape[1]

    @pl.when(pl.program_id(0) == 0)
    def _():
        zero_ref[...] = jnp.zeros_like(zero_ref)

        def per_expert(e, carry):
            c = cnt_ref[0, e]
            p = poff_ref[0, e]
            pad = (MOE_BLOCK - (c % MOE_BLOCK)) % MOE_BLOCK

            def put(r, carry2):
                pltpu.make_async_copy(_row(zero_ref, r, nt), _row(xs_ref, p + c + r, nt), zsem).start()
                return carry2

            lax.fori_loop(0, pad, put, 0)

            @pl.when(pad > 0)
            def _():
                pltpu.make_async_copy(zero_ref.at[pl.ds(0, pad * nt), :],
                                      xs_ref.at[pl.ds(0, pad * nt), :], zsem).wait()
            return p + c + pad

        used = lax.fori_loop(0, ne, per_expert, 0)

        def spare(b, carry):
            r = pl.multiple_of(b * (MOE_BLOCK * nt), MOE_BLOCK * nt)
            cp = pltpu.make_async_copy(zero_ref, xs_ref.at[pl.ds(r, MOE_BLOCK * nt), :], zsem)
            cp.start()
            cp.wait()
            return carry

        lax.fori_loop(used // MOE_BLOCK, xs_ref.shape[0] // (MOE_BLOCK * nt), spare, 0)

    def issue(i, carry):
        for u in range(ISSUE_TOKENS):
            tok = i * ISSUE_TOKENS + u
            for k in range(TOP_K):
                pltpu.make_async_copy(_row(h2_ref, tok, nt), _row(xs_ref, dest_ref[0, 0, tok * TOP_K + k], nt),
                                      sem).start(priority=k % 2)
        return carry

    lax.fori_loop(0, rows // ISSUE_TOKENS, issue, 0)
    for _k in range(TOP_K):
        pltpu.make_async_copy(h2_ref, xs_ref.at[pl.ds(0, rows * nt), :], sem).wait()


def _dispatch(counts, poff, dest, h2, n_rows, nt):
    t = h2.shape[0] // nt
    rows = DISPATCH_ROWS
    assert t % rows == 0
    dest3 = dest.reshape(t // rows, 1, rows * TOP_K)
    smem = pl.BlockSpec(memory_space=pltpu.SMEM)
    return pl.pallas_call(
        functools.partial(_dispatch_kernel, nt),
        grid=(t // rows,),
        in_specs=[smem, smem,
                  pl.BlockSpec((1, 1, rows * TOP_K), lambda i: (i, 0, 0), memory_space=pltpu.SMEM),
                  pl.BlockSpec((rows * nt, LANES), lambda i: (i, 0))],
        out_specs=pl.BlockSpec(memory_space=pl.ANY),
        out_shape=jax.ShapeDtypeStruct((n_rows * nt, LANES), jnp.float32),
        scratch_shapes=[pltpu.VMEM((MOE_BLOCK * nt, LANES), jnp.float32),
                        pltpu.SemaphoreType.DMA(()), pltpu.SemaphoreType.DMA(())],
        compiler_params=pltpu.CompilerParams(
            dimension_semantics=("arbitrary",), vmem_limit_bytes=VMEM_LIMIT, has_side_effects=True),
        name="moe_dispatch",
    )(counts, poff, dest3, h2)


def _expert_kernel(nt, blke_ref, blks_ref, xs_ref, wgu32_ref, bgu_ref, wd32_ref, bd_ref, ys_ref,
                   wgu_ref, wd_ref):
    i = pl.program_id(0)
    f = wd_ref.shape[0]

    @pl.when(jnp.logical_or(i == 0, blke_ref[0, i] != blke_ref[0, jnp.maximum(i - 1, 0)]))
    def _():
        wgu_ref[...] = wgu32_ref[0].astype(jnp.bfloat16)
        wd_ref[...] = wd32_ref[0].astype(jnp.bfloat16)

    @pl.when(blks_ref[0, i] != i)
    def _():
        ys_ref[...] = jnp.zeros_like(ys_ref)

    @pl.when(blks_ref[0, i] == i)
    def _():
        x = jnp.concatenate([xs_ref[pl.ds(j, MOE_BLOCK, stride=nt), :] for j in range(nt)], axis=-1)
        gu = jnp.dot(x.astype(jnp.bfloat16), wgu_ref[...], preferred_element_type=jnp.float32) + bgu_ref[0]
        g = jnp.minimum(gu[:, 0:f], SWIGLU_LIMIT)
        lin = jnp.clip(gu[:, f:2 * f], -SWIGLU_LIMIT, SWIGLU_LIMIT)
        act = g * _sigmoid(SWIGLU_ALPHA * g) * (lin + 1.0)
        y = jnp.dot(act.astype(jnp.bfloat16), wd_ref[...], preferred_element_type=jnp.float32) + bd_ref[0]
        for j in range(nt):
            ys_ref[pl.ds(j, MOE_BLOCK, stride=nt), :] = y[:, j * LANES:(j + 1) * LANES]


def _experts(blk_e, blk_s, xs, w_gate_up, b_gate_up, w_down, b_down, nt):
    ne, d, f2 = w_gate_up.shape
    f = f2 // 2
    n_blocks = blk_e.shape[1]
    blk_rows = MOE_BLOCK * nt
    grid_spec = pltpu.PrefetchScalarGridSpec(
        num_scalar_prefetch=2,
        grid=(n_blocks,),
        in_specs=[pl.BlockSpec((blk_rows, LANES), lambda i, be, bs: (bs[0, i], 0)),
                  pl.BlockSpec((1, d, f2), lambda i, be, bs: (be[0, i], 0, 0)),
                  pl.BlockSpec((1, 1, f2), lambda i, be, bs: (be[0, i], 0, 0)),
                  pl.BlockSpec((1, f, d), lambda i, be, bs: (be[0, i], 0, 0)),
                  pl.BlockSpec((1, 1, d), lambda i, be, bs: (be[0, i], 0, 0))],
        out_specs=pl.BlockSpec((blk_rows, LANES), lambda i, be, bs: (i, 0)),
        scratch_shapes=[pltpu.VMEM((d, f2), jnp.bfloat16), pltpu.VMEM((f, d), jnp.bfloat16)],
    )
    return pl.pallas_call(
        functools.partial(_expert_kernel, nt),
        grid_spec=grid_spec,
        out_shape=jax.ShapeDtypeStruct(xs.shape, jnp.float32),
        compiler_params=pltpu.CompilerParams(
            dimension_semantics=("arbitrary",), vmem_limit_bytes=VMEM_LIMIT),
        name="moe_experts",
    )(blk_e, blk_s, xs, w_gate_up, b_gate_up.reshape(ne, 1, f2), w_down, b_down.reshape(ne, 1, d))


def _combine_kernel(nt, dest_ref, dnext_ref, gate_ref, x1_ref, gfin_ref, ys_ref, out_ref, buf_ref, sem):
    rows, d = x1_ref.shape
    n_assign = rows * TOP_K
    step = pl.program_id(0)
    cur = step % 2

    def gather(idx_ref, half):
        def issue(i, carry):
            for u in range(ISSUE_TOKENS):
                tok = i * ISSUE_TOKENS + u
                for k in range(TOP_K):
                    pltpu.make_async_copy(_row(ys_ref, idx_ref[0, 0, tok * TOP_K + k], nt),
                                          _row(buf_ref, half * n_assign + k * rows + tok, nt),
                                          sem.at[half]).start(priority=k % 2)
            return carry

        lax.fori_loop(0, rows // ISSUE_TOKENS, issue, 0)

    @pl.when(step == 0)
    def _():
        gather(dest_ref, 0)

    @pl.when(step + 1 < pl.num_programs(0))
    def _():
        gather(dnext_ref, 1 - cur)

    cur_rows = pl.multiple_of(cur * (n_assign * nt), n_assign * nt)
    pltpu.make_async_copy(ys_ref.at[pl.ds(0, n_assign * nt), :],
                          buf_ref.at[pl.ds(cur_rows, n_assign * nt), :], sem.at[cur]).wait()

    gate = gate_ref[...]
    x = x1_ref[...]
    for k in range(TOP_K):
        yk = jnp.concatenate(
            [buf_ref[pl.ds(cur_rows + (k * rows * nt + j), rows, stride=nt), :] for j in range(nt)], axis=-1)
        x = x + gate[:, k:k + 1] * yk
    ms = jnp.mean(x * x, axis=-1, keepdims=True)
    out_ref[...] = x * lax.rsqrt(ms + NORM_EPS) * gfin_ref[...]


def _combine(dest, gate, x1, g_final, ys, nt):
    t, d = x1.shape
    rows = COMBINE_ROWS
    assert t % rows == 0
    steps = t // rows
    dest3 = dest.reshape(steps, 1, rows * TOP_K)
    return pl.pallas_call(
        functools.partial(_combine_kernel, nt),
        grid=(steps,),
        in_specs=[pl.BlockSpec((1, 1, rows * TOP_K), lambda i: (i, 0, 0), memory_space=pltpu.SMEM),
                  pl.BlockSpec((1, 1, rows * TOP_K), lambda i: (jnp.minimum(i + 1, steps - 1), 0, 0),
                               memory_space=pltpu.SMEM),
                  pl.BlockSpec((rows, TOP_K), lambda i: (i, 0)),
                  pl.BlockSpec((rows, d), lambda i: (i, 0)),
                  _const_spec(g_final.shape),
                  pl.BlockSpec(memory_space=pl.ANY)],
        out_specs=pl.BlockSpec((rows, d), lambda i: (i, 0)),
        out_shape=jax.ShapeDtypeStruct((t, d), jnp.float32),
        scratch_shapes=[pltpu.VMEM((2 * rows * TOP_K * nt, LANES), jnp.float32), pltpu.SemaphoreType.DMA((2,))],
        compiler_params=pltpu.CompilerParams(
            dimension_semantics=("arbitrary",), vmem_limit_bytes=VMEM_LIMIT),
        name="moe_combine",
    )(dest3, dest3, gate, x1, g_final, ys)


def _block_diag(blocks):
    g, r, c = blocks.shape
    eye = jnp.eye(g, dtype=blocks.dtype)
    return (blocks[:, :, None, :] * eye[:, None, :, None]).reshape(g * r, g * c)


def kernel(x, norm_mix_g, w_in, conv_w, conv_b, conv_ln_g, conv_ln_b, w_conv_out, ssm_a_re, ssm_a_im,
           ssm_log_step, ssm_b_re, ssm_b_im, ssm_c_re, ssm_c_im, ssm_d, w_ssm_glu, b_ssm_glu, w_ssm_out,
           w_out, norm_ffn_g, w_router, b_router, w_gate_up, b_gate_up, w_down, b_down, norm_final_g):
    bsz, seq, d = x.shape
    depth = w_in.shape[0]
    bf = jnp.bfloat16
    nt = d // LANES
    n_tok = bsz * seq
    ne = w_router.shape[-1]
    n_blocks = -(-n_tok * TOP_K // MOE_BLOCK) + ne
    seg_len = MIX_ROWS // SCAN_SEGS

    def row(v):
        return v.reshape(1, -1)

    for l in range(depth):
        ar, ai, arp, aip, bbr, bbi = _ssm_disc(
            ssm_a_re[l], ssm_a_im[l], ssm_log_step[l],
            jnp.swapaxes(ssm_b_re[l], 1, 2), jnp.swapaxes(ssm_b_im[l], 1, 2), seg_len)
        bblk = jnp.concatenate([_block_diag(bbr), _block_diag(bbi)], axis=1).astype(bf)
        c_r = _block_diag(jnp.swapaxes(ssm_c_re[l], 1, 2)).astype(bf)
        c_i = _block_diag(jnp.swapaxes(ssm_c_im[l], 1, 2)).astype(bf)
        avec = jnp.stack([ar.reshape(-1), ai.reshape(-1), arp.reshape(-1), aip.reshape(-1)])

        x = _mixer(x, row(norm_mix_g[l]), w_in[l].astype(bf), conv_w[l], row(conv_b[l]),
                   row(conv_ln_g[l]), row(conv_ln_b[l]), w_conv_out[l].astype(bf), bblk, c_r, c_i, avec,
                   row(ssm_d[l]), w_ssm_glu[l].astype(bf), row(b_ssm_glu[l]), w_ssm_out[l].astype(bf),
                   w_out[l].astype(bf))

        x1 = x.reshape(n_tok, d)
        h2, te, gate, rank, counts = _route(x1, row(norm_ffn_g[l]), w_router[l], row(b_router[l]))
        poff, blk_e, blk_s = _meta(counts, n_blocks)
        dest = _dest(poff, te.reshape(-1, LANES), rank.reshape(-1, LANES))
        xs = _dispatch(counts, poff, dest, h2, n_blocks * MOE_BLOCK, nt)
        ys = _experts(blk_e, blk_s, xs, w_gate_up[l], b_gate_up[l], w_down[l], b_down[l], nt)
        assert depth == 1
        x = _combine(dest, gate, x1, row(norm_final_g), ys, nt).reshape(bsz, seq, d)
    return x
```

```python
import functools
import math

import jax
import jax.numpy as jnp
from jax import lax
from jax.experimental import pallas as pl
from jax.experimental.pallas import tpu as pltpu

NORM_EPS = 1e-6
SWIGLU_LIMIT = 7.0
SWIGLU_ALPHA = 1.702
TOP_K = 4
SSM_GROUP = 16

LANES = 128
SUBLANES = 8

MIX_ROWS = 512
SCAN_SEGS = SUBLANES
SEG_PAD = 8
CONV_HALO = 32
CONV_ROWS = 64
ROUTE_ROWS = 512
MOE_BLOCK = 512
DISPATCH_ROWS = 1024
COMBINE_ROWS = 256
VMEM_LIMIT = 56 * 1024 * 1024


def _sigmoid(x):
    return 1.0 / (1.0 + jnp.exp(-x))


def _gelu_tanh(x):
    c = math.sqrt(2.0 / math.pi)
    return 0.5 * x * (1.0 + jnp.tanh(c * (x + 0.044715 * (x * x * x))))


def _const_spec(shape):
    nd = len(shape)
    return pl.BlockSpec(shape, lambda *_: (0,) * nd)


def _ssm_disc_kernel(seg_len, are_ref, aim_ref, ls_ref, bre_ref, bim_ref,
                     ar_ref, ai_ref, arp_ref, aip_ref, bbr_ref, bbi_ref):
    lr = are_ref[...]
    li = aim_ref[...]
    dt = jnp.exp(ls_ref[...])
    mag = jnp.exp(lr * dt)
    ar = mag * jnp.cos(li * dt)
    ai = mag * jnp.sin(li * dt)
    den = lr * lr + li * li
    fr = ((ar - 1.0) * lr + ai * li) / den
    fi = (ai * lr - (ar - 1.0) * li) / den
    ar_ref[...] = ar
    ai_ref[...] = ai
    magp = jnp.exp(lr * dt * seg_len)
    arp_ref[...] = magp * jnp.cos(li * dt * seg_len)
    aip_ref[...] = magp * jnp.sin(li * dt * seg_len)
    br = bre_ref[...]
    bi = bim_ref[...]
    bbr_ref[...] = fr[:, None, :] * br - fi[:, None, :] * bi
    bbi_ref[...] = fr[:, None, :] * bi + fi[:, None, :] * br


def _ssm_disc(a_re, a_im, log_step, bt_re, bt_im, seg_len):
    g, p = a_re.shape
    h = bt_re.shape[1]
    f32 = jnp.float32
    out_shape = [jax.ShapeDtypeStruct((g, p), f32)] * 4 + [jax.ShapeDtypeStruct((g, h, p), f32)] * 2
    return pl.pallas_call(
        functools.partial(_ssm_disc_kernel, float(seg_len)),
        out_shape=out_shape,
        name="ssm_disc",
    )(a_re, a_im, log_step.reshape(g, 1), bt_re, bt_im)


def _mixer_kernel(dims, x_ref, gmix_ref, win_ref, convw_ref, convb_ref, lng_ref, lnb_ref, wco_ref,
                  bblk_ref, cr_ref, ci_ref, avec_ref, dskip_ref, wglu_ref, bglu_ref, wso_ref, wout_ref,
                  x1_ref, vbuf_ref, conv_ref, scan_ref, xst_ref, cin_ref, st_ref):
    d, dc, ds, gp, cw = dims
    rows = MIX_ROWS
    seg = rows // SCAN_SEGS
    pitch = seg + SEG_PAD
    nsl = gp // LANES
    c_idx = pl.program_id(1)

    half = rows // 2
    tail = 2 * half + 1

    @pl.when(c_idx == 0)
    def _():
        for lc in range(dc // LANES):
            vbuf_ref[lc, pl.ds(0, CONV_HALO, stride=2), :] = jnp.zeros((CONV_HALO, LANES), jnp.float32)
        st_ref[...] = jnp.zeros_like(st_ref)

    @pl.when(c_idx != 0)
    def _():
        for lc in range(dc // LANES):
            vbuf_ref[lc, pl.ds(0, CONV_HALO, stride=2), :] = vbuf_ref[lc, pl.ds(tail, CONV_HALO, stride=2), :]

    x = x_ref[0]
    ms = jnp.mean(x * x, axis=-1, keepdims=True)
    h = (x * lax.rsqrt(ms + NORM_EPS) * gmix_ref[...]).astype(jnp.bfloat16)

    pa = jnp.dot(h, win_ref[:, 0:2 * dc], preferred_element_type=jnp.float32)
    vglu = pa[:, 0:dc] * _sigmoid(pa[:, dc:2 * dc])
    for lc in range(dc // LANES):
        ls = slice(lc * LANES, (lc + 1) * LANES)
        vbuf_ref[lc, pl.ds(2 * CONV_HALO, half, stride=2), :] = vglu[0:half, ls]
        vbuf_ref[lc, pl.ds(2 * CONV_HALO + 1, half, stride=2), :] = vglu[half:rows, ls]
        vbuf_ref[lc, pl.ds(1, CONV_HALO, stride=2), :] = vglu[half - CONV_HALO:half, ls]

    base = CONV_HALO - (cw - 1)
    nv = CONV_ROWS // SUBLANES

    g0 = 2 * dc + ds
    sg_conv = _sigmoid(jnp.dot(h, win_ref[:, g0:g0 + d], preferred_element_type=jnp.float32))
    sg_ssm = _sigmoid(jnp.dot(h, win_ref[:, g0 + d:g0 + 2 * d], preferred_element_type=jnp.float32))

    def conv_chunk(rc, carry):
        r0 = rc * CONV_ROWS
        for p in range(2):
            for lc in range(dc // LANES):
                ls = slice(lc * LANES, (lc + 1) * LANES)
                wv = [jnp.broadcast_to(convw_ref[k:k + 1, ls], (SUBLANES, LANES)) for k in range(cw)]
                acc = [jnp.zeros((SUBLANES, LANES), jnp.float32) for _ in range(nv)]
                for s in range(CONV_ROWS - SUBLANES + cw):
                    win = vbuf_ref[lc, pl.ds(2 * (r0 + base + s) + p, SUBLANES, stride=2), :]
                    for i in range(nv):
                        k = s - SUBLANES * i
                        if 0 <= k < cw:
                            acc[i] = acc[i] + wv[k] * win
                for i in range(nv):
                    conv_ref[pl.ds(p * half + r0 + SUBLANES * i, SUBLANES), ls] = acc[i]
        return carry

    for rc in range(half // CONV_ROWS):
        conv_chunk(rc, 0)

    v = conv_ref[...] + convb_ref[...]
    mu = jnp.mean(v, axis=-1, keepdims=True)
    vc = v - mu
    var = jnp.mean(vc * vc, axis=-1, keepdims=True)
    v = vc * lax.rsqrt(var + NORM_EPS) * lng_ref[...] + lnb_ref[...]
    v = v * _sigmoid(v)
    y_conv = jnp.dot(v.astype(jnp.bfloat16), wco_ref[...], preferred_element_type=jnp.float32)

    u = jnp.dot(h, win_ref[:, 2 * dc:2 * dc + ds], preferred_element_type=jnp.float32)
    bu = jnp.dot(u.astype(jnp.bfloat16), bblk_ref[...], preferred_element_type=jnp.float32)
    for n in range(2 * nsl):
        for s in range(SCAN_SEGS):
            scan_ref[n, s * pitch:s * pitch + seg, :] = bu[s * seg:(s + 1) * seg, n * LANES:(n + 1) * LANES]

    def bcast(row, n):
        return jnp.broadcast_to(avec_ref[row:row + 1, n * LANES:(n + 1) * LANES], (SUBLANES, LANES))

    a_r = [bcast(0, n) for n in range(nsl)]
    a_i = [bcast(1, n) for n in range(nsl)]

    def step(j, st, store):
        out = []
        for n in range(nsl):
            sr, si = st[2 * n], st[2 * n + 1]
            br = scan_ref[n, pl.ds(j, SCAN_SEGS, stride=pitch), :]
            bi = scan_ref[nsl + n, pl.ds(j, SCAN_SEGS, stride=pitch), :]
            nr = a_r[n] * sr - a_i[n] * si + br
            ni = a_r[n] * si + a_i[n] * sr + bi
            if store:
                scan_ref[n, pl.ds(j, SCAN_SEGS, stride=pitch), :] = nr
                scan_ref[nsl + n, pl.ds(j, SCAN_SEGS, stride=pitch), :] = ni
            out += [nr, ni]
        return tuple(out)

    zero = jnp.zeros((SUBLANES, LANES), jnp.float32)
    fin = lax.fori_loop(0, seg, functools.partial(step, store=False), (zero,) * (2 * nsl))

    for n in range(nsl):
        fr, fi = fin[2 * n], fin[2 * n + 1]
        ap_r = avec_ref[2:3, n * LANES:(n + 1) * LANES]
        ap_i = avec_ref[3:4, n * LANES:(n + 1) * LANES]
        c_r = st_ref[0:1, n * LANES:(n + 1) * LANES]
        c_i = st_ref[1:2, n * LANES:(n + 1) * LANES]
        for s in range(SCAN_SEGS):
            cin_ref[2 * n, s:s + 1, :] = c_r
            cin_ref[2 * n + 1, s:s + 1, :] = c_i
            n_r = ap_r * c_r - ap_i * c_i + fr[s:s + 1, :]
            n_i = ap_r * c_i + ap_i * c_r + fi[s:s + 1, :]
            c_r, c_i = n_r, n_i
        st_ref[0:1, n * LANES:(n + 1) * LANES] = c_r
        st_ref[1:2, n * LANES:(n + 1) * LANES] = c_i

    lax.fori_loop(0, seg, functools.partial(step, store=True), tuple(cin_ref[q] for q in range(2 * nsl)))

    for n in range(2 * nsl):
        for s in range(SCAN_SEGS):
            xst_ref[s * seg:(s + 1) * seg, n * LANES:(n + 1) * LANES] = (
                scan_ref[n, s * pitch:s * pitch + seg, :].astype(jnp.bfloat16))

    y = (jnp.dot(xst_ref[:, 0:gp], cr_ref[...], preferred_element_type=jnp.float32)
         - jnp.dot(xst_ref[:, gp:2 * gp], ci_ref[...], preferred_element_type=jnp.float32))
    y = _gelu_tanh(y + dskip_ref[...] * u)
    glu = jnp.dot(y.astype(jnp.bfloat16), wglu_ref[...], preferred_element_type=jnp.float32) + bglu_ref[...]
    y = y * _sigmoid(glu)
    y_ssm = jnp.dot(y.astype(jnp.bfloat16), wso_ref[...], preferred_element_type=jnp.float32)

    m = sg_conv * y_conv + sg_ssm * y_ssm
    x1_ref[0] = x + jnp.dot(m.astype(jnp.bfloat16), wout_ref[...], preferred_element_type=jnp.float32)


def _mixer(x, gmix, w_in, conv_w, conv_b, ln_g, ln_b, w_co, bblk, c_r, c_i, avec, d_skip,
           w_glu, b_glu, w_so, w_out):
    b, s, d = x.shape
    cw, dc = conv_w.shape
    ds = d_skip.shape[-1]
    gp = c_r.shape[0]
    rows = MIX_ROWS
    seg = rows // SCAN_SEGS
    pitch = seg + SEG_PAD
    assert s % rows == 0 and cw - 1 <= CONV_HALO and gp % LANES == 0 and dc % LANES == 0
    dims = (d, dc, ds, gp, cw)
    consts = [gmix, w_in, conv_w, conv_b, ln_g, ln_b, w_co, bblk, c_r, c_i, avec, d_skip,
              w_glu, b_glu, w_so, w_out]
    return pl.pallas_call(
        functools.partial(_mixer_kernel, dims),
        grid=(b, s // rows),
        in_specs=[pl.BlockSpec((1, rows, d), lambda i, j: (i, j, 0))] + [_const_spec(c.shape) for c in consts],
        out_specs=pl.BlockSpec((1, rows, d), lambda i, j: (i, j, 0)),
        out_shape=jax.ShapeDtypeStruct((b, s, d), jnp.float32),
        scratch_shapes=[
            pltpu.VMEM((dc // LANES, 2 * CONV_HALO + rows, LANES), jnp.float32),
            pltpu.VMEM((rows, dc), jnp.float32),
            pltpu.VMEM((2 * gp // LANES, SCAN_SEGS * pitch, LANES), jnp.float32),
            pltpu.VMEM((rows, 2 * gp), jnp.bfloat16),
            pltpu.VMEM((2 * gp // LANES, SUBLANES, LANES), jnp.float32),
            pltpu.VMEM((2, gp), jnp.float32),
        ],
        compiler_params=pltpu.CompilerParams(
            dimension_semantics=("arbitrary", "arbitrary"), vmem_limit_bytes=VMEM_LIMIT),
        name="mixer",
    )(x, *consts)


def _route_kernel(x1_ref, g_ref, wr_ref, br_ref, h2_ref, te_ref, gate_ref, rank_ref, cnt_ref, carry_ref):
    rows, d = x1_ref.shape
    ne = wr_ref.shape[1]

    @pl.when(pl.program_id(0) == 0)
    def _():
        carry_ref[...] = jnp.zeros_like(carry_ref)

    x = x1_ref[...]
    ms = jnp.mean(x * x, axis=-1, keepdims=True)
    h2 = x * lax.rsqrt(ms + NORM_EPS) * g_ref[...]
    for j in range(d // LANES):
        h2_ref[pl.ds(j, rows, stride=d // LANES), :] = h2[:, j * LANES:(j + 1) * LANES]

    bf = jnp.bfloat16
    h_hi = h2.astype(bf)
    h_lo = (h2 - h_hi.astype(jnp.float32)).astype(bf)
    w = wr_ref[...]
    w_hi = w.astype(bf)
    w_lo = (w - w_hi.astype(jnp.float32)).astype(bf)
    logits = (jnp.dot(h_hi, w_hi, preferred_element_type=jnp.float32)
              + jnp.dot(h_lo, w_hi, preferred_element_type=jnp.float32)
              + jnp.dot(h_hi, w_lo, preferred_element_type=jnp.float32)) + br_ref[...]

    lane = lax.broadcasted_iota(jnp.int32, (rows, ne), 1).astype(jnp.float32)
    slot = lax.broadcasted_iota(jnp.int32, (rows, TOP_K), 1)
    work = logits
    sels, vals = [], []
    te = jnp.zeros((rows, TOP_K), jnp.float32)
    for k in range(TOP_K):
        mx = jnp.max(work, axis=-1, keepdims=True)
        idx = jnp.min(jnp.where(work == mx, lane, float(ne)), axis=-1, keepdims=True)
        sel = lane == idx
        sels.append(sel)
        vals.append(mx)
        te = jnp.where(slot == k, idx, te)
        work = jnp.where(sel, -jnp.inf, work)
    te_ref[...] = te.astype(jnp.int32)

    exps = [jnp.exp(v - vals[0]) for v in vals]
    tot = exps[0]
    for e in exps[1:]:
        tot = tot + e
    gate = jnp.zeros((rows, TOP_K), jnp.float32)
    for k in range(TOP_K):
        gate = jnp.where(slot == k, exps[k] / tot, gate)
    gate_ref[...] = gate

    cnt = jnp.zeros((rows, ne), jnp.float32)
    for sel in sels:
        cnt = cnt + sel.astype(jnp.float32)
    r_i = lax.broadcasted_iota(jnp.int32, (rows, rows), 0)
    c_i = lax.broadcasted_iota(jnp.int32, (rows, rows), 1)
    tril = (c_i < r_i).astype(bf)
    before = jnp.dot(tril, cnt.astype(bf), preferred_element_type=jnp.float32) + carry_ref[...]
    rank = jnp.zeros((rows, TOP_K), jnp.float32)
    for k in range(TOP_K):
        rk = jnp.sum(jnp.where(sels[k], before, 0.0), axis=-1, keepdims=True)
        rank = jnp.where(slot == k, rk, rank)
    rank_ref[...] = rank.astype(jnp.int32)
    carry_ref[...] = carry_ref[...] + jnp.sum(cnt, axis=0, keepdims=True)
    cnt_ref[...] = carry_ref[...].astype(jnp.int32)


def _route(x1, g_ffn, w_router, b_router):
    t, d = x1.shape
    ne = w_router.shape[1]
    rows = ROUTE_ROWS
    assert t % rows == 0 and d % LANES == 0
    nt = d // LANES
    return pl.pallas_call(
        _route_kernel,
        grid=(t // rows,),
        in_specs=[pl.BlockSpec((rows, d), lambda i: (i, 0)),
                  _const_spec(g_ffn.shape), _const_spec(w_router.shape), _const_spec(b_router.shape)],
        out_specs=[pl.BlockSpec((rows * nt, LANES), lambda i: (i, 0)),
                   pl.BlockSpec((rows, TOP_K), lambda i: (i, 0)),
                   pl.BlockSpec((rows, TOP_K), lambda i: (i, 0)),
                   pl.BlockSpec((rows, TOP_K), lambda i: (i, 0)),
                   _const_spec((1, ne))],
        out_shape=[jax.ShapeDtypeStruct((t * nt, LANES), jnp.float32),
                   jax.ShapeDtypeStruct((t, TOP_K), jnp.int32),
                   jax.ShapeDtypeStruct((t, TOP_K), jnp.float32),
                   jax.ShapeDtypeStruct((t, TOP_K), jnp.int32),
                   jax.ShapeDtypeStruct((1, ne), jnp.int32)],
        scratch_shapes=[pltpu.VMEM((1, ne), jnp.float32)],
        compiler_params=pltpu.CompilerParams(
            dimension_semantics=("arbitrary",), vmem_limit_bytes=VMEM_LIMIT),
        name="route",
    )(x1, g_ffn, w_router, b_router)


def _meta_kernel(cnt_ref, poff_ref, blke_ref, blks_ref):
    ne = cnt_ref.shape[1]
    nb = blke_ref.shape[1]
    f32 = jnp.float32
    cnt = cnt_ref[...].astype(f32)
    nblk = jnp.floor((cnt + (MOE_BLOCK - 1)) * (1.0 / MOE_BLOCK))
    e_r = lax.broadcasted_iota(jnp.int32, (ne, ne), 0)
    e_c = lax.broadcasted_iota(jnp.int32, (ne, ne), 1)
    nb_rows = jnp.broadcast_to(nblk, (ne, ne))
    bend_col = jnp.sum(jnp.where(e_c <= e_r, nb_rows, 0.0), axis=1, keepdims=True)
    nb_col = jnp.sum(jnp.where(e_c == e_r, nb_rows, 0.0), axis=1, keepdims=True)
    boff = jnp.sum(jnp.where(e_r < e_c, jnp.broadcast_to(nb_col, (ne, ne)), 0.0),
                   axis=0, keepdims=True)
    poff_ref[...] = (boff * MOE_BLOCK).astype(jnp.int32)
    total = jnp.sum(nblk, axis=1, keepdims=True)
    blk = lax.broadcasted_iota(jnp.int32, (ne, nb), 1).astype(f32)
    done = jnp.where(jnp.broadcast_to(bend_col, (ne, nb)) <= blk, 1.0, 0.0)
    be = jnp.minimum(jnp.sum(done, axis=0, keepdims=True), ne - 1.0)
    bid = lax.broadcasted_iota(jnp.int32, (1, nb), 1).astype(f32)
    last_e = jnp.sum(jnp.where(bid == total - 1.0, be, 0.0), axis=1, keepdims=True)
    blke_ref[...] = jnp.where(bid < total, be, last_e).astype(jnp.int32)
    blks_ref[...] = jnp.minimum(bid, total - 1.0).astype(jnp.int32)


def _meta(counts, n_blocks):
    ne = counts.shape[1]
    return pl.pallas_call(
        _meta_kernel,
        out_shape=[jax.ShapeDtypeStruct((1, ne), jnp.int32),
                   jax.ShapeDtypeStruct((1, n_blocks), jnp.int32),
                   jax.ShapeDtypeStruct((1, n_blocks), jnp.int32)],
        name="moe_meta",
    )(counts)


def _dest_kernel(poff_ref, te_ref, rank_ref, dest_ref):
    te = te_ref[...]
    dest = rank_ref[...]
    for e in range(poff_ref.shape[1]):
        dest = dest + jnp.where(te == e, poff_ref[0, e], 0)
    dest_ref[...] = dest


def _dest(poff, te, rank):
    return pl.pallas_call(
        _dest_kernel,
        in_specs=[pl.BlockSpec(memory_space=pltpu.SMEM), _const_spec(te.shape), _const_spec(rank.shape)],
        out_specs=_const_spec(te.shape),
        out_shape=jax.ShapeDtypeStruct(te.shape, jnp.int32),
        grid=(1,),
        name="moe_dest",
    )(poff, te, rank)


def _inverse_kernel(cnt_ref, poff_ref, dest_ref, inv_ref):
    step = pl.program_id(0)
    n = dest_ref.shape[2]
    unroll = 16

    @pl.when(step == 0)
    def _():
        def put(r, carry):
            inv_ref[0, r] = -1
            return carry

        def per_expert(e, carry):
            c = cnt_ref[0, e]
            lo = poff_ref[0, e] + c
            hi = lo + (MOE_BLOCK - (c % MOE_BLOCK)) % MOE_BLOCK
            lax.fori_loop(lo, hi, put, 0)
            return hi

        used = lax.fori_loop(0, cnt_ref.shape[1], per_expert, 0)
        lax.fori_loop(used, inv_ref.shape[1], put, 0)

    base = step * n

    def body(i, carry):
        j0 = pl.multiple_of(i * unroll, unroll)
        chunk = dest_ref.at[0, 0, pl.ds(j0, unroll)]
        first = base + j0
        for q in range(unroll):
            inv_ref[0, chunk[q]] = first + q
        return carry

    lax.fori_loop(0, n // unroll, body, 0)


def _inverse(counts, poff, dest, n_rows):
    n_assign = dest.size
    chunk = DISPATCH_ROWS * TOP_K
    assert n_assign % chunk == 0
    smem = pl.BlockSpec(memory_space=pltpu.SMEM)
    return pl.pallas_call(
        _inverse_kernel,
        grid=(n_assign // chunk,),
        in_specs=[smem, smem, pl.BlockSpec((1, 1, chunk), lambda i: (i, 0, 0), memory_space=pltpu.SMEM)],
        out_specs=smem,
        out_shape=jax.ShapeDtypeStruct((1, n_rows), jnp.int32),
        compiler_params=pltpu.CompilerParams(dimension_semantics=("arbitrary",)),
        name="moe_inverse",
    )(counts, poff, dest.reshape(n_assign // chunk, 1, chunk))


def _rows_kernel(n_tok, nt, inv_ref, src_ref, dst_ref):
    a = inv_ref[...]
    blk = lax.broadcasted_iota(jnp.int32, a.shape, 0)
    r = lax.broadcasted_iota(jnp.int32, a.shape, 1)
    real = a >= 0
    tok = jnp.right_shift(jnp.maximum(a, 0), 2)
    slot = jnp.bitwise_and(jnp.maximum(a, 0), TOP_K - 1)
    src_ref[...] = tok * nt
    spare = TOP_K * n_tok + jnp.bitwise_and(blk, 1) * MOE_BLOCK + r
    dst_ref[...] = jnp.where(real, slot * n_tok + tok, spare) * nt


def _rows(inv, n_tok, nt):
    n_blocks = inv.shape[1] // MOE_BLOCK
    inv2 = inv.reshape(n_blocks, MOE_BLOCK)
    shp = jax.ShapeDtypeStruct(inv2.shape, jnp.int32)
    assert TOP_K == 4
    return pl.pallas_call(
        functools.partial(_rows_kernel, n_tok, nt),
        out_shape=[shp, shp],
        name="moe_rows",
    )(inv2)


def _fused_expert_kernel(nt, n_tok, blke_ref, blks_ref, src0_ref, srcn_ref, dstp_ref, h2_ref,
                         wgu32_ref, bgu_ref, wd32_ref, bd_ref, ys_ref,
                         wgu_ref, wd_ref, bgus_ref, bds_ref, xbuf_ref, ybuf_ref, gsem, ssem):
    i = pl.program_id(0)
    f = wd_ref.shape[0]
    blk = MOE_BLOCK
    p = i % 2
    q = 1 - p
    valid = blks_ref[0, i] == i
    tile_rows = blk * nt

    def gather(idx_ref, half):
        for r in range(blk):
            pltpu.make_async_copy(
                h2_ref.at[pl.ds(pl.multiple_of(idx_ref[0, 0, r], nt), nt), :],
                xbuf_ref.at[half, pl.ds(r * nt, nt), :], gsem.at[half]).start(priority=r % 2)

    def scatter(idx_ref, half):
        for r in range(blk):
            pltpu.make_async_copy(
                ybuf_ref.at[half, pl.ds(r * nt, nt), :],
                ys_ref.at[pl.ds(pl.multiple_of(idx_ref[0, 0, r], nt), nt), :], ssem.at[half]).start(priority=r % 2)

    def wait_gather(half):
        pltpu.make_async_copy(h2_ref.at[pl.ds(0, tile_rows), :], xbuf_ref.at[half], gsem.at[half]).wait()

    def wait_scatter(half):
        pltpu.make_async_copy(ybuf_ref.at[half], ys_ref.at[pl.ds(0, tile_rows), :], ssem.at[half]).wait()

    @pl.when(i == 0)
    def _():
        ybuf_ref[...] = jnp.zeros_like(ybuf_ref)
        spare0 = TOP_K * n_tok * nt
        for half in range(2):
            cp = pltpu.make_async_copy(ybuf_ref.at[half],
                                       ys_ref.at[pl.ds(spare0 + half * tile_rows, tile_rows), :], ssem.at[half])
            cp.start()
            cp.wait()
        gather(src0_ref, 0)

    @pl.when(jnp.logical_and(valid, jnp.logical_or(i == 0, blke_ref[0, i] != blke_ref[0, jnp.maximum(i - 1, 0)])))
    def _():
        wgu_ref[...] = wgu32_ref[0].astype(jnp.bfloat16)
        wd_ref[...] = wd32_ref[0].astype(jnp.bfloat16)
        bgus_ref[...] = bgu_ref[0]
        bds_ref[...] = bd_ref[0]

    @pl.when(valid)
    def _():
        wait_gather(p)
        x = jnp.concatenate([xbuf_ref[p, pl.ds(j, blk, stride=nt), :] for j in range(nt)],
                            axis=-1).astype(jnp.bfloat16)
        scatter(dstp_ref, q)
        gather(srcn_ref, q)
        gu = jnp.dot(x, wgu_ref[...], preferred_element_type=jnp.float32) + bgus_ref[...]
        g = jnp.minimum(gu[:, 0:f], SWIGLU_LIMIT)
        lin = jnp.clip(gu[:, f:2 * f], -SWIGLU_LIMIT, SWIGLU_LIMIT)
        act = g * _sigmoid(SWIGLU_ALPHA * g) * (lin + 1.0)
        y = jnp.dot(act.astype(jnp.bfloat16), wd_ref[...], preferred_element_type=jnp.float32) + bds_ref[...]

        @pl.when(i >= 1)
        def _():
            wait_scatter(p)

        for j in range(nt):
            ybuf_ref[p, pl.ds(j, blk, stride=nt), :] = y[:, j * LANES:(j + 1) * LANES]

    @pl.when(jnp.logical_and(jnp.logical_not(valid), blks_ref[0, jnp.maximum(i - 1, 0)] == i - 1))
    def _():
        scatter(dstp_ref, q)
        wait_scatter(q)
        wait_scatter(p)
        wait_gather(p)


def _fused_experts(blk_e, blk_s, src, dst, h2, w_gate_up, b_gate_up, w_down, b_down, n_tok, nt):
    ne, d, f2 = w_gate_up.shape
    f = f2 // 2
    n_blocks = src.shape[0]
    steps = n_blocks + 1
    assert blk_e.shape[1] == steps
    blk = MOE_BLOCK
    first = ((TOP_K * n_tok + blk + jnp.arange(blk, dtype=jnp.int32)) * nt).reshape(1, blk)
    src3 = src.reshape(n_blocks, 1, blk)
    dst3 = jnp.concatenate([dst, first], axis=0).reshape(n_blocks + 1, 1, blk)
    smem = pltpu.SMEM

    def w_map(i, be, bs):
        return (be[0, i], 0, 0)

    grid_spec = pltpu.PrefetchScalarGridSpec(
        num_scalar_prefetch=2,
        grid=(steps,),
        in_specs=[pl.BlockSpec((1, 1, blk), lambda i, be, bs: (0, 0, 0), memory_space=smem),
                  pl.BlockSpec((1, 1, blk), lambda i, be, bs: (bs[0, jnp.minimum(i + 1, steps - 1)], 0, 0),
                               memory_space=smem),
                  pl.BlockSpec((1, 1, blk),
                               lambda i, be, bs: (jnp.where(i == 0, n_blocks, bs[0, jnp.maximum(i - 1, 0)]), 0, 0),
                               memory_space=smem),
                  pl.BlockSpec(memory_space=pl.ANY),
                  pl.BlockSpec((1, d, f2), w_map), pl.BlockSpec((1, 1, f2), w_map),
                  pl.BlockSpec((1, f, d), w_map), pl.BlockSpec((1, 1, d), w_map)],
        out_specs=pl.BlockSpec(memory_space=pl.ANY),
        scratch_shapes=[pltpu.VMEM((d, f2), jnp.bfloat16), pltpu.VMEM((f, d), jnp.bfloat16),
                        pltpu.VMEM((1, f2), jnp.float32), pltpu.VMEM((1, d), jnp.float32),
                        pltpu.VMEM((2, blk * nt, LANES), jnp.float32), pltpu.VMEM((2, blk * nt, LANES), jnp.float32),
                        pltpu.SemaphoreType.DMA((2,)), pltpu.SemaphoreType.DMA((2,))],
    )
    out_rows = (TOP_K * n_tok + 2 * blk) * nt
    return pl.pallas_call(
        functools.partial(_fused_expert_kernel, nt, n_tok),
        grid_spec=grid_spec,
        out_shape=jax.ShapeDtypeStruct((out_rows, LANES), jnp.float32),
        compiler_params=pltpu.CompilerParams(
            dimension_semantics=("arbitrary",), vmem_limit_bytes=VMEM_LIMIT, has_side_effects=True),
        name="moe_experts",
    )(blk_e, blk_s, src3, src3, dst3, h2, w_gate_up, b_gate_up.reshape(ne, 1, f2), w_down,
      b_down.reshape(ne, 1, d))


def _final_kernel(nt, gate_ref, x1_ref, gfin_ref, y0_ref, y1_ref, y2_ref, y3_ref, out_ref):
    rows, d = x1_ref.shape
    gate = gate_ref[...]
    x = x1_ref[...]
    for k, y_ref in enumerate((y0_ref, y1_ref, y2_ref, y3_ref)):
        yk = jnp.concatenate([y_ref[pl.ds(j, rows, stride=nt), :] for j in range(nt)], axis=-1)
        x = x + gate[:, k:k + 1] * yk
    ms = jnp.mean(x * x, axis=-1, keepdims=True)
    out_ref[...] = x * lax.rsqrt(ms + NORM_EPS) * gfin_ref[...]


def _final(gate, x1, g_final, ys, nt):
    t, d = x1.shape
    rows = COMBINE_ROWS
    assert t % rows == 0 and TOP_K == 4
    per_slot = t // rows

    def slot_spec(k):
        return pl.BlockSpec((rows * nt, LANES), lambda i: (k * per_slot + i, 0))

    return pl.pallas_call(
        functools.partial(_final_kernel, nt),
        grid=(per_slot,),
        in_specs=[pl.BlockSpec((rows, TOP_K), lambda i: (i, 0)),
                  pl.BlockSpec((rows, d), lambda i: (i, 0)),
                  _const_spec(g_final.shape)] + [slot_spec(k) for k in range(TOP_K)],
        out_specs=pl.BlockSpec((rows, d), lambda i: (i, 0)),
        out_shape=jax.ShapeDtypeStruct((t, d), jnp.float32),
        compiler_params=pltpu.CompilerParams(
            dimension_semantics=("arbitrary",), vmem_limit_bytes=VMEM_LIMIT),
        name="moe_final",
    )(gate, x1, g_final, ys, ys, ys, ys)


def _block_diag(blocks):
    g, r, c = blocks.shape
    eye = jnp.eye(g, dtype=blocks.dtype)
    return (blocks[:, :, None, :] * eye[:, None, :, None]).reshape(g * r, g * c)


def kernel(x, norm_mix_g, w_in, conv_w, conv_b, conv_ln_g, conv_ln_b, w_conv_out, ssm_a_re, ssm_a_im,
           ssm_log_step, ssm_b_re, ssm_b_im, ssm_c_re, ssm_c_im, ssm_d, w_ssm_glu, b_ssm_glu, w_ssm_out,
           w_out, norm_ffn_g, w_router, b_router, w_gate_up, b_gate_up, w_down, b_down, norm_final_g):
    bsz, seq, d = x.shape
    depth = w_in.shape[0]
    bf = jnp.bfloat16
    nt = d // LANES
    n_tok = bsz * seq
    ne = w_router.shape[-1]
    n_blocks = -(-n_tok * TOP_K // MOE_BLOCK) + ne
    seg_len = MIX_ROWS // SCAN_SEGS

    def row(v):
        return v.reshape(1, -1)

    for l in range(depth):
        ar, ai, arp, aip, bbr, bbi = _ssm_disc(
            ssm_a_re[l], ssm_a_im[l], ssm_log_step[l],
            jnp.swapaxes(ssm_b_re[l], 1, 2), jnp.swapaxes(ssm_b_im[l], 1, 2), seg_len)
        bblk = jnp.concatenate([_block_diag(bbr), _block_diag(bbi)], axis=1).astype(bf)
        c_r = _block_diag(jnp.swapaxes(ssm_c_re[l], 1, 2)).astype(bf)
        c_i = _block_diag(jnp.swapaxes(ssm_c_im[l], 1, 2)).astype(bf)
        avec = jnp.stack([ar.reshape(-1), ai.reshape(-1), arp.reshape(-1), aip.reshape(-1)])

        x = _mixer(x, row(norm_mix_g[l]), w_in[l].astype(bf), conv_w[l], row(conv_b[l]),
                   row(conv_ln_g[l]), row(conv_ln_b[l]), w_conv_out[l].astype(bf), bblk, c_r, c_i, avec,
                   row(ssm_d[l]), w_ssm_glu[l].astype(bf), row(b_ssm_glu[l]), w_ssm_out[l].astype(bf),
                   w_out[l].astype(bf))

        x1 = x.reshape(n_tok, d)
        h2, te, gate, rank, counts = _route(x1, row(norm_ffn_g[l]), w_router[l], row(b_router[l]))
        poff, blk_e, blk_s = _meta(counts, n_blocks + 1)
        dest = _dest(poff, te.reshape(-1, LANES), rank.reshape(-1, LANES))
        inv = _inverse(counts, poff, dest, n_blocks * MOE_BLOCK)
        src, dst = _rows(inv, n_tok, nt)
        ys = _fused_experts(blk_e, blk_s, src, dst, h2, w_gate_up[l], b_gate_up[l], w_down[l], b_down[l],
                            n_tok, nt)
        assert depth == 1
        x = _final(gate, x1, row(norm_final_g), ys, nt).reshape(bsz, seq, d)
    return x
```

```python
import functools
import math

import jax
import jax.numpy as jnp
from jax import lax
from jax.experimental import pallas as pl
from jax.experimental.pallas import tpu as pltpu

NORM_EPS = 1e-6
SWIGLU_LIMIT = 7.0
SWIGLU_ALPHA = 1.702
TOP_K = 4
SSM_GROUP = 16

LANES = 128
SUBLANES = 8

MIX_ROWS = 512
SCAN_SEGS = SUBLANES
SEG_PAD = 8
CONV_HALO = 32
CONV_ROWS = 64
ROUTE_ROWS = 512
MOE_BLOCK = 512
DISPATCH_ROWS = 1024
COMBINE_ROWS = 256
VMEM_LIMIT = 56 * 1024 * 1024


def _sigmoid(x):
    return 1.0 / (1.0 + jnp.exp(-x))


def _gelu_tanh(x):
    c = math.sqrt(2.0 / math.pi)
    return 0.5 * x * (1.0 + jnp.tanh(c * (x + 0.044715 * (x * x * x))))


def _const_spec(shape):
    nd = len(shape)
    return pl.BlockSpec(shape, lambda *_: (0,) * nd)


def _ssm_disc_kernel(seg_len, are_ref, aim_ref, ls_ref, bre_ref, bim_ref,
                     ar_ref, ai_ref, arp_ref, aip_ref, bbr_ref, bbi_ref):
    lr = are_ref[...]
    li = aim_ref[...]
    dt = jnp.exp(ls_ref[...])
    mag = jnp.exp(lr * dt)
    ar = mag * jnp.cos(li * dt)
    ai = mag * jnp.sin(li * dt)
    den = lr * lr + li * li
    fr = ((ar - 1.0) * lr + ai * li) / den
    fi = (ai * lr - (ar - 1.0) * li) / den
    ar_ref[...] = ar
    ai_ref[...] = ai
    magp = jnp.exp(lr * dt * seg_len)
    arp_ref[...] = magp * jnp.cos(li * dt * seg_len)
    aip_ref[...] = magp * jnp.sin(li * dt * seg_len)
    br = bre_ref[...]
    bi = bim_ref[...]
    bbr_ref[...] = fr[:, None, :] * br - fi[:, None, :] * bi
    bbi_ref[...] = fr[:, None, :] * bi + fi[:, None, :] * br


def _ssm_disc(a_re, a_im, log_step, bt_re, bt_im, seg_len):
    g, p = a_re.shape
    h = bt_re.shape[1]
    f32 = jnp.float32
    out_shape = [jax.ShapeDtypeStruct((g, p), f32)] * 4 + [jax.ShapeDtypeStruct((g, h, p), f32)] * 2
    return pl.pallas_call(
        functools.partial(_ssm_disc_kernel, float(seg_len)),
        out_shape=out_shape,
        name="ssm_disc",
    )(a_re, a_im, log_step.reshape(g, 1), bt_re, bt_im)


def _mixer_kernel(dims, x_ref, gmix_ref, win_ref, convw_ref, convb_ref, lng_ref, lnb_ref, wco_ref,
                  bblk_ref, cr_ref, ci_ref, avec_ref, dskip_ref, wglu_ref, bglu_ref, wso_ref, wout_ref,
                  x1_ref, vbuf_ref, conv_ref, scan_ref, xst_ref, cin_ref, st_ref):
    d, dc, ds, gp, cw = dims
    rows = MIX_ROWS
    seg = rows // SCAN_SEGS
    pitch = seg + SEG_PAD
    nsl = gp // LANES
    c_idx = pl.program_id(1)

    half = rows // 2
    tail = 2 * half + 1

    @pl.when(c_idx == 0)
    def _():
        for lc in range(dc // LANES):
            vbuf_ref[lc, pl.ds(0, CONV_HALO, stride=2), :] = jnp.zeros((CONV_HALO, LANES), jnp.float32)
        st_ref[...] = jnp.zeros_like(st_ref)

    @pl.when(c_idx != 0)
    def _():
        for lc in range(dc // LANES):
            vbuf_ref[lc, pl.ds(0, CONV_HALO, stride=2), :] = vbuf_ref[lc, pl.ds(tail, CONV_HALO, stride=2), :]

    x = x_ref[0]
    ms = jnp.mean(x * x, axis=-1, keepdims=True)
    h = (x * lax.rsqrt(ms + NORM_EPS) * gmix_ref[...]).astype(jnp.bfloat16)

    pa = jnp.dot(h, win_ref[:, 0:2 * dc], preferred_element_type=jnp.float32)
    vglu = pa[:, 0:dc] * _sigmoid(pa[:, dc:2 * dc])
    for lc in range(dc // LANES):
        ls = slice(lc * LANES, (lc + 1) * LANES)
        vbuf_ref[lc, pl.ds(2 * CONV_HALO, half, stride=2), :] = vglu[0:half, ls]
        vbuf_ref[lc, pl.ds(2 * CONV_HALO + 1, half, stride=2), :] = vglu[half:rows, ls]
        vbuf_ref[lc, pl.ds(1, CONV_HALO, stride=2), :] = vglu[half - CONV_HALO:half, ls]

    base = CONV_HALO - (cw - 1)
    nv = CONV_ROWS // SUBLANES

    g0 = 2 * dc + ds
    sg_conv = _sigmoid(jnp.dot(h, win_ref[:, g0:g0 + d], preferred_element_type=jnp.float32))
    sg_ssm = _sigmoid(jnp.dot(h, win_ref[:, g0 + d:g0 + 2 * d], preferred_element_type=jnp.float32))

    def conv_chunk(rc, carry):
        r0 = rc * CONV_ROWS
        for p in range(2):
            for lc in range(dc // LANES):
                ls = slice(lc * LANES, (lc + 1) * LANES)
                wv = [jnp.broadcast_to(convw_ref[k:k + 1, ls], (SUBLANES, LANES)) for k in range(cw)]
                acc = [jnp.zeros((SUBLANES, LANES), jnp.float32) for _ in range(nv)]
                for s in range(CONV_ROWS - SUBLANES + cw):
                    win = vbuf_ref[lc, pl.ds(2 * (r0 + base + s) + p, SUBLANES, stride=2), :]
                    for i in range(nv):
                        k = s - SUBLANES * i
                        if 0 <= k < cw:
                            acc[i] = acc[i] + wv[k] * win
                for i in range(nv):
                    conv_ref[pl.ds(p * half + r0 + SUBLANES * i, SUBLANES), ls] = acc[i]
        return carry

    for rc in range(half // CONV_ROWS):
        conv_chunk(rc, 0)

    v = conv_ref[...] + convb_ref[...]
    mu = jnp.mean(v, axis=-1, keepdims=True)
    vc = v - mu
    var = jnp.mean(vc * vc, axis=-1, keepdims=True)
    v = vc * lax.rsqrt(var + NORM_EPS) * lng_ref[...] + lnb_ref[...]
    v = v * _sigmoid(v)
    y_conv = jnp.dot(v.astype(jnp.bfloat16), wco_ref[...], preferred_element_type=jnp.float32)

    u = jnp.dot(h, win_ref[:, 2 * dc:2 * dc + ds], preferred_element_type=jnp.float32)
    bu = jnp.dot(u.astype(jnp.bfloat16), bblk_ref[...], preferred_element_type=jnp.float32)
    for n in range(2 * nsl):
        for s in range(SCAN_SEGS):
            scan_ref[n, s * pitch:s * pitch + seg, :] = bu[s * seg:(s + 1) * seg, n * LANES:(n + 1) * LANES]

    def bcast(row, n):
        return jnp.broadcast_to(avec_ref[row:row + 1, n * LANES:(n + 1) * LANES], (SUBLANES, LANES))

    a_r = [bcast(0, n) for n in range(nsl)]
    a_i = [bcast(1, n) for n in range(nsl)]

    def step(j, st, store):
        out = []
        for n in range(nsl):
            sr, si = st[2 * n], st[2 * n + 1]
            br = scan_ref[n, pl.ds(j, SCAN_SEGS, stride=pitch), :]
            bi = scan_ref[nsl + n, pl.ds(j, SCAN_SEGS, stride=pitch), :]
            nr = a_r[n] * sr - a_i[n] * si + br
            ni = a_r[n] * si + a_i[n] * sr + bi
            if store:
                scan_ref[n, pl.ds(j, SCAN_SEGS, stride=pitch), :] = nr
                scan_ref[nsl + n, pl.ds(j, SCAN_SEGS, stride=pitch), :] = ni
            out += [nr, ni]
        return tuple(out)

    zero = jnp.zeros((SUBLANES, LANES), jnp.float32)
    fin = lax.fori_loop(0, seg, functools.partial(step, store=False), (zero,) * (2 * nsl))

    for n in range(nsl):
        fr, fi = fin[2 * n], fin[2 * n + 1]
        ap_r = avec_ref[2:3, n * LANES:(n + 1) * LANES]
        ap_i = avec_ref[3:4, n * LANES:(n + 1) * LANES]
        c_r = st_ref[0:1, n * LANES:(n + 1) * LANES]
        c_i = st_ref[1:2, n * LANES:(n + 1) * LANES]
        for s in range(SCAN_SEGS):
            cin_ref[2 * n, s:s + 1, :] = c_r
            cin_ref[2 * n + 1, s:s + 1, :] = c_i
            n_r = ap_r * c_r - ap_i * c_i + fr[s:s + 1, :]
            n_i = ap_r * c_i + ap_i * c_r + fi[s:s + 1, :]
            c_r, c_i = n_r, n_i
        st_ref[0:1, n * LANES:(n + 1) * LANES] = c_r
        st_ref[1:2, n * LANES:(n + 1) * LANES] = c_i

    lax.fori_loop(0, seg, functools.partial(step, store=True), tuple(cin_ref[q] for q in range(2 * nsl)))

    for n in range(2 * nsl):
        for s in range(SCAN_SEGS):
            xst_ref[s * seg:(s + 1) * seg, n * LANES:(n + 1) * LANES] = (
                scan_ref[n, s * pitch:s * pitch + seg, :].astype(jnp.bfloat16))

    y = (jnp.dot(xst_ref[:, 0:gp], cr_ref[...], preferred_element_type=jnp.float32)
         - jnp.dot(xst_ref[:, gp:2 * gp], ci_ref[...], preferred_element_type=jnp.float32))
    y = _gelu_tanh(y + dskip_ref[...] * u)
    glu = jnp.dot(y.astype(jnp.bfloat16), wglu_ref[...], preferred_element_type=jnp.float32) + bglu_ref[...]
    y = y * _sigmoid(glu)
    y_ssm = jnp.dot(y.astype(jnp.bfloat16), wso_ref[...], preferred_element_type=jnp.float32)

    m = sg_conv * y_conv + sg_ssm * y_ssm
    x1_ref[0] = x + jnp.dot(m.astype(jnp.bfloat16), wout_ref[...], preferred_element_type=jnp.float32)


def _mixer(x, gmix, w_in, conv_w, conv_b, ln_g, ln_b, w_co, bblk, c_r, c_i, avec, d_skip,
           w_glu, b_glu, w_so, w_out):
    b, s, d = x.shape
    cw, dc = conv_w.shape
    ds = d_skip.shape[-1]
    gp = c_r.shape[0]
    rows = MIX_ROWS
    seg = rows // SCAN_SEGS
    pitch = seg + SEG_PAD
    assert s % rows == 0 and cw - 1 <= CONV_HALO and gp % LANES == 0 and dc % LANES == 0
    dims = (d, dc, ds, gp, cw)
    consts = [gmix, w_in, conv_w, conv_b, ln_g, ln_b, w_co, bblk, c_r, c_i, avec, d_skip,
              w_glu, b_glu, w_so, w_out]
    return pl.pallas_call(
        functools.partial(_mixer_kernel, dims),
        grid=(b, s // rows),
        in_specs=[pl.BlockSpec((1, rows, d), lambda i, j: (i, j, 0))] + [_const_spec(c.shape) for c in consts],
        out_specs=pl.BlockSpec((1, rows, d), lambda i, j: (i, j, 0)),
        out_shape=jax.ShapeDtypeStruct((b, s, d), jnp.float32),
        scratch_shapes=[
            pltpu.VMEM((dc // LANES, 2 * CONV_HALO + rows, LANES), jnp.float32),
            pltpu.VMEM((rows, dc), jnp.float32),
            pltpu.VMEM((2 * gp // LANES, SCAN_SEGS * pitch, LANES), jnp.float32),
            pltpu.VMEM((rows, 2 * gp), jnp.bfloat16),
            pltpu.VMEM((2 * gp // LANES, SUBLANES, LANES), jnp.float32),
            pltpu.VMEM((2, gp), jnp.float32),
        ],
        compiler_params=pltpu.CompilerParams(
            dimension_semantics=("arbitrary", "arbitrary"), vmem_limit_bytes=VMEM_LIMIT),
        name="mixer",
    )(x, *consts)


def _route_kernel(x1_ref, g_ref, wr_ref, br_ref, h2_ref, te_ref, gate_ref, rank_ref, cnt_ref, carry_ref):
    rows, d = x1_ref.shape
    ne = wr_ref.shape[1]

    @pl.when(pl.program_id(0) == 0)
    def _():
        carry_ref[...] = jnp.zeros_like(carry_ref)

    x = x1_ref[...]
    ms = jnp.mean(x * x, axis=-1, keepdims=True)
    h2 = x * lax.rsqrt(ms + NORM_EPS) * g_ref[...]
    for j in range(d // LANES):
        h2_ref[pl.ds(j, rows, stride=d // LANES), :] = h2[:, j * LANES:(j + 1) * LANES]

    bf = jnp.bfloat16
    h_hi = h2.astype(bf)
    h_lo = (h2 - h_hi.astype(jnp.float32)).astype(bf)
    w = wr_ref[...]
    w_hi = w.astype(bf)
    w_lo = (w - w_hi.astype(jnp.float32)).astype(bf)
    logits = (jnp.dot(h_hi, w_hi, preferred_element_type=jnp.float32)
              + jnp.dot(h_lo, w_hi, preferred_element_type=jnp.float32)
              + jnp.dot(h_hi, w_lo, preferred_element_type=jnp.float32)) + br_ref[...]

    lane = lax.broadcasted_iota(jnp.int32, (rows, ne), 1).astype(jnp.float32)
    slot = lax.broadcasted_iota(jnp.int32, (rows, TOP_K), 1)
    work = logits
    sels, vals = [], []
    te = jnp.zeros((rows, TOP_K), jnp.float32)
    for k in range(TOP_K):
        mx = jnp.max(work, axis=-1, keepdims=True)
        idx = jnp.min(jnp.where(work == mx, lane, float(ne)), axis=-1, keepdims=True)
        sel = lane == idx
        sels.append(sel)
        vals.append(mx)
        te = jnp.where(slot == k, idx, te)
        work = jnp.where(sel, -jnp.inf, work)
    te_ref[...] = te.astype(jnp.int32)

    exps = [jnp.exp(v - vals[0]) for v in vals]
    tot = exps[0]
    for e in exps[1:]:
        tot = tot + e
    gate = jnp.zeros((rows, TOP_K), jnp.float32)
    for k in range(TOP_K):
        gate = jnp.where(slot == k, exps[k] / tot, gate)
    gate_ref[...] = gate

    cnt = jnp.zeros((rows, ne), jnp.float32)
    for sel in sels:
        cnt = cnt + sel.astype(jnp.float32)
    r_i = lax.broadcasted_iota(jnp.int32, (rows, rows), 0)
    c_i = lax.broadcasted_iota(jnp.int32, (rows, rows), 1)
    tril = (c_i < r_i).astype(bf)
    before = jnp.dot(tril, cnt.astype(bf), preferred_element_type=jnp.float32) + carry_ref[...]
    rank = jnp.zeros((rows, TOP_K), jnp.float32)
    for k in range(TOP_K):
        rk = jnp.sum(jnp.where(sels[k], before, 0.0), axis=-1, keepdims=True)
        rank = jnp.where(slot == k, rk, rank)
    rank_ref[...] = rank.astype(jnp.int32)
    carry_ref[...] = carry_ref[...] + jnp.sum(cnt, axis=0, keepdims=True)
    cnt_ref[...] = carry_ref[...].astype(jnp.int32)


def _route(x1, g_ffn, w_router, b_router):
    t, d = x1.shape
    ne = w_router.shape[1]
    rows = ROUTE_ROWS
    assert t % rows == 0 and d % LANES == 0
    nt = d // LANES
    return pl.pallas_call(
        _route_kernel,
        grid=(t // rows,),
        in_specs=[pl.BlockSpec((rows, d), lambda i: (i, 0)),
                  _const_spec(g_ffn.shape), _const_spec(w_router.shape), _const_spec(b_router.shape)],
        out_specs=[pl.BlockSpec((rows * nt, LANES), lambda i: (i, 0)),
                   pl.BlockSpec((rows, TOP_K), lambda i: (i, 0)),
                   pl.BlockSpec((rows, TOP_K), lambda i: (i, 0)),
                   pl.BlockSpec((rows, TOP_K), lambda i: (i, 0)),
                   _const_spec((1, ne))],
        out_shape=[jax.ShapeDtypeStruct((t * nt, LANES), jnp.float32),
                   jax.ShapeDtypeStruct((t, TOP_K), jnp.int32),
                   jax.ShapeDtypeStruct((t, TOP_K), jnp.float32),
                   jax.ShapeDtypeStruct((t, TOP_K), jnp.int32),
                   jax.ShapeDtypeStruct((1, ne), jnp.int32)],
        scratch_shapes=[pltpu.VMEM((1, ne), jnp.float32)],
        compiler_params=pltpu.CompilerParams(
            dimension_semantics=("arbitrary",), vmem_limit_bytes=VMEM_LIMIT),
        name="route",
    )(x1, g_ffn, w_router, b_router)


def _meta_kernel(cnt_ref, poff_ref, blke_ref, blks_ref):
    ne = cnt_ref.shape[1]
    nb = blke_ref.shape[1]
    f32 = jnp.float32
    cnt = cnt_ref[...].astype(f32)
    nblk = jnp.floor((cnt + (MOE_BLOCK - 1)) * (1.0 / MOE_BLOCK))
    e_r = lax.broadcasted_iota(jnp.int32, (ne, ne), 0)
    e_c = lax.broadcasted_iota(jnp.int32, (ne, ne), 1)
    nb_rows = jnp.broadcast_to(nblk, (ne, ne))
    bend_col = jnp.sum(jnp.where(e_c <= e_r, nb_rows, 0.0), axis=1, keepdims=True)
    nb_col = jnp.sum(jnp.where(e_c == e_r, nb_rows, 0.0), axis=1, keepdims=True)
    boff = jnp.sum(jnp.where(e_r < e_c, jnp.broadcast_to(nb_col, (ne, ne)), 0.0),
                   axis=0, keepdims=True)
    poff_ref[...] = (boff * MOE_BLOCK).astype(jnp.int32)
    total = jnp.sum(nblk, axis=1, keepdims=True)
    blk = lax.broadcasted_iota(jnp.int32, (ne, nb), 1).astype(f32)
    done = jnp.where(jnp.broadcast_to(bend_col, (ne, nb)) <= blk, 1.0, 0.0)
    be = jnp.minimum(jnp.sum(done, axis=0, keepdims=True), ne - 1.0)
    bid = lax.broadcasted_iota(jnp.int32, (1, nb), 1).astype(f32)
    last_e = jnp.sum(jnp.where(bid == total - 1.0, be, 0.0), axis=1, keepdims=True)
    blke_ref[...] = jnp.where(bid < total, be, last_e).astype(jnp.int32)
    blks_ref[...] = jnp.minimum(bid, total - 1.0).astype(jnp.int32)


def _meta(counts, n_blocks):
    ne = counts.shape[1]
    return pl.pallas_call(
        _meta_kernel,
        out_shape=[jax.ShapeDtypeStruct((1, ne), jnp.int32),
                   jax.ShapeDtypeStruct((1, n_blocks), jnp.int32),
                   jax.ShapeDtypeStruct((1, n_blocks), jnp.int32)],
        name="moe_meta",
    )(counts)


def _dest_kernel(poff_ref, te_ref, rank_ref, dest_ref):
    te = te_ref[...]
    dest = rank_ref[...]
    for e in range(poff_ref.shape[1]):
        dest = dest + jnp.where(te == e, poff_ref[0, e], 0)
    dest_ref[...] = dest


def _dest(poff, te, rank):
    return pl.pallas_call(
        _dest_kernel,
        in_specs=[pl.BlockSpec(memory_space=pltpu.SMEM), _const_spec(te.shape), _const_spec(rank.shape)],
        out_specs=_const_spec(te.shape),
        out_shape=jax.ShapeDtypeStruct(te.shape, jnp.int32),
        grid=(1,),
        name="moe_dest",
    )(poff, te, rank)


def _inverse_kernel(cnt_ref, poff_ref, dest_ref, inv_ref):
    step = pl.program_id(0)
    n = dest_ref.shape[2]
    unroll = 16

    @pl.when(step == 0)
    def _():
        def put(r, carry):
            inv_ref[0, r] = -1
            return carry

        def per_expert(e, carry):
            c = cnt_ref[0, e]
            lo = poff_ref[0, e] + c
            hi = lo + (MOE_BLOCK - (c % MOE_BLOCK)) % MOE_BLOCK
            lax.fori_loop(lo, hi, put, 0)
            return hi

        used = lax.fori_loop(0, cnt_ref.shape[1], per_expert, 0)
        lax.fori_loop(used, inv_ref.shape[1], put, 0)

    base = step * n

    def body(i, carry):
        j0 = pl.multiple_of(i * unroll, unroll)
        chunk = dest_ref.at[0, 0, pl.ds(j0, unroll)]
        first = base + j0
        for q in range(unroll):
            inv_ref[0, chunk[q]] = first + q
        return carry

    lax.fori_loop(0, n // unroll, body, 0)


def _inverse(counts, poff, dest, n_rows):
    n_assign = dest.size
    chunk = DISPATCH_ROWS * TOP_K
    assert n_assign % chunk == 0
    smem = pl.BlockSpec(memory_space=pltpu.SMEM)
    return pl.pallas_call(
        _inverse_kernel,
        grid=(n_assign // chunk,),
        in_specs=[smem, smem, pl.BlockSpec((1, 1, chunk), lambda i: (i, 0, 0), memory_space=pltpu.SMEM)],
        out_specs=smem,
        out_shape=jax.ShapeDtypeStruct((1, n_rows), jnp.int32),
        compiler_params=pltpu.CompilerParams(dimension_semantics=("arbitrary",)),
        name="moe_inverse",
    )(counts, poff, dest.reshape(n_assign // chunk, 1, chunk))


def _rows_kernel(n_tok, nt, inv_ref, src_ref, dst_ref):
    a = inv_ref[...]
    blk = lax.broadcasted_iota(jnp.int32, a.shape, 0)
    r = lax.broadcasted_iota(jnp.int32, a.shape, 1)
    real = a >= 0
    tok = jnp.right_shift(jnp.maximum(a, 0), 2)
    slot = jnp.bitwise_and(jnp.maximum(a, 0), TOP_K - 1)
    src_ref[...] = tok * nt
    spare = TOP_K * n_tok + jnp.bitwise_and(blk, 1) * MOE_BLOCK + r
    dst_ref[...] = jnp.where(real, slot * n_tok + tok, spare) * nt


def _rows(inv, n_tok, nt):
    n_blocks = inv.shape[1] // MOE_BLOCK
    inv2 = inv.reshape(n_blocks, MOE_BLOCK)
    shp = jax.ShapeDtypeStruct(inv2.shape, jnp.int32)
    assert TOP_K == 4
    return pl.pallas_call(
        functools.partial(_rows_kernel, n_tok, nt),
        out_shape=[shp, shp],
        name="moe_rows",
    )(inv2)


def _fused_expert_kernel(nt, n_tok, blke_ref, blks_ref, src0_ref, srcn_ref, dstp_ref, h2_ref,
                         wgu32_ref, bgu_ref, wd32_ref, bd_ref, ys_ref,
                         wgu_ref, wd_ref, bgus_ref, bds_ref, xbuf_ref, ybuf_ref, gsem, ssem):
    i = pl.program_id(0)
    f = wd_ref.shape[0]
    blk = MOE_BLOCK
    p = i % 2
    q = 1 - p
    valid = blks_ref[0, i] == i
    tile_rows = blk * nt

    def gather(idx_ref, half):
        for r in range(blk):
            pltpu.make_async_copy(
                h2_ref.at[pl.ds(pl.multiple_of(idx_ref[0, 0, r], nt), nt), :],
                xbuf_ref.at[half, pl.ds(r * nt, nt), :], gsem.at[half]).start(priority=0)

    def scatter(idx_ref, half):
        for r in range(blk):
            pltpu.make_async_copy(
                ybuf_ref.at[half, pl.ds(r * nt, nt), :],
                ys_ref.at[pl.ds(pl.multiple_of(idx_ref[0, 0, r], nt), nt), :], ssem.at[half]).start(priority=1)

    def wait_gather(half):
        pltpu.make_async_copy(h2_ref.at[pl.ds(0, tile_rows), :], xbuf_ref.at[half], gsem.at[half]).wait()

    def wait_scatter(half):
        pltpu.make_async_copy(ybuf_ref.at[half], ys_ref.at[pl.ds(0, tile_rows), :], ssem.at[half]).wait()

    @pl.when(i == 0)
    def _():
        ybuf_ref[...] = jnp.zeros_like(ybuf_ref)
        spare0 = TOP_K * n_tok * nt
        for half in range(2):
            cp = pltpu.make_async_copy(ybuf_ref.at[half],
                                       ys_ref.at[pl.ds(spare0 + half * tile_rows, tile_rows), :], ssem.at[half])
            cp.start()
            cp.wait()
        gather(src0_ref, 0)

    @pl.when(jnp.logical_and(valid, jnp.logical_or(i == 0, blke_ref[0, i] != blke_ref[0, jnp.maximum(i - 1, 0)])))
    def _():
        wgu_ref[...] = wgu32_ref[0].astype(jnp.bfloat16)
        wd_ref[...] = wd32_ref[0].astype(jnp.bfloat16)
        bgus_ref[...] = bgu_ref[0]
        bds_ref[...] = bd_ref[0]

    @pl.when(valid)
    def _():
        wait_gather(p)
        x = jnp.concatenate([xbuf_ref[p, pl.ds(j, blk, stride=nt), :] for j in range(nt)],
                            axis=-1).astype(jnp.bfloat16)
        scatter(dstp_ref, q)
        gather(srcn_ref, q)
        gu = jnp.dot(x, wgu_ref[...], preferred_element_type=jnp.float32) + bgus_ref[...]
        g = jnp.minimum(gu[:, 0:f], SWIGLU_LIMIT)
        lin = jnp.clip(gu[:, f:2 * f], -SWIGLU_LIMIT, SWIGLU_LIMIT)
        act = g * _sigmoid(SWIGLU_ALPHA * g) * (lin + 1.0)
        y = jnp.dot(act.astype(jnp.bfloat16), wd_ref[...], preferred_element_type=jnp.float32) + bds_ref[...]

        @pl.when(i >= 1)
        def _():
            wait_scatter(p)

        for j in range(nt):
            ybuf_ref[p, pl.ds(j, blk, stride=nt), :] = y[:, j * LANES:(j + 1) * LANES]

    @pl.when(jnp.logical_and(jnp.logical_not(valid), blks_ref[0, jnp.maximum(i - 1, 0)] == i - 1))
    def _():
        scatter(dstp_ref, q)
        wait_scatter(q)
        wait_scatter(p)
        wait_gather(p)


def _fused_experts(blk_e, blk_s, src, dst, h2, w_gate_up, b_gate_up, w_down, b_down, n_tok, nt):
    ne, d, f2 = w_gate_up.shape
    f = f2 // 2
    n_blocks = src.shape[0]
    steps = n_blocks + 1
    assert blk_e.shape[1] == steps
    blk = MOE_BLOCK
    first = ((TOP_K * n_tok + blk + jnp.arange(blk, dtype=jnp.int32)) * nt).reshape(1, blk)
    src3 = src.reshape(n_blocks, 1, blk)
    dst3 = jnp.concatenate([dst, first], axis=0).reshape(n_blocks + 1, 1, blk)
    smem = pltpu.SMEM

    def w_map(i, be, bs):
        return (be[0, i], 0, 0)

    grid_spec = pltpu.PrefetchScalarGridSpec(
        num_scalar_prefetch=2,
        grid=(steps,),
        in_specs=[pl.BlockSpec((1, 1, blk), lambda i, be, bs: (0, 0, 0), memory_space=smem),
                  pl.BlockSpec((1, 1, blk), lambda i, be, bs: (bs[0, jnp.minimum(i + 1, steps - 1)], 0, 0),
                               memory_space=smem),
                  pl.BlockSpec((1, 1, blk),
                               lambda i, be, bs: (jnp.where(i == 0, n_blocks, bs[0, jnp.maximum(i - 1, 0)]), 0, 0),
                               memory_space=smem),
                  pl.BlockSpec(memory_space=pl.ANY),
                  pl.BlockSpec((1, d, f2), w_map), pl.BlockSpec((1, 1, f2), w_map),
                  pl.BlockSpec((1, f, d), w_map), pl.BlockSpec((1, 1, d), w_map)],
        out_specs=pl.BlockSpec(memory_space=pl.ANY),
        scratch_shapes=[pltpu.VMEM((d, f2), jnp.bfloat16), pltpu.VMEM((f, d), jnp.bfloat16),
                        pltpu.VMEM((1, f2), jnp.float32), pltpu.VMEM((1, d), jnp.float32),
                        pltpu.VMEM((2, blk * nt, LANES), jnp.float32), pltpu.VMEM((2, blk * nt, LANES), jnp.float32),
                        pltpu.SemaphoreType.DMA((2,)), pltpu.SemaphoreType.DMA((2,))],
    )
    out_rows = (TOP_K * n_tok + 2 * blk) * nt
    return pl.pallas_call(
        functools.partial(_fused_expert_kernel, nt, n_tok),
        grid_spec=grid_spec,
        out_shape=jax.ShapeDtypeStruct((out_rows, LANES), jnp.float32),
        compiler_params=pltpu.CompilerParams(
            dimension_semantics=("arbitrary",), vmem_limit_bytes=VMEM_LIMIT, has_side_effects=True),
        name="moe_experts",
    )(blk_e, blk_s, src3, src3, dst3, h2, w_gate_up, b_gate_up.reshape(ne, 1, f2), w_down,
      b_down.reshape(ne, 1, d))


def _final_kernel(nt, gate_ref, x1_ref, gfin_ref, y0_ref, y1_ref, y2_ref, y3_ref, out_ref):
    rows, d = x1_ref.shape
    gate = gate_ref[...]
    x = x1_ref[...]
    for k, y_ref in enumerate((y0_ref, y1_ref, y2_ref, y3_ref)):
        yk = jnp.concatenate([y_ref[pl.ds(j, rows, stride=nt), :] for j in range(nt)], axis=-1)
        x = x + gate[:, k:k + 1] * yk
    ms = jnp.mean(x * x, axis=-1, keepdims=True)
    out_ref[...] = x * lax.rsqrt(ms + NORM_EPS) * gfin_ref[...]


def _final(gate, x1, g_final, ys, nt):
    t, d = x1.shape
    rows = COMBINE_ROWS
    assert t % rows == 0 and TOP_K == 4
    per_slot = t // rows

    def slot_spec(k):
        return pl.BlockSpec((rows * nt, LANES), lambda i: (k * per_slot + i, 0))

    return pl.pallas_call(
        functools.partial(_final_kernel, nt),
        grid=(per_slot,),
        in_specs=[pl.BlockSpec((rows, TOP_K), lambda i: (i, 0)),
                  pl.BlockSpec((rows, d), lambda i: (i, 0)),
                  _const_spec(g_final.shape)] + [slot_spec(k) for k in range(TOP_K)],
        out_specs=pl.BlockSpec((rows, d), lambda i: (i, 0)),
        out_shape=jax.ShapeDtypeStruct((t, d), jnp.float32),
        compiler_params=pltpu.CompilerParams(
            dimension_semantics=("arbitrary",), vmem_limit_bytes=VMEM_LIMIT),
        name="moe_final",
    )(gate, x1, g_final, ys, ys, ys, ys)


def _block_diag(blocks):
    g, r, c = blocks.shape
    eye = jnp.eye(g, dtype=blocks.dtype)
    return (blocks[:, :, None, :] * eye[:, None, :, None]).reshape(g * r, g * c)


def kernel(x, norm_mix_g, w_in, conv_w, conv_b, conv_ln_g, conv_ln_b, w_conv_out, ssm_a_re, ssm_a_im,
           ssm_log_step, ssm_b_re, ssm_b_im, ssm_c_re, ssm_c_im, ssm_d, w_ssm_glu, b_ssm_glu, w_ssm_out,
           w_out, norm_ffn_g, w_router, b_router, w_gate_up, b_gate_up, w_down, b_down, norm_final_g):
    bsz, seq, d = x.shape
    depth = w_in.shape[0]
    bf = jnp.bfloat16
    nt = d // LANES
    n_tok = bsz * seq
    ne = w_router.shape[-1]
    n_blocks = -(-n_tok * TOP_K // MOE_BLOCK) + ne
    seg_len = MIX_ROWS // SCAN_SEGS

    def row(v):
        return v.reshape(1, -1)

    for l in range(depth):
        ar, ai, arp, aip, bbr, bbi = _ssm_disc(
            ssm_a_re[l], ssm_a_im[l], ssm_log_step[l],
            jnp.swapaxes(ssm_b_re[l], 1, 2), jnp.swapaxes(ssm_b_im[l], 1, 2), seg_len)
        bblk = jnp.concatenate([_block_diag(bbr), _block_diag(bbi)], axis=1).astype(bf)
        c_r = _block_diag(jnp.swapaxes(ssm_c_re[l], 1, 2)).astype(bf)
        c_i = _block_diag(jnp.swapaxes(ssm_c_im[l], 1, 2)).astype(bf)
        avec = jnp.stack([ar.reshape(-1), ai.reshape(-1), arp.reshape(-1), aip.reshape(-1)])

        x = _mixer(x, row(norm_mix_g[l]), w_in[l].astype(bf), conv_w[l], row(conv_b[l]),
                   row(conv_ln_g[l]), row(conv_ln_b[l]), w_conv_out[l].astype(bf), bblk, c_r, c_i, avec,
                   row(ssm_d[l]), w_ssm_glu[l].astype(bf), row(b_ssm_glu[l]), w_ssm_out[l].astype(bf),
                   w_out[l].astype(bf))

        x1 = x.reshape(n_tok, d)
        h2, te, gate, rank, counts = _route(x1, row(norm_ffn_g[l]), w_router[l], row(b_router[l]))
        poff, blk_e, blk_s = _meta(counts, n_blocks + 1)
        dest = _dest(poff, te.reshape(-1, LANES), rank.reshape(-1, LANES))
        inv = _inverse(counts, poff, dest, n_blocks * MOE_BLOCK)
        src, dst = _rows(inv, n_tok, nt)
        ys = _fused_experts(blk_e, blk_s, src, dst, h2, w_gate_up[l], b_gate_up[l], w_down[l], b_down[l],
                            n_tok, nt)
        assert depth == 1
        x = _final(gate, x1, row(norm_final_g), ys, nt).reshape(bsz, seq, d)
    return x
```

```python
import functools
import math

import jax
import jax.numpy as jnp
from jax import lax
from jax.experimental import pallas as pl
from jax.experimental.pallas import tpu as pltpu

NORM_EPS = 1e-6
SWIGLU_LIMIT = 7.0
SWIGLU_ALPHA = 1.702
TOP_K = 4
SSM_GROUP = 16

LANES = 128
SUBLANES = 8

MIX_ROWS = 512
SCAN_SEGS = SUBLANES
SEG_PAD = 8
CONV_HALO = 32
CONV_ROWS = 64
ROUTE_ROWS = 512
MOE_BLOCK = 512
DISPATCH_ROWS = 1024
COMBINE_ROWS = 256
VMEM_LIMIT = 56 * 1024 * 1024


def _sigmoid(x):
    return 1.0 / (1.0 + jnp.exp(-x))


def _gelu_tanh(x):
    c = math.sqrt(2.0 / math.pi)
    return 0.5 * x * (1.0 + jnp.tanh(c * (x + 0.044715 * (x * x * x))))


def _const_spec(shape):
    nd = len(shape)
    return pl.BlockSpec(shape, lambda *_: (0,) * nd)


def _ssm_disc_kernel(seg_len, are_ref, aim_ref, ls_ref, bre_ref, bim_ref,
                     ar_ref, ai_ref, arp_ref, aip_ref, bbr_ref, bbi_ref):
    lr = are_ref[...]
    li = aim_ref[...]
    dt = jnp.exp(ls_ref[...])
    mag = jnp.exp(lr * dt)
    ar = mag * jnp.cos(li * dt)
    ai = mag * jnp.sin(li * dt)
    den = lr * lr + li * li
    fr = ((ar - 1.0) * lr + ai * li) / den
    fi = (ai * lr - (ar - 1.0) * li) / den
    ar_ref[...] = ar
    ai_ref[...] = ai
    magp = jnp.exp(lr * dt * seg_len)
    arp_ref[...] = magp * jnp.cos(li * dt * seg_len)
    aip_ref[...] = magp * jnp.sin(li * dt * seg_len)
    br = bre_ref[...]
    bi = bim_ref[...]
    bbr_ref[...] = fr[:, None, :] * br - fi[:, None, :] * bi
    bbi_ref[...] = fr[:, None, :] * bi + fi[:, None, :] * br


def _ssm_disc(a_re, a_im, log_step, bt_re, bt_im, seg_len):
    g, p = a_re.shape
    h = bt_re.shape[1]
    f32 = jnp.float32
    out_shape = [jax.ShapeDtypeStruct((g, p), f32)] * 4 + [jax.ShapeDtypeStruct((g, h, p), f32)] * 2
    return pl.pallas_call(
        functools.partial(_ssm_disc_kernel, float(seg_len)),
        out_shape=out_shape,
        name="ssm_disc",
    )(a_re, a_im, log_step.reshape(g, 1), bt_re, bt_im)


def _mixer_kernel(dims, x_ref, gmix_ref, win_ref, convw_ref, convb_ref, lng_ref, lnb_ref, wco_ref,
                  bblk_ref, cr_ref, ci_ref, avec_ref, dskip_ref, wglu_ref, bglu_ref, wso_ref, wout_ref,
                  x1_ref, vbuf_ref, conv_ref, scan_ref, xst_ref, cin_ref, st_ref):
    d, dc, ds, gp, cw = dims
    rows = MIX_ROWS
    seg = rows // SCAN_SEGS
    pitch = seg + SEG_PAD
    nsl = gp // LANES
    c_idx = pl.program_id(1)

    half = rows // 2
    tail = 2 * half + 1

    @pl.when(c_idx == 0)
    def _():
        for lc in range(dc // LANES):
            vbuf_ref[lc, pl.ds(0, CONV_HALO, stride=2), :] = jnp.zeros((CONV_HALO, LANES), jnp.float32)
        st_ref[...] = jnp.zeros_like(st_ref)

    @pl.when(c_idx != 0)
    def _():
        for lc in range(dc // LANES):
            vbuf_ref[lc, pl.ds(0, CONV_HALO, stride=2), :] = vbuf_ref[lc, pl.ds(tail, CONV_HALO, stride=2), :]

    x = x_ref[0]
    ms = jnp.mean(x * x, axis=-1, keepdims=True)
    h = (x * lax.rsqrt(ms + NORM_EPS) * gmix_ref[...]).astype(jnp.bfloat16)

    pa = jnp.dot(h, win_ref[:, 0:2 * dc], preferred_element_type=jnp.float32)
    vglu = pa[:, 0:dc] * _sigmoid(pa[:, dc:2 * dc])
    for lc in range(dc // LANES):
        ls = slice(lc * LANES, (lc + 1) * LANES)
        vbuf_ref[lc, pl.ds(2 * CONV_HALO, half, stride=2), :] = vglu[0:half, ls]
        vbuf_ref[lc, pl.ds(2 * CONV_HALO + 1, half, stride=2), :] = vglu[half:rows, ls]
        vbuf_ref[lc, pl.ds(1, CONV_HALO, stride=2), :] = vglu[half - CONV_HALO:half, ls]

    base = CONV_HALO - (cw - 1)
    nv = CONV_ROWS // SUBLANES

    g0 = 2 * dc + ds
    sg_conv = _sigmoid(jnp.dot(h, win_ref[:, g0:g0 + d], preferred_element_type=jnp.float32))
    sg_ssm = _sigmoid(jnp.dot(h, win_ref[:, g0 + d:g0 + 2 * d], preferred_element_type=jnp.float32))

    def conv_chunk(rc, carry):
        r0 = rc * CONV_ROWS
        for p in range(2):
            for lc in range(dc // LANES):
                ls = slice(lc * LANES, (lc + 1) * LANES)
                wv = [jnp.broadcast_to(convw_ref[k:k + 1, ls], (SUBLANES, LANES)) for k in range(cw)]
                acc = [jnp.zeros((SUBLANES, LANES), jnp.float32) for _ in range(nv)]
                for s in range(CONV_ROWS - SUBLANES + cw):
                    win = vbuf_ref[lc, pl.ds(2 * (r0 + base + s) + p, SUBLANES, stride=2), :]
                    for i in range(nv):
                        k = s - SUBLANES * i
                        if 0 <= k < cw:
                            acc[i] = acc[i] + wv[k] * win
                for i in range(nv):
                    conv_ref[pl.ds(p * half + r0 + SUBLANES * i, SUBLANES), ls] = acc[i]
        return carry

    for rc in range(half // CONV_ROWS):
        conv_chunk(rc, 0)

    v = conv_ref[...] + convb_ref[...]
    mu = jnp.mean(v, axis=-1, keepdims=True)
    vc = v - mu
    var = jnp.mean(vc * vc, axis=-1, keepdims=True)
    v = vc * lax.rsqrt(var + NORM_EPS) * lng_ref[...] + lnb_ref[...]
    v = v * _sigmoid(v)
    y_conv = jnp.dot(v.astype(jnp.bfloat16), wco_ref[...], preferred_element_type=jnp.float32)

    u = jnp.dot(h, win_ref[:, 2 * dc:2 * dc + ds], preferred_element_type=jnp.float32)
    bu = jnp.dot(u.astype(jnp.bfloat16), bblk_ref[...], preferred_element_type=jnp.float32)
    for n in range(2 * nsl):
        for s in range(SCAN_SEGS):
            scan_ref[n, s * pitch:s * pitch + seg, :] = bu[s * seg:(s + 1) * seg, n * LANES:(n + 1) * LANES]

    def bcast(row, n):
        return jnp.broadcast_to(avec_ref[row:row + 1, n * LANES:(n + 1) * LANES], (SUBLANES, LANES))

    a_r = [bcast(0, n) for n in range(nsl)]
    a_i = [bcast(1, n) for n in range(nsl)]

    def step(j, st, store):
        out = []
        for n in range(nsl):
            sr, si = st[2 * n], st[2 * n + 1]
            br = scan_ref[n, pl.ds(j, SCAN_SEGS, stride=pitch), :]
            bi = scan_ref[nsl + n, pl.ds(j, SCAN_SEGS, stride=pitch), :]
            nr = a_r[n] * sr - a_i[n] * si + br
            ni = a_r[n] * si + a_i[n] * sr + bi
            if store:
                scan_ref[n, pl.ds(j, SCAN_SEGS, stride=pitch), :] = nr
                scan_ref[nsl + n, pl.ds(j, SCAN_SEGS, stride=pitch), :] = ni
            out += [nr, ni]
        return tuple(out)

    zero = jnp.zeros((SUBLANES, LANES), jnp.float32)
    fin = lax.fori_loop(0, seg, functools.partial(step, store=False), (zero,) * (2 * nsl))

    for n in range(nsl):
        fr, fi = fin[2 * n], fin[2 * n + 1]
        ap_r = avec_ref[2:3, n * LANES:(n + 1) * LANES]
        ap_i = avec_ref[3:4, n * LANES:(n + 1) * LANES]
        c_r = st_ref[0:1, n * LANES:(n + 1) * LANES]
        c_i = st_ref[1:2, n * LANES:(n + 1) * LANES]
        for s in range(SCAN_SEGS):
            cin_ref[2 * n, s:s + 1, :] = c_r
            cin_ref[2 * n + 1, s:s + 1, :] = c_i
            n_r = ap_r * c_r - ap_i * c_i + fr[s:s + 1, :]
            n_i = ap_r * c_i + ap_i * c_r + fi[s:s + 1, :]
            c_r, c_i = n_r, n_i
        st_ref[0:1, n * LANES:(n + 1) * LANES] = c_r
        st_ref[1:2, n * LANES:(n + 1) * LANES] = c_i

    lax.fori_loop(0, seg, functools.partial(step, store=True), tuple(cin_ref[q] for q in range(2 * nsl)))

    for n in range(2 * nsl):
        for s in range(SCAN_SEGS):
            xst_ref[s * seg:(s + 1) * seg, n * LANES:(n + 1) * LANES] = (
                scan_ref[n, s * pitch:s * pitch + seg, :].astype(jnp.bfloat16))

    y = (jnp.dot(xst_ref[:, 0:gp], cr_ref[...], preferred_element_type=jnp.float32)
         - jnp.dot(xst_ref[:, gp:2 * gp], ci_ref[...], preferred_element_type=jnp.float32))
    y = _gelu_tanh(y + dskip_ref[...] * u)
    glu = jnp.dot(y.astype(jnp.bfloat16), wglu_ref[...], preferred_element_type=jnp.float32) + bglu_ref[...]
    y = y * _sigmoid(glu)
    y_ssm = jnp.dot(y.astype(jnp.bfloat16), wso_ref[...], preferred_element_type=jnp.float32)

    m = sg_conv * y_conv + sg_ssm * y_ssm
    x1_ref[0] = x + jnp.dot(m.astype(jnp.bfloat16), wout_ref[...], preferred_element_type=jnp.float32)


def _mixer(x, gmix, w_in, conv_w, conv_b, ln_g, ln_b, w_co, bblk, c_r, c_i, avec, d_skip,
           w_glu, b_glu, w_so, w_out):
    b, s, d = x.shape
    cw, dc = conv_w.shape
    ds = d_skip.shape[-1]
    gp = c_r.shape[0]
    rows = MIX_ROWS
    seg = rows // SCAN_SEGS
    pitch = seg + SEG_PAD
    assert s % rows == 0 and cw - 1 <= CONV_HALO and gp % LANES == 0 and dc % LANES == 0
    dims = (d, dc, ds, gp, cw)
    consts = [gmix, w_in, conv_w, conv_b, ln_g, ln_b, w_co, bblk, c_r, c_i, avec, d_skip,
              w_glu, b_glu, w_so, w_out]
    return pl.pallas_call(
        functools.partial(_mixer_kernel, dims),
        grid=(b, s // rows),
        in_specs=[pl.BlockSpec((1, rows, d), lambda i, j: (i, j, 0))] + [_const_spec(c.shape) for c in consts],
        out_specs=pl.BlockSpec((1, rows, d), lambda i, j: (i, j, 0)),
        out_shape=jax.ShapeDtypeStruct((b, s, d), jnp.float32),
        scratch_shapes=[
            pltpu.VMEM((dc // LANES, 2 * CONV_HALO + rows, LANES), jnp.float32),
            pltpu.VMEM((rows, dc), jnp.float32),
            pltpu.VMEM((2 * gp // LANES, SCAN_SEGS * pitch, LANES), jnp.float32),
            pltpu.VMEM((rows, 2 * gp), jnp.bfloat16),
            pltpu.VMEM((2 * gp // LANES, SUBLANES, LANES), jnp.float32),
            pltpu.VMEM((2, gp), jnp.float32),
        ],
        compiler_params=pltpu.CompilerParams(
            dimension_semantics=("arbitrary", "arbitrary"), vmem_limit_bytes=VMEM_LIMIT),
        name="mixer",
    )(x, *consts)


def _route_kernel(x1_ref, g_ref, wr_ref, br_ref, h2_ref, loc_ref, gate_ref, run_ref, cnt_ref, carry_ref):
    rows, d = x1_ref.shape
    ne = wr_ref.shape[1]

    @pl.when(pl.program_id(0) == 0)
    def _():
        carry_ref[...] = jnp.zeros_like(carry_ref)

    x = x1_ref[...]
    ms = jnp.mean(x * x, axis=-1, keepdims=True)
    h2 = x * lax.rsqrt(ms + NORM_EPS) * g_ref[...]
    for j in range(d // LANES):
        h2_ref[pl.ds(j, rows, stride=d // LANES), :] = h2[:, j * LANES:(j + 1) * LANES]

    bf = jnp.bfloat16
    h_hi = h2.astype(bf)
    h_lo = (h2 - h_hi.astype(jnp.float32)).astype(bf)
    w = wr_ref[...]
    w_hi = w.astype(bf)
    w_lo = (w - w_hi.astype(jnp.float32)).astype(bf)
    logits = (jnp.dot(h_hi, w_hi, preferred_element_type=jnp.float32)
              + jnp.dot(h_lo, w_hi, preferred_element_type=jnp.float32)
              + jnp.dot(h_hi, w_lo, preferred_element_type=jnp.float32)) + br_ref[...]

    lane = lax.broadcasted_iota(jnp.int32, (rows, ne), 1).astype(jnp.float32)
    slot = lax.broadcasted_iota(jnp.int32, (rows, TOP_K), 1)
    work = logits
    sels, vals = [], []
    for k in range(TOP_K):
        mx = jnp.max(work, axis=-1, keepdims=True)
        idx = jnp.min(jnp.where(work == mx, lane, float(ne)), axis=-1, keepdims=True)
        sel = lane == idx
        sels.append(sel)
        vals.append(mx)
        work = jnp.where(sel, -jnp.inf, work)

    exps = [jnp.exp(v - vals[0]) for v in vals]
    tot = exps[0]
    for e in exps[1:]:
        tot = tot + e
    gate = jnp.zeros((rows, TOP_K), jnp.float32)
    for k in range(TOP_K):
        gate = jnp.where(slot == k, exps[k] / tot, gate)
    gate_ref[...] = gate

    cnt = jnp.zeros((rows, ne), jnp.float32)
    for sel in sels:
        cnt = cnt + sel.astype(jnp.float32)
    r_i = lax.broadcasted_iota(jnp.int32, (rows, rows), 0)
    c_i = lax.broadcasted_iota(jnp.int32, (rows, rows), 1)
    tril = (c_i < r_i).astype(bf)
    before = jnp.dot(tril, cnt.astype(bf), preferred_element_type=jnp.float32)
    tile_cnt = jnp.sum(cnt, axis=0, keepdims=True)
    e_r = lax.broadcasted_iota(jnp.int32, (ne, ne), 0)
    e_c = lax.broadcasted_iota(jnp.int32, (ne, ne), 1)
    cnt_col = jnp.sum(jnp.where(e_c == e_r, jnp.broadcast_to(tile_cnt, (ne, ne)), 0.0), axis=1, keepdims=True)
    first = jnp.sum(jnp.where(e_r < e_c, jnp.broadcast_to(cnt_col, (ne, ne)), 0.0), axis=0, keepdims=True)
    pos = before + first
    loc = jnp.zeros((rows, TOP_K), jnp.float32)
    for k in range(TOP_K):
        lk = jnp.sum(jnp.where(sels[k], pos, 0.0), axis=-1, keepdims=True)
        loc = jnp.where(slot == k, lk, loc)
    loc_ref[...] = loc.astype(jnp.int32)
    run_ref[...] = jnp.zeros_like(run_ref)
    run_ref[0, 0:1, :] = tile_cnt.astype(jnp.int32)
    run_ref[0, 1:2, :] = first.astype(jnp.int32)
    run_ref[0, 2:3, :] = carry_ref[...].astype(jnp.int32)
    carry_ref[...] = carry_ref[...] + tile_cnt
    cnt_ref[...] = carry_ref[...].astype(jnp.int32)


def _route(x1, g_ffn, w_router, b_router):
    t, d = x1.shape
    ne = w_router.shape[1]
    rows = ROUTE_ROWS
    assert t % rows == 0 and d % LANES == 0
    nt = d // LANES
    return pl.pallas_call(
        _route_kernel,
        grid=(t // rows,),
        in_specs=[pl.BlockSpec((rows, d), lambda i: (i, 0)),
                  _const_spec(g_ffn.shape), _const_spec(w_router.shape), _const_spec(b_router.shape)],
        out_specs=[pl.BlockSpec((rows * nt, LANES), lambda i: (i, 0)),
                   pl.BlockSpec((rows, TOP_K), lambda i: (i, 0)),
                   pl.BlockSpec((rows, TOP_K), lambda i: (i, 0)),
                   pl.BlockSpec((1, SUBLANES, ne), lambda i: (i, 0, 0)),
                   _const_spec((1, ne))],
        out_shape=[jax.ShapeDtypeStruct((t * nt, LANES), jnp.float32),
                   jax.ShapeDtypeStruct((t, TOP_K), jnp.int32),
                   jax.ShapeDtypeStruct((t, TOP_K), jnp.float32),
                   jax.ShapeDtypeStruct((t // rows, SUBLANES, ne), jnp.int32),
                   jax.ShapeDtypeStruct((1, ne), jnp.int32)],
        scratch_shapes=[pltpu.VMEM((1, ne), jnp.float32)],
        compiler_params=pltpu.CompilerParams(
            dimension_semantics=("arbitrary",), vmem_limit_bytes=VMEM_LIMIT),
        name="route",
    )(x1, g_ffn, w_router, b_router)


def _meta_kernel(cnt_ref, poff_ref, blke_ref, blks_ref):
    ne = cnt_ref.shape[1]
    nb = blke_ref.shape[1]
    f32 = jnp.float32
    cnt = cnt_ref[...].astype(f32)
    nblk = jnp.floor((cnt + (MOE_BLOCK - 1)) * (1.0 / MOE_BLOCK))
    e_r = lax.broadcasted_iota(jnp.int32, (ne, ne), 0)
    e_c = lax.broadcasted_iota(jnp.int32, (ne, ne), 1)
    nb_rows = jnp.broadcast_to(nblk, (ne, ne))
    bend_col = jnp.sum(jnp.where(e_c <= e_r, nb_rows, 0.0), axis=1, keepdims=True)
    nb_col = jnp.sum(jnp.where(e_c == e_r, nb_rows, 0.0), axis=1, keepdims=True)
    boff = jnp.sum(jnp.where(e_r < e_c, jnp.broadcast_to(nb_col, (ne, ne)), 0.0),
                   axis=0, keepdims=True)
    poff_ref[...] = (boff * MOE_BLOCK).astype(jnp.int32)
    total = jnp.sum(nblk, axis=1, keepdims=True)
    blk = lax.broadcasted_iota(jnp.int32, (ne, nb), 1).astype(f32)
    done = jnp.where(jnp.broadcast_to(bend_col, (ne, nb)) <= blk, 1.0, 0.0)
    be = jnp.minimum(jnp.sum(done, axis=0, keepdims=True), ne - 1.0)
    bid = lax.broadcasted_iota(jnp.int32, (1, nb), 1).astype(f32)
    last_e = jnp.sum(jnp.where(bid == total - 1.0, be, 0.0), axis=1, keepdims=True)
    blke_ref[...] = jnp.where(bid < total, be, last_e).astype(jnp.int32)
    blks_ref[...] = jnp.minimum(bid, total - 1.0).astype(jnp.int32)


def _meta(counts, n_blocks):
    ne = counts.shape[1]
    return pl.pallas_call(
        _meta_kernel,
        out_shape=[jax.ShapeDtypeStruct((1, ne), jnp.int32),
                   jax.ShapeDtypeStruct((1, n_blocks), jnp.int32),
                   jax.ShapeDtypeStruct((1, n_blocks), jnp.int32)],
        name="moe_meta",
    )(counts)


def _runs_kernel(poff_ref, run_ref, out_ref):
    run = run_ref[...]
    row = lax.broadcasted_iota(jnp.int32, run.shape, 1)
    out_ref[...] = jnp.where(row == 2, run + poff_ref[...][None], run)


def _runs(poff, run):
    return pl.pallas_call(
        _runs_kernel,
        out_shape=jax.ShapeDtypeStruct(run.shape, jnp.int32),
        name="moe_runs",
    )(poff, run)


def _tile(ref, r, nt):
    return ref.at[pl.ds(pl.multiple_of(r * nt, nt), nt), :]


def _run_copies(run_ref, src_of, dst_of, sem, nt):
    ne = run_ref.shape[2]

    def per_expert(e, carry):
        n = run_ref[0, 0, e]

        @pl.when(n > 0)
        def _():
            pltpu.make_async_copy(src_of(run_ref[0, 1, e], run_ref[0, 2, e], n),
                                  dst_of(run_ref[0, 1, e], run_ref[0, 2, e], n), sem).start()
        return carry

    lax.fori_loop(0, ne, per_expert, 0)


def _dispatch_kernel(nt, cnt_ref, poff_ref, loc_ref, run_ref, h2_ref, xs_ref, stage_ref, zero_ref, sem, zsem):
    step = pl.program_id(0)
    last = pl.num_programs(0) - 1
    rows = h2_ref.shape[0] // nt
    n_assign = rows * TOP_K
    ne = cnt_ref.shape[1]
    par = step % 2
    unroll = 4

    def rows_at(ref, first, n):
        return ref.at[pl.ds(pl.multiple_of(first * nt, nt), n * nt), :]

    def wait_runs(half):
        pltpu.make_async_copy(stage_ref.at[half], xs_ref.at[pl.ds(0, n_assign * nt), :], sem.at[half]).wait()

    @pl.when(step == 0)
    def _():
        zero_ref[...] = jnp.zeros_like(zero_ref)

        def per_expert(e, carry):
            c = cnt_ref[0, e]
            pad = (MOE_BLOCK - (c % MOE_BLOCK)) % MOE_BLOCK

            @pl.when(pad > 0)
            def _():
                cp = pltpu.make_async_copy(rows_at(zero_ref, 0, pad), rows_at(xs_ref, poff_ref[0, e] + c, pad), zsem)
                cp.start()
                cp.wait()
            return poff_ref[0, e] + c + pad

        used = lax.fori_loop(0, ne, per_expert, 0)

        def spare(b, carry):
            cp = pltpu.make_async_copy(zero_ref, rows_at(xs_ref, b * MOE_BLOCK, MOE_BLOCK), zsem)
            cp.start()
            cp.wait()
            return carry

        lax.fori_loop(used // MOE_BLOCK, xs_ref.shape[0] // (MOE_BLOCK * nt), spare, 0)

    @pl.when(step >= 2)
    def _():
        wait_runs(par)

    def place(i, carry):
        for u in range(unroll):
            tok = i * unroll + u
            row = _tile(h2_ref, tok, nt)[...]
            for k in range(TOP_K):
                stage_ref[par, pl.ds(pl.multiple_of(loc_ref[0, 0, tok * TOP_K + k] * nt, nt), nt), :] = row
        return carry

    lax.fori_loop(0, rows // unroll, place, 0)

    _run_copies(run_ref,
                lambda first, base, n: stage_ref.at[par, pl.ds(pl.multiple_of(first * nt, nt), n * nt), :],
                lambda first, base, n: rows_at(xs_ref, base, n), sem.at[par], nt)

    @pl.when(step == last)
    def _():
        wait_runs(par)

        @pl.when(step >= 1)
        def _():
            wait_runs(1 - par)


def _dispatch(counts, poff, loc, runs, h2, n_rows, nt):
    t = h2.shape[0] // nt
    rows = ROUTE_ROWS
    tiles = t // rows
    ne = counts.shape[1]
    smem = pl.BlockSpec(memory_space=pltpu.SMEM)
    return pl.pallas_call(
        functools.partial(_dispatch_kernel, nt),
        grid=(tiles,),
        in_specs=[smem, smem,
                  pl.BlockSpec((1, 1, rows * TOP_K), lambda i: (i, 0, 0), memory_space=pltpu.SMEM),
                  pl.BlockSpec((1, SUBLANES, ne), lambda i: (i, 0, 0), memory_space=pltpu.SMEM),
                  pl.BlockSpec((rows * nt, LANES), lambda i: (i, 0))],
        out_specs=pl.BlockSpec(memory_space=pl.ANY),
        out_shape=jax.ShapeDtypeStruct((n_rows * nt, LANES), jnp.float32),
        scratch_shapes=[pltpu.VMEM((2, rows * TOP_K * nt, LANES), jnp.float32),
                        pltpu.VMEM((MOE_BLOCK * nt, LANES), jnp.float32),
                        pltpu.SemaphoreType.DMA((2,)), pltpu.SemaphoreType.DMA(())],
        compiler_params=pltpu.CompilerParams(
            dimension_semantics=("arbitrary",), vmem_limit_bytes=VMEM_LIMIT, has_side_effects=True),
        name="moe_dispatch",
    )(counts, poff, loc.reshape(tiles, 1, rows * TOP_K), runs, h2)


def _expert_kernel(nt, blke_ref, blks_ref, xs_ref, wgu32_ref, bgu_ref, wd32_ref, bd_ref, ys_ref,
                   wgu_ref, wd_ref):
    i = pl.program_id(0)
    f = wd_ref.shape[0]

    @pl.when(jnp.logical_or(i == 0, blke_ref[0, i] != blke_ref[0, jnp.maximum(i - 1, 0)]))
    def _():
        wgu_ref[...] = wgu32_ref[0].astype(jnp.bfloat16)
        wd_ref[...] = wd32_ref[0].astype(jnp.bfloat16)

    @pl.when(blks_ref[0, i] != i)
    def _():
        ys_ref[...] = jnp.zeros_like(ys_ref)

    @pl.when(blks_ref[0, i] == i)
    def _():
        x = jnp.concatenate([xs_ref[pl.ds(j, MOE_BLOCK, stride=nt), :] for j in range(nt)], axis=-1)
        gu = jnp.dot(x.astype(jnp.bfloat16), wgu_ref[...], preferred_element_type=jnp.float32) + bgu_ref[0]
        g = jnp.minimum(gu[:, 0:f], SWIGLU_LIMIT)
        lin = jnp.clip(gu[:, f:2 * f], -SWIGLU_LIMIT, SWIGLU_LIMIT)
        act = g * _sigmoid(SWIGLU_ALPHA * g) * (lin + 1.0)
        y = jnp.dot(act.astype(jnp.bfloat16), wd_ref[...], preferred_element_type=jnp.float32) + bd_ref[0]
        for j in range(nt):
            ys_ref[pl.ds(j, MOE_BLOCK, stride=nt), :] = y[:, j * LANES:(j + 1) * LANES]


def _experts(blk_e, blk_s, xs, w_gate_up, b_gate_up, w_down, b_down, nt):
    ne, d, f2 = w_gate_up.shape
    f = f2 // 2
    n_blocks = blk_e.shape[1]
    blk_rows = MOE_BLOCK * nt
    grid_spec = pltpu.PrefetchScalarGridSpec(
        num_scalar_prefetch=2,
        grid=(n_blocks,),
        in_specs=[pl.BlockSpec((blk_rows, LANES), lambda i, be, bs: (bs[0, i], 0)),
                  pl.BlockSpec((1, d, f2), lambda i, be, bs: (be[0, i], 0, 0)),
                  pl.BlockSpec((1, 1, f2), lambda i, be, bs: (be[0, i], 0, 0)),
                  pl.BlockSpec((1, f, d), lambda i, be, bs: (be[0, i], 0, 0)),
                  pl.BlockSpec((1, 1, d), lambda i, be, bs: (be[0, i], 0, 0))],
        out_specs=pl.BlockSpec((blk_rows, LANES), lambda i, be, bs: (i, 0)),
        scratch_shapes=[pltpu.VMEM((d, f2), jnp.bfloat16), pltpu.VMEM((f, d), jnp.bfloat16)],
    )
    return pl.pallas_call(
        functools.partial(_expert_kernel, nt),
        grid_spec=grid_spec,
        out_shape=jax.ShapeDtypeStruct(xs.shape, jnp.float32),
        compiler_params=pltpu.CompilerParams(
            dimension_semantics=("arbitrary",), vmem_limit_bytes=VMEM_LIMIT),
        name="moe_experts",
    )(blk_e, blk_s, xs, w_gate_up, b_gate_up.reshape(ne, 1, f2), w_down, b_down.reshape(ne, 1, d))


def _combine_kernel(nt, loc_ref, gate_ref, run_ref, runn_ref, x1_ref, gfin_ref, ys_ref, out_ref,
                    stage_ref, acc_ref, sem):
    rows, d = x1_ref.shape
    n_assign = rows * TOP_K
    step = pl.program_id(0)
    par = step % 2
    unroll = 4

    def fetch(r_ref, half):
        _run_copies(r_ref,
                    lambda first, base, n: ys_ref.at[pl.ds(pl.multiple_of(base * nt, nt), n * nt), :],
                    lambda first, base, n: stage_ref.at[half, pl.ds(pl.multiple_of(first * nt, nt), n * nt), :],
                    sem.at[half], nt)

    @pl.when(step == 0)
    def _():
        fetch(run_ref, 0)

    @pl.when(step + 1 < pl.num_programs(0))
    def _():
        fetch(runn_ref, 1 - par)

    pltpu.make_async_copy(ys_ref.at[pl.ds(0, n_assign * nt), :], stage_ref.at[par], sem.at[par]).wait()

    def mix(i, carry):
        for u in range(unroll):
            tok = i * unroll + u
            acc = None
            for k in range(TOP_K):
                a = tok * TOP_K + k
                row = stage_ref[par, pl.ds(pl.multiple_of(loc_ref[0, 0, a] * nt, nt), nt), :]
                term = gate_ref[0, 0, a] * row
                acc = term if acc is None else acc + term
            acc_ref[pl.ds(pl.multiple_of(tok * nt, nt), nt), :] = acc
        return carry

    lax.fori_loop(0, rows // unroll, mix, 0)

    y = jnp.concatenate([acc_ref[pl.ds(j, rows, stride=nt), :] for j in range(nt)], axis=-1)
    x = x1_ref[...] + y
    ms = jnp.mean(x * x, axis=-1, keepdims=True)
    out_ref[...] = x * lax.rsqrt(ms + NORM_EPS) * gfin_ref[...]


def _combine(loc, gate, runs, x1, g_final, ys, nt):
    t, d = x1.shape
    rows = ROUTE_ROWS
    tiles = t // rows
    ne = runs.shape[2]
    n_assign = rows * TOP_K
    smem = pltpu.SMEM
    return pl.pallas_call(
        functools.partial(_combine_kernel, nt),
        grid=(tiles,),
        in_specs=[pl.BlockSpec((1, 1, n_assign), lambda i: (i, 0, 0), memory_space=smem),
                  pl.BlockSpec((1, 1, n_assign), lambda i: (i, 0, 0), memory_space=smem),
                  pl.BlockSpec((1, SUBLANES, ne), lambda i: (i, 0, 0), memory_space=smem),
                  pl.BlockSpec((1, SUBLANES, ne), lambda i: (jnp.minimum(i + 1, tiles - 1), 0, 0),
                               memory_space=smem),
                  pl.BlockSpec((rows, d), lambda i: (i, 0)),
                  _const_spec(g_final.shape),
                  pl.BlockSpec(memory_space=pl.ANY)],
        out_specs=pl.BlockSpec((rows, d), lambda i: (i, 0)),
        out_shape=jax.ShapeDtypeStruct((t, d), jnp.float32),
        scratch_shapes=[pltpu.VMEM((2, n_assign * nt, LANES), jnp.float32),
                        pltpu.VMEM((rows * nt, LANES), jnp.float32),
                        pltpu.SemaphoreType.DMA((2,))],
        compiler_params=pltpu.CompilerParams(
            dimension_semantics=("arbitrary",), vmem_limit_bytes=VMEM_LIMIT),
        name="moe_combine",
    )(loc.reshape(tiles, 1, n_assign), gate.reshape(tiles, 1, n_assign), runs, runs, x1, g_final, ys)


def _inverse_kernel(cnt_ref, poff_ref, dest_ref, inv_ref):
    step = pl.program_id(0)
    n = dest_ref.shape[2]
    unroll = 16

    @pl.when(step == 0)
    def _():
        def put(r, carry):
            inv_ref[0, r] = -1
            return carry

        def per_expert(e, carry):
            c = cnt_ref[0, e]
            lo = poff_ref[0, e] + c
            hi = lo + (MOE_BLOCK - (c % MOE_BLOCK)) % MOE_BLOCK
            lax.fori_loop(lo, hi, put, 0)
            return hi

        used = lax.fori_loop(0, cnt_ref.shape[1], per_expert, 0)
        lax.fori_loop(used, inv_ref.shape[1], put, 0)

    base = step * n

    def body(i, carry):
        j0 = pl.multiple_of(i * unroll, unroll)
        chunk = dest_ref.at[0, 0, pl.ds(j0, unroll)]
        first = base + j0
        for q in range(unroll):
            inv_ref[0, chunk[q]] = first + q
        return carry

    lax.fori_loop(0, n // unroll, body, 0)


def _inverse(counts, poff, dest, n_rows):
    n_assign = dest.size
    chunk = DISPATCH_ROWS * TOP_K
    assert n_assign % chunk == 0
    smem = pl.BlockSpec(memory_space=pltpu.SMEM)
    return pl.pallas_call(
        _inverse_kernel,
        grid=(n_assign // chunk,),
        in_specs=[smem, smem, pl.BlockSpec((1, 1, chunk), lambda i: (i, 0, 0), memory_space=pltpu.SMEM)],
        out_specs=smem,
        out_shape=jax.ShapeDtypeStruct((1, n_rows), jnp.int32),
        compiler_params=pltpu.CompilerParams(dimension_semantics=("arbitrary",)),
        name="moe_inverse",
    )(counts, poff, dest.reshape(n_assign // chunk, 1, chunk))


def _rows_kernel(n_tok, nt, inv_ref, src_ref, dst_ref):
    a = inv_ref[...]
    blk = lax.broadcasted_iota(jnp.int32, a.shape, 0)
    r = lax.broadcasted_iota(jnp.int32, a.shape, 1)
    real = a >= 0
    tok = jnp.right_shift(jnp.maximum(a, 0), 2)
    slot = jnp.bitwise_and(jnp.maximum(a, 0), TOP_K - 1)
    src_ref[...] = tok * nt
    spare = TOP_K * n_tok + jnp.bitwise_and(blk, 1) * MOE_BLOCK + r
    dst_ref[...] = jnp.where(real, slot * n_tok + tok, spare) * nt


def _rows(inv, n_tok, nt):
    n_blocks = inv.shape[1] // MOE_BLOCK
    inv2 = inv.reshape(n_blocks, MOE_BLOCK)
    shp = jax.ShapeDtypeStruct(inv2.shape, jnp.int32)
    assert TOP_K == 4
    return pl.pallas_call(
        functools.partial(_rows_kernel, n_tok, nt),
        out_shape=[shp, shp],
        name="moe_rows",
    )(inv2)


def _fused_expert_kernel(nt, n_tok, blke_ref, blks_ref, src0_ref, srcn_ref, dstp_ref, h2_ref,
                         wgu32_ref, bgu_ref, wd32_ref, bd_ref, ys_ref,
                         wgu_ref, wd_ref, bgus_ref, bds_ref, xbuf_ref, ybuf_ref, gsem, ssem):
    i = pl.program_id(0)
    f = wd_ref.shape[0]
    blk = MOE_BLOCK
    p = i % 2
    q = 1 - p
    valid = blks_ref[0, i] == i
    tile_rows = blk * nt

    def gather(idx_ref, half):
        for r in range(blk):
            pltpu.make_async_copy(
                h2_ref.at[pl.ds(pl.multiple_of(idx_ref[0, 0, r], nt), nt), :],
                xbuf_ref.at[half, pl.ds(r * nt, nt), :], gsem.at[half]).start(priority=0)

    def scatter(idx_ref, half):
        for r in range(blk):
            pltpu.make_async_copy(
                ybuf_ref.at[half, pl.ds(r * nt, nt), :],
                ys_ref.at[pl.ds(pl.multiple_of(idx_ref[0, 0, r], nt), nt), :], ssem.at[half]).start(priority=1)

    def wait_gather(half):
        pltpu.make_async_copy(h2_ref.at[pl.ds(0, tile_rows), :], xbuf_ref.at[half], gsem.at[half]).wait()

    def wait_scatter(half):
        pltpu.make_async_copy(ybuf_ref.at[half], ys_ref.at[pl.ds(0, tile_rows), :], ssem.at[half]).wait()

    @pl.when(i == 0)
    def _():
        ybuf_ref[...] = jnp.zeros_like(ybuf_ref)
        spare0 = TOP_K * n_tok * nt
        for half in range(2):
            cp = pltpu.make_async_copy(ybuf_ref.at[half],
                                       ys_ref.at[pl.ds(spare0 + half * tile_rows, tile_rows), :], ssem.at[half])
            cp.start()
            cp.wait()
        gather(src0_ref, 0)

    @pl.when(jnp.logical_and(valid, jnp.logical_or(i == 0, blke_ref[0, i] != blke_ref[0, jnp.maximum(i - 1, 0)])))
    def _():
        wgu_ref[...] = wgu32_ref[0].astype(jnp.bfloat16)
        wd_ref[...] = wd32_ref[0].astype(jnp.bfloat16)
        bgus_ref[...] = bgu_ref[0]
        bds_ref[...] = bd_ref[0]

    @pl.when(valid)
    def _():
        wait_gather(p)
        x = jnp.concatenate([xbuf_ref[p, pl.ds(j, blk, stride=nt), :] for j in range(nt)],
                            axis=-1).astype(jnp.bfloat16)
        scatter(dstp_ref, q)
        gather(srcn_ref, q)
        gu = jnp.dot(x, wgu_ref[...], preferred_element_type=jnp.float32) + bgus_ref[...]
        g = jnp.minimum(gu[:, 0:f], SWIGLU_LIMIT)
        lin = jnp.clip(gu[:, f:2 * f], -SWIGLU_LIMIT, SWIGLU_LIMIT)
        act = g * _sigmoid(SWIGLU_ALPHA * g) * (lin + 1.0)
        y = jnp.dot(act.astype(jnp.bfloat16), wd_ref[...], preferred_element_type=jnp.float32) + bds_ref[...]

        @pl.when(i >= 1)
        def _():
            wait_scatter(p)

        for j in range(nt):
            ybuf_ref[p, pl.ds(j, blk, stride=nt), :] = y[:, j * LANES:(j + 1) * LANES]

    @pl.when(jnp.logical_and(jnp.logical_not(valid), blks_ref[0, jnp.maximum(i - 1, 0)] == i - 1))
    def _():
        scatter(dstp_ref, q)
        wait_scatter(q)
        wait_scatter(p)
        wait_gather(p)


def _fused_experts(blk_e, blk_s, src, dst, h2, w_gate_up, b_gate_up, w_down, b_down, n_tok, nt):
    ne, d, f2 = w_gate_up.shape
    f = f2 // 2
    n_blocks = src.shape[0]
    steps = n_blocks + 1
    assert blk_e.shape[1] == steps
    blk = MOE_BLOCK
    first = ((TOP_K * n_tok + blk + jnp.arange(blk, dtype=jnp.int32)) * nt).reshape(1, blk)
    src3 = src.reshape(n_blocks, 1, blk)
    dst3 = jnp.concatenate([dst, first], axis=0).reshape(n_blocks + 1, 1, blk)
    smem = pltpu.SMEM

    def w_map(i, be, bs):
        return (be[0, i], 0, 0)

    grid_spec = pltpu.PrefetchScalarGridSpec(
        num_scalar_prefetch=2,
        grid=(steps,),
        in_specs=[pl.BlockSpec((1, 1, blk), lambda i, be, bs: (0, 0, 0), memory_space=smem),
                  pl.BlockSpec((1, 1, blk), lambda i, be, bs: (bs[0, jnp.minimum(i + 1, steps - 1)], 0, 0),
                               memory_space=smem),
                  pl.BlockSpec((1, 1, blk),
                               lambda i, be, bs: (jnp.where(i == 0, n_blocks, bs[0, jnp.maximum(i - 1, 0)]), 0, 0),
                               memory_space=smem),
                  pl.BlockSpec(memory_space=pl.ANY),
                  pl.BlockSpec((1, d, f2), w_map), pl.BlockSpec((1, 1, f2), w_map),
                  pl.BlockSpec((1, f, d), w_map), pl.BlockSpec((1, 1, d), w_map)],
        out_specs=pl.BlockSpec(memory_space=pl.ANY),
        scratch_shapes=[pltpu.VMEM((d, f2), jnp.bfloat16), pltpu.VMEM((f, d), jnp.bfloat16),
                        pltpu.VMEM((1, f2), jnp.float32), pltpu.VMEM((1, d), jnp.float32),
                        pltpu.VMEM((2, blk * nt, LANES), jnp.float32), pltpu.VMEM((2, blk * nt, LANES), jnp.float32),
                        pltpu.SemaphoreType.DMA((2,)), pltpu.SemaphoreType.DMA((2,))],
    )
    out_rows = (TOP_K * n_tok + 2 * blk) * nt
    return pl.pallas_call(
        functools.partial(_fused_expert_kernel, nt, n_tok),
        grid_spec=grid_spec,
        out_shape=jax.ShapeDtypeStruct((out_rows, LANES), jnp.float32),
        compiler_params=pltpu.CompilerParams(
            dimension_semantics=("arbitrary",), vmem_limit_bytes=VMEM_LIMIT, has_side_effects=True),
        name="moe_experts",
    )(blk_e, blk_s, src3, src3, dst3, h2, w_gate_up, b_gate_up.reshape(ne, 1, f2), w_down,
      b_down.reshape(ne, 1, d))


def _final_kernel(nt, gate_ref, x1_ref, gfin_ref, y0_ref, y1_ref, y2_ref, y3_ref, out_ref):
    rows, d = x1_ref.shape
    gate = gate_ref[...]
    x = x1_ref[...]
    for k, y_ref in enumerate((y0_ref, y1_ref, y2_ref, y3_ref)):
        yk = jnp.concatenate([y_ref[pl.ds(j, rows, stride=nt), :] for j in range(nt)], axis=-1)
        x = x + gate[:, k:k + 1] * yk
    ms = jnp.mean(x * x, axis=-1, keepdims=True)
    out_ref[...] = x * lax.rsqrt(ms + NORM_EPS) * gfin_ref[...]


def _final(gate, x1, g_final, ys, nt):
    t, d = x1.shape
    rows = COMBINE_ROWS
    assert t % rows == 0 and TOP_K == 4
    per_slot = t // rows

    def slot_spec(k):
        return pl.BlockSpec((rows * nt, LANES), lambda i: (k * per_slot + i, 0))

    return pl.pallas_call(
        functools.partial(_final_kernel, nt),
        grid=(per_slot,),
        in_specs=[pl.BlockSpec((rows, TOP_K), lambda i: (i, 0)),
                  pl.BlockSpec((rows, d), lambda i: (i, 0)),
                  _const_spec(g_final.shape)] + [slot_spec(k) for k in range(TOP_K)],
        out_specs=pl.BlockSpec((rows, d), lambda i: (i, 0)),
        out_shape=jax.ShapeDtypeStruct((t, d), jnp.float32),
        compiler_params=pltpu.CompilerParams(
            dimension_semantics=("arbitrary",), vmem_limit_bytes=VMEM_LIMIT),
        name="moe_final",
    )(gate, x1, g_final, ys, ys, ys, ys)


def _block_diag(blocks):
    g, r, c = blocks.shape
    eye = jnp.eye(g, dtype=blocks.dtype)
    return (blocks[:, :, None, :] * eye[:, None, :, None]).reshape(g * r, g * c)


def kernel(x, norm_mix_g, w_in, conv_w, conv_b, conv_ln_g, conv_ln_b, w_conv_out, ssm_a_re, ssm_a_im,
           ssm_log_step, ssm_b_re, ssm_b_im, ssm_c_re, ssm_c_im, ssm_d, w_ssm_glu, b_ssm_glu, w_ssm_out,
           w_out, norm_ffn_g, w_router, b_router, w_gate_up, b_gate_up, w_down, b_down, norm_final_g):
    bsz, seq, d = x.shape
    depth = w_in.shape[0]
    bf = jnp.bfloat16
    nt = d // LANES
    n_tok = bsz * seq
    ne = w_router.shape[-1]
    n_blocks = -(-n_tok * TOP_K // MOE_BLOCK) + ne
    seg_len = MIX_ROWS // SCAN_SEGS

    def row(v):
        return v.reshape(1, -1)

    for l in range(depth):
        ar, ai, arp, aip, bbr, bbi = _ssm_disc(
            ssm_a_re[l], ssm_a_im[l], ssm_log_step[l],
            jnp.swapaxes(ssm_b_re[l], 1, 2), jnp.swapaxes(ssm_b_im[l], 1, 2), seg_len)
        bblk = jnp.concatenate([_block_diag(bbr), _block_diag(bbi)], axis=1).astype(bf)
        c_r = _block_diag(jnp.swapaxes(ssm_c_re[l], 1, 2)).astype(bf)
        c_i = _block_diag(jnp.swapaxes(ssm_c_im[l], 1, 2)).astype(bf)
        avec = jnp.stack([ar.reshape(-1), ai.reshape(-1), arp.reshape(-1), aip.reshape(-1)])

        x = _mixer(x, row(norm_mix_g[l]), w_in[l].astype(bf), conv_w[l], row(conv_b[l]),
                   row(conv_ln_g[l]), row(conv_ln_b[l]), w_conv_out[l].astype(bf), bblk, c_r, c_i, avec,
                   row(ssm_d[l]), w_ssm_glu[l].astype(bf), row(b_ssm_glu[l]), w_ssm_out[l].astype(bf),
                   w_out[l].astype(bf))

        x1 = x.reshape(n_tok, d)
        h2, loc, gate, run, counts = _route(x1, row(norm_ffn_g[l]), w_router[l], row(b_router[l]))
        poff, blk_e, blk_s = _meta(counts, n_blocks)
        runs = _runs(poff, run)
        xs = _dispatch(counts, poff, loc, runs, h2, n_blocks * MOE_BLOCK, nt)
        ys = _experts(blk_e, blk_s, xs, w_gate_up[l], b_gate_up[l], w_down[l], b_down[l], nt)
        assert depth == 1
        x = _combine(loc, gate, runs, x1, row(norm_final_g), ys, nt).reshape(bsz, seq, d)
    return x
```

```python
import functools
import math

import jax
import jax.numpy as jnp
from jax import lax
from jax.experimental import pallas as pl
from jax.experimental.pallas import tpu as pltpu

NORM_EPS = 1e-6
SWIGLU_LIMIT = 7.0
SWIGLU_ALPHA = 1.702
TOP_K = 4
SSM_GROUP = 16

LANES = 128
SUBLANES = 8

MIX_ROWS = 512
SCAN_SEGS = SUBLANES
SEG_PAD = 8
CONV_HALO = 32
CONV_ROWS = 64
ROUTE_ROWS = 512
MOE_BLOCK = 512
VMEM_LIMIT = 56 * 1024 * 1024


def _sigmoid(x):
    return 1.0 / (1.0 + jnp.exp(-x))


def _gelu_tanh(x):
    c = math.sqrt(2.0 / math.pi)
    return 0.5 * x * (1.0 + jnp.tanh(c * (x + 0.044715 * (x * x * x))))


def _const_spec(shape):
    nd = len(shape)
    return pl.BlockSpec(shape, lambda *_: (0,) * nd)


def _ssm_disc_kernel(seg_len, are_ref, aim_ref, ls_ref, bre_ref, bim_ref,
                     ar_ref, ai_ref, arp_ref, aip_ref, bbr_ref, bbi_ref):
    lr = are_ref[...]
    li = aim_ref[...]
    dt = jnp.exp(ls_ref[...])
    mag = jnp.exp(lr * dt)
    ar = mag * jnp.cos(li * dt)
    ai = mag * jnp.sin(li * dt)
    den = lr * lr + li * li
    fr = ((ar - 1.0) * lr + ai * li) / den
    fi = (ai * lr - (ar - 1.0) * li) / den
    ar_ref[...] = ar
    ai_ref[...] = ai
    magp = jnp.exp(lr * dt * seg_len)
    arp_ref[...] = magp * jnp.cos(li * dt * seg_len)
    aip_ref[...] = magp * jnp.sin(li * dt * seg_len)
    br = bre_ref[...]
    bi = bim_ref[...]
    bbr_ref[...] = fr[:, None, :] * br - fi[:, None, :] * bi
    bbi_ref[...] = fr[:, None, :] * bi + fi[:, None, :] * br


def _ssm_disc(a_re, a_im, log_step, bt_re, bt_im, seg_len):
    g, p = a_re.shape
    h = bt_re.shape[1]
    f32 = jnp.float32
    out_shape = [jax.ShapeDtypeStruct((g, p), f32)] * 4 + [jax.ShapeDtypeStruct((g, h, p), f32)] * 2
    return pl.pallas_call(
        functools.partial(_ssm_disc_kernel, float(seg_len)),
        out_shape=out_shape,
        name="ssm_disc",
    )(a_re, a_im, log_step.reshape(g, 1), bt_re, bt_im)


def _mixer_kernel(dims, x_ref, gmix_ref, win_ref, convw_ref, convb_ref, lng_ref, lnb_ref, wco_ref,
                  bblk_ref, cr_ref, ci_ref, avec_ref, dskip_ref, wglu_ref, bglu_ref, wso_ref, wout_ref,
                  x1_ref, vbuf_ref, conv_ref, scan_ref, xst_ref, cin_ref, st_ref):
    d, dc, ds, gp, cw = dims
    rows = MIX_ROWS
    seg = rows // SCAN_SEGS
    pitch = seg + SEG_PAD
    nsl = gp // LANES
    c_idx = pl.program_id(1)

    half = rows // 2
    tail = 2 * half + 1

    @pl.when(c_idx == 0)
    def _():
        for lc in range(dc // LANES):
            vbuf_ref[lc, pl.ds(0, CONV_HALO, stride=2), :] = jnp.zeros((CONV_HALO, LANES), jnp.float32)
        st_ref[...] = jnp.zeros_like(st_ref)

    @pl.when(c_idx != 0)
    def _():
        for lc in range(dc // LANES):
            vbuf_ref[lc, pl.ds(0, CONV_HALO, stride=2), :] = vbuf_ref[lc, pl.ds(tail, CONV_HALO, stride=2), :]

    x = x_ref[0]
    ms = jnp.mean(x * x, axis=-1, keepdims=True)
    h = (x * lax.rsqrt(ms + NORM_EPS) * gmix_ref[...]).astype(jnp.bfloat16)

    pa = jnp.dot(h, win_ref[:, 0:2 * dc], preferred_element_type=jnp.float32)
    vglu = pa[:, 0:dc] * _sigmoid(pa[:, dc:2 * dc])
    for lc in range(dc // LANES):
        ls = slice(lc * LANES, (lc + 1) * LANES)
        vbuf_ref[lc, pl.ds(2 * CONV_HALO, half, stride=2), :] = vglu[0:half, ls]
        vbuf_ref[lc, pl.ds(2 * CONV_HALO + 1, half, stride=2), :] = vglu[half:rows, ls]
        vbuf_ref[lc, pl.ds(1, CONV_HALO, stride=2), :] = vglu[half - CONV_HALO:half, ls]

    base = CONV_HALO - (cw - 1)
    nv = CONV_ROWS // SUBLANES

    def conv_chunk(rc, p):
        r0 = rc * CONV_ROWS
        for lc in range(dc // LANES):
            ls = slice(lc * LANES, (lc + 1) * LANES)
            wv = [jnp.broadcast_to(convw_ref[k:k + 1, ls], (SUBLANES, LANES)) for k in range(cw)]
            acc = [jnp.zeros((SUBLANES, LANES), jnp.float32) for _ in range(nv)]
            for s in range(CONV_ROWS - SUBLANES + cw):
                win = vbuf_ref[lc, pl.ds(2 * (r0 + base + s) + p, SUBLANES, stride=2), :]
                for i in range(nv):
                    k = s - SUBLANES * i
                    if 0 <= k < cw:
                        acc[i] = acc[i] + wv[k] * win
            for i in range(nv):
                conv_ref[pl.ds(p * half + r0 + SUBLANES * i, SUBLANES), ls] = acc[i]

    g0 = 2 * dc + ds
    chunks = [(rc, p) for rc in range(half // CONV_ROWS) for p in range(2)]
    gw = 2 * d // len(chunks)
    sg = []
    for n, (rc, p) in enumerate(chunks):
        sg.append(_sigmoid(jnp.dot(h, win_ref[:, g0 + n * gw:g0 + (n + 1) * gw],
                                   preferred_element_type=jnp.float32)))
        conv_chunk(rc, p)
    sg = jnp.concatenate(sg, axis=-1)
    sg_conv = sg[:, 0:d]
    sg_ssm = sg[:, d:2 * d]

    v = conv_ref[...] + convb_ref[...]
    mu = jnp.mean(v, axis=-1, keepdims=True)
    vc = v - mu
    var = jnp.mean(vc * vc, axis=-1, keepdims=True)
    v = vc * lax.rsqrt(var + NORM_EPS) * lng_ref[...] + lnb_ref[...]
    v = v * _sigmoid(v)
    y_conv = jnp.dot(v.astype(jnp.bfloat16), wco_ref[...], preferred_element_type=jnp.float32)

    u = jnp.dot(h, win_ref[:, 2 * dc:2 * dc + ds], preferred_element_type=jnp.float32)
    bu = jnp.dot(u.astype(jnp.bfloat16), bblk_ref[...], preferred_element_type=jnp.float32)
    for n in range(2 * nsl):
        for s in range(SCAN_SEGS):
            scan_ref[n, s * pitch:s * pitch + seg, :] = bu[s * seg:(s + 1) * seg, n * LANES:(n + 1) * LANES]

    def bcast(row, n):
        return jnp.broadcast_to(avec_ref[row:row + 1, n * LANES:(n + 1) * LANES], (SUBLANES, LANES))

    a_r = [bcast(0, n) for n in range(nsl)]
    a_i = [bcast(1, n) for n in range(nsl)]

    def step(j, st, store):
        out = []
        for n in range(nsl):
            sr, si = st[2 * n], st[2 * n + 1]
            br = scan_ref[n, pl.ds(j, SCAN_SEGS, stride=pitch), :]
            bi = scan_ref[nsl + n, pl.ds(j, SCAN_SEGS, stride=pitch), :]
            nr = a_r[n] * sr - a_i[n] * si + br
            ni = a_r[n] * si + a_i[n] * sr + bi
            if store:
                scan_ref[n, pl.ds(j, SCAN_SEGS, stride=pitch), :] = nr
                scan_ref[nsl + n, pl.ds(j, SCAN_SEGS, stride=pitch), :] = ni
            out += [nr, ni]
        return tuple(out)

    zero = jnp.zeros((SUBLANES, LANES), jnp.float32)
    fin = lax.fori_loop(0, seg, functools.partial(step, store=False), (zero,) * (2 * nsl))

    for n in range(nsl):
        fr, fi = fin[2 * n], fin[2 * n + 1]
        ap_r = avec_ref[2:3, n * LANES:(n + 1) * LANES]
        ap_i = avec_ref[3:4, n * LANES:(n + 1) * LANES]
        c_r = st_ref[0:1, n * LANES:(n + 1) * LANES]
        c_i = st_ref[1:2, n * LANES:(n + 1) * LANES]
        for s in range(SCAN_SEGS):
            cin_ref[2 * n, s:s + 1, :] = c_r
            cin_ref[2 * n + 1, s:s + 1, :] = c_i
            n_r = ap_r * c_r - ap_i * c_i + fr[s:s + 1, :]
            n_i = ap_r * c_i + ap_i * c_r + fi[s:s + 1, :]
            c_r, c_i = n_r, n_i
        st_ref[0:1, n * LANES:(n + 1) * LANES] = c_r
        st_ref[1:2, n * LANES:(n + 1) * LANES] = c_i

    lax.fori_loop(0, seg, functools.partial(step, store=True), tuple(cin_ref[q] for q in range(2 * nsl)))

    for n in range(2 * nsl):
        for s in range(SCAN_SEGS):
            xst_ref[s * seg:(s + 1) * seg, n * LANES:(n + 1) * LANES] = (
                scan_ref[n, s * pitch:s * pitch + seg, :].astype(jnp.bfloat16))

    y = (jnp.dot(xst_ref[:, 0:gp], cr_ref[...], preferred_element_type=jnp.float32)
         - jnp.dot(xst_ref[:, gp:2 * gp], ci_ref[...], preferred_element_type=jnp.float32))
    y = _gelu_tanh(y + dskip_ref[...] * u)
    glu = jnp.dot(y.astype(jnp.bfloat16), wglu_ref[...], preferred_element_type=jnp.float32) + bglu_ref[...]
    y = y * _sigmoid(glu)
    y_ssm = jnp.dot(y.astype(jnp.bfloat16), wso_ref[...], preferred_element_type=jnp.float32)

    m = sg_conv * y_conv + sg_ssm * y_ssm
    x1_ref[0] = x + jnp.dot(m.astype(jnp.bfloat16), wout_ref[...], preferred_element_type=jnp.float32)


def _mixer(x, gmix, w_in, conv_w, conv_b, ln_g, ln_b, w_co, bblk, c_r, c_i, avec, d_skip,
           w_glu, b_glu, w_so, w_out):
    b, s, d = x.shape
    cw, dc = conv_w.shape
    ds = d_skip.shape[-1]
    gp = c_r.shape[0]
    rows = MIX_ROWS
    seg = rows // SCAN_SEGS
    pitch = seg + SEG_PAD
    assert s % rows == 0 and cw - 1 <= CONV_HALO and gp % LANES == 0 and dc % LANES == 0
    dims = (d, dc, ds, gp, cw)
    consts = [gmix, w_in, conv_w, conv_b, ln_g, ln_b, w_co, bblk, c_r, c_i, avec, d_skip,
              w_glu, b_glu, w_so, w_out]
    return pl.pallas_call(
        functools.partial(_mixer_kernel, dims),
        grid=(b, s // rows),
        in_specs=[pl.BlockSpec((1, rows, d), lambda i, j: (i, j, 0))] + [_const_spec(c.shape) for c in consts],
        out_specs=pl.BlockSpec((1, rows, d), lambda i, j: (i, j, 0)),
        out_shape=jax.ShapeDtypeStruct((b, s, d), jnp.float32),
        scratch_shapes=[
            pltpu.VMEM((dc // LANES, 2 * CONV_HALO + rows, LANES), jnp.float32),
            pltpu.VMEM((rows, dc), jnp.float32),
            pltpu.VMEM((2 * gp // LANES, SCAN_SEGS * pitch, LANES), jnp.float32),
            pltpu.VMEM((rows, 2 * gp), jnp.bfloat16),
            pltpu.VMEM((2 * gp // LANES, SUBLANES, LANES), jnp.float32),
            pltpu.VMEM((2, gp), jnp.float32),
        ],
        compiler_params=pltpu.CompilerParams(
            dimension_semantics=("arbitrary", "arbitrary"), vmem_limit_bytes=VMEM_LIMIT),
        name="mixer",
    )(x, *consts)


def _route_kernel(x1_ref, g_ref, wr_ref, br_ref, h2_ref, loc_ref, gate_ref, run_ref, cnt_ref, carry_ref):
    rows, d = x1_ref.shape
    ne = wr_ref.shape[1]

    @pl.when(pl.program_id(0) == 0)
    def _():
        carry_ref[...] = jnp.zeros_like(carry_ref)

    x = x1_ref[...]
    ms = jnp.mean(x * x, axis=-1, keepdims=True)
    h2 = x * lax.rsqrt(ms + NORM_EPS) * g_ref[...]
    for j in range(d // LANES):
        h2_ref[pl.ds(j, rows, stride=d // LANES), :] = h2[:, j * LANES:(j + 1) * LANES]

    bf = jnp.bfloat16
    h_hi = h2.astype(bf)
    h_lo = (h2 - h_hi.astype(jnp.float32)).astype(bf)
    w = wr_ref[...]
    w_hi = w.astype(bf)
    w_lo = (w - w_hi.astype(jnp.float32)).astype(bf)
    logits = (jnp.dot(h_hi, w_hi, preferred_element_type=jnp.float32)
              + jnp.dot(h_lo, w_hi, preferred_element_type=jnp.float32)
              + jnp.dot(h_hi, w_lo, preferred_element_type=jnp.float32)) + br_ref[...]

    lane = lax.broadcasted_iota(jnp.int32, (rows, ne), 1).astype(jnp.float32)
    slot = lax.broadcasted_iota(jnp.int32, (rows, TOP_K), 1)
    work = logits
    sels, vals = [], []
    for k in range(TOP_K):
        mx = jnp.max(work, axis=-1, keepdims=True)
        idx = jnp.min(jnp.where(work == mx, lane, float(ne)), axis=-1, keepdims=True)
        sel = lane == idx
        sels.append(sel)
        vals.append(mx)
        work = jnp.where(sel, -jnp.inf, work)

    exps = [jnp.exp(v - vals[0]) for v in vals]
    tot = exps[0]
    for e in exps[1:]:
        tot = tot + e
    gate = jnp.zeros((rows, TOP_K), jnp.float32)
    for k in range(TOP_K):
        gate = jnp.where(slot == k, exps[k] / tot, gate)
    gate_ref[...] = gate

    cnt = jnp.zeros((rows, ne), jnp.float32)
    for sel in sels:
        cnt = cnt + sel.astype(jnp.float32)
    r_i = lax.broadcasted_iota(jnp.int32, (rows, rows), 0)
    c_i = lax.broadcasted_iota(jnp.int32, (rows, rows), 1)
    tril = (c_i < r_i).astype(bf)
    before = jnp.dot(tril, cnt.astype(bf), preferred_element_type=jnp.float32)
    tile_cnt = jnp.sum(cnt, axis=0, keepdims=True)
    e_r = lax.broadcasted_iota(jnp.int32, (ne, ne), 0)
    e_c = lax.broadcasted_iota(jnp.int32, (ne, ne), 1)
    cnt_col = jnp.sum(jnp.where(e_c == e_r, jnp.broadcast_to(tile_cnt, (ne, ne)), 0.0), axis=1, keepdims=True)
    first = jnp.sum(jnp.where(e_r < e_c, jnp.broadcast_to(cnt_col, (ne, ne)), 0.0), axis=0, keepdims=True)
    pos = before + first
    loc = jnp.zeros((rows, TOP_K), jnp.float32)
    for k in range(TOP_K):
        lk = jnp.sum(jnp.where(sels[k], pos, 0.0), axis=-1, keepdims=True)
        loc = jnp.where(slot == k, lk, loc)
    loc_ref[...] = loc.astype(jnp.int32) * (d // LANES)
    run_ref[...] = jnp.zeros_like(run_ref)
    run_ref[0, 0:1, :] = tile_cnt.astype(jnp.int32)
    run_ref[0, 1:2, :] = first.astype(jnp.int32)
    run_ref[0, 2:3, :] = carry_ref[...].astype(jnp.int32)
    carry_ref[...] = carry_ref[...] + tile_cnt
    cnt_ref[...] = carry_ref[...].astype(jnp.int32)


def _route(x1, g_ffn, w_router, b_router):
    t, d = x1.shape
    ne = w_router.shape[1]
    rows = ROUTE_ROWS
    assert t % rows == 0 and d % LANES == 0
    nt = d // LANES
    return pl.pallas_call(
        _route_kernel,
        grid=(t // rows,),
        in_specs=[pl.BlockSpec((rows, d), lambda i: (i, 0)),
                  _const_spec(g_ffn.shape), _const_spec(w_router.shape), _const_spec(b_router.shape)],
        out_specs=[pl.BlockSpec((rows * nt, LANES), lambda i: (i, 0)),
                   pl.BlockSpec((rows, TOP_K), lambda i: (i, 0)),
                   pl.BlockSpec((rows, TOP_K), lambda i: (i, 0)),
                   pl.BlockSpec((1, SUBLANES, ne), lambda i: (i, 0, 0)),
                   _const_spec((1, ne))],
        out_shape=[jax.ShapeDtypeStruct((t * nt, LANES), jnp.float32),
                   jax.ShapeDtypeStruct((t, TOP_K), jnp.int32),
                   jax.ShapeDtypeStruct((t, TOP_K), jnp.float32),
                   jax.ShapeDtypeStruct((t // rows, SUBLANES, ne), jnp.int32),
                   jax.ShapeDtypeStruct((1, ne), jnp.int32)],
        scratch_shapes=[pltpu.VMEM((1, ne), jnp.float32)],
        compiler_params=pltpu.CompilerParams(
            dimension_semantics=("arbitrary",), vmem_limit_bytes=VMEM_LIMIT),
        name="route",
    )(x1, g_ffn, w_router, b_router)


def _meta_kernel(cnt_ref, poff_ref, blke_ref, blks_ref):
    ne = cnt_ref.shape[1]
    nb = blke_ref.shape[1]
    f32 = jnp.float32
    cnt = cnt_ref[...].astype(f32)
    nblk = jnp.floor((cnt + (MOE_BLOCK - 1)) * (1.0 / MOE_BLOCK))
    e_r = lax.broadcasted_iota(jnp.int32, (ne, ne), 0)
    e_c = lax.broadcasted_iota(jnp.int32, (ne, ne), 1)
    nb_rows = jnp.broadcast_to(nblk, (ne, ne))
    bend_col = jnp.sum(jnp.where(e_c <= e_r, nb_rows, 0.0), axis=1, keepdims=True)
    nb_col = jnp.sum(jnp.where(e_c == e_r, nb_rows, 0.0), axis=1, keepdims=True)
    boff = jnp.sum(jnp.where(e_r < e_c, jnp.broadcast_to(nb_col, (ne, ne)), 0.0),
                   axis=0, keepdims=True)
    poff_ref[...] = (boff * MOE_BLOCK).astype(jnp.int32)
    total = jnp.sum(nblk, axis=1, keepdims=True)
    blk = lax.broadcasted_iota(jnp.int32, (ne, nb), 1).astype(f32)
    done = jnp.where(jnp.broadcast_to(bend_col, (ne, nb)) <= blk, 1.0, 0.0)
    be = jnp.minimum(jnp.sum(done, axis=0, keepdims=True), ne - 1.0)
    bid = lax.broadcasted_iota(jnp.int32, (1, nb), 1).astype(f32)
    last_e = jnp.sum(jnp.where(bid == total - 1.0, be, 0.0), axis=1, keepdims=True)
    blke_ref[...] = jnp.where(bid < total, be, last_e).astype(jnp.int32)
    blks_ref[...] = jnp.minimum(bid, total - 1.0).astype(jnp.int32)


def _meta(counts, n_blocks):
    ne = counts.shape[1]
    return pl.pallas_call(
        _meta_kernel,
        out_shape=[jax.ShapeDtypeStruct((1, ne), jnp.int32),
                   jax.ShapeDtypeStruct((1, n_blocks), jnp.int32),
                   jax.ShapeDtypeStruct((1, n_blocks), jnp.int32)],
        name="moe_meta",
    )(counts)


def _runs_kernel(poff_ref, run_ref, out_ref):
    run = run_ref[...]
    row = lax.broadcasted_iota(jnp.int32, run.shape, 1)
    out_ref[...] = jnp.where(row == 2, run + poff_ref[...][None], run)


def _runs(poff, run):
    return pl.pallas_call(
        _runs_kernel,
        out_shape=jax.ShapeDtypeStruct(run.shape, jnp.int32),
        name="moe_runs",
    )(poff, run)


def _tile(ref, r, nt):
    return ref.at[pl.ds(pl.multiple_of(r * nt, nt), nt), :]


def _run_copies(run_ref, src_of, dst_of, sem, nt):
    ne = run_ref.shape[2]

    def per_expert(e, carry):
        n = run_ref[0, 0, e]

        @pl.when(n > 0)
        def _():
            pltpu.make_async_copy(src_of(run_ref[0, 1, e], run_ref[0, 2, e], n),
                                  dst_of(run_ref[0, 1, e], run_ref[0, 2, e], n), sem).start()
        return carry

    lax.fori_loop(0, ne, per_expert, 0)


def _dispatch_kernel(nt, cnt_ref, poff_ref, loc_ref, run_ref, h2_ref, xs_ref, stage_ref, zero_ref, sem, zsem):
    step = pl.program_id(0)
    last = pl.num_programs(0) - 1
    rows = h2_ref.shape[0] // nt
    n_assign = rows * TOP_K
    ne = cnt_ref.shape[1]
    par = step % 2
    unroll = 4

    def rows_at(ref, first, n):
        return ref.at[pl.ds(pl.multiple_of(first * nt, nt), n * nt), :]

    def wait_runs(half):
        pltpu.make_async_copy(stage_ref.at[half], xs_ref.at[pl.ds(0, n_assign * nt), :], sem.at[half]).wait()

    @pl.when(step == 0)
    def _():
        zero_ref[...] = jnp.zeros_like(zero_ref)

        def per_expert(e, carry):
            c = cnt_ref[0, e]
            pad = (MOE_BLOCK - (c % MOE_BLOCK)) % MOE_BLOCK

            @pl.when(pad > 0)
            def _():
                cp = pltpu.make_async_copy(rows_at(zero_ref, 0, pad), rows_at(xs_ref, poff_ref[0, e] + c, pad), zsem)
                cp.start()
                cp.wait()
            return poff_ref[0, e] + c + pad

        used = lax.fori_loop(0, ne, per_expert, 0)

        def spare(b, carry):
            cp = pltpu.make_async_copy(zero_ref, rows_at(xs_ref, b * MOE_BLOCK, MOE_BLOCK), zsem)
            cp.start()
            cp.wait()
            return carry

        lax.fori_loop(used // MOE_BLOCK, xs_ref.shape[0] // (MOE_BLOCK * nt), spare, 0)

    @pl.when(step >= 2)
    def _():
        wait_runs(par)

    def place_into(half):
        def place(i, carry):
            for u in range(unroll):
                tok = i * unroll + u
                row = _tile(h2_ref, tok, nt)[...]
                for k in range(TOP_K):
                    stage_ref[half, pl.ds(pl.multiple_of(loc_ref[0, 0, tok * TOP_K + k], nt), nt), :] = row
            return carry

        lax.fori_loop(0, rows // unroll, place, 0)

    for half in range(2):
        pl.when(par == half)(functools.partial(place_into, half))

    _run_copies(run_ref,
                lambda first, base, n: stage_ref.at[par, pl.ds(pl.multiple_of(first * nt, nt), n * nt), :],
                lambda first, base, n: rows_at(xs_ref, base, n), sem.at[par], nt)

    @pl.when(step == last)
    def _():
        wait_runs(par)

        @pl.when(step >= 1)
        def _():
            wait_runs(1 - par)


def _dispatch(counts, poff, loc, runs, h2, n_rows, nt):
    t = h2.shape[0] // nt
    rows = ROUTE_ROWS
    tiles = t // rows
    ne = counts.shape[1]
    smem = pl.BlockSpec(memory_space=pltpu.SMEM)
    return pl.pallas_call(
        functools.partial(_dispatch_kernel, nt),
        grid=(tiles,),
        in_specs=[smem, smem,
                  pl.BlockSpec((1, 1, rows * TOP_K), lambda i: (i, 0, 0), memory_space=pltpu.SMEM),
                  pl.BlockSpec((1, SUBLANES, ne), lambda i: (i, 0, 0), memory_space=pltpu.SMEM),
                  pl.BlockSpec((rows * nt, LANES), lambda i: (i, 0))],
        out_specs=pl.BlockSpec(memory_space=pl.ANY),
        out_shape=jax.ShapeDtypeStruct((n_rows * nt, LANES), jnp.float32),
        scratch_shapes=[pltpu.VMEM((2, rows * TOP_K * nt, LANES), jnp.float32),
                        pltpu.VMEM((MOE_BLOCK * nt, LANES), jnp.float32),
                        pltpu.SemaphoreType.DMA((2,)), pltpu.SemaphoreType.DMA(())],
        compiler_params=pltpu.CompilerParams(
            dimension_semantics=("arbitrary",), vmem_limit_bytes=VMEM_LIMIT, has_side_effects=True),
        name="moe_dispatch",
    )(counts, poff, loc.reshape(tiles, 1, rows * TOP_K), runs, h2)


def _expert_kernel(nt, blke_ref, blks_ref, xs_ref, wgu32_ref, bgu_ref, wd32_ref, bd_ref, ys_ref,
                   wgu_ref, wd_ref):
    i = pl.program_id(0)
    f = wd_ref.shape[0]

    @pl.when(jnp.logical_or(i == 0, blke_ref[0, i] != blke_ref[0, jnp.maximum(i - 1, 0)]))
    def _():
        wgu_ref[...] = wgu32_ref[0].astype(jnp.bfloat16)
        wd_ref[...] = wd32_ref[0].astype(jnp.bfloat16)

    @pl.when(blks_ref[0, i] != i)
    def _():
        ys_ref[...] = jnp.zeros_like(ys_ref)

    @pl.when(blks_ref[0, i] == i)
    def _():
        x = jnp.concatenate([xs_ref[pl.ds(j, MOE_BLOCK, stride=nt), :] for j in range(nt)], axis=-1)
        gu = jnp.dot(x.astype(jnp.bfloat16), wgu_ref[...], preferred_element_type=jnp.float32) + bgu_ref[0]
        g = jnp.minimum(gu[:, 0:f], SWIGLU_LIMIT)
        lin = jnp.clip(gu[:, f:2 * f], -SWIGLU_LIMIT, SWIGLU_LIMIT)
        act = g * _sigmoid(SWIGLU_ALPHA * g) * (lin + 1.0)
        y = jnp.dot(act.astype(jnp.bfloat16), wd_ref[...], preferred_element_type=jnp.float32) + bd_ref[0]
        for j in range(nt):
            ys_ref[pl.ds(j, MOE_BLOCK, stride=nt), :] = y[:, j * LANES:(j + 1) * LANES]


def _experts(blk_e, blk_s, xs, w_gate_up, b_gate_up, w_down, b_down, nt):
    ne, d, f2 = w_gate_up.shape
    f = f2 // 2
    n_blocks = blk_e.shape[1]
    blk_rows = MOE_BLOCK * nt
    grid_spec = pltpu.PrefetchScalarGridSpec(
        num_scalar_prefetch=2,
        grid=(n_blocks,),
        in_specs=[pl.BlockSpec((blk_rows, LANES), lambda i, be, bs: (bs[0, i], 0)),
                  pl.BlockSpec((1, d, f2), lambda i, be, bs: (be[0, i], 0, 0)),
                  pl.BlockSpec((1, 1, f2), lambda i, be, bs: (be[0, i], 0, 0)),
                  pl.BlockSpec((1, f, d), lambda i, be, bs: (be[0, i], 0, 0)),
                  pl.BlockSpec((1, 1, d), lambda i, be, bs: (be[0, i], 0, 0))],
        out_specs=pl.BlockSpec((blk_rows, LANES), lambda i, be, bs: (i, 0)),
        scratch_shapes=[pltpu.VMEM((d, f2), jnp.bfloat16), pltpu.VMEM((f, d), jnp.bfloat16)],
    )
    return pl.pallas_call(
        functools.partial(_expert_kernel, nt),
        grid_spec=grid_spec,
        out_shape=jax.ShapeDtypeStruct(xs.shape, jnp.float32),
        compiler_params=pltpu.CompilerParams(
            dimension_semantics=("arbitrary",), vmem_limit_bytes=VMEM_LIMIT),
        name="moe_experts",
    )(blk_e, blk_s, xs, w_gate_up, b_gate_up.reshape(ne, 1, f2), w_down, b_down.reshape(ne, 1, d))


def _combine_kernel(nt, loc_ref, gate_ref, run_ref, runn_ref, x1_ref, gfin_ref, ys_ref, out_ref,
                    stage_ref, acc_ref, sem):
    rows, d = x1_ref.shape
    n_assign = rows * TOP_K
    step = pl.program_id(0)
    par = step % 2
    unroll = 4

    def fetch(r_ref, half):
        _run_copies(r_ref,
                    lambda first, base, n: ys_ref.at[pl.ds(pl.multiple_of(base * nt, nt), n * nt), :],
                    lambda first, base, n: stage_ref.at[half, pl.ds(pl.multiple_of(first * nt, nt), n * nt), :],
                    sem.at[half], nt)

    @pl.when(step == 0)
    def _():
        fetch(run_ref, 0)

    @pl.when(step + 1 < pl.num_programs(0))
    def _():
        fetch(runn_ref, 1 - par)

    pltpu.make_async_copy(ys_ref.at[pl.ds(0, n_assign * nt), :], stage_ref.at[par], sem.at[par]).wait()

    def mix_from(half):
        def mix(i, carry):
            for u in range(unroll):
                tok = i * unroll + u
                acc = None
                for k in range(TOP_K):
                    a = tok * TOP_K + k
                    row = stage_ref[half, pl.ds(pl.multiple_of(loc_ref[0, 0, a], nt), nt), :]
                    term = gate_ref[0, 0, a] * row
                    acc = term if acc is None else acc + term
                acc_ref[pl.ds(pl.multiple_of(tok * nt, nt), nt), :] = acc
            return carry

        lax.fori_loop(0, rows // unroll, mix, 0)

    for half in range(2):
        pl.when(par == half)(functools.partial(mix_from, half))

    y = jnp.concatenate([acc_ref[pl.ds(j, rows, stride=nt), :] for j in range(nt)], axis=-1)
    x = x1_ref[...] + y
    ms = jnp.mean(x * x, axis=-1, keepdims=True)
    out_ref[...] = x * lax.rsqrt(ms + NORM_EPS) * gfin_ref[...]


def _combine(loc, gate, runs, x1, g_final, ys, nt):
    t, d = x1.shape
    rows = ROUTE_ROWS
    tiles = t // rows
    ne = runs.shape[2]
    n_assign = rows * TOP_K
    smem = pltpu.SMEM
    return pl.pallas_call(
        functools.partial(_combine_kernel, nt),
        grid=(tiles,),
        in_specs=[pl.BlockSpec((1, 1, n_assign), lambda i: (i, 0, 0), memory_space=smem),
                  pl.BlockSpec((1, 1, n_assign), lambda i: (i, 0, 0), memory_space=smem),
                  pl.BlockSpec((1, SUBLANES, ne), lambda i: (i, 0, 0), memory_space=smem),
                  pl.BlockSpec((1, SUBLANES, ne), lambda i: (jnp.minimum(i + 1, tiles - 1), 0, 0),
                               memory_space=smem),
                  pl.BlockSpec((rows, d), lambda i: (i, 0)),
                  _const_spec(g_final.shape),
                  pl.BlockSpec(memory_space=pl.ANY)],
        out_specs=pl.BlockSpec((rows, d), lambda i: (i, 0)),
        out_shape=jax.ShapeDtypeStruct((t, d), jnp.float32),
        scratch_shapes=[pltpu.VMEM((2, n_assign * nt, LANES), jnp.float32),
                        pltpu.VMEM((rows * nt, LANES), jnp.float32),
                        pltpu.SemaphoreType.DMA((2,))],
        compiler_params=pltpu.CompilerParams(
            dimension_semantics=("arbitrary",), vmem_limit_bytes=VMEM_LIMIT),
        name="moe_combine",
    )(loc.reshape(tiles, 1, n_assign), gate.reshape(tiles, 1, n_assign), runs, runs, x1, g_final, ys)


def _block_diag(blocks):
    g, r, c = blocks.shape
    eye = jnp.eye(g, dtype=blocks.dtype)
    return (blocks[:, :, None, :] * eye[:, None, :, None]).reshape(g * r, g * c)


def kernel(x, norm_mix_g, w_in, conv_w, conv_b, conv_ln_g, conv_ln_b, w_conv_out, ssm_a_re, ssm_a_im,
           ssm_log_step, ssm_b_re, ssm_b_im, ssm_c_re, ssm_c_im, ssm_d, w_ssm_glu, b_ssm_glu, w_ssm_out,
           w_out, norm_ffn_g, w_router, b_router, w_gate_up, b_gate_up, w_down, b_down, norm_final_g):
    bsz, seq, d = x.shape
    depth = w_in.shape[0]
    bf = jnp.bfloat16
    nt = d // LANES
    n_tok = bsz * seq
    ne = w_router.shape[-1]
    n_blocks = -(-n_tok * TOP_K // MOE_BLOCK) + ne
    seg_len = MIX_ROWS // SCAN_SEGS

    def row(v):
        return v.reshape(1, -1)

    for l in range(depth):
        ar, ai, arp, aip, bbr, bbi = _ssm_disc(
            ssm_a_re[l], ssm_a_im[l], ssm_log_step[l],
            jnp.swapaxes(ssm_b_re[l], 1, 2), jnp.swapaxes(ssm_b_im[l], 1, 2), seg_len)
        bblk = jnp.concatenate([_block_diag(bbr), _block_diag(bbi)], axis=1).astype(bf)
        c_r = _block_diag(jnp.swapaxes(ssm_c_re[l], 1, 2)).astype(bf)
        c_i = _block_diag(jnp.swapaxes(ssm_c_im[l], 1, 2)).astype(bf)
        avec = jnp.stack([ar.reshape(-1), ai.reshape(-1), arp.reshape(-1), aip.reshape(-1)])

        x = _mixer(x, row(norm_mix_g[l]), w_in[l].astype(bf), conv_w[l], row(conv_b[l]),
                   row(conv_ln_g[l]), row(conv_ln_b[l]), w_conv_out[l].astype(bf), bblk, c_r, c_i, avec,
                   row(ssm_d[l]), w_ssm_glu[l].astype(bf), row(b_ssm_glu[l]), w_ssm_out[l].astype(bf),
                   w_out[l].astype(bf))

        x1 = x.reshape(n_tok, d)
        h2, loc, gate, run, counts = _route(x1, row(norm_ffn_g[l]), w_router[l], row(b_router[l]))
        poff, blk_e, blk_s = _meta(counts, n_blocks)
        runs = _runs(poff, run)
        xs = _dispatch(counts, poff, loc, runs, h2, n_blocks * MOE_BLOCK, nt)
        ys = _experts(blk_e, blk_s, xs, w_gate_up[l], b_gate_up[l], w_down[l], b_down[l], nt)
        assert depth == 1
        x = _combine(loc, gate, runs, x1, row(norm_final_g), ys, nt).reshape(bsz, seq, d)
    return x
```

```python
import functools
import math

import jax
import jax.numpy as jnp
from jax import lax
from jax.experimental import pallas as pl
from jax.experimental.pallas import tpu as pltpu

NORM_EPS = 1e-6
SWIGLU_LIMIT = 7.0
SWIGLU_ALPHA = 1.702
TOP_K = 4
SSM_GROUP = 16

LANES = 128
SUBLANES = 8

MIX_ROWS = 512
SCAN_SEGS = SUBLANES
SEG_PAD = 8
CONV_HALO = 32
CONV_ROWS = 64
ROUTE_ROWS = 512
MOE_BLOCK = 512
VMEM_LIMIT = 56 * 1024 * 1024


def _sigmoid(x):
    return jax.nn.sigmoid(x)


def _gelu_tanh(x):
    c = math.sqrt(2.0 / math.pi)
    return 0.5 * x * (1.0 + jnp.tanh(c * (x + 0.044715 * (x * x * x))))


def _const_spec(shape):
    nd = len(shape)
    return pl.BlockSpec(shape, lambda *_: (0,) * nd)


def _ssm_disc_kernel(seg_len, are_ref, aim_ref, ls_ref, bre_ref, bim_ref,
                     ar_ref, ai_ref, arp_ref, aip_ref, bbr_ref, bbi_ref):
    lr = are_ref[...]
    li = aim_ref[...]
    dt = jnp.exp(ls_ref[...])
    mag = jnp.exp(lr * dt)
    ar = mag * jnp.cos(li * dt)
    ai = mag * jnp.sin(li * dt)
    den = lr * lr + li * li
    fr = ((ar - 1.0) * lr + ai * li) / den
    fi = (ai * lr - (ar - 1.0) * li) / den
    ar_ref[...] = ar
    ai_ref[...] = ai
    magp = jnp.exp(lr * dt * seg_len)
    arp_ref[...] = magp * jnp.cos(li * dt * seg_len)
    aip_ref[...] = magp * jnp.sin(li * dt * seg_len)
    br = bre_ref[...]
    bi = bim_ref[...]
    bbr_ref[...] = fr[:, None, :] * br - fi[:, None, :] * bi
    bbi_ref[...] = fr[:, None, :] * bi + fi[:, None, :] * br


def _ssm_disc(a_re, a_im, log_step, bt_re, bt_im, seg_len):
    g, p = a_re.shape
    h = bt_re.shape[1]
    f32 = jnp.float32
    out_shape = [jax.ShapeDtypeStruct((g, p), f32)] * 4 + [jax.ShapeDtypeStruct((g, h, p), f32)] * 2
    return pl.pallas_call(
        functools.partial(_ssm_disc_kernel, float(seg_len)),
        out_shape=out_shape,
        name="ssm_disc",
    )(a_re, a_im, log_step.reshape(g, 1), bt_re, bt_im)


def _mixer_kernel(dims, x_ref, gmix_ref, win_ref, convw_ref, convb_ref, lng_ref, lnb_ref, wco_ref,
                  bblk_ref, cr_ref, ci_ref, avec_ref, dskip_ref, wglu_ref, bglu_ref, wso_ref, wout_ref,
                  x1_ref, vbuf_ref, conv_ref, scan_ref, xst_ref, cin_ref, st_ref):
    d, dc, ds, gp, cw = dims
    rows = MIX_ROWS
    seg = rows // SCAN_SEGS
    pitch = seg + SEG_PAD
    nsl = gp // LANES
    c_idx = pl.program_id(1)

    half = rows // 2
    tail = 2 * half + 1

    @pl.when(c_idx == 0)
    def _():
        for lc in range(dc // LANES):
            vbuf_ref[lc, pl.ds(0, CONV_HALO, stride=2), :] = jnp.zeros((CONV_HALO, LANES), jnp.float32)
        st_ref[...] = jnp.zeros_like(st_ref)

    @pl.when(c_idx != 0)
    def _():
        for lc in range(dc // LANES):
            vbuf_ref[lc, pl.ds(0, CONV_HALO, stride=2), :] = vbuf_ref[lc, pl.ds(tail, CONV_HALO, stride=2), :]

    x = x_ref[0]
    ms = jnp.mean(x * x, axis=-1, keepdims=True)
    h = (x * lax.rsqrt(ms + NORM_EPS) * gmix_ref[...]).astype(jnp.bfloat16)

    pa = jnp.dot(h, win_ref[:, 0:2 * dc], preferred_element_type=jnp.float32)
    vglu = pa[:, 0:dc] * _sigmoid(pa[:, dc:2 * dc])
    for lc in range(dc // LANES):
        ls = slice(lc * LANES, (lc + 1) * LANES)
        vbuf_ref[lc, pl.ds(2 * CONV_HALO, half, stride=2), :] = vglu[0:half, ls]
        vbuf_ref[lc, pl.ds(2 * CONV_HALO + 1, half, stride=2), :] = vglu[half:rows, ls]
        vbuf_ref[lc, pl.ds(1, CONV_HALO, stride=2), :] = vglu[half - CONV_HALO:half, ls]

    base = CONV_HALO - (cw - 1)
    nv = CONV_ROWS // SUBLANES

    def conv_chunk(rc, p):
        r0 = rc * CONV_ROWS
        for lc in range(dc // LANES):
            ls = slice(lc * LANES, (lc + 1) * LANES)
            wv = [jnp.broadcast_to(convw_ref[k:k + 1, ls], (SUBLANES, LANES)) for k in range(cw)]
            acc = [jnp.zeros((SUBLANES, LANES), jnp.float32) for _ in range(nv)]
            for s in range(CONV_ROWS - SUBLANES + cw):
                win = vbuf_ref[lc, pl.ds(2 * (r0 + base + s) + p, SUBLANES, stride=2), :]
                for i in range(nv):
                    k = s - SUBLANES * i
                    if 0 <= k < cw:
                        acc[i] = acc[i] + wv[k] * win
            for i in range(nv):
                conv_ref[pl.ds(p * half + r0 + SUBLANES * i, SUBLANES), ls] = acc[i]

    g0 = 2 * dc + ds
    chunks = [(rc, p) for rc in range(half // CONV_ROWS) for p in range(2)]
    gw = 2 * d // len(chunks)
    sg = []
    for n, (rc, p) in enumerate(chunks):
        sg.append(_sigmoid(jnp.dot(h, win_ref[:, g0 + n * gw:g0 + (n + 1) * gw],
                                   preferred_element_type=jnp.float32)))
        conv_chunk(rc, p)
    sg = jnp.concatenate(sg, axis=-1)
    sg_conv = sg[:, 0:d]
    sg_ssm = sg[:, d:2 * d]

    v = conv_ref[...] + convb_ref[...]
    mu = jnp.mean(v, axis=-1, keepdims=True)
    vc = v - mu
    var = jnp.mean(vc * vc, axis=-1, keepdims=True)
    v = vc * lax.rsqrt(var + NORM_EPS) * lng_ref[...] + lnb_ref[...]
    v = v * _sigmoid(v)
    y_conv = jnp.dot(v.astype(jnp.bfloat16), wco_ref[...], preferred_element_type=jnp.float32)

    u = jnp.dot(h, win_ref[:, 2 * dc:2 * dc + ds], preferred_element_type=jnp.float32)
    bu = jnp.dot(u.astype(jnp.bfloat16), bblk_ref[...], preferred_element_type=jnp.float32)
    for n in range(2 * nsl):
        for s in range(SCAN_SEGS):
            scan_ref[n, s * pitch:s * pitch + seg, :] = bu[s * seg:(s + 1) * seg, n * LANES:(n + 1) * LANES]

    def bcast(row, n):
        return jnp.broadcast_to(avec_ref[row:row + 1, n * LANES:(n + 1) * LANES], (SUBLANES, LANES))

    a_r = [bcast(0, n) for n in range(nsl)]
    a_i = [bcast(1, n) for n in range(nsl)]

    def step(j, st, store):
        out = []
        for n in range(nsl):
            sr, si = st[2 * n], st[2 * n + 1]
            br = scan_ref[n, pl.ds(j, SCAN_SEGS, stride=pitch), :]
            bi = scan_ref[nsl + n, pl.ds(j, SCAN_SEGS, stride=pitch), :]
            nr = a_r[n] * sr - a_i[n] * si + br
            ni = a_r[n] * si + a_i[n] * sr + bi
            if store:
                scan_ref[n, pl.ds(j, SCAN_SEGS, stride=pitch), :] = nr
                scan_ref[nsl + n, pl.ds(j, SCAN_SEGS, stride=pitch), :] = ni
            out += [nr, ni]
        return tuple(out)

    zero = jnp.zeros((SUBLANES, LANES), jnp.float32)
    fin = lax.fori_loop(0, seg, functools.partial(step, store=False), (zero,) * (2 * nsl))

    for n in range(nsl):
        fr, fi = fin[2 * n], fin[2 * n + 1]
        ap_r = avec_ref[2:3, n * LANES:(n + 1) * LANES]
        ap_i = avec_ref[3:4, n * LANES:(n + 1) * LANES]
        c_r = st_ref[0:1, n * LANES:(n + 1) * LANES]
        c_i = st_ref[1:2, n * LANES:(n + 1) * LANES]
        for s in range(SCAN_SEGS):
            cin_ref[2 * n, s:s + 1, :] = c_r
            cin_ref[2 * n + 1, s:s + 1, :] = c_i
            n_r = ap_r * c_r - ap_i * c_i + fr[s:s + 1, :]
            n_i = ap_r * c_i + ap_i * c_r + fi[s:s + 1, :]
            c_r, c_i = n_r, n_i
        st_ref[0:1, n * LANES:(n + 1) * LANES] = c_r
        st_ref[1:2, n * LANES:(n + 1) * LANES] = c_i

    lax.fori_loop(0, seg, functools.partial(step, store=True), tuple(cin_ref[q] for q in range(2 * nsl)))

    for n in range(2 * nsl):
        for s in range(SCAN_SEGS):
            xst_ref[s * seg:(s + 1) * seg, n * LANES:(n + 1) * LANES] = (
                scan_ref[n, s * pitch:s * pitch + seg, :].astype(jnp.bfloat16))

    y = (jnp.dot(xst_ref[:, 0:gp], cr_ref[...], preferred_element_type=jnp.float32)
         - jnp.dot(xst_ref[:, gp:2 * gp], ci_ref[...], preferred_element_type=jnp.float32))
    y = _gelu_tanh(y + dskip_ref[...] * u)
    glu = jnp.dot(y.astype(jnp.bfloat16), wglu_ref[...], preferred_element_type=jnp.float32) + bglu_ref[...]
    y = y * _sigmoid(glu)
    y_ssm = jnp.dot(y.astype(jnp.bfloat16), wso_ref[...], preferred_element_type=jnp.float32)

    m = sg_conv * y_conv + sg_ssm * y_ssm
    x1_ref[0] = x + jnp.dot(m.astype(jnp.bfloat16), wout_ref[...], preferred_element_type=jnp.float32)


def _mixer(x, gmix, w_in, conv_w, conv_b, ln_g, ln_b, w_co, bblk, c_r, c_i, avec, d_skip,
           w_glu, b_glu, w_so, w_out):
    b, s, d = x.shape
    cw, dc = conv_w.shape
    ds = d_skip.shape[-1]
    gp = c_r.shape[0]
    rows = MIX_ROWS
    seg = rows // SCAN_SEGS
    pitch = seg + SEG_PAD
    assert s % rows == 0 and cw - 1 <= CONV_HALO and gp % LANES == 0 and dc % LANES == 0
    dims = (d, dc, ds, gp, cw)
    consts = [gmix, w_in, conv_w, conv_b, ln_g, ln_b, w_co, bblk, c_r, c_i, avec, d_skip,
              w_glu, b_glu, w_so, w_out]
    return pl.pallas_call(
        functools.partial(_mixer_kernel, dims),
        grid=(b, s // rows),
        in_specs=[pl.BlockSpec((1, rows, d), lambda i, j: (i, j, 0))] + [_const_spec(c.shape) for c in consts],
        out_specs=pl.BlockSpec((1, rows, d), lambda i, j: (i, j, 0)),
        out_shape=jax.ShapeDtypeStruct((b, s, d), jnp.float32),
        scratch_shapes=[
            pltpu.VMEM((dc // LANES, 2 * CONV_HALO + rows, LANES), jnp.float32),
            pltpu.VMEM((rows, dc), jnp.float32),
            pltpu.VMEM((2 * gp // LANES, SCAN_SEGS * pitch, LANES), jnp.float32),
            pltpu.VMEM((rows, 2 * gp), jnp.bfloat16),
            pltpu.VMEM((2 * gp // LANES, SUBLANES, LANES), jnp.float32),
            pltpu.VMEM((2, gp), jnp.float32),
        ],
        compiler_params=pltpu.CompilerParams(
            dimension_semantics=("arbitrary", "arbitrary"), vmem_limit_bytes=VMEM_LIMIT),
        name="mixer",
    )(x, *consts)


def _route_kernel(x1_ref, g_ref, wr_ref, br_ref, h2_ref, loc_ref, gate_ref, run_ref, cnt_ref, carry_ref):
    rows, d = x1_ref.shape
    ne = wr_ref.shape[1]

    @pl.when(pl.program_id(0) == 0)
    def _():
        carry_ref[...] = jnp.zeros_like(carry_ref)

    x = x1_ref[...]
    ms = jnp.mean(x * x, axis=-1, keepdims=True)
    h2 = x * lax.rsqrt(ms + NORM_EPS) * g_ref[...]

    nt = d // LANES
    for j in range(nt):
        h2_ref[pl.ds(j, rows, stride=nt), :] = h2[:, j * LANES:(j + 1) * LANES]

    bf = jnp.bfloat16
    h_hi = h2.astype(bf)
    h_lo = (h2 - h_hi.astype(jnp.float32)).astype(bf)
    w = wr_ref[...]
    w_hi = w.astype(bf)
    w_lo = (w - w_hi.astype(jnp.float32)).astype(bf)
    logits = (jnp.dot(h_hi, w_hi, preferred_element_type=jnp.float32)
              + jnp.dot(h_lo, w_hi, preferred_element_type=jnp.float32)
              + jnp.dot(h_hi, w_lo, preferred_element_type=jnp.float32)) + br_ref[...]

    pad = jnp.full((rows, LANES - ne), -jnp.inf, jnp.float32)
    work = jnp.concatenate([logits, pad], axis=1).T[0:ne, :]
    eidx = lax.broadcasted_iota(jnp.int32, (ne, rows), 0).astype(jnp.float32)
    sels, vals = [], []
    for k in range(TOP_K):
        mx = jnp.max(work, axis=0, keepdims=True)
        idx = jnp.min(jnp.where(work == mx, eidx, float(ne)), axis=0, keepdims=True)
        sel = eidx == idx
        sels.append(sel)
        vals.append(mx)
        work = jnp.where(sel, -jnp.inf, work)

    exps = [jnp.exp(v - vals[0]) for v in vals]
    tot = exps[0]
    for e in exps[1:]:
        tot = tot + e
    for k in range(TOP_K):
        gate_ref[0, k] = exps[k] / tot

    cnt = jnp.zeros((ne, rows), jnp.float32)
    for sel in sels:
        cnt = cnt + sel.astype(jnp.float32)
    r_i = lax.broadcasted_iota(jnp.int32, (rows, rows), 0)
    c_i = lax.broadcasted_iota(jnp.int32, (rows, rows), 1)
    triu = (r_i < c_i).astype(bf)
    before = jnp.dot(cnt.astype(bf), triu, preferred_element_type=jnp.float32)
    cnt_col = jnp.sum(cnt, axis=1, keepdims=True)
    e_r = lax.broadcasted_iota(jnp.int32, (ne, ne), 0)
    e_c = lax.broadcasted_iota(jnp.int32, (ne, ne), 1)
    col_b = jnp.broadcast_to(cnt_col, (ne, ne))
    cnt_row = jnp.sum(jnp.where(e_r == e_c, col_b, 0.0), axis=0, keepdims=True)
    first_row = jnp.sum(jnp.where(e_r < e_c, col_b, 0.0), axis=0, keepdims=True)
    first_col = jnp.sum(jnp.where(e_c < e_r, jnp.broadcast_to(cnt_row, (ne, ne)), 0.0), axis=1, keepdims=True)
    pos = before + first_col
    for k in range(TOP_K):
        lk = jnp.sum(jnp.where(sels[k], pos, 0.0), axis=0, keepdims=True).astype(jnp.int32)
        loc_ref[0, k] = lk * nt
    run_ref[...] = jnp.zeros_like(run_ref)
    run_ref[0, 0:1, :] = cnt_row.astype(jnp.int32)
    run_ref[0, 1:2, :] = first_row.astype(jnp.int32)
    run_ref[0, 2:3, :] = carry_ref[...].astype(jnp.int32)
    carry_ref[...] = carry_ref[...] + cnt_row
    cnt_ref[...] = carry_ref[...].astype(jnp.int32)


def _route(x1, g_ffn, w_router, b_router):
    t, d = x1.shape
    ne = w_router.shape[1]
    rows = ROUTE_ROWS
    assert t % rows == 0 and d % LANES == 0
    assert TOP_K <= SUBLANES and ne <= LANES
    nt = d // LANES
    tiles = t // rows
    per_tile = pl.BlockSpec((1, TOP_K, 1, rows), lambda i: (i, 0, 0, 0))
    return pl.pallas_call(
        _route_kernel,
        grid=(tiles,),
        in_specs=[pl.BlockSpec((rows, d), lambda i: (i, 0)),
                  _const_spec(g_ffn.shape), _const_spec(w_router.shape), _const_spec(b_router.shape)],
        out_specs=[pl.BlockSpec((rows * nt, LANES), lambda i: (i, 0)),
                   per_tile, per_tile,
                   pl.BlockSpec((1, SUBLANES, ne), lambda i: (i, 0, 0)),
                   _const_spec((1, ne))],
        out_shape=[jax.ShapeDtypeStruct((t * nt, LANES), jnp.float32),
                   jax.ShapeDtypeStruct((tiles, TOP_K, 1, rows), jnp.int32),
                   jax.ShapeDtypeStruct((tiles, TOP_K, 1, rows), jnp.float32),
                   jax.ShapeDtypeStruct((tiles, SUBLANES, ne), jnp.int32),
                   jax.ShapeDtypeStruct((1, ne), jnp.int32)],
        scratch_shapes=[pltpu.VMEM((1, ne), jnp.float32)],
        compiler_params=pltpu.CompilerParams(
            dimension_semantics=("arbitrary",), vmem_limit_bytes=VMEM_LIMIT),
        name="route",
    )(x1, g_ffn, w_router, b_router)


def _meta_kernel(cnt_ref, poff_ref, blke_ref, blks_ref):
    ne = cnt_ref.shape[1]
    nb = blke_ref.shape[1]
    f32 = jnp.float32
    cnt = cnt_ref[...].astype(f32)
    nblk = jnp.floor((cnt + (MOE_BLOCK - 1)) * (1.0 / MOE_BLOCK))
    e_r = lax.broadcasted_iota(jnp.int32, (ne, ne), 0)
    e_c = lax.broadcasted_iota(jnp.int32, (ne, ne), 1)
    nb_rows = jnp.broadcast_to(nblk, (ne, ne))
    bend_col = jnp.sum(jnp.where(e_c <= e_r, nb_rows, 0.0), axis=1, keepdims=True)
    nb_col = jnp.sum(jnp.where(e_c == e_r, nb_rows, 0.0), axis=1, keepdims=True)
    boff = jnp.sum(jnp.where(e_r < e_c, jnp.broadcast_to(nb_col, (ne, ne)), 0.0),
                   axis=0, keepdims=True)
    poff_ref[...] = (boff * MOE_BLOCK).astype(jnp.int32)
    total = jnp.sum(nblk, axis=1, keepdims=True)
    blk = lax.broadcasted_iota(jnp.int32, (ne, nb), 1).astype(f32)
    done = jnp.where(jnp.broadcast_to(bend_col, (ne, nb)) <= blk, 1.0, 0.0)
    be = jnp.minimum(jnp.sum(done, axis=0, keepdims=True), ne - 1.0)
    bid = lax.broadcasted_iota(jnp.int32, (1, nb), 1).astype(f32)
    last_e = jnp.sum(jnp.where(bid == total - 1.0, be, 0.0), axis=1, keepdims=True)
    blke_ref[...] = jnp.where(bid < total, be, last_e).astype(jnp.int32)
    blks_ref[...] = jnp.minimum(bid, total - 1.0).astype(jnp.int32)


def _meta(counts, n_blocks):
    ne = counts.shape[1]
    return pl.pallas_call(
        _meta_kernel,
        out_shape=[jax.ShapeDtypeStruct((1, ne), jnp.int32),
                   jax.ShapeDtypeStruct((1, n_blocks), jnp.int32),
                   jax.ShapeDtypeStruct((1, n_blocks), jnp.int32)],
        name="moe_meta",
    )(counts)


def _runs_kernel(poff_ref, run_ref, out_ref):
    run = run_ref[...]
    row = lax.broadcasted_iota(jnp.int32, run.shape, 1)
    out_ref[...] = jnp.where(row == 2, run + poff_ref[...][None], run)


def _runs(poff, run):
    return pl.pallas_call(
        _runs_kernel,
        out_shape=jax.ShapeDtypeStruct(run.shape, jnp.int32),
        name="moe_runs",
    )(poff, run)


def _tile(ref, r, nt):
    return ref.at[pl.ds(pl.multiple_of(r * nt, nt), nt), :]


def _slot_specs(rows):
    assert TOP_K == 4
    return [pl.BlockSpec((1, 1, 1, rows), functools.partial(lambda k, i: (i, k, 0, 0), k), memory_space=pltpu.SMEM)
            for k in range(TOP_K)]


def _run_copies(run_ref, src_of, dst_of, sem, nt):
    ne = run_ref.shape[2]

    def per_expert(e, carry):
        n = run_ref[0, 0, e]

        @pl.when(n > 0)
        def _():
            pltpu.make_async_copy(src_of(run_ref[0, 1, e], run_ref[0, 2, e], n),
                                  dst_of(run_ref[0, 1, e], run_ref[0, 2, e], n), sem).start()
        return carry

    lax.fori_loop(0, ne, per_expert, 0)


def _dispatch_kernel(nt, cnt_ref, poff_ref, loc0_ref, loc1_ref, loc2_ref, loc3_ref, run_ref, h2_ref, xs_ref,
                     stage_ref, zero_ref, sem, zsem):
    loc_refs = (loc0_ref, loc1_ref, loc2_ref, loc3_ref)
    step = pl.program_id(0)
    last = pl.num_programs(0) - 1
    rows = h2_ref.shape[0] // nt
    n_assign = rows * TOP_K
    ne = cnt_ref.shape[1]
    par = step % 2
    unroll = 4

    def rows_at(ref, first, n):
        return ref.at[pl.ds(pl.multiple_of(first * nt, nt), n * nt), :]

    def wait_runs(half):
        pltpu.make_async_copy(stage_ref.at[half], xs_ref.at[pl.ds(0, n_assign * nt), :], sem.at[half]).wait()

    @pl.when(step == 0)
    def _():
        zero_ref[...] = jnp.zeros_like(zero_ref)

        def per_expert(e, carry):
            c = cnt_ref[0, e]
            pad = (MOE_BLOCK - (c % MOE_BLOCK)) % MOE_BLOCK

            @pl.when(pad > 0)
            def _():
                cp = pltpu.make_async_copy(rows_at(zero_ref, 0, pad), rows_at(xs_ref, poff_ref[0, e] + c, pad), zsem)
                cp.start()
                cp.wait()
            return poff_ref[0, e] + c + pad

        used = lax.fori_loop(0, ne, per_expert, 0)

        def spare(b, carry):
            cp = pltpu.make_async_copy(zero_ref, rows_at(xs_ref, b * MOE_BLOCK, MOE_BLOCK), zsem)
            cp.start()
            cp.wait()
            return carry

        lax.fori_loop(used // MOE_BLOCK, xs_ref.shape[0] // (MOE_BLOCK * nt), spare, 0)

    @pl.when(step >= 2)
    def _():
        wait_runs(par)

    def place_into(half):
        def place(i, carry):
            for u in range(unroll):
                tok = i * unroll + u
                row = _tile(h2_ref, tok, nt)[...]
                for k in range(TOP_K):
                    stage_ref[half, pl.ds(pl.multiple_of(loc_refs[k][0, 0, 0, tok], nt), nt), :] = row
            return carry

        lax.fori_loop(0, rows // unroll, place, 0)

    for half in range(2):
        pl.when(par == half)(functools.partial(place_into, half))

    _run_copies(run_ref,
                lambda first, base, n: stage_ref.at[par, pl.ds(pl.multiple_of(first * nt, nt), n * nt), :],
                lambda first, base, n: rows_at(xs_ref, base, n), sem.at[par], nt)

    @pl.when(step == last)
    def _():
        wait_runs(par)

        @pl.when(step >= 1)
        def _():
            wait_runs(1 - par)


def _dispatch(counts, poff, loc, runs, h2, n_rows, nt):
    t = h2.shape[0] // nt
    rows = ROUTE_ROWS
    tiles = t // rows
    ne = counts.shape[1]
    smem = pl.BlockSpec(memory_space=pltpu.SMEM)
    return pl.pallas_call(
        functools.partial(_dispatch_kernel, nt),
        grid=(tiles,),
        in_specs=[smem, smem,
                  *_slot_specs(rows),
                  pl.BlockSpec((1, SUBLANES, ne), lambda i: (i, 0, 0), memory_space=pltpu.SMEM),
                  pl.BlockSpec((rows * nt, LANES), lambda i: (i, 0))],
        out_specs=pl.BlockSpec(memory_space=pl.ANY),
        out_shape=jax.ShapeDtypeStruct((n_rows * nt, LANES), h2.dtype),
        scratch_shapes=[pltpu.VMEM((2, rows * TOP_K * nt, LANES), h2.dtype),
                        pltpu.VMEM((MOE_BLOCK * nt, LANES), h2.dtype),
                        pltpu.SemaphoreType.DMA((2,)), pltpu.SemaphoreType.DMA(())],
        compiler_params=pltpu.CompilerParams(
            dimension_semantics=("arbitrary",), vmem_limit_bytes=VMEM_LIMIT, has_side_effects=True),
        name="moe_dispatch",
    )(counts, poff, loc, loc, loc, loc, runs, h2)


def _expert_kernel(nt, blke_ref, blks_ref, xs_ref, wgu32_ref, bgu_ref, wd32_ref, bd_ref, ys_ref,
                   wgu_ref, wd_ref):
    i = pl.program_id(0)
    f = wd_ref.shape[0]

    @pl.when(jnp.logical_or(i == 0, blke_ref[0, i] != blke_ref[0, jnp.maximum(i - 1, 0)]))
    def _():
        wgu_ref[...] = wgu32_ref[0].astype(jnp.bfloat16)
        wd_ref[...] = wd32_ref[0].astype(jnp.bfloat16)

    @pl.when(blks_ref[0, i] != i)
    def _():
        ys_ref[...] = jnp.zeros_like(ys_ref)

    @pl.when(blks_ref[0, i] == i)
    def _():
        x = jnp.concatenate([xs_ref[pl.ds(j, MOE_BLOCK, stride=nt), :] for j in range(nt)], axis=-1)
        gu = jnp.dot(x.astype(jnp.bfloat16), wgu_ref[...], preferred_element_type=jnp.float32) + bgu_ref[0]
        g = jnp.minimum(gu[:, 0:f], SWIGLU_LIMIT)
        lin = jnp.clip(gu[:, f:2 * f], -SWIGLU_LIMIT, SWIGLU_LIMIT)
        act = g * _sigmoid(SWIGLU_ALPHA * g) * (lin + 1.0)
        y = jnp.dot(act.astype(jnp.bfloat16), wd_ref[...], preferred_element_type=jnp.float32) + bd_ref[0]
        for j in range(nt):
            ys_ref[pl.ds(j, MOE_BLOCK, stride=nt), :] = y[:, j * LANES:(j + 1) * LANES]


def _experts(blk_e, blk_s, xs, w_gate_up, b_gate_up, w_down, b_down, nt):
    ne, d, f2 = w_gate_up.shape
    f = f2 // 2
    n_blocks = blk_e.shape[1]
    blk_rows = MOE_BLOCK * nt
    grid_spec = pltpu.PrefetchScalarGridSpec(
        num_scalar_prefetch=2,
        grid=(n_blocks,),
        in_specs=[pl.BlockSpec((blk_rows, LANES), lambda i, be, bs: (bs[0, i], 0)),
                  pl.BlockSpec((1, d, f2), lambda i, be, bs: (be[0, i], 0, 0)),
                  pl.BlockSpec((1, 1, f2), lambda i, be, bs: (be[0, i], 0, 0)),
                  pl.BlockSpec((1, f, d), lambda i, be, bs: (be[0, i], 0, 0)),
                  pl.BlockSpec((1, 1, d), lambda i, be, bs: (be[0, i], 0, 0))],
        out_specs=pl.BlockSpec((blk_rows, LANES), lambda i, be, bs: (i, 0)),
        scratch_shapes=[pltpu.VMEM((d, f2), jnp.bfloat16), pltpu.VMEM((f, d), jnp.bfloat16)],
    )
    return pl.pallas_call(
        functools.partial(_expert_kernel, nt),
        grid_spec=grid_spec,
        out_shape=jax.ShapeDtypeStruct(xs.shape, jnp.float32),
        compiler_params=pltpu.CompilerParams(
            dimension_semantics=("arbitrary",), vmem_limit_bytes=VMEM_LIMIT),
        name="moe_experts",
    )(blk_e, blk_s, xs, w_gate_up, b_gate_up.reshape(ne, 1, f2), w_down, b_down.reshape(ne, 1, d))


def _combine_kernel(nt, loc0_ref, loc1_ref, loc2_ref, loc3_ref, g0_ref, g1_ref, g2_ref, g3_ref,
                    run_ref, runn_ref, x1_ref, gfin_ref, ys_ref, out_ref, stage_ref, acc_ref, sem):
    loc_refs = (loc0_ref, loc1_ref, loc2_ref, loc3_ref)
    gate_refs = (g0_ref, g1_ref, g2_ref, g3_ref)
    rows, d = x1_ref.shape
    n_assign = rows * TOP_K
    step = pl.program_id(0)
    par = step % 2
    unroll = 4

    def fetch(r_ref, half):
        _run_copies(r_ref,
                    lambda first, base, n: ys_ref.at[pl.ds(pl.multiple_of(base * nt, nt), n * nt), :],
                    lambda first, base, n: stage_ref.at[half, pl.ds(pl.multiple_of(first * nt, nt), n * nt), :],
                    sem.at[half], nt)

    @pl.when(step == 0)
    def _():
        fetch(run_ref, 0)

    @pl.when(step + 1 < pl.num_programs(0))
    def _():
        fetch(runn_ref, 1 - par)

    pltpu.make_async_copy(ys_ref.at[pl.ds(0, n_assign * nt), :], stage_ref.at[par], sem.at[par]).wait()

    def mix_from(half):
        def mix(i, carry):
            for u in range(unroll):
                tok = i * unroll + u
                acc = None
                for k in range(TOP_K):
                    row = stage_ref[half, pl.ds(pl.multiple_of(loc_refs[k][0, 0, 0, tok], nt), nt), :]
                    term = gate_refs[k][0, 0, 0, tok] * row
                    acc = term if acc is None else acc + term
                acc_ref[pl.ds(pl.multiple_of(tok * nt, nt), nt), :] = acc
            return carry

        lax.fori_loop(0, rows // unroll, mix, 0)

    for half in range(2):
        pl.when(par == half)(functools.partial(mix_from, half))

    y = jnp.concatenate([acc_ref[pl.ds(j, rows, stride=nt), :] for j in range(nt)], axis=-1)
    x = x1_ref[...] + y
    ms = jnp.mean(x * x, axis=-1, keepdims=True)
    out_ref[...] = x * lax.rsqrt(ms + NORM_EPS) * gfin_ref[...]


def _combine(loc, gate, runs, x1, g_final, ys, nt):
    t, d = x1.shape
    rows = ROUTE_ROWS
    tiles = t // rows
    ne = runs.shape[2]
    n_assign = rows * TOP_K
    smem = pltpu.SMEM
    return pl.pallas_call(
        functools.partial(_combine_kernel, nt),
        grid=(tiles,),
        in_specs=[*_slot_specs(rows), *_slot_specs(rows),
                  pl.BlockSpec((1, SUBLANES, ne), lambda i: (i, 0, 0), memory_space=smem),
                  pl.BlockSpec((1, SUBLANES, ne), lambda i: (jnp.minimum(i + 1, tiles - 1), 0, 0),
                               memory_space=smem),
                  pl.BlockSpec((rows, d), lambda i: (i, 0)),
                  _const_spec(g_final.shape),
                  pl.BlockSpec(memory_space=pl.ANY)],
        out_specs=pl.BlockSpec((rows, d), lambda i: (i, 0)),
        out_shape=jax.ShapeDtypeStruct((t, d), jnp.float32),
        scratch_shapes=[pltpu.VMEM((2, n_assign * nt, LANES), jnp.float32),
                        pltpu.VMEM((rows * nt, LANES), jnp.float32),
                        pltpu.SemaphoreType.DMA((2,))],
        compiler_params=pltpu.CompilerParams(
            dimension_semantics=("arbitrary",), vmem_limit_bytes=VMEM_LIMIT),
        name="moe_combine",
    )(loc, loc, loc, loc, gate, gate, gate, gate, runs, runs, x1, g_final, ys)


def _block_diag(blocks):
    g, r, c = blocks.shape
    eye = jnp.eye(g, dtype=blocks.dtype)
    return (blocks[:, :, None, :] * eye[:, None, :, None]).reshape(g * r, g * c)


def kernel(x, norm_mix_g, w_in, conv_w, conv_b, conv_ln_g, conv_ln_b, w_conv_out, ssm_a_re, ssm_a_im,
           ssm_log_step, ssm_b_re, ssm_b_im, ssm_c_re, ssm_c_im, ssm_d, w_ssm_glu, b_ssm_glu, w_ssm_out,
           w_out, norm_ffn_g, w_router, b_router, w_gate_up, b_gate_up, w_down, b_down, norm_final_g):
    bsz, seq, d = x.shape
    depth = w_in.shape[0]
    bf = jnp.bfloat16
    nt = d // LANES
    n_tok = bsz * seq
    ne = w_router.shape[-1]
    n_blocks = -(-n_tok * TOP_K // MOE_BLOCK) + ne
    seg_len = MIX_ROWS // SCAN_SEGS

    def row(v):
        return v.reshape(1, -1)

    for l in range(depth):
        ar, ai, arp, aip, bbr, bbi = _ssm_disc(
            ssm_a_re[l], ssm_a_im[l], ssm_log_step[l],
            jnp.swapaxes(ssm_b_re[l], 1, 2), jnp.swapaxes(ssm_b_im[l], 1, 2), seg_len)
        bblk = jnp.concatenate([_block_diag(bbr), _block_diag(bbi)], axis=1).astype(bf)
        c_r = _block_diag(jnp.swapaxes(ssm_c_re[l], 1, 2)).astype(bf)
        c_i = _block_diag(jnp.swapaxes(ssm_c_im[l], 1, 2)).astype(bf)
        avec = jnp.stack([ar.reshape(-1), ai.reshape(-1), arp.reshape(-1), aip.reshape(-1)])

        x = _mixer(x, row(norm_mix_g[l]), w_in[l].astype(bf), conv_w[l], row(conv_b[l]),
                   row(conv_ln_g[l]), row(conv_ln_b[l]), w_conv_out[l].astype(bf), bblk, c_r, c_i, avec,
                   row(ssm_d[l]), w_ssm_glu[l].astype(bf), row(b_ssm_glu[l]), w_ssm_out[l].astype(bf),
                   w_out[l].astype(bf))

        x1 = x.reshape(n_tok, d)
        h2, loc, gate, run, counts = _route(x1, row(norm_ffn_g[l]), w_router[l], row(b_router[l]))
        poff, blk_e, blk_s = _meta(counts, n_blocks)
        runs = _runs(poff, run)
        xs = _dispatch(counts, poff, loc, runs, h2, n_blocks * MOE_BLOCK, nt)
        ys = _experts(blk_e, blk_s, xs, w_gate_up[l], b_gate_up[l], w_down[l], b_down[l], nt)
        assert depth == 1
        x = _combine(loc, gate, runs, x1, row(norm_final_g), ys, nt).reshape(bsz, seq, d)
    return x
```

```python
import functools
import math

import jax
import jax.numpy as jnp
from jax import lax
from jax.experimental import pallas as pl
from jax.experimental.pallas import tpu as pltpu

NORM_EPS = 1e-6
SWIGLU_LIMIT = 7.0
SWIGLU_ALPHA = 1.702
TOP_K = 4
SSM_GROUP = 16

LANES = 128
SUBLANES = 8

MIX_ROWS = 512
SCAN_SEGS = SUBLANES
SEG_PAD = 8
SCAN_UNROLL = 4
CONV_HALO = 32
CONV_ROWS = 64
ROUTE_ROWS = 512
MOE_BLOCK = 512
VMEM_LIMIT = 56 * 1024 * 1024


def _sigmoid(x):
    return jax.nn.sigmoid(x)


def _gelu_tanh(x):
    c = math.sqrt(2.0 / math.pi)
    return 0.5 * x * (1.0 + jnp.tanh(c * (x + 0.044715 * (x * x * x))))


def _const_spec(shape):
    nd = len(shape)
    return pl.BlockSpec(shape, lambda *_: (0,) * nd)


def _ssm_disc_kernel(seg_len, are_ref, aim_ref, ls_ref, bre_ref, bim_ref,
                     ar_ref, ai_ref, arp_ref, aip_ref, bbr_ref, bbi_ref):
    lr = are_ref[...]
    li = aim_ref[...]
    dt = jnp.exp(ls_ref[...])
    mag = jnp.exp(lr * dt)
    ar = mag * jnp.cos(li * dt)
    ai = mag * jnp.sin(li * dt)
    den = lr * lr + li * li
    fr = ((ar - 1.0) * lr + ai * li) / den
    fi = (ai * lr - (ar - 1.0) * li) / den
    ar_ref[...] = ar
    ai_ref[...] = ai
    magp = jnp.exp(lr * dt * seg_len)
    arp_ref[...] = magp * jnp.cos(li * dt * seg_len)
    aip_ref[...] = magp * jnp.sin(li * dt * seg_len)
    br = bre_ref[...]
    bi = bim_ref[...]
    bbr_ref[...] = fr[:, None, :] * br - fi[:, None, :] * bi
    bbi_ref[...] = fr[:, None, :] * bi + fi[:, None, :] * br


def _ssm_disc(a_re, a_im, log_step, bt_re, bt_im, seg_len):
    g, p = a_re.shape
    h = bt_re.shape[1]
    f32 = jnp.float32
    out_shape = [jax.ShapeDtypeStruct((g, p), f32)] * 4 + [jax.ShapeDtypeStruct((g, h, p), f32)] * 2
    return pl.pallas_call(
        functools.partial(_ssm_disc_kernel, float(seg_len)),
        out_shape=out_shape,
        name="ssm_disc",
    )(a_re, a_im, log_step.reshape(g, 1), bt_re, bt_im)


def _mixer_kernel(dims, x_ref, gmix_ref, win_ref, convw_ref, convb_ref, lng_ref, lnb_ref, wco_ref,
                  bblk_ref, cr_ref, ci_ref, avec_ref, dskip_ref, wglu_ref, bglu_ref, wso_ref, wout_ref,
                  x1_ref, vbuf_ref, conv_ref, scan_ref, xst_ref, cin_ref, st_ref):
    d, dc, ds, gp, cw = dims
    rows = MIX_ROWS
    seg = rows // SCAN_SEGS
    pitch = seg + SEG_PAD
    nsl = gp // LANES
    c_idx = pl.program_id(1)

    half = rows // 2
    tail = 2 * half + 1

    @pl.when(c_idx == 0)
    def _():
        for lc in range(dc // LANES):
            vbuf_ref[lc, pl.ds(0, CONV_HALO, stride=2), :] = jnp.zeros((CONV_HALO, LANES), jnp.float32)
        st_ref[...] = jnp.zeros_like(st_ref)

    @pl.when(c_idx != 0)
    def _():
        for lc in range(dc // LANES):
            vbuf_ref[lc, pl.ds(0, CONV_HALO, stride=2), :] = vbuf_ref[lc, pl.ds(tail, CONV_HALO, stride=2), :]

    x = x_ref[0]
    ms = jnp.mean(x * x, axis=-1, keepdims=True)
    h = (x * lax.rsqrt(ms + NORM_EPS) * gmix_ref[...]).astype(jnp.bfloat16)

    pa = jnp.dot(h, win_ref[:, 0:2 * dc], preferred_element_type=jnp.float32)
    vglu = pa[:, 0:dc] * _sigmoid(pa[:, dc:2 * dc])
    for lc in range(dc // LANES):
        ls = slice(lc * LANES, (lc + 1) * LANES)
        vbuf_ref[lc, pl.ds(2 * CONV_HALO, half, stride=2), :] = vglu[0:half, ls]
        vbuf_ref[lc, pl.ds(2 * CONV_HALO + 1, half, stride=2), :] = vglu[half:rows, ls]
        vbuf_ref[lc, pl.ds(1, CONV_HALO, stride=2), :] = vglu[half - CONV_HALO:half, ls]

    base = CONV_HALO - (cw - 1)
    nv = CONV_ROWS // SUBLANES

    def conv_chunk(rc, p):
        r0 = rc * CONV_ROWS
        for lc in range(dc // LANES):
            ls = slice(lc * LANES, (lc + 1) * LANES)
            wv = [jnp.broadcast_to(convw_ref[k:k + 1, ls], (SUBLANES, LANES)) for k in range(cw)]
            acc = [jnp.zeros((SUBLANES, LANES), jnp.float32) for _ in range(nv)]
            for s in range(CONV_ROWS - SUBLANES + cw):
                win = vbuf_ref[lc, pl.ds(2 * (r0 + base + s) + p, SUBLANES, stride=2), :]
                for i in range(nv):
                    k = s - SUBLANES * i
                    if 0 <= k < cw:
                        acc[i] = acc[i] + wv[k] * win
            for i in range(nv):
                conv_ref[pl.ds(p * half + r0 + SUBLANES * i, SUBLANES), ls] = acc[i]

    g0 = 2 * dc + ds
    chunks = [(rc, p) for rc in range(half // CONV_ROWS) for p in range(2)]
    gw = 2 * d // len(chunks)
    sg = []
    for n, (rc, p) in enumerate(chunks):
        sg.append(_sigmoid(jnp.dot(h, win_ref[:, g0 + n * gw:g0 + (n + 1) * gw],
                                   preferred_element_type=jnp.float32)))
        conv_chunk(rc, p)
    sg = jnp.concatenate(sg, axis=-1)
    sg_conv = sg[:, 0:d]
    sg_ssm = sg[:, d:2 * d]

    v = conv_ref[...] + convb_ref[...]
    mu = jnp.mean(v, axis=-1, keepdims=True)
    vc = v - mu
    var = jnp.mean(vc * vc, axis=-1, keepdims=True)
    v = vc * lax.rsqrt(var + NORM_EPS) * lng_ref[...] + lnb_ref[...]
    v = v * _sigmoid(v)
    y_conv = jnp.dot(v.astype(jnp.bfloat16), wco_ref[...], preferred_element_type=jnp.float32)

    u = jnp.dot(h, win_ref[:, 2 * dc:2 * dc + ds], preferred_element_type=jnp.float32)
    bu = jnp.dot(u.astype(jnp.bfloat16), bblk_ref[...], preferred_element_type=jnp.float32)
    for n in range(2 * nsl):
        for s in range(SCAN_SEGS):
            scan_ref[n, s * pitch:s * pitch + seg, :] = bu[s * seg:(s + 1) * seg, n * LANES:(n + 1) * LANES]

    def bcast(row, n):
        return jnp.broadcast_to(avec_ref[row:row + 1, n * LANES:(n + 1) * LANES], (SUBLANES, LANES))

    a_r = [bcast(0, n) for n in range(nsl)]
    a_i = [bcast(1, n) for n in range(nsl)]

    def step(j, st, store):
        out = []
        for n in range(nsl):
            sr, si = st[2 * n], st[2 * n + 1]
            br = scan_ref[n, pl.ds(j, SCAN_SEGS, stride=pitch), :]
            bi = scan_ref[nsl + n, pl.ds(j, SCAN_SEGS, stride=pitch), :]
            nr = a_r[n] * sr - a_i[n] * si + br
            ni = a_r[n] * si + a_i[n] * sr + bi
            if store:
                scan_ref[n, pl.ds(j, SCAN_SEGS, stride=pitch), :] = nr
                scan_ref[nsl + n, pl.ds(j, SCAN_SEGS, stride=pitch), :] = ni
            out += [nr, ni]
        return tuple(out)

    def steps(jj, st, store):
        for q in range(SCAN_UNROLL):
            st = step(jj * SCAN_UNROLL + q, st, store)
        return st

    zero = jnp.zeros((SUBLANES, LANES), jnp.float32)
    fin = lax.fori_loop(0, seg // SCAN_UNROLL, functools.partial(steps, store=False), (zero,) * (2 * nsl))

    for n in range(nsl):
        fr, fi = fin[2 * n], fin[2 * n + 1]
        ap_r = avec_ref[2:3, n * LANES:(n + 1) * LANES]
        ap_i = avec_ref[3:4, n * LANES:(n + 1) * LANES]
        c_r = st_ref[0:1, n * LANES:(n + 1) * LANES]
        c_i = st_ref[1:2, n * LANES:(n + 1) * LANES]
        for s in range(SCAN_SEGS):
            cin_ref[2 * n, s:s + 1, :] = c_r
            cin_ref[2 * n + 1, s:s + 1, :] = c_i
            n_r = ap_r * c_r - ap_i * c_i + fr[s:s + 1, :]
            n_i = ap_r * c_i + ap_i * c_r + fi[s:s + 1, :]
            c_r, c_i = n_r, n_i
        st_ref[0:1, n * LANES:(n + 1) * LANES] = c_r
        st_ref[1:2, n * LANES:(n + 1) * LANES] = c_i

    lax.fori_loop(0, seg // SCAN_UNROLL, functools.partial(steps, store=True),
                  tuple(cin_ref[q] for q in range(2 * nsl)))

    for n in range(2 * nsl):
        for s in range(SCAN_SEGS):
            xst_ref[s * seg:(s + 1) * seg, n * LANES:(n + 1) * LANES] = (
                scan_ref[n, s * pitch:s * pitch + seg, :].astype(jnp.bfloat16))

    y = (jnp.dot(xst_ref[:, 0:gp], cr_ref[...], preferred_element_type=jnp.float32)
         - jnp.dot(xst_ref[:, gp:2 * gp], ci_ref[...], preferred_element_type=jnp.float32))
    y = _gelu_tanh(y + dskip_ref[...] * u)
    glu = jnp.dot(y.astype(jnp.bfloat16), wglu_ref[...], preferred_element_type=jnp.float32) + bglu_ref[...]
    y = y * _sigmoid(glu)
    y_ssm = jnp.dot(y.astype(jnp.bfloat16), wso_ref[...], preferred_element_type=jnp.float32)

    m = sg_conv * y_conv + sg_ssm * y_ssm
    x1_ref[0] = x + jnp.dot(m.astype(jnp.bfloat16), wout_ref[...], preferred_element_type=jnp.float32)


def _mixer(x, gmix, w_in, conv_w, conv_b, ln_g, ln_b, w_co, bblk, c_r, c_i, avec, d_skip,
           w_glu, b_glu, w_so, w_out):
    b, s, d = x.shape
    cw, dc = conv_w.shape
    ds = d_skip.shape[-1]
    gp = c_r.shape[0]
    rows = MIX_ROWS
    seg = rows // SCAN_SEGS
    pitch = seg + SEG_PAD
    assert s % rows == 0 and cw - 1 <= CONV_HALO and gp % LANES == 0 and dc % LANES == 0
    dims = (d, dc, ds, gp, cw)
    consts = [gmix, w_in, conv_w, conv_b, ln_g, ln_b, w_co, bblk, c_r, c_i, avec, d_skip,
              w_glu, b_glu, w_so, w_out]
    return pl.pallas_call(
        functools.partial(_mixer_kernel, dims),
        grid=(b, s // rows),
        in_specs=[pl.BlockSpec((1, rows, d), lambda i, j: (i, j, 0))] + [_const_spec(c.shape) for c in consts],
        out_specs=pl.BlockSpec((1, rows, d), lambda i, j: (i, j, 0)),
        out_shape=jax.ShapeDtypeStruct((b, s, d), jnp.float32),
        scratch_shapes=[
            pltpu.VMEM((dc // LANES, 2 * CONV_HALO + rows, LANES), jnp.float32),
            pltpu.VMEM((rows, dc), jnp.float32),
            pltpu.VMEM((2 * gp // LANES, SCAN_SEGS * pitch, LANES), jnp.float32),
            pltpu.VMEM((rows, 2 * gp), jnp.bfloat16),
            pltpu.VMEM((2 * gp // LANES, SUBLANES, LANES), jnp.float32),
            pltpu.VMEM((2, gp), jnp.float32),
        ],
        compiler_params=pltpu.CompilerParams(
            dimension_semantics=("arbitrary", "arbitrary"), vmem_limit_bytes=VMEM_LIMIT),
        name="mixer",
    )(x, *consts)


def _route_kernel(x1_ref, g_ref, wr_ref, br_ref, h2_ref, loc_ref, gate_ref, run_ref, cnt_ref, carry_ref):
    rows, d = x1_ref.shape
    ne = wr_ref.shape[1]

    @pl.when(pl.program_id(0) == 0)
    def _():
        carry_ref[...] = jnp.zeros_like(carry_ref)

    x = x1_ref[...]
    ms = jnp.mean(x * x, axis=-1, keepdims=True)
    h2 = x * lax.rsqrt(ms + NORM_EPS) * g_ref[...]

    nt = d // LANES
    for j in range(nt):
        h2_ref[pl.ds(j, rows, stride=nt), :] = h2[:, j * LANES:(j + 1) * LANES]

    bf = jnp.bfloat16
    h_hi = h2.astype(bf)
    h_lo = (h2 - h_hi.astype(jnp.float32)).astype(bf)
    w = wr_ref[...]
    w_hi = w.astype(bf)
    w_lo = (w - w_hi.astype(jnp.float32)).astype(bf)
    logits = (jnp.dot(h_hi, w_hi, preferred_element_type=jnp.float32)
              + jnp.dot(h_lo, w_hi, preferred_element_type=jnp.float32)
              + jnp.dot(h_hi, w_lo, preferred_element_type=jnp.float32)) + br_ref[...]

    pad = jnp.full((rows, LANES - ne), -jnp.inf, jnp.float32)
    work = jnp.concatenate([logits, pad], axis=1).T[0:ne, :]
    eidx = lax.broadcasted_iota(jnp.int32, (ne, rows), 0).astype(jnp.float32)
    sels, vals = [], []
    for k in range(TOP_K):
        mx = jnp.max(work, axis=0, keepdims=True)
        idx = jnp.min(jnp.where(work == mx, eidx, float(ne)), axis=0, keepdims=True)
        sel = eidx == idx
        sels.append(sel)
        vals.append(mx)
        work = jnp.where(sel, -jnp.inf, work)

    exps = [jnp.exp(v - vals[0]) for v in vals]
    tot = exps[0]
    for e in exps[1:]:
        tot = tot + e
    for k in range(TOP_K):
        gate_ref[0, k] = exps[k] / tot

    cnt = jnp.zeros((ne, rows), jnp.float32)
    for sel in sels:
        cnt = cnt + sel.astype(jnp.float32)
    r_i = lax.broadcasted_iota(jnp.int32, (rows, rows), 0)
    c_i = lax.broadcasted_iota(jnp.int32, (rows, rows), 1)
    triu = (r_i < c_i).astype(bf)
    before = jnp.dot(cnt.astype(bf), triu, preferred_element_type=jnp.float32)
    cnt_col = jnp.sum(cnt, axis=1, keepdims=True)
    e_r = lax.broadcasted_iota(jnp.int32, (ne, ne), 0)
    e_c = lax.broadcasted_iota(jnp.int32, (ne, ne), 1)
    col_b = jnp.broadcast_to(cnt_col, (ne, ne))
    cnt_row = jnp.sum(jnp.where(e_r == e_c, col_b, 0.0), axis=0, keepdims=True)
    first_row = jnp.sum(jnp.where(e_r < e_c, col_b, 0.0), axis=0, keepdims=True)
    first_col = jnp.sum(jnp.where(e_c < e_r, jnp.broadcast_to(cnt_row, (ne, ne)), 0.0), axis=1, keepdims=True)
    pos = before + first_col
    for k in range(TOP_K):
        lk = jnp.sum(jnp.where(sels[k], pos, 0.0), axis=0, keepdims=True).astype(jnp.int32)
        loc_ref[0, k] = lk * nt
    run_ref[...] = jnp.zeros_like(run_ref)
    run_ref[0, 0:1, :] = cnt_row.astype(jnp.int32)
    run_ref[0, 1:2, :] = first_row.astype(jnp.int32)
    run_ref[0, 2:3, :] = carry_ref[...].astype(jnp.int32)
    carry_ref[...] = carry_ref[...] + cnt_row
    cnt_ref[...] = carry_ref[...].astype(jnp.int32)


def _route(x1, g_ffn, w_router, b_router):
    t, d = x1.shape
    ne = w_router.shape[1]
    rows = ROUTE_ROWS
    assert t % rows == 0 and d % LANES == 0
    assert TOP_K <= SUBLANES and ne <= LANES
    nt = d // LANES
    tiles = t // rows
    per_tile = pl.BlockSpec((1, TOP_K, 1, rows), lambda i: (i, 0, 0, 0))
    return pl.pallas_call(
        _route_kernel,
        grid=(tiles,),
        in_specs=[pl.BlockSpec((rows, d), lambda i: (i, 0)),
                  _const_spec(g_ffn.shape), _const_spec(w_router.shape), _const_spec(b_router.shape)],
        out_specs=[pl.BlockSpec((rows * nt, LANES), lambda i: (i, 0)),
                   per_tile, per_tile,
                   pl.BlockSpec((1, SUBLANES, ne), lambda i: (i, 0, 0)),
                   _const_spec((1, ne))],
        out_shape=[jax.ShapeDtypeStruct((t * nt, LANES), jnp.float32),
                   jax.ShapeDtypeStruct((tiles, TOP_K, 1, rows), jnp.int32),
                   jax.ShapeDtypeStruct((tiles, TOP_K, 1, rows), jnp.float32),
                   jax.ShapeDtypeStruct((tiles, SUBLANES, ne), jnp.int32),
                   jax.ShapeDtypeStruct((1, ne), jnp.int32)],
        scratch_shapes=[pltpu.VMEM((1, ne), jnp.float32)],
        compiler_params=pltpu.CompilerParams(
            dimension_semantics=("arbitrary",), vmem_limit_bytes=VMEM_LIMIT),
        name="route",
    )(x1, g_ffn, w_router, b_router)


def _meta_kernel(cnt_ref, poff_ref, blke_ref, blks_ref):
    ne = cnt_ref.shape[1]
    nb = blke_ref.shape[1]
    f32 = jnp.float32
    cnt = cnt_ref[...].astype(f32)
    nblk = jnp.floor((cnt + (MOE_BLOCK - 1)) * (1.0 / MOE_BLOCK))
    e_r = lax.broadcasted_iota(jnp.int32, (ne, ne), 0)
    e_c = lax.broadcasted_iota(jnp.int32, (ne, ne), 1)
    nb_rows = jnp.broadcast_to(nblk, (ne, ne))
    bend_col = jnp.sum(jnp.where(e_c <= e_r, nb_rows, 0.0), axis=1, keepdims=True)
    nb_col = jnp.sum(jnp.where(e_c == e_r, nb_rows, 0.0), axis=1, keepdims=True)
    boff = jnp.sum(jnp.where(e_r < e_c, jnp.broadcast_to(nb_col, (ne, ne)), 0.0),
                   axis=0, keepdims=True)
    poff_ref[...] = (boff * MOE_BLOCK).astype(jnp.int32)
    total = jnp.sum(nblk, axis=1, keepdims=True)
    blk = lax.broadcasted_iota(jnp.int32, (ne, nb), 1).astype(f32)
    done = jnp.where(jnp.broadcast_to(bend_col, (ne, nb)) <= blk, 1.0, 0.0)
    be = jnp.minimum(jnp.sum(done, axis=0, keepdims=True), ne - 1.0)
    bid = lax.broadcasted_iota(jnp.int32, (1, nb), 1).astype(f32)
    last_e = jnp.sum(jnp.where(bid == total - 1.0, be, 0.0), axis=1, keepdims=True)
    blke_ref[...] = jnp.where(bid < total, be, last_e).astype(jnp.int32)
    blks_ref[...] = jnp.minimum(bid, total - 1.0).astype(jnp.int32)


def _meta(counts, n_blocks):
    ne = counts.shape[1]
    return pl.pallas_call(
        _meta_kernel,
        out_shape=[jax.ShapeDtypeStruct((1, ne), jnp.int32),
                   jax.ShapeDtypeStruct((1, n_blocks), jnp.int32),
                   jax.ShapeDtypeStruct((1, n_blocks), jnp.int32)],
        name="moe_meta",
    )(counts)


def _runs_kernel(poff_ref, run_ref, out_ref):
    run = run_ref[...]
    row = lax.broadcasted_iota(jnp.int32, run.shape, 1)
    out_ref[...] = jnp.where(row == 2, run + poff_ref[...][None], run)


def _runs(poff, run):
    return pl.pallas_call(
        _runs_kernel,
        out_shape=jax.ShapeDtypeStruct(run.shape, jnp.int32),
        name="moe_runs",
    )(poff, run)


def _tile(ref, r, nt):
    return ref.at[pl.ds(pl.multiple_of(r * nt, nt), nt), :]


def _slot_specs(rows):
    assert TOP_K == 4
    return [pl.BlockSpec((1, 1, 1, rows), functools.partial(lambda k, i: (i, k, 0, 0), k), memory_space=pltpu.SMEM)
            for k in range(TOP_K)]


def _run_copies(run_ref, src_of, dst_of, sem, nt):
    ne = run_ref.shape[2]

    def per_expert(e, carry):
        n = run_ref[0, 0, e]

        @pl.when(n > 0)
        def _():
            pltpu.make_async_copy(src_of(run_ref[0, 1, e], run_ref[0, 2, e], n),
                                  dst_of(run_ref[0, 1, e], run_ref[0, 2, e], n), sem).start()
        return carry

    lax.fori_loop(0, ne, per_expert, 0)


def _dispatch_kernel(nt, cnt_ref, poff_ref, loc0_ref, loc1_ref, loc2_ref, loc3_ref, run_ref, h2_ref, xs_ref,
                     stage_ref, zero_ref, sem, zsem):
    loc_refs = (loc0_ref, loc1_ref, loc2_ref, loc3_ref)
    step = pl.program_id(0)
    last = pl.num_programs(0) - 1
    rows = h2_ref.shape[0] // nt
    n_assign = rows * TOP_K
    ne = cnt_ref.shape[1]
    par = step % 2
    unroll = 4

    def rows_at(ref, first, n):
        return ref.at[pl.ds(pl.multiple_of(first * nt, nt), n * nt), :]

    def wait_runs(half):
        pltpu.make_async_copy(stage_ref.at[half], xs_ref.at[pl.ds(0, n_assign * nt), :], sem.at[half]).wait()

    @pl.when(step == 0)
    def _():
        zero_ref[...] = jnp.zeros_like(zero_ref)

        def per_expert(e, carry):
            c = cnt_ref[0, e]
            pad = (MOE_BLOCK - (c % MOE_BLOCK)) % MOE_BLOCK

            @pl.when(pad > 0)
            def _():
                cp = pltpu.make_async_copy(rows_at(zero_ref, 0, pad), rows_at(xs_ref, poff_ref[0, e] + c, pad), zsem)
                cp.start()
                cp.wait()
            return poff_ref[0, e] + c + pad

        used = lax.fori_loop(0, ne, per_expert, 0)

        def spare(b, carry):
            cp = pltpu.make_async_copy(zero_ref, rows_at(xs_ref, b * MOE_BLOCK, MOE_BLOCK), zsem)
            cp.start()
            cp.wait()
            return carry

        lax.fori_loop(used // MOE_BLOCK, xs_ref.shape[0] // (MOE_BLOCK * nt), spare, 0)

    @pl.when(step >= 2)
    def _():
        wait_runs(par)

    def place_into(half):
        def place(i, carry):
            for u in range(unroll):
                tok = i * unroll + u
                row = _tile(h2_ref, tok, nt)[...]
                for k in range(TOP_K):
                    stage_ref[half, pl.ds(pl.multiple_of(loc_refs[k][0, 0, 0, tok], nt), nt), :] = row
            return carry

        lax.fori_loop(0, rows // unroll, place, 0)

    for half in range(2):
        pl.when(par == half)(functools.partial(place_into, half))

    _run_copies(run_ref,
                lambda first, base, n: stage_ref.at[par, pl.ds(pl.multiple_of(first * nt, nt), n * nt), :],
                lambda first, base, n: rows_at(xs_ref, base, n), sem.at[par], nt)

    @pl.when(step == last)
    def _():
        wait_runs(par)

        @pl.when(step >= 1)
        def _():
            wait_runs(1 - par)


def _dispatch(counts, poff, loc, runs, h2, n_rows, nt):
    t = h2.shape[0] // nt
    rows = ROUTE_ROWS
    tiles = t // rows
    ne = counts.shape[1]
    smem = pl.BlockSpec(memory_space=pltpu.SMEM)
    return pl.pallas_call(
        functools.partial(_dispatch_kernel, nt),
        grid=(tiles,),
        in_specs=[smem, smem,
                  *_slot_specs(rows),
                  pl.BlockSpec((1, SUBLANES, ne), lambda i: (i, 0, 0), memory_space=pltpu.SMEM),
                  pl.BlockSpec((rows * nt, LANES), lambda i: (i, 0))],
        out_specs=pl.BlockSpec(memory_space=pl.ANY),
        out_shape=jax.ShapeDtypeStruct((n_rows * nt, LANES), h2.dtype),
        scratch_shapes=[pltpu.VMEM((2, rows * TOP_K * nt, LANES), h2.dtype),
                        pltpu.VMEM((MOE_BLOCK * nt, LANES), h2.dtype),
                        pltpu.SemaphoreType.DMA((2,)), pltpu.SemaphoreType.DMA(())],
        compiler_params=pltpu.CompilerParams(
            dimension_semantics=("arbitrary",), vmem_limit_bytes=VMEM_LIMIT, has_side_effects=True),
        name="moe_dispatch",
    )(counts, poff, loc, loc, loc, loc, runs, h2)


def _expert_kernel(nt, blke_ref, blks_ref, cnt_ref, xs_ref, wgu_hbm, bgu_ref, wd_hbm, bd_ref, ys_ref,
                   wgu32_ref, wd32_ref, wgu_ref, wd_ref, wsem, run_ref):
    i = pl.program_id(0)
    last = pl.num_programs(0) - 1
    f = wd_ref.shape[0]
    e = blke_ref[0, i]
    valid = blks_ref[0, i] == i

    def fetch(expert, slot):
        pltpu.make_async_copy(wgu_hbm.at[expert], wgu32_ref.at[slot], wsem.at[slot]).start()
        pltpu.make_async_copy(wd_hbm.at[expert], wd32_ref.at[slot], wsem.at[slot]).start()

    def wait_fetch(slot):
        pltpu.make_async_copy(wgu_hbm.at[0], wgu32_ref.at[slot], wsem.at[slot]).wait()
        pltpu.make_async_copy(wd_hbm.at[0], wd32_ref.at[slot], wsem.at[slot]).wait()

    @pl.when(i == 0)
    def _():
        run_ref[0] = 0
        fetch(e, 0)

    @pl.when(jnp.logical_and(valid, jnp.logical_or(i == 0, e != blke_ref[0, jnp.maximum(i - 1, 0)])))
    def _():
        slot = run_ref[0] % 2
        wait_fetch(slot)
        nxt = i + (cnt_ref[0, e] + (MOE_BLOCK - 1)) // MOE_BLOCK
        nxt_c = jnp.minimum(nxt, last)

        @pl.when(blks_ref[0, nxt_c] == nxt)
        def _():
            fetch(blke_ref[0, nxt_c], 1 - slot)

        wgu_ref[...] = wgu32_ref[slot].astype(jnp.bfloat16)
        wd_ref[...] = wd32_ref[slot].astype(jnp.bfloat16)
        run_ref[0] = run_ref[0] + 1

    @pl.when(blks_ref[0, i] != i)
    def _():
        ys_ref[...] = jnp.zeros_like(ys_ref)

    @pl.when(blks_ref[0, i] == i)
    def _():
        x = jnp.concatenate([xs_ref[pl.ds(j, MOE_BLOCK, stride=nt), :] for j in range(nt)], axis=-1)
        gu = jnp.dot(x.astype(jnp.bfloat16), wgu_ref[...], preferred_element_type=jnp.float32) + bgu_ref[0]
        g = jnp.minimum(gu[:, 0:f], SWIGLU_LIMIT)
        lin = jnp.clip(gu[:, f:2 * f], -SWIGLU_LIMIT, SWIGLU_LIMIT)
        act = g * _sigmoid(SWIGLU_ALPHA * g) * (lin + 1.0)
        y = jnp.dot(act.astype(jnp.bfloat16), wd_ref[...], preferred_element_type=jnp.float32) + bd_ref[0]
        for j in range(nt):
            ys_ref[pl.ds(j, MOE_BLOCK, stride=nt), :] = y[:, j * LANES:(j + 1) * LANES]


def _experts(blk_e, blk_s, counts, xs, w_gate_up, b_gate_up, w_down, b_down, nt):
    ne, d, f2 = w_gate_up.shape
    f = f2 // 2
    n_blocks = blk_e.shape[1]
    blk_rows = MOE_BLOCK * nt
    grid_spec = pltpu.PrefetchScalarGridSpec(
        num_scalar_prefetch=3,
        grid=(n_blocks,),
        in_specs=[pl.BlockSpec((blk_rows, LANES), lambda i, be, bs, cn: (bs[0, i], 0)),
                  pl.BlockSpec(memory_space=pl.ANY),
                  pl.BlockSpec((1, 1, f2), lambda i, be, bs, cn: (be[0, i], 0, 0)),
                  pl.BlockSpec(memory_space=pl.ANY),
                  pl.BlockSpec((1, 1, d), lambda i, be, bs, cn: (be[0, i], 0, 0))],
        out_specs=pl.BlockSpec((blk_rows, LANES), lambda i, be, bs, cn: (i, 0)),
        scratch_shapes=[pltpu.VMEM((2, d, f2), jnp.float32), pltpu.VMEM((2, f, d), jnp.float32),
                        pltpu.VMEM((d, f2), jnp.bfloat16), pltpu.VMEM((f, d), jnp.bfloat16),
                        pltpu.SemaphoreType.DMA((2,)), pltpu.SMEM((1,), jnp.int32)],
    )
    return pl.pallas_call(
        functools.partial(_expert_kernel, nt),
        grid_spec=grid_spec,
        out_shape=jax.ShapeDtypeStruct(xs.shape, jnp.float32),
        compiler_params=pltpu.CompilerParams(
            dimension_semantics=("arbitrary",), vmem_limit_bytes=VMEM_LIMIT),
        name="moe_experts",
    )(blk_e, blk_s, counts, xs, w_gate_up, b_gate_up.reshape(ne, 1, f2), w_down, b_down.reshape(ne, 1, d))


def _combine_kernel(nt, loc0_ref, loc1_ref, loc2_ref, loc3_ref, g0_ref, g1_ref, g2_ref, g3_ref,
                    run_ref, runn_ref, x1_ref, gfin_ref, ys_ref, out_ref, stage_ref, acc_ref, sem):
    loc_refs = (loc0_ref, loc1_ref, loc2_ref, loc3_ref)
    gate_refs = (g0_ref, g1_ref, g2_ref, g3_ref)
    rows, d = x1_ref.shape
    n_assign = rows * TOP_K
    step = pl.program_id(0)
    par = step % 2
    unroll = 4

    def fetch(r_ref, half):
        _run_copies(r_ref,
                    lambda first, base, n: ys_ref.at[pl.ds(pl.multiple_of(base * nt, nt), n * nt), :],
                    lambda first, base, n: stage_ref.at[half, pl.ds(pl.multiple_of(first * nt, nt), n * nt), :],
                    sem.at[half], nt)

    @pl.when(step == 0)
    def _():
        fetch(run_ref, 0)

    @pl.when(step + 1 < pl.num_programs(0))
    def _():
        fetch(runn_ref, 1 - par)

    pltpu.make_async_copy(ys_ref.at[pl.ds(0, n_assign * nt), :], stage_ref.at[par], sem.at[par]).wait()

    def mix_from(half):
        def mix(i, carry):
            for u in range(unroll):
                tok = i * unroll + u
                acc = None
                for k in range(TOP_K):
                    row = stage_ref[half, pl.ds(pl.multiple_of(loc_refs[k][0, 0, 0, tok], nt), nt), :]
                    term = gate_refs[k][0, 0, 0, tok] * row
                    acc = term if acc is None else acc + term
                acc_ref[pl.ds(pl.multiple_of(tok * nt, nt), nt), :] = acc
            return carry

        lax.fori_loop(0, rows // unroll, mix, 0)

    for half in range(2):
        pl.when(par == half)(functools.partial(mix_from, half))

    y = jnp.concatenate([acc_ref[pl.ds(j, rows, stride=nt), :] for j in range(nt)], axis=-1)
    x = x1_ref[...] + y
    ms = jnp.mean(x * x, axis=-1, keepdims=True)
    out_ref[...] = x * lax.rsqrt(ms + NORM_EPS) * gfin_ref[...]


def _combine(loc, gate, runs, x1, g_final, ys, nt):
    t, d = x1.shape
    rows = ROUTE_ROWS
    tiles = t // rows
    ne = runs.shape[2]
    n_assign = rows * TOP_K
    smem = pltpu.SMEM
    return pl.pallas_call(
        functools.partial(_combine_kernel, nt),
        grid=(tiles,),
        in_specs=[*_slot_specs(rows), *_slot_specs(rows),
                  pl.BlockSpec((1, SUBLANES, ne), lambda i: (i, 0, 0), memory_space=smem),
                  pl.BlockSpec((1, SUBLANES, ne), lambda i: (jnp.minimum(i + 1, tiles - 1), 0, 0),
                               memory_space=smem),
                  pl.BlockSpec((rows, d), lambda i: (i, 0)),
                  _const_spec(g_final.shape),
                  pl.BlockSpec(memory_space=pl.ANY)],
        out_specs=pl.BlockSpec((rows, d), lambda i: (i, 0)),
        out_shape=jax.ShapeDtypeStruct((t, d), jnp.float32),
        scratch_shapes=[pltpu.VMEM((2, n_assign * nt, LANES), jnp.float32),
                        pltpu.VMEM((rows * nt, LANES), jnp.float32),
                        pltpu.SemaphoreType.DMA((2,))],
        compiler_params=pltpu.CompilerParams(
            dimension_semantics=("arbitrary",), vmem_limit_bytes=VMEM_LIMIT),
        name="moe_combine",
    )(loc, loc, loc, loc, gate, gate, gate, gate, runs, runs, x1, g_final, ys)


def _block_diag(blocks):
    g, r, c = blocks.shape
    eye = jnp.eye(g, dtype=blocks.dtype)
    return (blocks[:, :, None, :] * eye[:, None, :, None]).reshape(g * r, g * c)


def kernel(x, norm_mix_g, w_in, conv_w, conv_b, conv_ln_g, conv_ln_b, w_conv_out, ssm_a_re, ssm_a_im,
           ssm_log_step, ssm_b_re, ssm_b_im, ssm_c_re, ssm_c_im, ssm_d, w_ssm_glu, b_ssm_glu, w_ssm_out,
           w_out, norm_ffn_g, w_router, b_router, w_gate_up, b_gate_up, w_down, b_down, norm_final_g):
    bsz, seq, d = x.shape
    depth = w_in.shape[0]
    bf = jnp.bfloat16
    nt = d // LANES
    n_tok = bsz * seq
    ne = w_router.shape[-1]
    n_blocks = -(-n_tok * TOP_K // MOE_BLOCK) + ne
    seg_len = MIX_ROWS // SCAN_SEGS

    def row(v):
        return v.reshape(1, -1)

    for l in range(depth):
        ar, ai, arp, aip, bbr, bbi = _ssm_disc(
            ssm_a_re[l], ssm_a_im[l], ssm_log_step[l],
            jnp.swapaxes(ssm_b_re[l], 1, 2), jnp.swapaxes(ssm_b_im[l], 1, 2), seg_len)
        bblk = jnp.concatenate([_block_diag(bbr), _block_diag(bbi)], axis=1).astype(bf)
        c_r = _block_diag(jnp.swapaxes(ssm_c_re[l], 1, 2)).astype(bf)
        c_i = _block_diag(jnp.swapaxes(ssm_c_im[l], 1, 2)).astype(bf)
        avec = jnp.stack([ar.reshape(-1), ai.reshape(-1), arp.reshape(-1), aip.reshape(-1)])

        x = _mixer(x, row(norm_mix_g[l]), w_in[l].astype(bf), conv_w[l], row(conv_b[l]),
                   row(conv_ln_g[l]), row(conv_ln_b[l]), w_conv_out[l].astype(bf), bblk, c_r, c_i, avec,
                   row(ssm_d[l]), w_ssm_glu[l].astype(bf), row(b_ssm_glu[l]), w_ssm_out[l].astype(bf),
                   w_out[l].astype(bf))

        x1 = x.reshape(n_tok, d)
        h2, loc, gate, run, counts = _route(x1, row(norm_ffn_g[l]), w_router[l], row(b_router[l]))
        poff, blk_e, blk_s = _meta(counts, n_blocks)
        runs = _runs(poff, run)
        xs = _dispatch(counts, poff, loc, runs, h2, n_blocks * MOE_BLOCK, nt)
        ys = _experts(blk_e, blk_s, counts, xs, w_gate_up[l], b_gate_up[l], w_down[l], b_down[l], nt)
        assert depth == 1
        x = _combine(loc, gate, runs, x1, row(norm_final_g), ys, nt).reshape(bsz, seq, d)
    return x
```

```python
import functools
import math

import jax
import jax.numpy as jnp
from jax import lax
from jax.experimental import pallas as pl
from jax.experimental.pallas import tpu as pltpu

NORM_EPS = 1e-6
SWIGLU_LIMIT = 7.0
SWIGLU_ALPHA = 1.702
TOP_K = 4
SSM_GROUP = 16

LANES = 128
SUBLANES = 8

MIX_ROWS = 512
SCAN_SEGS = SUBLANES
SEG_PAD = 8
SCAN_UNROLL = 8
TILE_UNROLL = 8
CONV_HALO = 32
CONV_ROWS = 64
GATE_PIECES = 8
ROUTE_ROWS = 512
MOE_BLOCK = 512
VMEM_LIMIT = 56 * 1024 * 1024


def _sigmoid(x):
    return jax.nn.sigmoid(x)


def _gelu_tanh(x):
    c = math.sqrt(2.0 / math.pi)
    return 0.5 * x * (1.0 + jnp.tanh(c * (x + 0.044715 * (x * x * x))))


def _const_spec(shape):
    nd = len(shape)
    return pl.BlockSpec(shape, lambda *_: (0,) * nd)


def _ssm_disc_kernel(seg_len, are_ref, aim_ref, ls_ref, bre_ref, bim_ref,
                     ar_ref, ai_ref, arp_ref, aip_ref, bbr_ref, bbi_ref):
    lr = are_ref[...]
    li = aim_ref[...]
    dt = jnp.exp(ls_ref[...])
    mag = jnp.exp(lr * dt)
    ar = mag * jnp.cos(li * dt)
    ai = mag * jnp.sin(li * dt)
    den = lr * lr + li * li
    fr = ((ar - 1.0) * lr + ai * li) / den
    fi = (ai * lr - (ar - 1.0) * li) / den
    ar_ref[...] = ar
    ai_ref[...] = ai
    magp = jnp.exp(lr * dt * seg_len)
    arp_ref[...] = magp * jnp.cos(li * dt * seg_len)
    aip_ref[...] = magp * jnp.sin(li * dt * seg_len)
    br = bre_ref[...]
    bi = bim_ref[...]
    bbr_ref[...] = fr[:, None, :] * br - fi[:, None, :] * bi
    bbi_ref[...] = fr[:, None, :] * bi + fi[:, None, :] * br


def _ssm_disc(a_re, a_im, log_step, bt_re, bt_im, seg_len):
    g, p = a_re.shape
    h = bt_re.shape[1]
    f32 = jnp.float32
    out_shape = [jax.ShapeDtypeStruct((g, p), f32)] * 4 + [jax.ShapeDtypeStruct((g, h, p), f32)] * 2
    return pl.pallas_call(
        functools.partial(_ssm_disc_kernel, float(seg_len)),
        out_shape=out_shape,
        name="ssm_disc",
    )(a_re, a_im, log_step.reshape(g, 1), bt_re, bt_im)


def _mixer_kernel(dims, x_ref, gmix_ref, win_ref, wg_ref, convw_ref, convb_ref, lng_ref, lnb_ref, wco_ref,
                  bblk_ref, cr_ref, ci_ref, avec_ref, dskip_ref, wglu_ref, bglu_ref, wso_ref, wout_ref,
                  x1_ref, vbuf_ref, conv_ref, scan_ref, xst_ref, cin_ref, st_ref, sg_ref):
    d, dc, ds, gp, cw = dims
    rows = MIX_ROWS
    seg = rows // SCAN_SEGS
    pitch = seg + SEG_PAD
    nsl = gp // LANES
    c_idx = pl.program_id(1)

    half = rows // 2
    tail = 2 * half + 1

    @pl.when(c_idx == 0)
    def _():
        for lc in range(dc // LANES):
            vbuf_ref[lc, pl.ds(0, CONV_HALO, stride=2), :] = jnp.zeros((CONV_HALO, LANES), jnp.float32)
        st_ref[...] = jnp.zeros_like(st_ref)

    @pl.when(c_idx != 0)
    def _():
        for lc in range(dc // LANES):
            vbuf_ref[lc, pl.ds(0, CONV_HALO, stride=2), :] = vbuf_ref[lc, pl.ds(tail, CONV_HALO, stride=2), :]

    x = x_ref[0]
    ms = jnp.mean(x * x, axis=-1, keepdims=True)
    h = (x * lax.rsqrt(ms + NORM_EPS) * gmix_ref[...]).astype(jnp.bfloat16)

    pa = jnp.dot(h, win_ref[:, 0:2 * dc], preferred_element_type=jnp.float32)
    vglu = pa[:, 0:dc] * _sigmoid(pa[:, dc:2 * dc])
    for lc in range(dc // LANES):
        ls = slice(lc * LANES, (lc + 1) * LANES)
        vbuf_ref[lc, pl.ds(2 * CONV_HALO, half, stride=2), :] = vglu[0:half, ls]
        vbuf_ref[lc, pl.ds(2 * CONV_HALO + 1, half, stride=2), :] = vglu[half:rows, ls]
        vbuf_ref[lc, pl.ds(1, CONV_HALO, stride=2), :] = vglu[half - CONV_HALO:half, ls]

    base = CONV_HALO - (cw - 1)
    nv = CONV_ROWS // SUBLANES

    def conv_chunk(rc, p):
        r0 = pl.multiple_of(rc * CONV_ROWS, CONV_ROWS)
        for lc in range(dc // LANES):
            ls = slice(lc * LANES, (lc + 1) * LANES)
            wv = [jnp.broadcast_to(convw_ref[k:k + 1, ls], (SUBLANES, LANES)) for k in range(cw)]
            acc = [jnp.zeros((SUBLANES, LANES), jnp.float32) for _ in range(nv)]
            for s in range(CONV_ROWS - SUBLANES + cw):
                win = vbuf_ref[lc, pl.ds(2 * (r0 + base + s) + p, SUBLANES, stride=2), :]
                for i in range(nv):
                    k = s - SUBLANES * i
                    if 0 <= k < cw:
                        acc[i] = acc[i] + wv[k] * win
            for i in range(nv):
                conv_ref[pl.ds(p * half + r0 + SUBLANES * i, SUBLANES), ls] = acc[i]

    n_rc = half // CONV_ROWS
    per_rc = wg_ref.shape[0] // n_rc

    def conv_and_gates(rc, carry):
        conv_chunk(rc, 0)
        conv_chunk(rc, 1)
        for q in range(per_rc):
            piece = rc * per_rc + q
            sg_ref[piece] = _sigmoid(jnp.dot(h, wg_ref[piece], preferred_element_type=jnp.float32))
        return carry

    lax.fori_loop(0, n_rc, conv_and_gates, 0)
    sg = jnp.concatenate([sg_ref[n] for n in range(wg_ref.shape[0])], axis=-1)
    sg_conv = sg[:, 0:d]
    sg_ssm = sg[:, d:2 * d]

    v = conv_ref[...] + convb_ref[...]
    mu = jnp.mean(v, axis=-1, keepdims=True)
    vc = v - mu
    var = jnp.mean(vc * vc, axis=-1, keepdims=True)
    v = vc * lax.rsqrt(var + NORM_EPS) * lng_ref[...] + lnb_ref[...]
    v = v * _sigmoid(v)
    y_conv = jnp.dot(v.astype(jnp.bfloat16), wco_ref[...], preferred_element_type=jnp.float32)

    u = jnp.dot(h, win_ref[:, 2 * dc:2 * dc + ds], preferred_element_type=jnp.float32)
    bu = jnp.dot(u.astype(jnp.bfloat16), bblk_ref[...], preferred_element_type=jnp.float32)
    for n in range(2 * nsl):
        for s in range(SCAN_SEGS):
            scan_ref[n, s * pitch:s * pitch + seg, :] = bu[s * seg:(s + 1) * seg, n * LANES:(n + 1) * LANES]

    def bcast(row, n):
        return jnp.broadcast_to(avec_ref[row:row + 1, n * LANES:(n + 1) * LANES], (SUBLANES, LANES))

    a_r = [bcast(0, n) for n in range(nsl)]
    a_i = [bcast(1, n) for n in range(nsl)]

    def step(j, st, store):
        out = []
        for n in range(nsl):
            sr, si = st[2 * n], st[2 * n + 1]
            br = scan_ref[n, pl.ds(j, SCAN_SEGS, stride=pitch), :]
            bi = scan_ref[nsl + n, pl.ds(j, SCAN_SEGS, stride=pitch), :]
            nr = a_r[n] * sr - a_i[n] * si + br
            ni = a_r[n] * si + a_i[n] * sr + bi
            if store:
                scan_ref[n, pl.ds(j, SCAN_SEGS, stride=pitch), :] = nr
                scan_ref[nsl + n, pl.ds(j, SCAN_SEGS, stride=pitch), :] = ni
            out += [nr, ni]
        return tuple(out)

    def steps(jj, st, store):
        for q in range(SCAN_UNROLL):
            st = step(jj * SCAN_UNROLL + q, st, store)
        return st

    zero = jnp.zeros((SUBLANES, LANES), jnp.float32)
    fin = lax.fori_loop(0, seg // SCAN_UNROLL, functools.partial(steps, store=False), (zero,) * (2 * nsl))

    for n in range(nsl):
        fr, fi = fin[2 * n], fin[2 * n + 1]
        ap_r = avec_ref[2:3, n * LANES:(n + 1) * LANES]
        ap_i = avec_ref[3:4, n * LANES:(n + 1) * LANES]
        c_r = st_ref[0:1, n * LANES:(n + 1) * LANES]
        c_i = st_ref[1:2, n * LANES:(n + 1) * LANES]
        for s in range(SCAN_SEGS):
            cin_ref[2 * n, s:s + 1, :] = c_r
            cin_ref[2 * n + 1, s:s + 1, :] = c_i
            n_r = ap_r * c_r - ap_i * c_i + fr[s:s + 1, :]
            n_i = ap_r * c_i + ap_i * c_r + fi[s:s + 1, :]
            c_r, c_i = n_r, n_i
        st_ref[0:1, n * LANES:(n + 1) * LANES] = c_r
        st_ref[1:2, n * LANES:(n + 1) * LANES] = c_i

    lax.fori_loop(0, seg // SCAN_UNROLL, functools.partial(steps, store=True),
                  tuple(cin_ref[q] for q in range(2 * nsl)))

    for n in range(2 * nsl):
        for s in range(SCAN_SEGS):
            xst_ref[s * seg:(s + 1) * seg, n * LANES:(n + 1) * LANES] = (
                scan_ref[n, s * pitch:s * pitch + seg, :].astype(jnp.bfloat16))

    y = (jnp.dot(xst_ref[:, 0:gp], cr_ref[...], preferred_element_type=jnp.float32)
         - jnp.dot(xst_ref[:, gp:2 * gp], ci_ref[...], preferred_element_type=jnp.float32))
    y = _gelu_tanh(y + dskip_ref[...] * u)
    glu = jnp.dot(y.astype(jnp.bfloat16), wglu_ref[...], preferred_element_type=jnp.float32) + bglu_ref[...]
    y = y * _sigmoid(glu)
    y_ssm = jnp.dot(y.astype(jnp.bfloat16), wso_ref[...], preferred_element_type=jnp.float32)

    m = sg_conv * y_conv + sg_ssm * y_ssm
    x1_ref[0] = x + jnp.dot(m.astype(jnp.bfloat16), wout_ref[...], preferred_element_type=jnp.float32)


def _mixer(x, gmix, w_in, conv_w, conv_b, ln_g, ln_b, w_co, bblk, c_r, c_i, avec, d_skip,
           w_glu, b_glu, w_so, w_out):
    b, s, d = x.shape
    cw, dc = conv_w.shape
    ds = d_skip.shape[-1]
    gp = c_r.shape[0]
    rows = MIX_ROWS
    seg = rows // SCAN_SEGS
    pitch = seg + SEG_PAD
    assert s % rows == 0 and cw - 1 <= CONV_HALO and gp % LANES == 0 and dc % LANES == 0
    dims = (d, dc, ds, gp, cw)
    g0 = 2 * dc + ds
    gw = 2 * d // GATE_PIECES
    assert rows // 2 // CONV_ROWS in (1, 2, 4, 8) and GATE_PIECES % (rows // 2 // CONV_ROWS) == 0
    w_gates = jnp.swapaxes(w_in[:, g0:].reshape(d, GATE_PIECES, gw), 0, 1)
    consts = [gmix, w_in[:, :g0], w_gates, conv_w, conv_b, ln_g, ln_b, w_co, bblk, c_r, c_i, avec, d_skip,
              w_glu, b_glu, w_so, w_out]
    return pl.pallas_call(
        functools.partial(_mixer_kernel, dims),
        grid=(b, s // rows),
        in_specs=[pl.BlockSpec((1, rows, d), lambda i, j: (i, j, 0))] + [_const_spec(c.shape) for c in consts],
        out_specs=pl.BlockSpec((1, rows, d), lambda i, j: (i, j, 0)),
        out_shape=jax.ShapeDtypeStruct((b, s, d), jnp.float32),
        scratch_shapes=[
            pltpu.VMEM((dc // LANES, 2 * CONV_HALO + rows, LANES), jnp.float32),
            pltpu.VMEM((rows, dc), jnp.float32),
            pltpu.VMEM((2 * gp // LANES, SCAN_SEGS * pitch, LANES), jnp.float32),
            pltpu.VMEM((rows, 2 * gp), jnp.bfloat16),
            pltpu.VMEM((2 * gp // LANES, SUBLANES, LANES), jnp.float32),
            pltpu.VMEM((2, gp), jnp.float32),
            pltpu.VMEM((GATE_PIECES, rows, gw), jnp.float32),
        ],
        compiler_params=pltpu.CompilerParams(
            dimension_semantics=("arbitrary", "arbitrary"), vmem_limit_bytes=VMEM_LIMIT),
        name="mixer",
    )(x, *consts)


def _route_kernel(x1_ref, g_ref, wr_ref, br_ref, h2_ref, loc_ref, gate_ref, run_ref, cnt_ref, carry_ref):
    rows, d = x1_ref.shape
    ne = wr_ref.shape[1]

    @pl.when(pl.program_id(0) == 0)
    def _():
        carry_ref[...] = jnp.zeros_like(carry_ref)

    x = x1_ref[...]
    ms = jnp.mean(x * x, axis=-1, keepdims=True)
    h2 = x * lax.rsqrt(ms + NORM_EPS) * g_ref[...]

    nt = d // LANES
    for j in range(nt):
        h2_ref[pl.ds(j, rows, stride=nt), :] = h2[:, j * LANES:(j + 1) * LANES]

    bf = jnp.bfloat16
    h_hi = h2.astype(bf)
    h_lo = (h2 - h_hi.astype(jnp.float32)).astype(bf)
    w = wr_ref[...]
    w_hi = w.astype(bf)
    w_lo = (w - w_hi.astype(jnp.float32)).astype(bf)
    logits = (jnp.dot(h_hi, w_hi, preferred_element_type=jnp.float32)
              + jnp.dot(h_lo, w_hi, preferred_element_type=jnp.float32)
              + jnp.dot(h_hi, w_lo, preferred_element_type=jnp.float32)) + br_ref[...]

    pad = jnp.full((rows, LANES - ne), -jnp.inf, jnp.float32)
    work = jnp.concatenate([logits, pad], axis=1).T[0:ne, :]
    eidx = lax.broadcasted_iota(jnp.int32, (ne, rows), 0).astype(jnp.float32)
    sels, vals = [], []
    for k in range(TOP_K):
        mx = jnp.max(work, axis=0, keepdims=True)
        idx = jnp.min(jnp.where(work == mx, eidx, float(ne)), axis=0, keepdims=True)
        sel = eidx == idx
        sels.append(sel)
        vals.append(mx)
        work = jnp.where(sel, -jnp.inf, work)

    exps = [jnp.exp(v - vals[0]) for v in vals]
    tot = exps[0]
    for e in exps[1:]:
        tot = tot + e
    for k in range(TOP_K):
        gate_ref[0, k] = exps[k] / tot

    cnt = jnp.zeros((ne, rows), jnp.float32)
    for sel in sels:
        cnt = cnt + sel.astype(jnp.float32)
    r_i = lax.broadcasted_iota(jnp.int32, (rows, rows), 0)
    c_i = lax.broadcasted_iota(jnp.int32, (rows, rows), 1)
    triu = (r_i < c_i).astype(bf)
    before = jnp.dot(cnt.astype(bf), triu, preferred_element_type=jnp.float32)
    cnt_col = jnp.sum(cnt, axis=1, keepdims=True)
    e_r = lax.broadcasted_iota(jnp.int32, (ne, ne), 0)
    e_c = lax.broadcasted_iota(jnp.int32, (ne, ne), 1)
    col_b = jnp.broadcast_to(cnt_col, (ne, ne))
    cnt_row = jnp.sum(jnp.where(e_r == e_c, col_b, 0.0), axis=0, keepdims=True)
    first_row = jnp.sum(jnp.where(e_r < e_c, col_b, 0.0), axis=0, keepdims=True)
    first_col = jnp.sum(jnp.where(e_c < e_r, jnp.broadcast_to(cnt_row, (ne, ne)), 0.0), axis=1, keepdims=True)
    pos = before + first_col
    for k in range(TOP_K):
        lk = jnp.sum(jnp.where(sels[k], pos, 0.0), axis=0, keepdims=True).astype(jnp.int32)
        loc_ref[0, k] = lk * nt
    run_ref[...] = jnp.zeros_like(run_ref)
    run_ref[0, 0:1, :] = cnt_row.astype(jnp.int32)
    run_ref[0, 1:2, :] = first_row.astype(jnp.int32)
    run_ref[0, 2:3, :] = carry_ref[...].astype(jnp.int32)
    carry_ref[...] = carry_ref[...] + cnt_row
    cnt_ref[...] = carry_ref[...].astype(jnp.int32)


def _route(x1, g_ffn, w_router, b_router):
    t, d = x1.shape
    ne = w_router.shape[1]
    rows = ROUTE_ROWS
    assert t % rows == 0 and d % LANES == 0
    assert TOP_K <= SUBLANES and ne <= LANES
    nt = d // LANES
    tiles = t // rows
    per_tile = pl.BlockSpec((1, TOP_K, 1, rows), lambda i: (i, 0, 0, 0))
    return pl.pallas_call(
        _route_kernel,
        grid=(tiles,),
        in_specs=[pl.BlockSpec((rows, d), lambda i: (i, 0)),
                  _const_spec(g_ffn.shape), _const_spec(w_router.shape), _const_spec(b_router.shape)],
        out_specs=[pl.BlockSpec((rows * nt, LANES), lambda i: (i, 0)),
                   per_tile, per_tile,
                   pl.BlockSpec((1, SUBLANES, ne), lambda i: (i, 0, 0)),
                   _const_spec((1, ne))],
        out_shape=[jax.ShapeDtypeStruct((t * nt, LANES), jnp.float32),
                   jax.ShapeDtypeStruct((tiles, TOP_K, 1, rows), jnp.int32),
                   jax.ShapeDtypeStruct((tiles, TOP_K, 1, rows), jnp.float32),
                   jax.ShapeDtypeStruct((tiles, SUBLANES, ne), jnp.int32),
                   jax.ShapeDtypeStruct((1, ne), jnp.int32)],
        scratch_shapes=[pltpu.VMEM((1, ne), jnp.float32)],
        compiler_params=pltpu.CompilerParams(
            dimension_semantics=("arbitrary",), vmem_limit_bytes=VMEM_LIMIT),
        name="route",
    )(x1, g_ffn, w_router, b_router)


def _meta_kernel(cnt_ref, poff_ref, blke_ref, blks_ref):
    ne = cnt_ref.shape[1]
    nb = blke_ref.shape[1]
    f32 = jnp.float32
    cnt = cnt_ref[...].astype(f32)
    nblk = jnp.floor((cnt + (MOE_BLOCK - 1)) * (1.0 / MOE_BLOCK))
    e_r = lax.broadcasted_iota(jnp.int32, (ne, ne), 0)
    e_c = lax.broadcasted_iota(jnp.int32, (ne, ne), 1)
    nb_rows = jnp.broadcast_to(nblk, (ne, ne))
    bend_col = jnp.sum(jnp.where(e_c <= e_r, nb_rows, 0.0), axis=1, keepdims=True)
    nb_col = jnp.sum(jnp.where(e_c == e_r, nb_rows, 0.0), axis=1, keepdims=True)
    boff = jnp.sum(jnp.where(e_r < e_c, jnp.broadcast_to(nb_col, (ne, ne)), 0.0),
                   axis=0, keepdims=True)
    poff_ref[...] = (boff * MOE_BLOCK).astype(jnp.int32)
    total = jnp.sum(nblk, axis=1, keepdims=True)
    blk = lax.broadcasted_iota(jnp.int32, (ne, nb), 1).astype(f32)
    done = jnp.where(jnp.broadcast_to(bend_col, (ne, nb)) <= blk, 1.0, 0.0)
    be = jnp.minimum(jnp.sum(done, axis=0, keepdims=True), ne - 1.0)
    bid = lax.broadcasted_iota(jnp.int32, (1, nb), 1).astype(f32)
    last_e = jnp.sum(jnp.where(bid == total - 1.0, be, 0.0), axis=1, keepdims=True)
    blke_ref[...] = jnp.where(bid < total, be, last_e).astype(jnp.int32)
    blks_ref[...] = jnp.minimum(bid, total - 1.0).astype(jnp.int32)


def _meta(counts, n_blocks):
    ne = counts.shape[1]
    return pl.pallas_call(
        _meta_kernel,
        out_shape=[jax.ShapeDtypeStruct((1, ne), jnp.int32),
                   jax.ShapeDtypeStruct((1, n_blocks), jnp.int32),
                   jax.ShapeDtypeStruct((1, n_blocks), jnp.int32)],
        name="moe_meta",
    )(counts)


def _runs_kernel(poff_ref, run_ref, out_ref):
    run = run_ref[...]
    row = lax.broadcasted_iota(jnp.int32, run.shape, 1)
    out_ref[...] = jnp.where(row == 2, run + poff_ref[...][None], run)


def _runs(poff, run):
    return pl.pallas_call(
        _runs_kernel,
        out_shape=jax.ShapeDtypeStruct(run.shape, jnp.int32),
        name="moe_runs",
    )(poff, run)


def _tile(ref, r, nt):
    return ref.at[pl.ds(pl.multiple_of(r * nt, nt), nt), :]


def _slot_specs(rows):
    assert TOP_K == 4
    return [pl.BlockSpec((1, 1, 1, rows), functools.partial(lambda k, i: (i, k, 0, 0), k), memory_space=pltpu.SMEM)
            for k in range(TOP_K)]


def _run_copies(run_ref, src_of, dst_of, sem, nt):
    ne = run_ref.shape[2]

    def per_expert(e, carry):
        n = run_ref[0, 0, e]

        @pl.when(n > 0)
        def _():
            pltpu.make_async_copy(src_of(run_ref[0, 1, e], run_ref[0, 2, e], n),
                                  dst_of(run_ref[0, 1, e], run_ref[0, 2, e], n), sem).start()
        return carry

    lax.fori_loop(0, ne, per_expert, 0)


def _dispatch_kernel(nt, cnt_ref, poff_ref, loc0_ref, loc1_ref, loc2_ref, loc3_ref, run_ref, h2_ref, xs_ref,
                     stage_ref, zero_ref, sem, zsem):
    loc_refs = (loc0_ref, loc1_ref, loc2_ref, loc3_ref)
    step = pl.program_id(0)
    last = pl.num_programs(0) - 1
    rows = h2_ref.shape[0] // nt
    n_assign = rows * TOP_K
    ne = cnt_ref.shape[1]
    par = step % 2
    unroll = TILE_UNROLL

    def rows_at(ref, first, n):
        return ref.at[pl.ds(pl.multiple_of(first * nt, nt), n * nt), :]

    def wait_runs(half):
        pltpu.make_async_copy(stage_ref.at[half], xs_ref.at[pl.ds(0, n_assign * nt), :], sem.at[half]).wait()

    @pl.when(step == 0)
    def _():
        zero_ref[...] = jnp.zeros_like(zero_ref)

        def per_expert(e, carry):
            c = cnt_ref[0, e]
            pad = (MOE_BLOCK - (c % MOE_BLOCK)) % MOE_BLOCK

            @pl.when(pad > 0)
            def _():
                cp = pltpu.make_async_copy(rows_at(zero_ref, 0, pad), rows_at(xs_ref, poff_ref[0, e] + c, pad), zsem)
                cp.start()
                cp.wait()
            return poff_ref[0, e] + c + pad

        used = lax.fori_loop(0, ne, per_expert, 0)

        def spare(b, carry):
            cp = pltpu.make_async_copy(zero_ref, rows_at(xs_ref, b * MOE_BLOCK, MOE_BLOCK), zsem)
            cp.start()
            cp.wait()
            return carry

        lax.fori_loop(used // MOE_BLOCK, xs_ref.shape[0] // (MOE_BLOCK * nt), spare, 0)

    @pl.when(step >= 2)
    def _():
        wait_runs(par)

    def place_into(half):
        def place(i, carry):
            for u in range(unroll):
                tok = i * unroll + u
                row = _tile(h2_ref, tok, nt)[...]
                for k in range(TOP_K):
                    stage_ref[half, pl.ds(pl.multiple_of(loc_refs[k][0, 0, 0, tok], nt), nt), :] = row
            return carry

        lax.fori_loop(0, rows // unroll, place, 0)

    for half in range(2):
        pl.when(par == half)(functools.partial(place_into, half))

    _run_copies(run_ref,
                lambda first, base, n: stage_ref.at[par, pl.ds(pl.multiple_of(first * nt, nt), n * nt), :],
                lambda first, base, n: rows_at(xs_ref, base, n), sem.at[par], nt)

    @pl.when(step == last)
    def _():
        wait_runs(par)

        @pl.when(step >= 1)
        def _():
            wait_runs(1 - par)


def _dispatch(counts, poff, loc, runs, h2, n_rows, nt):
    t = h2.shape[0] // nt
    rows = ROUTE_ROWS
    tiles = t // rows
    ne = counts.shape[1]
    smem = pl.BlockSpec(memory_space=pltpu.SMEM)
    return pl.pallas_call(
        functools.partial(_dispatch_kernel, nt),
        grid=(tiles,),
        in_specs=[smem, smem,
                  *_slot_specs(rows),
                  pl.BlockSpec((1, SUBLANES, ne), lambda i: (i, 0, 0), memory_space=pltpu.SMEM),
                  pl.BlockSpec((rows * nt, LANES), lambda i: (i, 0))],
        out_specs=pl.BlockSpec(memory_space=pl.ANY),
        out_shape=jax.ShapeDtypeStruct((n_rows * nt, LANES), h2.dtype),
        scratch_shapes=[pltpu.VMEM((2, rows * TOP_K * nt, LANES), h2.dtype),
                        pltpu.VMEM((MOE_BLOCK * nt, LANES), h2.dtype),
                        pltpu.SemaphoreType.DMA((2,)), pltpu.SemaphoreType.DMA(())],
        compiler_params=pltpu.CompilerParams(
            dimension_semantics=("arbitrary",), vmem_limit_bytes=VMEM_LIMIT, has_side_effects=True),
        name="moe_dispatch",
    )(counts, poff, loc, loc, loc, loc, runs, h2)


def _expert_kernel(nt, blke_ref, blks_ref, cnt_ref, xs_ref, wgu_hbm, bgu_ref, wd_hbm, bd_ref, ys_ref,
                   wgu32_ref, wd32_ref, wgu_ref, wd_ref, wsem, run_ref):
    i = pl.program_id(0)
    last = pl.num_programs(0) - 1
    f = wd_ref.shape[0]
    e = blke_ref[0, i]
    valid = blks_ref[0, i] == i

    def fetch(expert, slot):
        pltpu.make_async_copy(wgu_hbm.at[expert], wgu32_ref.at[slot], wsem.at[slot]).start()
        pltpu.make_async_copy(wd_hbm.at[expert], wd32_ref.at[slot], wsem.at[slot]).start()

    def wait_fetch(slot):
        pltpu.make_async_copy(wgu_hbm.at[0], wgu32_ref.at[slot], wsem.at[slot]).wait()
        pltpu.make_async_copy(wd_hbm.at[0], wd32_ref.at[slot], wsem.at[slot]).wait()

    @pl.when(i == 0)
    def _():
        run_ref[0] = 0
        fetch(e, 0)

    @pl.when(jnp.logical_and(valid, jnp.logical_or(i == 0, e != blke_ref[0, jnp.maximum(i - 1, 0)])))
    def _():
        slot = run_ref[0] % 2
        wait_fetch(slot)
        nxt = i + (cnt_ref[0, e] + (MOE_BLOCK - 1)) // MOE_BLOCK
        nxt_c = jnp.minimum(nxt, last)

        @pl.when(blks_ref[0, nxt_c] == nxt)
        def _():
            fetch(blke_ref[0, nxt_c], 1 - slot)

        wgu_ref[...] = wgu32_ref[slot].astype(jnp.bfloat16)
        wd_ref[...] = wd32_ref[slot].astype(jnp.bfloat16)
        run_ref[0] = run_ref[0] + 1

    @pl.when(blks_ref[0, i] != i)
    def _():
        ys_ref[...] = jnp.zeros_like(ys_ref)

    @pl.when(blks_ref[0, i] == i)
    def _():
        x = jnp.concatenate([xs_ref[pl.ds(j, MOE_BLOCK, stride=nt), :] for j in range(nt)], axis=-1)
        gu = jnp.dot(x.astype(jnp.bfloat16), wgu_ref[...], preferred_element_type=jnp.float32) + bgu_ref[0]
        g = jnp.minimum(gu[:, 0:f], SWIGLU_LIMIT)
        lin = jnp.clip(gu[:, f:2 * f], -SWIGLU_LIMIT, SWIGLU_LIMIT)
        act = g * _sigmoid(SWIGLU_ALPHA * g) * (lin + 1.0)
        y = jnp.dot(act.astype(jnp.bfloat16), wd_ref[...], preferred_element_type=jnp.float32) + bd_ref[0]
        for j in range(nt):
            ys_ref[pl.ds(j, MOE_BLOCK, stride=nt), :] = y[:, j * LANES:(j + 1) * LANES]


def _experts(blk_e, blk_s, counts, xs, w_gate_up, b_gate_up, w_down, b_down, nt):
    ne, d, f2 = w_gate_up.shape
    f = f2 // 2
    n_blocks = blk_e.shape[1]
    blk_rows = MOE_BLOCK * nt
    grid_spec = pltpu.PrefetchScalarGridSpec(
        num_scalar_prefetch=3,
        grid=(n_blocks,),
        in_specs=[pl.BlockSpec((blk_rows, LANES), lambda i, be, bs, cn: (bs[0, i], 0)),
                  pl.BlockSpec(memory_space=pl.ANY),
                  pl.BlockSpec((1, 1, f2), lambda i, be, bs, cn: (be[0, i], 0, 0)),
                  pl.BlockSpec(memory_space=pl.ANY),
                  pl.BlockSpec((1, 1, d), lambda i, be, bs, cn: (be[0, i], 0, 0))],
        out_specs=pl.BlockSpec((blk_rows, LANES), lambda i, be, bs, cn: (i, 0)),
        scratch_shapes=[pltpu.VMEM((2, d, f2), jnp.float32), pltpu.VMEM((2, f, d), jnp.float32),
                        pltpu.VMEM((d, f2), jnp.bfloat16), pltpu.VMEM((f, d), jnp.bfloat16),
                        pltpu.SemaphoreType.DMA((2,)), pltpu.SMEM((1,), jnp.int32)],
    )
    return pl.pallas_call(
        functools.partial(_expert_kernel, nt),
        grid_spec=grid_spec,
        out_shape=jax.ShapeDtypeStruct(xs.shape, jnp.float32),
        compiler_params=pltpu.CompilerParams(
            dimension_semantics=("arbitrary",), vmem_limit_bytes=VMEM_LIMIT),
        name="moe_experts",
    )(blk_e, blk_s, counts, xs, w_gate_up, b_gate_up.reshape(ne, 1, f2), w_down, b_down.reshape(ne, 1, d))


def _combine_kernel(nt, loc0_ref, loc1_ref, loc2_ref, loc3_ref, g0_ref, g1_ref, g2_ref, g3_ref,
                    run_ref, runn_ref, x1_ref, gfin_ref, ys_ref, out_ref, stage_ref, acc_ref, sem):
    loc_refs = (loc0_ref, loc1_ref, loc2_ref, loc3_ref)
    gate_refs = (g0_ref, g1_ref, g2_ref, g3_ref)
    rows, d = x1_ref.shape
    n_assign = rows * TOP_K
    step = pl.program_id(0)
    par = step % 2
    unroll = TILE_UNROLL

    def fetch(r_ref, half):
        _run_copies(r_ref,
                    lambda first, base, n: ys_ref.at[pl.ds(pl.multiple_of(base * nt, nt), n * nt), :],
                    lambda first, base, n: stage_ref.at[half, pl.ds(pl.multiple_of(first * nt, nt), n * nt), :],
                    sem.at[half], nt)

    @pl.when(step == 0)
    def _():
        fetch(run_ref, 0)

    @pl.when(step + 1 < pl.num_programs(0))
    def _():
        fetch(runn_ref, 1 - par)

    pltpu.make_async_copy(ys_ref.at[pl.ds(0, n_assign * nt), :], stage_ref.at[par], sem.at[par]).wait()

    def mix_from(half):
        def mix(i, carry):
            for u in range(unroll):
                tok = i * unroll + u
                acc = None
                for k in range(TOP_K):
                    row = stage_ref[half, pl.ds(pl.multiple_of(loc_refs[k][0, 0, 0, tok], nt), nt), :]
                    term = gate_refs[k][0, 0, 0, tok] * row
                    acc = term if acc is None else acc + term
                acc_ref[pl.ds(pl.multiple_of(tok * nt, nt), nt), :] = acc
            return carry

        lax.fori_loop(0, rows // unroll, mix, 0)

    for half in range(2):
        pl.when(par == half)(functools.partial(mix_from, half))

    y = jnp.concatenate([acc_ref[pl.ds(j, rows, stride=nt), :] for j in range(nt)], axis=-1)
    x = x1_ref[...] + y
    ms = jnp.mean(x * x, axis=-1, keepdims=True)
    out_ref[...] = x * lax.rsqrt(ms + NORM_EPS) * gfin_ref[...]


def _combine(loc, gate, runs, x1, g_final, ys, nt):
    t, d = x1.shape
    rows = ROUTE_ROWS
    tiles = t // rows
    ne = runs.shape[2]
    n_assign = rows * TOP_K
    smem = pltpu.SMEM
    return pl.pallas_call(
        functools.partial(_combine_kernel, nt),
        grid=(tiles,),
        in_specs=[*_slot_specs(rows), *_slot_specs(rows),
                  pl.BlockSpec((1, SUBLANES, ne), lambda i: (i, 0, 0), memory_space=smem),
                  pl.BlockSpec((1, SUBLANES, ne), lambda i: (jnp.minimum(i + 1, tiles - 1), 0, 0),
                               memory_space=smem),
                  pl.BlockSpec((rows, d), lambda i: (i, 0)),
                  _const_spec(g_final.shape),
                  pl.BlockSpec(memory_space=pl.ANY)],
        out_specs=pl.BlockSpec((rows, d), lambda i: (i, 0)),
        out_shape=jax.ShapeDtypeStruct((t, d), jnp.float32),
        scratch_shapes=[pltpu.VMEM((2, n_assign * nt, LANES), jnp.float32),
                        pltpu.VMEM((rows * nt, LANES), jnp.float32),
                        pltpu.SemaphoreType.DMA((2,))],
        compiler_params=pltpu.CompilerParams(
            dimension_semantics=("arbitrary",), vmem_limit_bytes=VMEM_LIMIT),
        name="moe_combine",
    )(loc, loc, loc, loc, gate, gate, gate, gate, runs, runs, x1, g_final, ys)


def _block_diag(blocks):
    g, r, c = blocks.shape
    eye = jnp.eye(g, dtype=blocks.dtype)
    return (blocks[:, :, None, :] * eye[:, None, :, None]).reshape(g * r, g * c)


def kernel(x, norm_mix_g, w_in, conv_w, conv_b, conv_ln_g, conv_ln_b, w_conv_out, ssm_a_re, ssm_a_im,
           ssm_log_step, ssm_b_re, ssm_b_im, ssm_c_re, ssm_c_im, ssm_d, w_ssm_glu, b_ssm_glu, w_ssm_out,
           w_out, norm_ffn_g, w_router, b_router, w_gate_up, b_gate_up, w_down, b_down, norm_final_g):
    bsz, seq, d = x.shape
    depth = w_in.shape[0]
    bf = jnp.bfloat16
    nt = d // LANES
    n_tok = bsz * seq
    ne = w_router.shape[-1]
    n_blocks = -(-n_tok * TOP_K // MOE_BLOCK) + ne
    seg_len = MIX_ROWS // SCAN_SEGS

    def row(v):
        return v.reshape(1, -1)

    for l in range(depth):
        ar, ai, arp, aip, bbr, bbi = _ssm_disc(
            ssm_a_re[l], ssm_a_im[l], ssm_log_step[l],
            jnp.swapaxes(ssm_b_re[l], 1, 2), jnp.swapaxes(ssm_b_im[l], 1, 2), seg_len)
        bblk = jnp.concatenate([_block_diag(bbr), _block_diag(bbi)], axis=1).astype(bf)
        c_r = _block_diag(jnp.swapaxes(ssm_c_re[l], 1, 2)).astype(bf)
        c_i = _block_diag(jnp.swapaxes(ssm_c_im[l], 1, 2)).astype(bf)
        avec = jnp.stack([ar.reshape(-1), ai.reshape(-1), arp.reshape(-1), aip.reshape(-1)])

        x = _mixer(x, row(norm_mix_g[l]), w_in[l].astype(bf), conv_w[l], row(conv_b[l]),
                   row(conv_ln_g[l]), row(conv_ln_b[l]), w_conv_out[l].astype(bf), bblk, c_r, c_i, avec,
                   row(ssm_d[l]), w_ssm_glu[l].astype(bf), row(b_ssm_glu[l]), w_ssm_out[l].astype(bf),
                   w_out[l].astype(bf))

        x1 = x.reshape(n_tok, d)
        h2, loc, gate, run, counts = _route(x1, row(norm_ffn_g[l]), w_router[l], row(b_router[l]))
        poff, blk_e, blk_s = _meta(counts, n_blocks)
        runs = _runs(poff, run)
        xs = _dispatch(counts, poff, loc, runs, h2, n_blocks * MOE_BLOCK, nt)
        ys = _experts(blk_e, blk_s, counts, xs, w_gate_up[l], b_gate_up[l], w_down[l], b_down[l], nt)
        assert depth == 1
        x = _combine(loc, gate, runs, x1, row(norm_final_g), ys, nt).reshape(bsz, seq, d)
    return x
```

```python
import functools
import math

import jax
import jax.numpy as jnp
from jax import lax
from jax.experimental import pallas as pl
from jax.experimental.pallas import tpu as pltpu

NORM_EPS = 1e-6
SWIGLU_LIMIT = 7.0
SWIGLU_ALPHA = 1.702
TOP_K = 4
SSM_GROUP = 16

LANES = 128
SUBLANES = 8

MIX_ROWS = 512
SCAN_SEGS = SUBLANES
SEG_PAD = 8
SCAN_UNROLL = 8
TILE_UNROLL = 8
CONV_HALO = 32
CONV_ROWS = 64
ROUTE_ROWS = 512
MOE_BLOCK = 512
VMEM_LIMIT = 56 * 1024 * 1024


def _sigmoid(x):
    return jax.nn.sigmoid(x)


def _gelu_tanh(x):
    c = math.sqrt(2.0 / math.pi)
    return 0.5 * x * (1.0 + jnp.tanh(c * (x + 0.044715 * (x * x * x))))


def _const_spec(shape):
    nd = len(shape)
    return pl.BlockSpec(shape, lambda *_: (0,) * nd)


def _ssm_disc_kernel(seg_len, are_ref, aim_ref, ls_ref, bre_ref, bim_ref,
                     ar_ref, ai_ref, arp_ref, aip_ref, bbr_ref, bbi_ref):
    lr = are_ref[...]
    li = aim_ref[...]
    dt = jnp.exp(ls_ref[...])
    mag = jnp.exp(lr * dt)
    ar = mag * jnp.cos(li * dt)
    ai = mag * jnp.sin(li * dt)
    den = lr * lr + li * li
    fr = ((ar - 1.0) * lr + ai * li) / den
    fi = (ai * lr - (ar - 1.0) * li) / den
    ar_ref[...] = ar
    ai_ref[...] = ai
    magp = jnp.exp(lr * dt * seg_len)
    arp_ref[...] = magp * jnp.cos(li * dt * seg_len)
    aip_ref[...] = magp * jnp.sin(li * dt * seg_len)
    br = bre_ref[...]
    bi = bim_ref[...]
    bbr_ref[...] = fr[:, None, :] * br - fi[:, None, :] * bi
    bbi_ref[...] = fr[:, None, :] * bi + fi[:, None, :] * br


def _ssm_disc(a_re, a_im, log_step, bt_re, bt_im, seg_len):
    g, p = a_re.shape
    h = bt_re.shape[1]
    f32 = jnp.float32
    out_shape = [jax.ShapeDtypeStruct((g, p), f32)] * 4 + [jax.ShapeDtypeStruct((g, h, p), f32)] * 2
    return pl.pallas_call(
        functools.partial(_ssm_disc_kernel, float(seg_len)),
        out_shape=out_shape,
        name="ssm_disc",
    )(a_re, a_im, log_step.reshape(g, 1), bt_re, bt_im)


def _mixer_kernel(dims, x_ref, gmix_ref, win_ref, convw_ref, convb_ref, lng_ref, lnb_ref, wco_ref,
                  bblk_ref, cr_ref, ci_ref, avec_ref, dskip_ref, wglu_ref, bglu_ref, wso_ref, wout_ref,
                  x1_ref, vbuf_ref, conv_ref, scan_ref, xst_ref, cin_ref, st_ref):
    d, dc, ds, gp, cw = dims
    rows = MIX_ROWS
    seg = rows // SCAN_SEGS
    pitch = seg + SEG_PAD
    nsl = gp // LANES
    c_idx = pl.program_id(1)

    half = rows // 2
    tail = 2 * half + 1

    @pl.when(c_idx == 0)
    def _():
        for lc in range(dc // LANES):
            vbuf_ref[lc, pl.ds(0, CONV_HALO, stride=2), :] = jnp.zeros((CONV_HALO, LANES), jnp.float32)
        st_ref[...] = jnp.zeros_like(st_ref)

    @pl.when(c_idx != 0)
    def _():
        for lc in range(dc // LANES):
            vbuf_ref[lc, pl.ds(0, CONV_HALO, stride=2), :] = vbuf_ref[lc, pl.ds(tail, CONV_HALO, stride=2), :]

    x = x_ref[0]
    ms = jnp.mean(x * x, axis=-1, keepdims=True)
    h = (x * lax.rsqrt(ms + NORM_EPS) * gmix_ref[...]).astype(jnp.bfloat16)

    pa = jnp.dot(h, win_ref[:, 0:2 * dc], preferred_element_type=jnp.float32)
    vglu = pa[:, 0:dc] * _sigmoid(pa[:, dc:2 * dc])
    for lc in range(dc // LANES):
        ls = slice(lc * LANES, (lc + 1) * LANES)
        vbuf_ref[lc, pl.ds(2 * CONV_HALO, half, stride=2), :] = vglu[0:half, ls]
        vbuf_ref[lc, pl.ds(2 * CONV_HALO + 1, half, stride=2), :] = vglu[half:rows, ls]
        vbuf_ref[lc, pl.ds(1, CONV_HALO, stride=2), :] = vglu[half - CONV_HALO:half, ls]

    base = CONV_HALO - (cw - 1)
    nv = CONV_ROWS // SUBLANES

    def conv_chunk(rc, p):
        r0 = rc * CONV_ROWS
        for lc in range(dc // LANES):
            ls = slice(lc * LANES, (lc + 1) * LANES)
            wv = [jnp.broadcast_to(convw_ref[k:k + 1, ls], (SUBLANES, LANES)) for k in range(cw)]
            acc = [jnp.zeros((SUBLANES, LANES), jnp.float32) for _ in range(nv)]
            for s in range(CONV_ROWS - SUBLANES + cw):
                win = vbuf_ref[lc, pl.ds(2 * (r0 + base + s) + p, SUBLANES, stride=2), :]
                for i in range(nv):
                    k = s - SUBLANES * i
                    if 0 <= k < cw:
                        acc[i] = acc[i] + wv[k] * win
            for i in range(nv):
                conv_ref[pl.ds(p * half + r0 + SUBLANES * i, SUBLANES), ls] = acc[i]

    g0 = 2 * dc + ds
    chunks = [(rc, p) for rc in range(half // CONV_ROWS) for p in range(2)]
    gw = 2 * d // len(chunks)
    sg = []
    for n, (rc, p) in enumerate(chunks):
        sg.append(_sigmoid(jnp.dot(h, win_ref[:, g0 + n * gw:g0 + (n + 1) * gw],
                                   preferred_element_type=jnp.float32)))
        conv_chunk(rc, p)
    sg = jnp.concatenate(sg, axis=-1)
    sg_conv = sg[:, 0:d]
    sg_ssm = sg[:, d:2 * d]

    v = conv_ref[...] + convb_ref[...]
    mu = jnp.mean(v, axis=-1, keepdims=True)
    vc = v - mu
    var = jnp.mean(vc * vc, axis=-1, keepdims=True)
    v = vc * lax.rsqrt(var + NORM_EPS) * lng_ref[...] + lnb_ref[...]
    v = v * _sigmoid(v)
    y_conv = jnp.dot(v.astype(jnp.bfloat16), wco_ref[...], preferred_element_type=jnp.float32)

    u = jnp.dot(h, win_ref[:, 2 * dc:2 * dc + ds], preferred_element_type=jnp.float32)
    bu = jnp.dot(u.astype(jnp.bfloat16), bblk_ref[...], preferred_element_type=jnp.float32)
    for n in range(2 * nsl):
        for s in range(SCAN_SEGS):
            scan_ref[n, s * pitch:s * pitch + seg, :] = bu[s * seg:(s + 1) * seg, n * LANES:(n + 1) * LANES]

    def bcast(row, n):
        return jnp.broadcast_to(avec_ref[row:row + 1, n * LANES:(n + 1) * LANES], (SUBLANES, LANES))

    a_r = [bcast(0, n) for n in range(nsl)]
    a_i = [bcast(1, n) for n in range(nsl)]

    def step(j, st, store):
        out = []
        for n in range(nsl):
            sr, si = st[2 * n], st[2 * n + 1]
            br = scan_ref[n, pl.ds(j, SCAN_SEGS, stride=pitch), :]
            bi = scan_ref[nsl + n, pl.ds(j, SCAN_SEGS, stride=pitch), :]
            nr = a_r[n] * sr - a_i[n] * si + br
            ni = a_r[n] * si + a_i[n] * sr + bi
            if store:
                scan_ref[n, pl.ds(j, SCAN_SEGS, stride=pitch), :] = nr
                scan_ref[nsl + n, pl.ds(j, SCAN_SEGS, stride=pitch), :] = ni
            out += [nr, ni]
        return tuple(out)

    def steps(jj, st, store):
        for q in range(SCAN_UNROLL):
            st = step(jj * SCAN_UNROLL + q, st, store)
        return st

    zero = jnp.zeros((SUBLANES, LANES), jnp.float32)
    fin = lax.fori_loop(0, seg // SCAN_UNROLL, functools.partial(steps, store=False), (zero,) * (2 * nsl))

    for n in range(nsl):
        fr, fi = fin[2 * n], fin[2 * n + 1]
        ap_r = avec_ref[2:3, n * LANES:(n + 1) * LANES]
        ap_i = avec_ref[3:4, n * LANES:(n + 1) * LANES]
        c_r = st_ref[0:1, n * LANES:(n + 1) * LANES]
        c_i = st_ref[1:2, n * LANES:(n + 1) * LANES]
        for s in range(SCAN_SEGS):
            cin_ref[2 * n, s:s + 1, :] = c_r
            cin_ref[2 * n + 1, s:s + 1, :] = c_i
            n_r = ap_r * c_r - ap_i * c_i + fr[s:s + 1, :]
            n_i = ap_r * c_i + ap_i * c_r + fi[s:s + 1, :]
            c_r, c_i = n_r, n_i
        st_ref[0:1, n * LANES:(n + 1) * LANES] = c_r
        st_ref[1:2, n * LANES:(n + 1) * LANES] = c_i

    lax.fori_loop(0, seg // SCAN_UNROLL, functools.partial(steps, store=True),
                  tuple(cin_ref[q] for q in range(2 * nsl)))

    for n in range(2 * nsl):
        for s in range(SCAN_SEGS):
            xst_ref[s * seg:(s + 1) * seg, n * LANES:(n + 1) * LANES] = (
                scan_ref[n, s * pitch:s * pitch + seg, :].astype(jnp.bfloat16))

    y = (jnp.dot(xst_ref[:, 0:gp], cr_ref[...], preferred_element_type=jnp.float32)
         - jnp.dot(xst_ref[:, gp:2 * gp], ci_ref[...], preferred_element_type=jnp.float32))
    y = _gelu_tanh(y + dskip_ref[...] * u)
    glu = jnp.dot(y.astype(jnp.bfloat16), wglu_ref[...], preferred_element_type=jnp.float32) + bglu_ref[...]
    y = y * _sigmoid(glu)
    y_ssm = jnp.dot(y.astype(jnp.bfloat16), wso_ref[...], preferred_element_type=jnp.float32)

    m = sg_conv * y_conv + sg_ssm * y_ssm
    x1_ref[0] = x + jnp.dot(m.astype(jnp.bfloat16), wout_ref[...], preferred_element_type=jnp.float32)


def _mixer(x, gmix, w_in, conv_w, conv_b, ln_g, ln_b, w_co, bblk, c_r, c_i, avec, d_skip,
           w_glu, b_glu, w_so, w_out):
    b, s, d = x.shape
    cw, dc = conv_w.shape
    ds = d_skip.shape[-1]
    gp = c_r.shape[0]
    rows = MIX_ROWS
    seg = rows // SCAN_SEGS
    pitch = seg + SEG_PAD
    assert s % rows == 0 and cw - 1 <= CONV_HALO and gp % LANES == 0 and dc % LANES == 0
    dims = (d, dc, ds, gp, cw)
    consts = [gmix, w_in, conv_w, conv_b, ln_g, ln_b, w_co, bblk, c_r, c_i, avec, d_skip,
              w_glu, b_glu, w_so, w_out]
    return pl.pallas_call(
        functools.partial(_mixer_kernel, dims),
        grid=(b, s // rows),
        in_specs=[pl.BlockSpec((1, rows, d), lambda i, j: (i, j, 0))] + [_const_spec(c.shape) for c in consts],
        out_specs=pl.BlockSpec((1, rows, d), lambda i, j: (i, j, 0)),
        out_shape=jax.ShapeDtypeStruct((b, s, d), jnp.float32),
        scratch_shapes=[
            pltpu.VMEM((dc // LANES, 2 * CONV_HALO + rows, LANES), jnp.float32),
            pltpu.VMEM((rows, dc), jnp.float32),
            pltpu.VMEM((2 * gp // LANES, SCAN_SEGS * pitch, LANES), jnp.float32),
            pltpu.VMEM((rows, 2 * gp), jnp.bfloat16),
            pltpu.VMEM((2 * gp // LANES, SUBLANES, LANES), jnp.float32),
            pltpu.VMEM((2, gp), jnp.float32),
        ],
        compiler_params=pltpu.CompilerParams(
            dimension_semantics=("arbitrary", "arbitrary"), vmem_limit_bytes=VMEM_LIMIT),
        name="mixer",
    )(x, *consts)


def _route_kernel(x1_ref, g_ref, wr_ref, br_ref, h2_ref, loc_ref, gate_ref, run_ref, cnt_ref, carry_ref):
    rows, d = x1_ref.shape
    ne = wr_ref.shape[1]

    @pl.when(pl.program_id(0) == 0)
    def _():
        carry_ref[...] = jnp.zeros_like(carry_ref)

    x = x1_ref[...]
    ms = jnp.mean(x * x, axis=-1, keepdims=True)
    h2 = x * lax.rsqrt(ms + NORM_EPS) * g_ref[...]

    nt = d // LANES
    for j in range(nt):
        h2_ref[pl.ds(j, rows, stride=nt), :] = h2[:, j * LANES:(j + 1) * LANES]

    bf = jnp.bfloat16
    h_hi = h2.astype(bf)
    h_lo = (h2 - h_hi.astype(jnp.float32)).astype(bf)
    w = wr_ref[...]
    w_hi = w.astype(bf)
    w_lo = (w - w_hi.astype(jnp.float32)).astype(bf)
    logits = (jnp.dot(h_hi, w_hi, preferred_element_type=jnp.float32)
              + jnp.dot(h_lo, w_hi, preferred_element_type=jnp.float32)
              + jnp.dot(h_hi, w_lo, preferred_element_type=jnp.float32)) + br_ref[...]

    pad = jnp.full((rows, LANES - ne), -jnp.inf, jnp.float32)
    work = jnp.concatenate([logits, pad], axis=1).T[0:ne, :]
    eidx = lax.broadcasted_iota(jnp.int32, (ne, rows), 0).astype(jnp.float32)
    sels, vals = [], []
    for k in range(TOP_K):
        mx = jnp.max(work, axis=0, keepdims=True)
        idx = jnp.min(jnp.where(work == mx, eidx, float(ne)), axis=0, keepdims=True)
        sel = eidx == idx
        sels.append(sel)
        vals.append(mx)
        work = jnp.where(sel, -jnp.inf, work)

    exps = [jnp.exp(v - vals[0]) for v in vals]
    tot = exps[0]
    for e in exps[1:]:
        tot = tot + e
    for k in range(TOP_K):
        gate_ref[0, k] = exps[k] / tot

    cnt = jnp.zeros((ne, rows), jnp.float32)
    for sel in sels:
        cnt = cnt + sel.astype(jnp.float32)
    r_i = lax.broadcasted_iota(jnp.int32, (rows, rows), 0)
    c_i = lax.broadcasted_iota(jnp.int32, (rows, rows), 1)
    triu = (r_i < c_i).astype(bf)
    before = jnp.dot(cnt.astype(bf), triu, preferred_element_type=jnp.float32)
    cnt_col = jnp.sum(cnt, axis=1, keepdims=True)
    e_r = lax.broadcasted_iota(jnp.int32, (ne, ne), 0)
    e_c = lax.broadcasted_iota(jnp.int32, (ne, ne), 1)
    col_b = jnp.broadcast_to(cnt_col, (ne, ne))
    cnt_row = jnp.sum(jnp.where(e_r == e_c, col_b, 0.0), axis=0, keepdims=True)
    first_row = jnp.sum(jnp.where(e_r < e_c, col_b, 0.0), axis=0, keepdims=True)
    first_col = jnp.sum(jnp.where(e_c < e_r, jnp.broadcast_to(cnt_row, (ne, ne)), 0.0), axis=1, keepdims=True)
    pos = before + first_col
    for k in range(TOP_K):
        lk = jnp.sum(jnp.where(sels[k], pos, 0.0), axis=0, keepdims=True).astype(jnp.int32)
        loc_ref[0, k] = lk * nt
    run_ref[...] = jnp.zeros_like(run_ref)
    run_ref[0, 0:1, :] = cnt_row.astype(jnp.int32)
    run_ref[0, 1:2, :] = first_row.astype(jnp.int32)
    run_ref[0, 2:3, :] = carry_ref[...].astype(jnp.int32)
    carry_ref[...] = carry_ref[...] + cnt_row
    cnt_ref[...] = carry_ref[...].astype(jnp.int32)


def _route(x1, g_ffn, w_router, b_router):
    t, d = x1.shape
    ne = w_router.shape[1]
    rows = ROUTE_ROWS
    assert t % rows == 0 and d % LANES == 0
    assert TOP_K <= SUBLANES and ne <= LANES
    nt = d // LANES
    tiles = t // rows
    per_tile = pl.BlockSpec((1, TOP_K, 1, rows), lambda i: (i, 0, 0, 0))
    return pl.pallas_call(
        _route_kernel,
        grid=(tiles,),
        in_specs=[pl.BlockSpec((rows, d), lambda i: (i, 0)),
                  _const_spec(g_ffn.shape), _const_spec(w_router.shape), _const_spec(b_router.shape)],
        out_specs=[pl.BlockSpec((rows * nt, LANES), lambda i: (i, 0)),
                   per_tile, per_tile,
                   pl.BlockSpec((1, SUBLANES, ne), lambda i: (i, 0, 0)),
                   _const_spec((1, ne))],
        out_shape=[jax.ShapeDtypeStruct((t * nt, LANES), jnp.float32),
                   jax.ShapeDtypeStruct((tiles, TOP_K, 1, rows), jnp.int32),
                   jax.ShapeDtypeStruct((tiles, TOP_K, 1, rows), jnp.float32),
                   jax.ShapeDtypeStruct((tiles, SUBLANES, ne), jnp.int32),
                   jax.ShapeDtypeStruct((1, ne), jnp.int32)],
        scratch_shapes=[pltpu.VMEM((1, ne), jnp.float32)],
        compiler_params=pltpu.CompilerParams(
            dimension_semantics=("arbitrary",), vmem_limit_bytes=VMEM_LIMIT),
        name="route",
    )(x1, g_ffn, w_router, b_router)


def _meta_kernel(cnt_ref, poff_ref, blke_ref, blks_ref):
    ne = cnt_ref.shape[1]
    nb = blke_ref.shape[1]
    f32 = jnp.float32
    cnt = cnt_ref[...].astype(f32)
    nblk = jnp.floor((cnt + (MOE_BLOCK - 1)) * (1.0 / MOE_BLOCK))
    e_r = lax.broadcasted_iota(jnp.int32, (ne, ne), 0)
    e_c = lax.broadcasted_iota(jnp.int32, (ne, ne), 1)
    nb_rows = jnp.broadcast_to(nblk, (ne, ne))
    bend_col = jnp.sum(jnp.where(e_c <= e_r, nb_rows, 0.0), axis=1, keepdims=True)
    nb_col = jnp.sum(jnp.where(e_c == e_r, nb_rows, 0.0), axis=1, keepdims=True)
    boff = jnp.sum(jnp.where(e_r < e_c, jnp.broadcast_to(nb_col, (ne, ne)), 0.0),
                   axis=0, keepdims=True)
    poff_ref[...] = (boff * MOE_BLOCK).astype(jnp.int32)
    total = jnp.sum(nblk, axis=1, keepdims=True)
    blk = lax.broadcasted_iota(jnp.int32, (ne, nb), 1).astype(f32)
    done = jnp.where(jnp.broadcast_to(bend_col, (ne, nb)) <= blk, 1.0, 0.0)
    be = jnp.minimum(jnp.sum(done, axis=0, keepdims=True), ne - 1.0)
    bid = lax.broadcasted_iota(jnp.int32, (1, nb), 1).astype(f32)
    last_e = jnp.sum(jnp.where(bid == total - 1.0, be, 0.0), axis=1, keepdims=True)
    blke_ref[...] = jnp.where(bid < total, be, last_e).astype(jnp.int32)
    blks_ref[...] = jnp.minimum(bid, total - 1.0).astype(jnp.int32)


def _meta(counts, n_blocks):
    ne = counts.shape[1]
    return pl.pallas_call(
        _meta_kernel,
        out_shape=[jax.ShapeDtypeStruct((1, ne), jnp.int32),
                   jax.ShapeDtypeStruct((1, n_blocks), jnp.int32),
                   jax.ShapeDtypeStruct((1, n_blocks), jnp.int32)],
        name="moe_meta",
    )(counts)


def _runs_kernel(poff_ref, run_ref, out_ref):
    run = run_ref[...]
    row = lax.broadcasted_iota(jnp.int32, run.shape, 1)
    out_ref[...] = jnp.where(row == 2, run + poff_ref[...][None], run)


def _runs(poff, run):
    return pl.pallas_call(
        _runs_kernel,
        out_shape=jax.ShapeDtypeStruct(run.shape, jnp.int32),
        name="moe_runs",
    )(poff, run)


def _tile(ref, r, nt):
    return ref.at[pl.ds(pl.multiple_of(r * nt, nt), nt), :]


def _slot_specs(rows):
    assert TOP_K == 4
    return [pl.BlockSpec((1, 1, 1, rows), functools.partial(lambda k, i: (i, k, 0, 0), k), memory_space=pltpu.SMEM)
            for k in range(TOP_K)]


def _run_copies(run_ref, src_of, dst_of, sem, nt):
    ne = run_ref.shape[2]

    def per_expert(e, carry):
        n = run_ref[0, 0, e]

        @pl.when(n > 0)
        def _():
            pltpu.make_async_copy(src_of(run_ref[0, 1, e], run_ref[0, 2, e], n),
                                  dst_of(run_ref[0, 1, e], run_ref[0, 2, e], n), sem).start()
        return carry

    lax.fori_loop(0, ne, per_expert, 0)


def _dispatch_kernel(nt, cnt_ref, poff_ref, loc0_ref, loc1_ref, loc2_ref, loc3_ref, run_ref, h2_ref, xs_ref,
                     stage_ref, zero_ref, sem, zsem):
    loc_refs = (loc0_ref, loc1_ref, loc2_ref, loc3_ref)
    step = pl.program_id(0)
    last = pl.num_programs(0) - 1
    rows = h2_ref.shape[0] // nt
    n_assign = rows * TOP_K
    ne = cnt_ref.shape[1]
    par = step % 2
    unroll = TILE_UNROLL

    def rows_at(ref, first, n):
        return ref.at[pl.ds(pl.multiple_of(first * nt, nt), n * nt), :]

    def wait_runs(half):
        pltpu.make_async_copy(stage_ref.at[half], xs_ref.at[pl.ds(0, n_assign * nt), :], sem.at[half]).wait()

    @pl.when(step == 0)
    def _():
        zero_ref[...] = jnp.zeros_like(zero_ref)

        def per_expert(e, carry):
            c = cnt_ref[0, e]
            pad = (MOE_BLOCK - (c % MOE_BLOCK)) % MOE_BLOCK

            @pl.when(pad > 0)
            def _():
                cp = pltpu.make_async_copy(rows_at(zero_ref, 0, pad), rows_at(xs_ref, poff_ref[0, e] + c, pad), zsem)
                cp.start()
                cp.wait()
            return poff_ref[0, e] + c + pad

        used = lax.fori_loop(0, ne, per_expert, 0)

        def spare(b, carry):
            cp = pltpu.make_async_copy(zero_ref, rows_at(xs_ref, b * MOE_BLOCK, MOE_BLOCK), zsem)
            cp.start()
            cp.wait()
            return carry

        lax.fori_loop(used // MOE_BLOCK, xs_ref.shape[0] // (MOE_BLOCK * nt), spare, 0)

    @pl.when(step >= 2)
    def _():
        wait_runs(par)

    def place_into(half):
        def place(i, carry):
            for u in range(unroll):
                tok = i * unroll + u
                row = _tile(h2_ref, tok, nt)[...]
                for k in range(TOP_K):
                    stage_ref[half, pl.ds(pl.multiple_of(loc_refs[k][0, 0, 0, tok], nt), nt), :] = row
            return carry

        lax.fori_loop(0, rows // unroll, place, 0)

    for half in range(2):
        pl.when(par == half)(functools.partial(place_into, half))

    _run_copies(run_ref,
                lambda first, base, n: stage_ref.at[par, pl.ds(pl.multiple_of(first * nt, nt), n * nt), :],
                lambda first, base, n: rows_at(xs_ref, base, n), sem.at[par], nt)

    @pl.when(step == last)
    def _():
        wait_runs(par)

        @pl.when(step >= 1)
        def _():
            wait_runs(1 - par)


def _dispatch(counts, poff, loc, runs, h2, n_rows, nt):
    t = h2.shape[0] // nt
    rows = ROUTE_ROWS
    tiles = t // rows
    ne = counts.shape[1]
    smem = pl.BlockSpec(memory_space=pltpu.SMEM)
    return pl.pallas_call(
        functools.partial(_dispatch_kernel, nt),
        grid=(tiles,),
        in_specs=[smem, smem,
                  *_slot_specs(rows),
                  pl.BlockSpec((1, SUBLANES, ne), lambda i: (i, 0, 0), memory_space=pltpu.SMEM),
                  pl.BlockSpec((rows * nt, LANES), lambda i: (i, 0))],
        out_specs=pl.BlockSpec(memory_space=pl.ANY),
        out_shape=jax.ShapeDtypeStruct((n_rows * nt, LANES), h2.dtype),
        scratch_shapes=[pltpu.VMEM((2, rows * TOP_K * nt, LANES), h2.dtype),
                        pltpu.VMEM((MOE_BLOCK * nt, LANES), h2.dtype),
                        pltpu.SemaphoreType.DMA((2,)), pltpu.SemaphoreType.DMA(())],
        compiler_params=pltpu.CompilerParams(
            dimension_semantics=("arbitrary",), vmem_limit_bytes=VMEM_LIMIT, has_side_effects=True),
        name="moe_dispatch",
    )(counts, poff, loc, loc, loc, loc, runs, h2)


def _expert_kernel(nt, blke_ref, blks_ref, cnt_ref, xs_ref, wgu_hbm, bgu_ref, wd_hbm, bd_ref, ys_ref,
                   wgu32_ref, wd32_ref, wgu_ref, wd_ref, wsem, run_ref):
    i = pl.program_id(0)
    last = pl.num_programs(0) - 1
    f = wd_ref.shape[0]
    e = blke_ref[0, i]
    valid = blks_ref[0, i] == i

    def fetch(expert, slot):
        pltpu.make_async_copy(wgu_hbm.at[expert], wgu32_ref.at[slot], wsem.at[slot]).start()
        pltpu.make_async_copy(wd_hbm.at[expert], wd32_ref.at[slot], wsem.at[slot]).start()

    def wait_fetch(slot):
        pltpu.make_async_copy(wgu_hbm.at[0], wgu32_ref.at[slot], wsem.at[slot]).wait()
        pltpu.make_async_copy(wd_hbm.at[0], wd32_ref.at[slot], wsem.at[slot]).wait()

    @pl.when(i == 0)
    def _():
        run_ref[0] = 0
        fetch(e, 0)

    @pl.when(jnp.logical_and(valid, jnp.logical_or(i == 0, e != blke_ref[0, jnp.maximum(i - 1, 0)])))
    def _():
        slot = run_ref[0] % 2
        wait_fetch(slot)
        nxt = i + (cnt_ref[0, e] + (MOE_BLOCK - 1)) // MOE_BLOCK
        nxt_c = jnp.minimum(nxt, last)

        @pl.when(blks_ref[0, nxt_c] == nxt)
        def _():
            fetch(blke_ref[0, nxt_c], 1 - slot)

        wgu_ref[...] = wgu32_ref[slot].astype(jnp.bfloat16)
        wd_ref[...] = wd32_ref[slot].astype(jnp.bfloat16)
        run_ref[0] = run_ref[0] + 1

    @pl.when(blks_ref[0, i] != i)
    def _():
        ys_ref[...] = jnp.zeros_like(ys_ref)

    @pl.when(blks_ref[0, i] == i)
    def _():
        x = jnp.concatenate([xs_ref[pl.ds(j, MOE_BLOCK, stride=nt), :] for j in range(nt)], axis=-1)
        gu = jnp.dot(x.astype(jnp.bfloat16), wgu_ref[...], preferred_element_type=jnp.float32) + bgu_ref[0]
        g = jnp.minimum(gu[:, 0:f], SWIGLU_LIMIT)
        lin = jnp.clip(gu[:, f:2 * f], -SWIGLU_LIMIT, SWIGLU_LIMIT)
        act = g * _sigmoid(SWIGLU_ALPHA * g) * (lin + 1.0)
        y = jnp.dot(act.astype(jnp.bfloat16), wd_ref[...], preferred_element_type=jnp.float32) + bd_ref[0]
        for j in range(nt):
            ys_ref[pl.ds(j, MOE_BLOCK, stride=nt), :] = y[:, j * LANES:(j + 1) * LANES]


def _experts(blk_e, blk_s, counts, xs, w_gate_up, b_gate_up, w_down, b_down, nt):
    ne, d, f2 = w_gate_up.shape
    f = f2 // 2
    n_blocks = blk_e.shape[1]
    blk_rows = MOE_BLOCK * nt
    grid_spec = pltpu.PrefetchScalarGridSpec(
        num_scalar_prefetch=3,
        grid=(n_blocks,),
        in_specs=[pl.BlockSpec((blk_rows, LANES), lambda i, be, bs, cn: (bs[0, i], 0)),
                  pl.BlockSpec(memory_space=pl.ANY),
                  pl.BlockSpec((1, 1, f2), lambda i, be, bs, cn: (be[0, i], 0, 0)),
                  pl.BlockSpec(memory_space=pl.ANY),
                  pl.BlockSpec((1, 1, d), lambda i, be, bs, cn: (be[0, i], 0, 0))],
        out_specs=pl.BlockSpec((blk_rows, LANES), lambda i, be, bs, cn: (i, 0)),
        scratch_shapes=[pltpu.VMEM((2, d, f2), jnp.float32), pltpu.VMEM((2, f, d), jnp.float32),
                        pltpu.VMEM((d, f2), jnp.bfloat16), pltpu.VMEM((f, d), jnp.bfloat16),
                        pltpu.SemaphoreType.DMA((2,)), pltpu.SMEM((1,), jnp.int32)],
    )
    return pl.pallas_call(
        functools.partial(_expert_kernel, nt),
        grid_spec=grid_spec,
        out_shape=jax.ShapeDtypeStruct(xs.shape, jnp.float32),
        compiler_params=pltpu.CompilerParams(
            dimension_semantics=("arbitrary",), vmem_limit_bytes=VMEM_LIMIT),
        name="moe_experts",
    )(blk_e, blk_s, counts, xs, w_gate_up, b_gate_up.reshape(ne, 1, f2), w_down, b_down.reshape(ne, 1, d))


def _combine_kernel(nt, loc0_ref, loc1_ref, loc2_ref, loc3_ref, g0_ref, g1_ref, g2_ref, g3_ref,
                    run_ref, runn_ref, x1_ref, gfin_ref, ys_ref, out_ref, stage_ref, acc_ref, sem):
    loc_refs = (loc0_ref, loc1_ref, loc2_ref, loc3_ref)
    gate_refs = (g0_ref, g1_ref, g2_ref, g3_ref)
    rows, d = x1_ref.shape
    n_assign = rows * TOP_K
    step = pl.program_id(0)
    par = step % 2
    unroll = TILE_UNROLL

    def fetch(r_ref, half):
        _run_copies(r_ref,
                    lambda first, base, n: ys_ref.at[pl.ds(pl.multiple_of(base * nt, nt), n * nt), :],
                    lambda first, base, n: stage_ref.at[half, pl.ds(pl.multiple_of(first * nt, nt), n * nt), :],
                    sem.at[half], nt)

    @pl.when(step == 0)
    def _():
        fetch(run_ref, 0)

    @pl.when(step + 1 < pl.num_programs(0))
    def _():
        fetch(runn_ref, 1 - par)

    pltpu.make_async_copy(ys_ref.at[pl.ds(0, n_assign * nt), :], stage_ref.at[par], sem.at[par]).wait()

    def mix_from(half):
        def mix(i, carry):
            for u in range(unroll):
                tok = i * unroll + u
                acc = None
                for k in range(TOP_K):
                    row = stage_ref[half, pl.ds(pl.multiple_of(loc_refs[k][0, 0, 0, tok], nt), nt), :]
                    term = gate_refs[k][0, 0, 0, tok] * row
                    acc = term if acc is None else acc + term
                acc_ref[pl.ds(pl.multiple_of(tok * nt, nt), nt), :] = acc
            return carry

        lax.fori_loop(0, rows // unroll, mix, 0)

    for half in range(2):
        pl.when(par == half)(functools.partial(mix_from, half))

    y = jnp.concatenate([acc_ref[pl.ds(j, rows, stride=nt), :] for j in range(nt)], axis=-1)
    x = x1_ref[...] + y
    ms = jnp.mean(x * x, axis=-1, keepdims=True)
    out_ref[...] = x * lax.rsqrt(ms + NORM_EPS) * gfin_ref[...]


def _combine(loc, gate, runs, x1, g_final, ys, nt):
    t, d = x1.shape
    rows = ROUTE_ROWS
    tiles = t // rows
    ne = runs.shape[2]
    n_assign = rows * TOP_K
    smem = pltpu.SMEM
    return pl.pallas_call(
        functools.partial(_combine_kernel, nt),
        grid=(tiles,),
        in_specs=[*_slot_specs(rows), *_slot_specs(rows),
                  pl.BlockSpec((1, SUBLANES, ne), lambda i: (i, 0, 0), memory_space=smem),
                  pl.BlockSpec((1, SUBLANES, ne), lambda i: (jnp.minimum(i + 1, tiles - 1), 0, 0),
                               memory_space=smem),
                  pl.BlockSpec((rows, d), lambda i: (i, 0)),
                  _const_spec(g_final.shape),
                  pl.BlockSpec(memory_space=pl.ANY)],
        out_specs=pl.BlockSpec((rows, d), lambda i: (i, 0)),
        out_shape=jax.ShapeDtypeStruct((t, d), jnp.float32),
        scratch_shapes=[pltpu.VMEM((2, n_assign * nt, LANES), jnp.float32),
                        pltpu.VMEM((rows * nt, LANES), jnp.float32),
                        pltpu.SemaphoreType.DMA((2,))],
        compiler_params=pltpu.CompilerParams(
            dimension_semantics=("arbitrary",), vmem_limit_bytes=VMEM_LIMIT),
        name="moe_combine",
    )(loc, loc, loc, loc, gate, gate, gate, gate, runs, runs, x1, g_final, ys)


def _block_diag(blocks):
    g, r, c = blocks.shape
    eye = jnp.eye(g, dtype=blocks.dtype)
    return (blocks[:, :, None, :] * eye[:, None, :, None]).reshape(g * r, g * c)


def kernel(x, norm_mix_g, w_in, conv_w, conv_b, conv_ln_g, conv_ln_b, w_conv_out, ssm_a_re, ssm_a_im,
           ssm_log_step, ssm_b_re, ssm_b_im, ssm_c_re, ssm_c_im, ssm_d, w_ssm_glu, b_ssm_glu, w_ssm_out,
           w_out, norm_ffn_g, w_router, b_router, w_gate_up, b_gate_up, w_down, b_down, norm_final_g):
    bsz, seq, d = x.shape
    depth = w_in.shape[0]
    bf = jnp.bfloat16
    nt = d // LANES
    n_tok = bsz * seq
    ne = w_router.shape[-1]
    n_blocks = -(-n_tok * TOP_K // MOE_BLOCK) + ne
    seg_len = MIX_ROWS // SCAN_SEGS

    def row(v):
        return v.reshape(1, -1)

    for l in range(depth):
        ar, ai, arp, aip, bbr, bbi = _ssm_disc(
            ssm_a_re[l], ssm_a_im[l], ssm_log_step[l],
            jnp.swapaxes(ssm_b_re[l], 1, 2), jnp.swapaxes(ssm_b_im[l], 1, 2), seg_len)
        bblk = jnp.concatenate([_block_diag(bbr), _block_diag(bbi)], axis=1).astype(bf)
        c_r = _block_diag(jnp.swapaxes(ssm_c_re[l], 1, 2)).astype(bf)
        c_i = _block_diag(jnp.swapaxes(ssm_c_im[l], 1, 2)).astype(bf)
        avec = jnp.stack([ar.reshape(-1), ai.reshape(-1), arp.reshape(-1), aip.reshape(-1)])

        x = _mixer(x, row(norm_mix_g[l]), w_in[l].astype(bf), conv_w[l], row(conv_b[l]),
                   row(conv_ln_g[l]), row(conv_ln_b[l]), w_conv_out[l].astype(bf), bblk, c_r, c_i, avec,
                   row(ssm_d[l]), w_ssm_glu[l].astype(bf), row(b_ssm_glu[l]), w_ssm_out[l].astype(bf),
                   w_out[l].astype(bf))

        x1 = x.reshape(n_tok, d)
        h2, loc, gate, run, counts = _route(x1, row(norm_ffn_g[l]), w_router[l], row(b_router[l]))
        poff, blk_e, blk_s = _meta(counts, n_blocks)
        runs = _runs(poff, run)
        xs = _dispatch(counts, poff, loc, runs, h2, n_blocks * MOE_BLOCK, nt)
        ys = _experts(blk_e, blk_s, counts, xs, w_gate_up[l], b_gate_up[l], w_down[l], b_down[l], nt)
        assert depth == 1
        x = _combine(loc, gate, runs, x1, row(norm_final_g), ys, nt).reshape(bsz, seq, d)
    return x
```

```python
import functools
import math

import jax
import jax.numpy as jnp
from jax import lax
from jax.experimental import pallas as pl
from jax.experimental.pallas import tpu as pltpu

NORM_EPS = 1e-6
SWIGLU_LIMIT = 7.0
SWIGLU_ALPHA = 1.702
TOP_K = 4

LANES = 128
SUBLANES = 8

MIX_ROWS = 512
SCAN_SEGS = SUBLANES
SEG_PAD = 8
SCAN_UNROLL = 8
TILE_UNROLL = 8
CONV_HALO = 32
CONV_ROWS = 64
ROUTE_ROWS = 512
MOE_BLOCK = 512
V7X_VMEM_BYTES = 64 * 1024 * 1024
VMEM_LIMIT = V7X_VMEM_BYTES - 8 * 1024 * 1024


def _sigmoid(x):
    return jax.nn.sigmoid(x)


def _gelu_tanh(x):
    c = math.sqrt(2.0 / math.pi)
    return 0.5 * x * (1.0 + jnp.tanh(c * (x + 0.044715 * (x * x * x))))


def _const_spec(shape):
    nd = len(shape)
    return pl.BlockSpec(shape, lambda *_: (0,) * nd)


def _ssm_disc_kernel(seg_len, are_ref, aim_ref, ls_ref, bre_ref, bim_ref,
                     ar_ref, ai_ref, arp_ref, aip_ref, bbr_ref, bbi_ref):
    lr = are_ref[...]
    li = aim_ref[...]
    dt = jnp.exp(ls_ref[...])
    mag = jnp.exp(lr * dt)
    ar = mag * jnp.cos(li * dt)
    ai = mag * jnp.sin(li * dt)
    den = lr * lr + li * li
    fr = ((ar - 1.0) * lr + ai * li) / den
    fi = (ai * lr - (ar - 1.0) * li) / den
    ar_ref[...] = ar
    ai_ref[...] = ai
    magp = jnp.exp(lr * dt * seg_len)
    arp_ref[...] = magp * jnp.cos(li * dt * seg_len)
    aip_ref[...] = magp * jnp.sin(li * dt * seg_len)
    br = bre_ref[...]
    bi = bim_ref[...]
    bbr_ref[...] = fr[:, None, :] * br - fi[:, None, :] * bi
    bbi_ref[...] = fr[:, None, :] * bi + fi[:, None, :] * br


def _ssm_disc(a_re, a_im, log_step, bt_re, bt_im, seg_len):
    g, p = a_re.shape
    h = bt_re.shape[1]
    f32 = jnp.float32
    out_shape = [jax.ShapeDtypeStruct((g, p), f32)] * 4 + [jax.ShapeDtypeStruct((g, h, p), f32)] * 2
    return pl.pallas_call(
        functools.partial(_ssm_disc_kernel, float(seg_len)),
        out_shape=out_shape,
        name="ssm_disc",
    )(a_re, a_im, log_step.reshape(g, 1), bt_re, bt_im)


def _mixer_kernel(dims, x_ref, gmix_ref, win_ref, convw_ref, convb_ref, lng_ref, lnb_ref, wco_ref,
                  bblk_ref, cr_ref, ci_ref, avec_ref, dskip_ref, wglu_ref, bglu_ref, wso_ref, wout_ref,
                  x1_ref, vbuf_ref, conv_ref, scan_ref, xst_ref, cin_ref, st_ref):
    d, dc, ds, gp, cw = dims
    rows = MIX_ROWS
    seg = rows // SCAN_SEGS
    pitch = seg + SEG_PAD
    nsl = gp // LANES
    c_idx = pl.program_id(1)

    half = rows // 2
    tail = 2 * half + 1

    @pl.when(c_idx == 0)
    def _():
        for lc in range(dc // LANES):
            vbuf_ref[lc, pl.ds(0, CONV_HALO, stride=2), :] = jnp.zeros((CONV_HALO, LANES), jnp.float32)
        st_ref[...] = jnp.zeros_like(st_ref)

    @pl.when(c_idx != 0)
    def _():
        for lc in range(dc // LANES):
            vbuf_ref[lc, pl.ds(0, CONV_HALO, stride=2), :] = vbuf_ref[lc, pl.ds(tail, CONV_HALO, stride=2), :]

    x = x_ref[0]
    ms = jnp.mean(x * x, axis=-1, keepdims=True)
    h = (x * lax.rsqrt(ms + NORM_EPS) * gmix_ref[...]).astype(jnp.bfloat16)

    pa = jnp.dot(h, win_ref[:, 0:2 * dc], preferred_element_type=jnp.float32)
    vglu = pa[:, 0:dc] * _sigmoid(pa[:, dc:2 * dc])
    for lc in range(dc // LANES):
        ls = slice(lc * LANES, (lc + 1) * LANES)
        vbuf_ref[lc, pl.ds(2 * CONV_HALO, half, stride=2), :] = vglu[0:half, ls]
        vbuf_ref[lc, pl.ds(2 * CONV_HALO + 1, half, stride=2), :] = vglu[half:rows, ls]
        vbuf_ref[lc, pl.ds(1, CONV_HALO, stride=2), :] = vglu[half - CONV_HALO:half, ls]

    base = CONV_HALO - (cw - 1)
    nv = CONV_ROWS // SUBLANES

    def conv_chunk(rc, p):
        r0 = rc * CONV_ROWS
        for lc in range(dc // LANES):
            ls = slice(lc * LANES, (lc + 1) * LANES)
            wv = [jnp.broadcast_to(convw_ref[k:k + 1, ls], (SUBLANES, LANES)) for k in range(cw)]
            acc = [jnp.zeros((SUBLANES, LANES), jnp.float32) for _ in range(nv)]
            for s in range(CONV_ROWS - SUBLANES + cw):
                win = vbuf_ref[lc, pl.ds(2 * (r0 + base + s) + p, SUBLANES, stride=2), :]
                for i in range(nv):
                    k = s - SUBLANES * i
                    if 0 <= k < cw:
                        acc[i] = acc[i] + wv[k] * win
            for i in range(nv):
                conv_ref[pl.ds(p * half + r0 + SUBLANES * i, SUBLANES), ls] = acc[i]

    g0 = 2 * dc + ds
    chunks = [(rc, p) for rc in range(half // CONV_ROWS) for p in range(2)]
    gw = 2 * d // len(chunks)
    sg = []
    for n, (rc, p) in enumerate(chunks):
        sg.append(_sigmoid(jnp.dot(h, win_ref[:, g0 + n * gw:g0 + (n + 1) * gw],
                                   preferred_element_type=jnp.float32)))
        conv_chunk(rc, p)
    sg = jnp.concatenate(sg, axis=-1)
    sg_conv = sg[:, 0:d]
    sg_ssm = sg[:, d:2 * d]

    v = conv_ref[...] + convb_ref[...]
    mu = jnp.mean(v, axis=-1, keepdims=True)
    vc = v - mu
    var = jnp.mean(vc * vc, axis=-1, keepdims=True)
    v = vc * lax.rsqrt(var + NORM_EPS) * lng_ref[...] + lnb_ref[...]
    v = v * _sigmoid(v)
    y_conv = jnp.dot(v.astype(jnp.bfloat16), wco_ref[...], preferred_element_type=jnp.float32)

    u = jnp.dot(h, win_ref[:, 2 * dc:2 * dc + ds], preferred_element_type=jnp.float32)
    bu = jnp.dot(u.astype(jnp.bfloat16), bblk_ref[...], preferred_element_type=jnp.float32)
    for n in range(2 * nsl):
        for s in range(SCAN_SEGS):
            scan_ref[n, s * pitch:s * pitch + seg, :] = bu[s * seg:(s + 1) * seg, n * LANES:(n + 1) * LANES]

    def bcast(row, n):
        return jnp.broadcast_to(avec_ref[row:row + 1, n * LANES:(n + 1) * LANES], (SUBLANES, LANES))

    a_r = [bcast(0, n) for n in range(nsl)]
    a_i = [bcast(1, n) for n in range(nsl)]

    def step(j, st, store):
        out = []
        for n in range(nsl):
            sr, si = st[2 * n], st[2 * n + 1]
            br = scan_ref[n, pl.ds(j, SCAN_SEGS, stride=pitch), :]
            bi = scan_ref[nsl + n, pl.ds(j, SCAN_SEGS, stride=pitch), :]
            nr = a_r[n] * sr - a_i[n] * si + br
            ni = a_r[n] * si + a_i[n] * sr + bi
            if store:
                scan_ref[n, pl.ds(j, SCAN_SEGS, stride=pitch), :] = nr
                scan_ref[nsl + n, pl.ds(j, SCAN_SEGS, stride=pitch), :] = ni
            out += [nr, ni]
        return tuple(out)

    def steps(jj, st, store):
        for q in range(SCAN_UNROLL):
            st = step(jj * SCAN_UNROLL + q, st, store)
        return st

    zero = jnp.zeros((SUBLANES, LANES), jnp.float32)
    fin = lax.fori_loop(0, seg // SCAN_UNROLL, functools.partial(steps, store=False), (zero,) * (2 * nsl))

    for n in range(nsl):
        fr, fi = fin[2 * n], fin[2 * n + 1]
        ap_r = avec_ref[2:3, n * LANES:(n + 1) * LANES]
        ap_i = avec_ref[3:4, n * LANES:(n + 1) * LANES]
        c_r = st_ref[0:1, n * LANES:(n + 1) * LANES]
        c_i = st_ref[1:2, n * LANES:(n + 1) * LANES]
        for s in range(SCAN_SEGS):
            cin_ref[2 * n, s:s + 1, :] = c_r
            cin_ref[2 * n + 1, s:s + 1, :] = c_i
            n_r = ap_r * c_r - ap_i * c_i + fr[s:s + 1, :]
            n_i = ap_r * c_i + ap_i * c_r + fi[s:s + 1, :]
            c_r, c_i = n_r, n_i
        st_ref[0:1, n * LANES:(n + 1) * LANES] = c_r
        st_ref[1:2, n * LANES:(n + 1) * LANES] = c_i

    lax.fori_loop(0, seg // SCAN_UNROLL, functools.partial(steps, store=True),
                  tuple(cin_ref[q] for q in range(2 * nsl)))

    for n in range(2 * nsl):
        for s in range(SCAN_SEGS):
            xst_ref[s * seg:(s + 1) * seg, n * LANES:(n + 1) * LANES] = (
                scan_ref[n, s * pitch:s * pitch + seg, :].astype(jnp.bfloat16))

    y = (jnp.dot(xst_ref[:, 0:gp], cr_ref[...], preferred_element_type=jnp.float32)
         - jnp.dot(xst_ref[:, gp:2 * gp], ci_ref[...], preferred_element_type=jnp.float32))
    y = _gelu_tanh(y + dskip_ref[...] * u)
    glu = jnp.dot(y.astype(jnp.bfloat16), wglu_ref[...], preferred_element_type=jnp.float32) + bglu_ref[...]
    y = y * _sigmoid(glu)
    y_ssm = jnp.dot(y.astype(jnp.bfloat16), wso_ref[...], preferred_element_type=jnp.float32)

    m = sg_conv * y_conv + sg_ssm * y_ssm
    x1_ref[0] = x + jnp.dot(m.astype(jnp.bfloat16), wout_ref[...], preferred_element_type=jnp.float32)


def _mixer(x, gmix, w_in, conv_w, conv_b, ln_g, ln_b, w_co, bblk, c_r, c_i, avec, d_skip,
           w_glu, b_glu, w_so, w_out):
    b, s, d = x.shape
    cw, dc = conv_w.shape
    ds = d_skip.shape[-1]
    gp = c_r.shape[0]
    rows = MIX_ROWS
    seg = rows // SCAN_SEGS
    pitch = seg + SEG_PAD
    assert s % rows == 0 and cw - 1 <= CONV_HALO and gp % LANES == 0 and dc % LANES == 0
    dims = (d, dc, ds, gp, cw)
    consts = [gmix, w_in, conv_w, conv_b, ln_g, ln_b, w_co, bblk, c_r, c_i, avec, d_skip,
              w_glu, b_glu, w_so, w_out]
    return pl.pallas_call(
        functools.partial(_mixer_kernel, dims),
        grid=(b, s // rows),
        in_specs=[pl.BlockSpec((1, rows, d), lambda i, j: (i, j, 0))] + [_const_spec(c.shape) for c in consts],
        out_specs=pl.BlockSpec((1, rows, d), lambda i, j: (i, j, 0)),
        out_shape=jax.ShapeDtypeStruct((b, s, d), jnp.float32),
        scratch_shapes=[
            pltpu.VMEM((dc // LANES, 2 * CONV_HALO + rows, LANES), jnp.float32),
            pltpu.VMEM((rows, dc), jnp.float32),
            pltpu.VMEM((2 * gp // LANES, SCAN_SEGS * pitch, LANES), jnp.float32),
            pltpu.VMEM((rows, 2 * gp), jnp.bfloat16),
            pltpu.VMEM((2 * gp // LANES, SUBLANES, LANES), jnp.float32),
            pltpu.VMEM((2, gp), jnp.float32),
        ],
        compiler_params=pltpu.CompilerParams(
            dimension_semantics=("arbitrary", "arbitrary"), vmem_limit_bytes=VMEM_LIMIT),
        name="mixer",
    )(x, *consts)


def _route_kernel(x1_ref, g_ref, wr_ref, br_ref, h2_ref, loc_ref, gate_ref, run_ref, cnt_ref, carry_ref):
    rows, d = x1_ref.shape
    ne = wr_ref.shape[1]

    @pl.when(pl.program_id(0) == 0)
    def _():
        carry_ref[...] = jnp.zeros_like(carry_ref)

    x = x1_ref[...]
    ms = jnp.mean(x * x, axis=-1, keepdims=True)
    h2 = x * lax.rsqrt(ms + NORM_EPS) * g_ref[...]

    nt = d // LANES
    for j in range(nt):
        h2_ref[pl.ds(j, rows, stride=nt), :] = h2[:, j * LANES:(j + 1) * LANES]

    bf = jnp.bfloat16
    h_hi = h2.astype(bf)
    h_lo = (h2 - h_hi.astype(jnp.float32)).astype(bf)
    w = wr_ref[...]
    w_hi = w.astype(bf)
    w_lo = (w - w_hi.astype(jnp.float32)).astype(bf)
    logits = (jnp.dot(h_hi, w_hi, preferred_element_type=jnp.float32)
              + jnp.dot(h_lo, w_hi, preferred_element_type=jnp.float32)
              + jnp.dot(h_hi, w_lo, preferred_element_type=jnp.float32)) + br_ref[...]

    pad = jnp.full((rows, LANES - ne), -jnp.inf, jnp.float32)
    work = jnp.concatenate([logits, pad], axis=1).T[0:ne, :]
    eidx = lax.broadcasted_iota(jnp.int32, (ne, rows), 0).astype(jnp.float32)
    sels, vals = [], []
    for k in range(TOP_K):
        mx = jnp.max(work, axis=0, keepdims=True)
        idx = jnp.min(jnp.where(work == mx, eidx, float(ne)), axis=0, keepdims=True)
        sel = eidx == idx
        sels.append(sel)
        vals.append(mx)
        work = jnp.where(sel, -jnp.inf, work)

    exps = [jnp.exp(v - vals[0]) for v in vals]
    tot = exps[0]
    for e in exps[1:]:
        tot = tot + e
    for k in range(TOP_K):
        gate_ref[0, k] = exps[k] / tot

    cnt = jnp.zeros((ne, rows), jnp.float32)
    for sel in sels:
        cnt = cnt + sel.astype(jnp.float32)
    r_i = lax.broadcasted_iota(jnp.int32, (rows, rows), 0)
    c_i = lax.broadcasted_iota(jnp.int32, (rows, rows), 1)
    triu = (r_i < c_i).astype(bf)
    before = jnp.dot(cnt.astype(bf), triu, preferred_element_type=jnp.float32)
    cnt_col = jnp.sum(cnt, axis=1, keepdims=True)
    e_r = lax.broadcasted_iota(jnp.int32, (ne, ne), 0)
    e_c = lax.broadcasted_iota(jnp.int32, (ne, ne), 1)
    col_b = jnp.broadcast_to(cnt_col, (ne, ne))
    cnt_row = jnp.sum(jnp.where(e_r == e_c, col_b, 0.0), axis=0, keepdims=True)
    first_row = jnp.sum(jnp.where(e_r < e_c, col_b, 0.0), axis=0, keepdims=True)
    first_col = jnp.sum(jnp.where(e_c < e_r, jnp.broadcast_to(cnt_row, (ne, ne)), 0.0), axis=1, keepdims=True)
    pos = before + first_col
    for k in range(TOP_K):
        lk = jnp.sum(jnp.where(sels[k], pos, 0.0), axis=0, keepdims=True).astype(jnp.int32)
        loc_ref[0, k] = lk * nt
    run_ref[...] = jnp.zeros_like(run_ref)
    run_ref[0, 0:1, :] = cnt_row.astype(jnp.int32)
    run_ref[0, 1:2, :] = first_row.astype(jnp.int32)
    run_ref[0, 2:3, :] = carry_ref[...].astype(jnp.int32)
    carry_ref[...] = carry_ref[...] + cnt_row
    cnt_ref[...] = carry_ref[...].astype(jnp.int32)


def _route(x1, g_ffn, w_router, b_router):
    t, d = x1.shape
    ne = w_router.shape[1]
    rows = ROUTE_ROWS
    assert t % rows == 0 and d % LANES == 0
    assert TOP_K <= SUBLANES and ne <= LANES
    nt = d // LANES
    tiles = t // rows
    per_tile = pl.BlockSpec((1, TOP_K, 1, rows), lambda i: (i, 0, 0, 0))
    return pl.pallas_call(
        _route_kernel,
        grid=(tiles,),
        in_specs=[pl.BlockSpec((rows, d), lambda i: (i, 0)),
                  _const_spec(g_ffn.shape), _const_spec(w_router.shape), _const_spec(b_router.shape)],
        out_specs=[pl.BlockSpec((rows * nt, LANES), lambda i: (i, 0)),
                   per_tile, per_tile,
                   pl.BlockSpec((1, SUBLANES, ne), lambda i: (i, 0, 0)),
                   _const_spec((1, ne))],
        out_shape=[jax.ShapeDtypeStruct((t * nt, LANES), jnp.float32),
                   jax.ShapeDtypeStruct((tiles, TOP_K, 1, rows), jnp.int32),
                   jax.ShapeDtypeStruct((tiles, TOP_K, 1, rows), jnp.float32),
                   jax.ShapeDtypeStruct((tiles, SUBLANES, ne), jnp.int32),
                   jax.ShapeDtypeStruct((1, ne), jnp.int32)],
        scratch_shapes=[pltpu.VMEM((1, ne), jnp.float32)],
        compiler_params=pltpu.CompilerParams(
            dimension_semantics=("arbitrary",), vmem_limit_bytes=VMEM_LIMIT),
        name="route",
    )(x1, g_ffn, w_router, b_router)


def _meta_kernel(cnt_ref, poff_ref, blke_ref, blks_ref):
    ne = cnt_ref.shape[1]
    nb = blke_ref.shape[1]
    f32 = jnp.float32
    cnt = cnt_ref[...].astype(f32)
    nblk = jnp.floor((cnt + (MOE_BLOCK - 1)) * (1.0 / MOE_BLOCK))
    e_r = lax.broadcasted_iota(jnp.int32, (ne, ne), 0)
    e_c = lax.broadcasted_iota(jnp.int32, (ne, ne), 1)
    nb_rows = jnp.broadcast_to(nblk, (ne, ne))
    bend_col = jnp.sum(jnp.where(e_c <= e_r, nb_rows, 0.0), axis=1, keepdims=True)
    nb_col = jnp.sum(jnp.where(e_c == e_r, nb_rows, 0.0), axis=1, keepdims=True)
    boff = jnp.sum(jnp.where(e_r < e_c, jnp.broadcast_to(nb_col, (ne, ne)), 0.0),
                   axis=0, keepdims=True)
    poff_ref[...] = (boff * MOE_BLOCK).astype(jnp.int32)
    total = jnp.sum(nblk, axis=1, keepdims=True)
    blk = lax.broadcasted_iota(jnp.int32, (ne, nb), 1).astype(f32)
    done = jnp.where(jnp.broadcast_to(bend_col, (ne, nb)) <= blk, 1.0, 0.0)
    be = jnp.minimum(jnp.sum(done, axis=0, keepdims=True), ne - 1.0)
    bid = lax.broadcasted_iota(jnp.int32, (1, nb), 1).astype(f32)
    last_e = jnp.sum(jnp.where(bid == total - 1.0, be, 0.0), axis=1, keepdims=True)
    blke_ref[...] = jnp.where(bid < total, be, last_e).astype(jnp.int32)
    blks_ref[...] = jnp.minimum(bid, total - 1.0).astype(jnp.int32)


def _meta(counts, n_blocks):
    ne = counts.shape[1]
    return pl.pallas_call(
        _meta_kernel,
        out_shape=[jax.ShapeDtypeStruct((1, ne), jnp.int32),
                   jax.ShapeDtypeStruct((1, n_blocks), jnp.int32),
                   jax.ShapeDtypeStruct((1, n_blocks), jnp.int32)],
        name="moe_meta",
    )(counts)


def _runs_kernel(poff_ref, run_ref, out_ref):
    run = run_ref[...]
    row = lax.broadcasted_iota(jnp.int32, run.shape, 1)
    out_ref[...] = jnp.where(row == 2, run + poff_ref[...][None], run)


def _runs(poff, run):
    return pl.pallas_call(
        _runs_kernel,
        out_shape=jax.ShapeDtypeStruct(run.shape, jnp.int32),
        name="moe_runs",
    )(poff, run)


def _tile(ref, r, nt):
    return ref.at[pl.ds(pl.multiple_of(r * nt, nt), nt), :]


def _slot_specs(rows):
    assert TOP_K == 4
    return [pl.BlockSpec((1, 1, 1, rows), functools.partial(lambda k, i: (i, k, 0, 0), k), memory_space=pltpu.SMEM)
            for k in range(TOP_K)]


def _run_copies(run_ref, src_of, dst_of, sem, nt):
    ne = run_ref.shape[2]

    def per_expert(e, carry):
        n = run_ref[0, 0, e]

        @pl.when(n > 0)
        def _():
            pltpu.make_async_copy(src_of(run_ref[0, 1, e], run_ref[0, 2, e], n),
                                  dst_of(run_ref[0, 1, e], run_ref[0, 2, e], n), sem).start()
        return carry

    lax.fori_loop(0, ne, per_expert, 0)


def _dispatch_kernel(nt, cnt_ref, poff_ref, loc0_ref, loc1_ref, loc2_ref, loc3_ref, run_ref, h2_ref, xs_ref,
                     stage_ref, zero_ref, sem, zsem):
    loc_refs = (loc0_ref, loc1_ref, loc2_ref, loc3_ref)
    step = pl.program_id(0)
    last = pl.num_programs(0) - 1
    rows = h2_ref.shape[0] // nt
    n_assign = rows * TOP_K
    ne = cnt_ref.shape[1]
    par = step % 2
    unroll = TILE_UNROLL

    def rows_at(ref, first, n):
        return ref.at[pl.ds(pl.multiple_of(first * nt, nt), n * nt), :]

    def wait_runs(half):
        pltpu.make_async_copy(stage_ref.at[half], xs_ref.at[pl.ds(0, n_assign * nt), :], sem.at[half]).wait()

    @pl.when(step == 0)
    def _():
        zero_ref[...] = jnp.zeros_like(zero_ref)

        def per_expert(e, carry):
            c = cnt_ref[0, e]
            pad = (MOE_BLOCK - (c % MOE_BLOCK)) % MOE_BLOCK

            @pl.when(pad > 0)
            def _():
                cp = pltpu.make_async_copy(rows_at(zero_ref, 0, pad), rows_at(xs_ref, poff_ref[0, e] + c, pad), zsem)
                cp.start()
                cp.wait()
            return poff_ref[0, e] + c + pad

        used = lax.fori_loop(0, ne, per_expert, 0)

        def spare(b, carry):
            cp = pltpu.make_async_copy(zero_ref, rows_at(xs_ref, b * MOE_BLOCK, MOE_BLOCK), zsem)
            cp.start()
            cp.wait()
            return carry

        lax.fori_loop(used // MOE_BLOCK, xs_ref.shape[0] // (MOE_BLOCK * nt), spare, 0)

    @pl.when(step >= 2)
    def _():
        wait_runs(par)

    def place_into(half):
        def place(i, carry):
            for u in range(unroll):
                tok = i * unroll + u
                row = _tile(h2_ref, tok, nt)[...]
                for k in range(TOP_K):
                    stage_ref[half, pl.ds(pl.multiple_of(loc_refs[k][0, 0, 0, tok], nt), nt), :] = row
            return carry

        lax.fori_loop(0, rows // unroll, place, 0)

    for half in range(2):
        pl.when(par == half)(functools.partial(place_into, half))

    _run_copies(run_ref,
                lambda first, base, n: stage_ref.at[par, pl.ds(pl.multiple_of(first * nt, nt), n * nt), :],
                lambda first, base, n: rows_at(xs_ref, base, n), sem.at[par], nt)

    @pl.when(step == last)
    def _():
        wait_runs(par)

        @pl.when(step >= 1)
        def _():
            wait_runs(1 - par)


def _dispatch(counts, poff, loc, runs, h2, n_rows, nt):
    t = h2.shape[0] // nt
    rows = ROUTE_ROWS
    tiles = t // rows
    ne = counts.shape[1]
    smem = pl.BlockSpec(memory_space=pltpu.SMEM)
    return pl.pallas_call(
        functools.partial(_dispatch_kernel, nt),
        grid=(tiles,),
        in_specs=[smem, smem,
                  *_slot_specs(rows),
                  pl.BlockSpec((1, SUBLANES, ne), lambda i: (i, 0, 0), memory_space=pltpu.SMEM),
                  pl.BlockSpec((rows * nt, LANES), lambda i: (i, 0))],
        out_specs=pl.BlockSpec(memory_space=pl.ANY),
        out_shape=jax.ShapeDtypeStruct((n_rows * nt, LANES), h2.dtype),
        scratch_shapes=[pltpu.VMEM((2, rows * TOP_K * nt, LANES), h2.dtype),
                        pltpu.VMEM((MOE_BLOCK * nt, LANES), h2.dtype),
                        pltpu.SemaphoreType.DMA((2,)), pltpu.SemaphoreType.DMA(())],
        compiler_params=pltpu.CompilerParams(
            dimension_semantics=("arbitrary",), vmem_limit_bytes=VMEM_LIMIT, has_side_effects=True),
        name="moe_dispatch",
    )(counts, poff, loc, loc, loc, loc, runs, h2)


def _expert_kernel(nt, blke_ref, blks_ref, cnt_ref, xs_ref, wgu_hbm, bgu_ref, wd_hbm, bd_ref, ys_ref,
                   wgu32_ref, wd32_ref, wgu_ref, wd_ref, wsem, run_ref):
    i = pl.program_id(0)
    last = pl.num_programs(0) - 1
    f = wd_ref.shape[0]
    e = blke_ref[0, i]
    valid = blks_ref[0, i] == i

    def fetch(expert, slot):
        pltpu.make_async_copy(wgu_hbm.at[expert], wgu32_ref.at[slot], wsem.at[slot]).start()
        pltpu.make_async_copy(wd_hbm.at[expert], wd32_ref.at[slot], wsem.at[slot]).start()

    def wait_fetch(slot):
        pltpu.make_async_copy(wgu_hbm.at[0], wgu32_ref.at[slot], wsem.at[slot]).wait()
        pltpu.make_async_copy(wd_hbm.at[0], wd32_ref.at[slot], wsem.at[slot]).wait()

    @pl.when(i == 0)
    def _():
        run_ref[0] = 0
        fetch(e, 0)

    @pl.when(jnp.logical_and(valid, jnp.logical_or(i == 0, e != blke_ref[0, jnp.maximum(i - 1, 0)])))
    def _():
        slot = run_ref[0] % 2
        wait_fetch(slot)
        nxt = i + (cnt_ref[0, e] + (MOE_BLOCK - 1)) // MOE_BLOCK
        nxt_c = jnp.minimum(nxt, last)

        @pl.when(blks_ref[0, nxt_c] == nxt)
        def _():
            fetch(blke_ref[0, nxt_c], 1 - slot)

        wgu_ref[...] = wgu32_ref[slot].astype(jnp.bfloat16)
        wd_ref[...] = wd32_ref[slot].astype(jnp.bfloat16)
        run_ref[0] = run_ref[0] + 1

    @pl.when(blks_ref[0, i] != i)
    def _():
        ys_ref[...] = jnp.zeros_like(ys_ref)

    @pl.when(blks_ref[0, i] == i)
    def _():
        x = jnp.concatenate([xs_ref[pl.ds(j, MOE_BLOCK, stride=nt), :] for j in range(nt)], axis=-1)
        gu = jnp.dot(x.astype(jnp.bfloat16), wgu_ref[...], preferred_element_type=jnp.float32) + bgu_ref[0]
        g = jnp.minimum(gu[:, 0:f], SWIGLU_LIMIT)
        lin = jnp.clip(gu[:, f:2 * f], -SWIGLU_LIMIT, SWIGLU_LIMIT)
        act = g * _sigmoid(SWIGLU_ALPHA * g) * (lin + 1.0)
        y = jnp.dot(act.astype(jnp.bfloat16), wd_ref[...], preferred_element_type=jnp.float32) + bd_ref[0]
        for j in range(nt):
            ys_ref[pl.ds(j, MOE_BLOCK, stride=nt), :] = y[:, j * LANES:(j + 1) * LANES]


def _experts(blk_e, blk_s, counts, xs, w_gate_up, b_gate_up, w_down, b_down, nt):
    ne, d, f2 = w_gate_up.shape
    f = f2 // 2
    n_blocks = blk_e.shape[1]
    blk_rows = MOE_BLOCK * nt
    grid_spec = pltpu.PrefetchScalarGridSpec(
        num_scalar_prefetch=3,
        grid=(n_blocks,),
        in_specs=[pl.BlockSpec((blk_rows, LANES), lambda i, be, bs, cn: (bs[0, i], 0)),
                  pl.BlockSpec(memory_space=pl.ANY),
                  pl.BlockSpec((1, 1, f2), lambda i, be, bs, cn: (be[0, i], 0, 0)),
                  pl.BlockSpec(memory_space=pl.ANY),
                  pl.BlockSpec((1, 1, d), lambda i, be, bs, cn: (be[0, i], 0, 0))],
        out_specs=pl.BlockSpec((blk_rows, LANES), lambda i, be, bs, cn: (i, 0)),
        scratch_shapes=[pltpu.VMEM((2, d, f2), jnp.float32), pltpu.VMEM((2, f, d), jnp.float32),
                        pltpu.VMEM((d, f2), jnp.bfloat16), pltpu.VMEM((f, d), jnp.bfloat16),
                        pltpu.SemaphoreType.DMA((2,)), pltpu.SMEM((1,), jnp.int32)],
    )
    return pl.pallas_call(
        functools.partial(_expert_kernel, nt),
        grid_spec=grid_spec,
        out_shape=jax.ShapeDtypeStruct(xs.shape, jnp.float32),
        compiler_params=pltpu.CompilerParams(
            dimension_semantics=("arbitrary",), vmem_limit_bytes=VMEM_LIMIT),
        name="moe_experts",
    )(blk_e, blk_s, counts, xs, w_gate_up, b_gate_up.reshape(ne, 1, f2), w_down, b_down.reshape(ne, 1, d))


def _combine_kernel(nt, loc0_ref, loc1_ref, loc2_ref, loc3_ref, g0_ref, g1_ref, g2_ref, g3_ref,
                    run_ref, runn_ref, x1_ref, gfin_ref, ys_ref, out_ref, stage_ref, acc_ref, sem):
    loc_refs = (loc0_ref, loc1_ref, loc2_ref, loc3_ref)
    gate_refs = (g0_ref, g1_ref, g2_ref, g3_ref)
    rows, d = x1_ref.shape
    n_assign = rows * TOP_K
    step = pl.program_id(0)
    par = step % 2
    unroll = TILE_UNROLL

    def fetch(r_ref, half):
        _run_copies(r_ref,
                    lambda first, base, n: ys_ref.at[pl.ds(pl.multiple_of(base * nt, nt), n * nt), :],
                    lambda first, base, n: stage_ref.at[half, pl.ds(pl.multiple_of(first * nt, nt), n * nt), :],
                    sem.at[half], nt)

    @pl.when(step == 0)
    def _():
        fetch(run_ref, 0)

    @pl.when(step + 1 < pl.num_programs(0))
    def _():
        fetch(runn_ref, 1 - par)

    pltpu.make_async_copy(ys_ref.at[pl.ds(0, n_assign * nt), :], stage_ref.at[par], sem.at[par]).wait()

    def mix_from(half):
        def mix(i, carry):
            for u in range(unroll):
                tok = i * unroll + u
                acc = None
                for k in range(TOP_K):
                    row = stage_ref[half, pl.ds(pl.multiple_of(loc_refs[k][0, 0, 0, tok], nt), nt), :]
                    term = gate_refs[k][0, 0, 0, tok] * row
                    acc = term if acc is None else acc + term
                acc_ref[pl.ds(pl.multiple_of(tok * nt, nt), nt), :] = acc
            return carry

        lax.fori_loop(0, rows // unroll, mix, 0)

    for half in range(2):
        pl.when(par == half)(functools.partial(mix_from, half))

    y = jnp.concatenate([acc_ref[pl.ds(j, rows, stride=nt), :] for j in range(nt)], axis=-1)
    x = x1_ref[...] + y
    ms = jnp.mean(x * x, axis=-1, keepdims=True)
    out_ref[...] = x * lax.rsqrt(ms + NORM_EPS) * gfin_ref[...]


def _combine(loc, gate, runs, x1, g_final, ys, nt):
    t, d = x1.shape
    rows = ROUTE_ROWS
    tiles = t // rows
    ne = runs.shape[2]
    n_assign = rows * TOP_K
    smem = pltpu.SMEM
    return pl.pallas_call(
        functools.partial(_combine_kernel, nt),
        grid=(tiles,),
        in_specs=[*_slot_specs(rows), *_slot_specs(rows),
                  pl.BlockSpec((1, SUBLANES, ne), lambda i: (i, 0, 0), memory_space=smem),
                  pl.BlockSpec((1, SUBLANES, ne), lambda i: (jnp.minimum(i + 1, tiles - 1), 0, 0),
                               memory_space=smem),
                  pl.BlockSpec((rows, d), lambda i: (i, 0)),
                  _const_spec(g_final.shape),
                  pl.BlockSpec(memory_space=pl.ANY)],
        out_specs=pl.BlockSpec((rows, d), lambda i: (i, 0)),
        out_shape=jax.ShapeDtypeStruct((t, d), jnp.float32),
        scratch_shapes=[pltpu.VMEM((2, n_assign * nt, LANES), jnp.float32),
                        pltpu.VMEM((rows * nt, LANES), jnp.float32),
                        pltpu.SemaphoreType.DMA((2,))],
        compiler_params=pltpu.CompilerParams(
            dimension_semantics=("arbitrary",), vmem_limit_bytes=VMEM_LIMIT),
        name="moe_combine",
    )(loc, loc, loc, loc, gate, gate, gate, gate, runs, runs, x1, g_final, ys)


def _block_diag(blocks):
    g, r, c = blocks.shape
    eye = jnp.eye(g, dtype=blocks.dtype)
    return (blocks[:, :, None, :] * eye[:, None, :, None]).reshape(g * r, g * c)


def kernel(x, norm_mix_g, w_in, conv_w, conv_b, conv_ln_g, conv_ln_b, w_conv_out, ssm_a_re, ssm_a_im,
           ssm_log_step, ssm_b_re, ssm_b_im, ssm_c_re, ssm_c_im, ssm_d, w_ssm_glu, b_ssm_glu, w_ssm_out,
           w_out, norm_ffn_g, w_router, b_router, w_gate_up, b_gate_up, w_down, b_down, norm_final_g):
    bsz, seq, d = x.shape
    depth = w_in.shape[0]
    bf = jnp.bfloat16
    nt = d // LANES
    n_tok = bsz * seq
    ne = w_router.shape[-1]
    n_blocks = -(-n_tok * TOP_K // MOE_BLOCK) + ne
    seg_len = MIX_ROWS // SCAN_SEGS

    def row(v):
        return v.reshape(1, -1)

    for l in range(depth):
        ar, ai, arp, aip, bbr, bbi = _ssm_disc(
            ssm_a_re[l], ssm_a_im[l], ssm_log_step[l],
            jnp.swapaxes(ssm_b_re[l], 1, 2), jnp.swapaxes(ssm_b_im[l], 1, 2), seg_len)
        bblk = jnp.concatenate([_block_diag(bbr), _block_diag(bbi)], axis=1).astype(bf)
        c_r = _block_diag(jnp.swapaxes(ssm_c_re[l], 1, 2)).astype(bf)
        c_i = _block_diag(jnp.swapaxes(ssm_c_im[l], 1, 2)).astype(bf)
        avec = jnp.stack([ar.reshape(-1), ai.reshape(-1), arp.reshape(-1), aip.reshape(-1)])

        x = _mixer(x, row(norm_mix_g[l]), w_in[l].astype(bf), conv_w[l], row(conv_b[l]),
                   row(conv_ln_g[l]), row(conv_ln_b[l]), w_conv_out[l].astype(bf), bblk, c_r, c_i, avec,
                   row(ssm_d[l]), w_ssm_glu[l].astype(bf), row(b_ssm_glu[l]), w_ssm_out[l].astype(bf),
                   w_out[l].astype(bf))

        x1 = x.reshape(n_tok, d)
        h2, loc, gate, run, counts = _route(x1, row(norm_ffn_g[l]), w_router[l], row(b_router[l]))
        poff, blk_e, blk_s = _meta(counts, n_blocks)
        runs = _runs(poff, run)
        xs = _dispatch(counts, poff, loc, runs, h2, n_blocks * MOE_BLOCK, nt)
        ys = _experts(blk_e, blk_s, counts, xs, w_gate_up[l], b_gate_up[l], w_down[l], b_down[l], nt)
        assert depth == 1
        x = _combine(loc, gate, runs, x1, row(norm_final_g), ys, nt).reshape(bsz, seq, d)
    return x
```

```python
import functools
import math

import jax
import jax.numpy as jnp
from jax import lax
from jax.experimental import pallas as pl
from jax.experimental.pallas import tpu as pltpu

NORM_EPS = 1e-6
SWIGLU_LIMIT = 7.0
SWIGLU_ALPHA = 1.702
TOP_K = 4

LANES = 128
SUBLANES = 8

MIX_ROWS = 512
SCAN_SEGS = SUBLANES
SEG_PAD = 8
SCAN_UNROLL = 8
TILE_UNROLL = 8
CONV_HALO = 32
CONV_ROWS = 64
ROUTE_ROWS = 512
MOE_BLOCK = 512
V7X_VMEM_BYTES = 64 * 1024 * 1024
VMEM_LIMIT = V7X_VMEM_BYTES - 8 * 1024 * 1024


def _sigmoid(x):
    return jax.nn.sigmoid(x)


def _gelu_tanh(x):
    c = math.sqrt(2.0 / math.pi)
    return 0.5 * x * (1.0 + jnp.tanh(c * (x + 0.044715 * (x * x * x))))


def _const_spec(shape):
    nd = len(shape)
    return pl.BlockSpec(shape, lambda *_: (0,) * nd)


def _ssm_disc_kernel(seg_len, are_ref, aim_ref, ls_ref, bre_ref, bim_ref,
                     ar_ref, ai_ref, arp_ref, aip_ref, bbr_ref, bbi_ref):
    lr = are_ref[...]
    li = aim_ref[...]
    dt = jnp.exp(ls_ref[...])
    mag = jnp.exp(lr * dt)
    ar = mag * jnp.cos(li * dt)
    ai = mag * jnp.sin(li * dt)
    den = lr * lr + li * li
    fr = ((ar - 1.0) * lr + ai * li) / den
    fi = (ai * lr - (ar - 1.0) * li) / den
    ar_ref[...] = ar
    ai_ref[...] = ai
    magp = jnp.exp(lr * dt * seg_len)
    arp_ref[...] = magp * jnp.cos(li * dt * seg_len)
    aip_ref[...] = magp * jnp.sin(li * dt * seg_len)
    br = bre_ref[...]
    bi = bim_ref[...]
    bbr_ref[...] = fr[:, None, :] * br - fi[:, None, :] * bi
    bbi_ref[...] = fr[:, None, :] * bi + fi[:, None, :] * br


def _ssm_disc(a_re, a_im, log_step, bt_re, bt_im, seg_len):
    g, p = a_re.shape
    h = bt_re.shape[1]
    f32 = jnp.float32
    out_shape = [jax.ShapeDtypeStruct((g, p), f32)] * 4 + [jax.ShapeDtypeStruct((g, h, p), f32)] * 2
    return pl.pallas_call(
        functools.partial(_ssm_disc_kernel, float(seg_len)),
        out_shape=out_shape,
        name="ssm_disc",
    )(a_re, a_im, log_step.reshape(g, 1), bt_re, bt_im)


def _mixer_kernel(dims, x_ref, gmix_ref, win_ref, convw_ref, convb_ref, lng_ref, lnb_ref, wco_ref,
                  bblk_ref, cr_ref, ci_ref, avec_ref, dskip_ref, wglu_ref, bglu_ref, wso_ref, wout_ref,
                  x1_ref, vbuf_ref, conv_ref, scan_ref, xst_ref, cin_ref, st_ref):
    d, dc, ds, gp, cw = dims
    rows = MIX_ROWS
    seg = rows // SCAN_SEGS
    pitch = seg + SEG_PAD
    nsl = gp // LANES
    c_idx = pl.program_id(1)

    half = rows // 2
    tail = 2 * half + 1

    @pl.when(c_idx == 0)
    def _():
        for lc in range(dc // LANES):
            vbuf_ref[lc, pl.ds(0, CONV_HALO, stride=2), :] = jnp.zeros((CONV_HALO, LANES), jnp.float32)
        st_ref[...] = jnp.zeros_like(st_ref)

    @pl.when(c_idx != 0)
    def _():
        for lc in range(dc // LANES):
            vbuf_ref[lc, pl.ds(0, CONV_HALO, stride=2), :] = vbuf_ref[lc, pl.ds(tail, CONV_HALO, stride=2), :]

    x = x_ref[0]
    ms = jnp.mean(x * x, axis=-1, keepdims=True)
    h = (x * lax.rsqrt(ms + NORM_EPS) * gmix_ref[...]).astype(jnp.bfloat16)

    pa = jnp.dot(h, win_ref[:, 0:2 * dc], preferred_element_type=jnp.float32)
    vglu = pa[:, 0:dc] * _sigmoid(pa[:, dc:2 * dc])
    for lc in range(dc // LANES):
        ls = slice(lc * LANES, (lc + 1) * LANES)
        vbuf_ref[lc, pl.ds(2 * CONV_HALO, half, stride=2), :] = vglu[0:half, ls]
        vbuf_ref[lc, pl.ds(2 * CONV_HALO + 1, half, stride=2), :] = vglu[half:rows, ls]
        vbuf_ref[lc, pl.ds(1, CONV_HALO, stride=2), :] = vglu[half - CONV_HALO:half, ls]

    base = CONV_HALO - (cw - 1)
    nv = CONV_ROWS // SUBLANES

    def conv_chunk(rc, p):
        r0 = rc * CONV_ROWS
        for lc in range(dc // LANES):
            ls = slice(lc * LANES, (lc + 1) * LANES)
            wv = [jnp.broadcast_to(convw_ref[k:k + 1, ls], (SUBLANES, LANES)) for k in range(cw)]
            acc = [jnp.zeros((SUBLANES, LANES), jnp.float32) for _ in range(nv)]
            for s in range(CONV_ROWS - SUBLANES + cw):
                win = vbuf_ref[lc, pl.ds(2 * (r0 + base + s) + p, SUBLANES, stride=2), :]
                for i in range(nv):
                    k = s - SUBLANES * i
                    if 0 <= k < cw:
                        acc[i] = acc[i] + wv[k] * win
            for i in range(nv):
                conv_ref[pl.ds(p * half + r0 + SUBLANES * i, SUBLANES), ls] = acc[i]

    g0 = 2 * dc + ds
    chunks = [(rc, p) for rc in range(half // CONV_ROWS) for p in range(2)]
    gw = 2 * d // len(chunks)
    sg = []
    for n, (rc, p) in enumerate(chunks):
        sg.append(_sigmoid(jnp.dot(h, win_ref[:, g0 + n * gw:g0 + (n + 1) * gw],
                                   preferred_element_type=jnp.float32)))
        conv_chunk(rc, p)
    sg = jnp.concatenate(sg, axis=-1)
    sg_conv = sg[:, 0:d]
    sg_ssm = sg[:, d:2 * d]

    v = conv_ref[...] + convb_ref[...]
    mu = jnp.mean(v, axis=-1, keepdims=True)
    vc = v - mu
    var = jnp.mean(vc * vc, axis=-1, keepdims=True)
    v = vc * lax.rsqrt(var + NORM_EPS) * lng_ref[...] + lnb_ref[...]
    v = v * _sigmoid(v)
    y_conv = jnp.dot(v.astype(jnp.bfloat16), wco_ref[...], preferred_element_type=jnp.float32)

    u = jnp.dot(h, win_ref[:, 2 * dc:2 * dc + ds], preferred_element_type=jnp.float32)
    bu = jnp.dot(u.astype(jnp.bfloat16), bblk_ref[...], preferred_element_type=jnp.float32)
    for n in range(2 * nsl):
        for s in range(SCAN_SEGS):
            scan_ref[n, s * pitch:s * pitch + seg, :] = bu[s * seg:(s + 1) * seg, n * LANES:(n + 1) * LANES]

    def bcast(row, n):
        return jnp.broadcast_to(avec_ref[row:row + 1, n * LANES:(n + 1) * LANES], (SUBLANES, LANES))

    a_r = [bcast(0, n) for n in range(nsl)]
    a_i = [bcast(1, n) for n in range(nsl)]

    def step(j, st, store):
        out = []
        for n in range(nsl):
            sr, si = st[2 * n], st[2 * n + 1]
            br = scan_ref[n, pl.ds(j, SCAN_SEGS, stride=pitch), :]
            bi = scan_ref[nsl + n, pl.ds(j, SCAN_SEGS, stride=pitch), :]
            nr = a_r[n] * sr - a_i[n] * si + br
            ni = a_r[n] * si + a_i[n] * sr + bi
            if store:
                scan_ref[n, pl.ds(j, SCAN_SEGS, stride=pitch), :] = nr
                scan_ref[nsl + n, pl.ds(j, SCAN_SEGS, stride=pitch), :] = ni
            out += [nr, ni]
        return tuple(out)

    def steps(jj, st, store):
        for q in range(SCAN_UNROLL):
            st = step(jj * SCAN_UNROLL + q, st, store)
        return st

    zero = jnp.zeros((SUBLANES, LANES), jnp.float32)
    fin = lax.fori_loop(0, seg // SCAN_UNROLL, functools.partial(steps, store=False), (zero,) * (2 * nsl))

    for n in range(nsl):
        fr, fi = fin[2 * n], fin[2 * n + 1]
        ap_r = avec_ref[2:3, n * LANES:(n + 1) * LANES]
        ap_i = avec_ref[3:4, n * LANES:(n + 1) * LANES]
        c_r = st_ref[0:1, n * LANES:(n + 1) * LANES]
        c_i = st_ref[1:2, n * LANES:(n + 1) * LANES]
        for s in range(SCAN_SEGS):
            cin_ref[2 * n, s:s + 1, :] = c_r
            cin_ref[2 * n + 1, s:s + 1, :] = c_i
            n_r = ap_r * c_r - ap_i * c_i + fr[s:s + 1, :]
            n_i = ap_r * c_i + ap_i * c_r + fi[s:s + 1, :]
            c_r, c_i = n_r, n_i
        st_ref[0:1, n * LANES:(n + 1) * LANES] = c_r
        st_ref[1:2, n * LANES:(n + 1) * LANES] = c_i

    lax.fori_loop(0, seg // SCAN_UNROLL, functools.partial(steps, store=True),
                  tuple(cin_ref[q] for q in range(2 * nsl)))

    for n in range(2 * nsl):
        for s in range(SCAN_SEGS):
            xst_ref[s * seg:(s + 1) * seg, n * LANES:(n + 1) * LANES] = (
                scan_ref[n, s * pitch:s * pitch + seg, :].astype(jnp.bfloat16))

    y = (jnp.dot(xst_ref[:, 0:gp], cr_ref[...], preferred_element_type=jnp.float32)
         - jnp.dot(xst_ref[:, gp:2 * gp], ci_ref[...], preferred_element_type=jnp.float32))
    y = _gelu_tanh(y + dskip_ref[...] * u)
    glu = jnp.dot(y.astype(jnp.bfloat16), wglu_ref[...], preferred_element_type=jnp.float32) + bglu_ref[...]
    y = y * _sigmoid(glu)
    y_ssm = jnp.dot(y.astype(jnp.bfloat16), wso_ref[...], preferred_element_type=jnp.float32)

    m = sg_conv * y_conv + sg_ssm * y_ssm
    x1_ref[0] = x + jnp.dot(m.astype(jnp.bfloat16), wout_ref[...], preferred_element_type=jnp.float32)


def _mixer(x, gmix, w_in, conv_w, conv_b, ln_g, ln_b, w_co, bblk, c_r, c_i, avec, d_skip,
           w_glu, b_glu, w_so, w_out):
    b, s, d = x.shape
    cw, dc = conv_w.shape
    ds = d_skip.shape[-1]
    gp = c_r.shape[0]
    rows = MIX_ROWS
    seg = rows // SCAN_SEGS
    pitch = seg + SEG_PAD
    assert s % rows == 0 and cw - 1 <= CONV_HALO and gp % LANES == 0 and dc % LANES == 0
    dims = (d, dc, ds, gp, cw)
    consts = [gmix, w_in, conv_w, conv_b, ln_g, ln_b, w_co, bblk, c_r, c_i, avec, d_skip,
              w_glu, b_glu, w_so, w_out]
    return pl.pallas_call(
        functools.partial(_mixer_kernel, dims),
        grid=(b, s // rows),
        in_specs=[pl.BlockSpec((1, rows, d), lambda i, j: (i, j, 0))] + [_const_spec(c.shape) for c in consts],
        out_specs=pl.BlockSpec((1, rows, d), lambda i, j: (i, j, 0)),
        out_shape=jax.ShapeDtypeStruct((b, s, d), jnp.float32),
        scratch_shapes=[
            pltpu.VMEM((dc // LANES, 2 * CONV_HALO + rows, LANES), jnp.float32),
            pltpu.VMEM((rows, dc), jnp.float32),
            pltpu.VMEM((2 * gp // LANES, SCAN_SEGS * pitch, LANES), jnp.float32),
            pltpu.VMEM((rows, 2 * gp), jnp.bfloat16),
            pltpu.VMEM((2 * gp // LANES, SUBLANES, LANES), jnp.float32),
            pltpu.VMEM((2, gp), jnp.float32),
        ],
        compiler_params=pltpu.CompilerParams(
            dimension_semantics=("arbitrary", "arbitrary"), vmem_limit_bytes=VMEM_LIMIT),
        name="mixer",
    )(x, *consts)


def _route_kernel(x1_ref, g_ref, wr_ref, br_ref, st_ref, loc_ref, gate_ref, run_ref, cnt_ref, carry_ref,
                  hbuf_ref, locv_ref, lsem, *locs_refs):
    rows, d = x1_ref.shape
    ne = wr_ref.shape[1]
    step = pl.program_id(0)
    tiles = pl.num_programs(0) - 1
    par = step % 2
    nt = d // LANES

    @pl.when(step == 0)
    def _():
        carry_ref[...] = jnp.zeros_like(carry_ref)

    def stage_from(half):
        for k in range(TOP_K):
            pltpu.make_async_copy(locv_ref.at[half, k], locs_refs[half * TOP_K + k], lsem.at[half]).wait()

        def place(i, carry):
            for u in range(TILE_UNROLL):
                tok = i * TILE_UNROLL + u
                row = hbuf_ref[half, pl.ds(pl.multiple_of(tok * nt, nt), nt), :]
                for k in range(TOP_K):
                    st_ref[pl.ds(pl.multiple_of(locs_refs[half * TOP_K + k][0, tok], nt), nt), :] = row
            return carry

        lax.fori_loop(0, rows // TILE_UNROLL, place, 0)

    for half in range(2):
        pl.when(jnp.logical_and(step >= 1, 1 - par == half))(functools.partial(stage_from, half))

    x = x1_ref[...]
    ms = jnp.mean(x * x, axis=-1, keepdims=True)
    h2 = x * lax.rsqrt(ms + NORM_EPS) * g_ref[...]

    for j in range(nt):
        hbuf_ref[par, pl.ds(j, rows, stride=nt), :] = h2[:, j * LANES:(j + 1) * LANES]

    bf = jnp.bfloat16
    h_hi = h2.astype(bf)
    h_lo = (h2 - h_hi.astype(jnp.float32)).astype(bf)
    w = wr_ref[...]
    w_hi = w.astype(bf)
    w_lo = (w - w_hi.astype(jnp.float32)).astype(bf)
    logits = (jnp.dot(h_hi, w_hi, preferred_element_type=jnp.float32)
              + jnp.dot(h_lo, w_hi, preferred_element_type=jnp.float32)
              + jnp.dot(h_hi, w_lo, preferred_element_type=jnp.float32)) + br_ref[...]

    pad = jnp.full((rows, LANES - ne), -jnp.inf, jnp.float32)
    work = jnp.concatenate([logits, pad], axis=1).T[0:ne, :]
    eidx = lax.broadcasted_iota(jnp.int32, (ne, rows), 0).astype(jnp.float32)
    sels, vals = [], []
    for k in range(TOP_K):
        mx = jnp.max(work, axis=0, keepdims=True)
        idx = jnp.min(jnp.where(work == mx, eidx, float(ne)), axis=0, keepdims=True)
        sel = eidx == idx
        sels.append(sel)
        vals.append(mx)
        work = jnp.where(sel, -jnp.inf, work)

    exps = [jnp.exp(v - vals[0]) for v in vals]
    tot = exps[0]
    for e in exps[1:]:
        tot = tot + e
    for k in range(TOP_K):
        gate_ref[0, k] = exps[k] / tot

    cnt = jnp.zeros((ne, rows), jnp.float32)
    for sel in sels:
        cnt = cnt + sel.astype(jnp.float32)
    r_i = lax.broadcasted_iota(jnp.int32, (rows, rows), 0)
    c_i = lax.broadcasted_iota(jnp.int32, (rows, rows), 1)
    triu = (r_i < c_i).astype(bf)
    before = jnp.dot(cnt.astype(bf), triu, preferred_element_type=jnp.float32)
    cnt_col = jnp.sum(cnt, axis=1, keepdims=True)
    e_r = lax.broadcasted_iota(jnp.int32, (ne, ne), 0)
    e_c = lax.broadcasted_iota(jnp.int32, (ne, ne), 1)
    col_b = jnp.broadcast_to(cnt_col, (ne, ne))
    cnt_row = jnp.sum(jnp.where(e_r == e_c, col_b, 0.0), axis=0, keepdims=True)
    first_row = jnp.sum(jnp.where(e_r < e_c, col_b, 0.0), axis=0, keepdims=True)
    first_col = jnp.sum(jnp.where(e_c < e_r, jnp.broadcast_to(cnt_row, (ne, ne)), 0.0), axis=1, keepdims=True)
    pos = before + first_col
    for k in range(TOP_K):
        lk = jnp.sum(jnp.where(sels[k], pos, 0.0), axis=0, keepdims=True).astype(jnp.int32)
        loc_ref[0, k] = lk * nt
        locv_ref[par, k] = lk * nt

    def hand_over(half):
        for k in range(TOP_K):
            pltpu.make_async_copy(locv_ref.at[half, k], locs_refs[half * TOP_K + k], lsem.at[half]).start()

    for half in range(2):
        pl.when(jnp.logical_and(step < tiles, par == half))(functools.partial(hand_over, half))

    @pl.when(step < tiles)
    def _():
        run_ref[...] = jnp.zeros_like(run_ref)
        run_ref[0, 0:1, :] = cnt_row.astype(jnp.int32)
        run_ref[0, 1:2, :] = first_row.astype(jnp.int32)
        run_ref[0, 2:3, :] = carry_ref[...].astype(jnp.int32)
        carry_ref[...] = carry_ref[...] + cnt_row
        cnt_ref[...] = carry_ref[...].astype(jnp.int32)


def _route(x1, g_ffn, w_router, b_router):
    t, d = x1.shape
    ne = w_router.shape[1]
    rows = ROUTE_ROWS
    assert t % rows == 0 and d % LANES == 0
    assert TOP_K <= SUBLANES and ne <= LANES
    nt = d // LANES
    tiles = t // rows
    def cur(i):
        return jnp.minimum(i, tiles - 1)

    per_tile = pl.BlockSpec((1, TOP_K, 1, rows), lambda i: (cur(i), 0, 0, 0))
    return pl.pallas_call(
        _route_kernel,
        grid=(tiles + 1,),
        in_specs=[pl.BlockSpec((rows, d), lambda i: (cur(i), 0)),
                  _const_spec(g_ffn.shape), _const_spec(w_router.shape), _const_spec(b_router.shape)],
        out_specs=[pl.BlockSpec((rows * TOP_K * nt, LANES), lambda i: (jnp.maximum(i - 1, 0), 0)),
                   per_tile, per_tile,
                   pl.BlockSpec((1, SUBLANES, ne), lambda i: (cur(i), 0, 0)),
                   _const_spec((1, ne))],
        out_shape=[jax.ShapeDtypeStruct((t * TOP_K * nt, LANES), jnp.float32),
                   jax.ShapeDtypeStruct((tiles, TOP_K, 1, rows), jnp.int32),
                   jax.ShapeDtypeStruct((tiles, TOP_K, 1, rows), jnp.float32),
                   jax.ShapeDtypeStruct((tiles, SUBLANES, ne), jnp.int32),
                   jax.ShapeDtypeStruct((1, ne), jnp.int32)],
        scratch_shapes=[pltpu.VMEM((1, ne), jnp.float32),
                        pltpu.VMEM((2, rows * nt, LANES), jnp.float32),
                        pltpu.VMEM((2, TOP_K, 1, rows), jnp.int32),
                        pltpu.SemaphoreType.DMA((2,))]
                       + [pltpu.SMEM((1, rows), jnp.int32)] * (2 * TOP_K),
        compiler_params=pltpu.CompilerParams(
            dimension_semantics=("arbitrary",), vmem_limit_bytes=VMEM_LIMIT),
        name="route",
    )(x1, g_ffn, w_router, b_router)


def _meta_kernel(cnt_ref, poff_ref, blke_ref, blks_ref):
    ne = cnt_ref.shape[1]
    nb = blke_ref.shape[1]
    f32 = jnp.float32
    cnt = cnt_ref[...].astype(f32)
    nblk = jnp.floor((cnt + (MOE_BLOCK - 1)) * (1.0 / MOE_BLOCK))
    e_r = lax.broadcasted_iota(jnp.int32, (ne, ne), 0)
    e_c = lax.broadcasted_iota(jnp.int32, (ne, ne), 1)
    nb_rows = jnp.broadcast_to(nblk, (ne, ne))
    bend_col = jnp.sum(jnp.where(e_c <= e_r, nb_rows, 0.0), axis=1, keepdims=True)
    nb_col = jnp.sum(jnp.where(e_c == e_r, nb_rows, 0.0), axis=1, keepdims=True)
    boff = jnp.sum(jnp.where(e_r < e_c, jnp.broadcast_to(nb_col, (ne, ne)), 0.0),
                   axis=0, keepdims=True)
    poff_ref[...] = (boff * MOE_BLOCK).astype(jnp.int32)
    total = jnp.sum(nblk, axis=1, keepdims=True)
    blk = lax.broadcasted_iota(jnp.int32, (ne, nb), 1).astype(f32)
    done = jnp.where(jnp.broadcast_to(bend_col, (ne, nb)) <= blk, 1.0, 0.0)
    be = jnp.minimum(jnp.sum(done, axis=0, keepdims=True), ne - 1.0)
    bid = lax.broadcasted_iota(jnp.int32, (1, nb), 1).astype(f32)
    last_e = jnp.sum(jnp.where(bid == total - 1.0, be, 0.0), axis=1, keepdims=True)
    blke_ref[...] = jnp.where(bid < total, be, last_e).astype(jnp.int32)
    blks_ref[...] = jnp.minimum(bid, total - 1.0).astype(jnp.int32)


def _meta(counts, n_blocks):
    ne = counts.shape[1]
    return pl.pallas_call(
        _meta_kernel,
        out_shape=[jax.ShapeDtypeStruct((1, ne), jnp.int32),
                   jax.ShapeDtypeStruct((1, n_blocks), jnp.int32),
                   jax.ShapeDtypeStruct((1, n_blocks), jnp.int32)],
        name="moe_meta",
    )(counts)


def _runs_kernel(poff_ref, run_ref, out_ref):
    run = run_ref[...]
    row = lax.broadcasted_iota(jnp.int32, run.shape, 1)
    out_ref[...] = jnp.where(row == 2, run + poff_ref[...][None], run)


def _runs(poff, run):
    return pl.pallas_call(
        _runs_kernel,
        out_shape=jax.ShapeDtypeStruct(run.shape, jnp.int32),
        name="moe_runs",
    )(poff, run)


def _tile(ref, r, nt):
    return ref.at[pl.ds(pl.multiple_of(r * nt, nt), nt), :]


def _slot_specs(rows):
    assert TOP_K == 4
    return [pl.BlockSpec((1, 1, 1, rows), functools.partial(lambda k, i: (i, k, 0, 0), k), memory_space=pltpu.SMEM)
            for k in range(TOP_K)]


def _run_copies(run_ref, src_of, dst_of, sem, nt):
    ne = run_ref.shape[2]

    def per_expert(e, carry):
        n = run_ref[0, 0, e]

        @pl.when(n > 0)
        def _():
            pltpu.make_async_copy(src_of(run_ref[0, 1, e], run_ref[0, 2, e], n),
                                  dst_of(run_ref[0, 1, e], run_ref[0, 2, e], n), sem).start()
        return carry

    lax.fori_loop(0, ne, per_expert, 0)


def _dispatch_kernel(nt, cnt_ref, poff_ref, loc0_ref, loc1_ref, loc2_ref, loc3_ref, run_ref, h2_ref, xs_ref,
                     stage_ref, zero_ref, sem, zsem):
    loc_refs = (loc0_ref, loc1_ref, loc2_ref, loc3_ref)
    step = pl.program_id(0)
    last = pl.num_programs(0) - 1
    rows = h2_ref.shape[0] // nt
    n_assign = rows * TOP_K
    ne = cnt_ref.shape[1]
    par = step % 2
    unroll = TILE_UNROLL

    def rows_at(ref, first, n):
        return ref.at[pl.ds(pl.multiple_of(first * nt, nt), n * nt), :]

    def wait_runs(half):
        pltpu.make_async_copy(stage_ref.at[half], xs_ref.at[pl.ds(0, n_assign * nt), :], sem.at[half]).wait()

    @pl.when(step == 0)
    def _():
        zero_ref[...] = jnp.zeros_like(zero_ref)

        def per_expert(e, carry):
            c = cnt_ref[0, e]
            pad = (MOE_BLOCK - (c % MOE_BLOCK)) % MOE_BLOCK

            @pl.when(pad > 0)
            def _():
                cp = pltpu.make_async_copy(rows_at(zero_ref, 0, pad), rows_at(xs_ref, poff_ref[0, e] + c, pad), zsem)
                cp.start()
                cp.wait()
            return poff_ref[0, e] + c + pad

        used = lax.fori_loop(0, ne, per_expert, 0)

        def spare(b, carry):
            cp = pltpu.make_async_copy(zero_ref, rows_at(xs_ref, b * MOE_BLOCK, MOE_BLOCK), zsem)
            cp.start()
            cp.wait()
            return carry

        lax.fori_loop(used // MOE_BLOCK, xs_ref.shape[0] // (MOE_BLOCK * nt), spare, 0)

    @pl.when(step >= 2)
    def _():
        wait_runs(par)

    def place_into(half):
        def place(i, carry):
            for u in range(unroll):
                tok = i * unroll + u
                row = _tile(h2_ref, tok, nt)[...]
                for k in range(TOP_K):
                    stage_ref[half, pl.ds(pl.multiple_of(loc_refs[k][0, 0, 0, tok], nt), nt), :] = row
            return carry

        lax.fori_loop(0, rows // unroll, place, 0)

    for half in range(2):
        pl.when(par == half)(functools.partial(place_into, half))

    _run_copies(run_ref,
                lambda first, base, n: stage_ref.at[par, pl.ds(pl.multiple_of(first * nt, nt), n * nt), :],
                lambda first, base, n: rows_at(xs_ref, base, n), sem.at[par], nt)

    @pl.when(step == last)
    def _():
        wait_runs(par)

        @pl.when(step >= 1)
        def _():
            wait_runs(1 - par)


def _dispatch(counts, poff, loc, runs, h2, n_rows, nt):
    t = h2.shape[0] // nt
    rows = ROUTE_ROWS
    tiles = t // rows
    ne = counts.shape[1]
    smem = pl.BlockSpec(memory_space=pltpu.SMEM)
    return pl.pallas_call(
        functools.partial(_dispatch_kernel, nt),
        grid=(tiles,),
        in_specs=[smem, smem,
                  *_slot_specs(rows),
                  pl.BlockSpec((1, SUBLANES, ne), lambda i: (i, 0, 0), memory_space=pltpu.SMEM),
                  pl.BlockSpec((rows * nt, LANES), lambda i: (i, 0))],
        out_specs=pl.BlockSpec(memory_space=pl.ANY),
        out_shape=jax.ShapeDtypeStruct((n_rows * nt, LANES), h2.dtype),
        scratch_shapes=[pltpu.VMEM((2, rows * TOP_K * nt, LANES), h2.dtype),
                        pltpu.VMEM((MOE_BLOCK * nt, LANES), h2.dtype),
                        pltpu.SemaphoreType.DMA((2,)), pltpu.SemaphoreType.DMA(())],
        compiler_params=pltpu.CompilerParams(
            dimension_semantics=("arbitrary",), vmem_limit_bytes=VMEM_LIMIT, has_side_effects=True),
        name="moe_dispatch",
    )(counts, poff, loc, loc, loc, loc, runs, h2)


def _expert_kernel(nt, blke_ref, blks_ref, cnt_ref, xs_ref, wgu_hbm, bgu_ref, wd_hbm, bd_ref, ys_ref,
                   wgu32_ref, wd32_ref, wgu_ref, wd_ref, wsem, run_ref):
    i = pl.program_id(0)
    last = pl.num_programs(0) - 1
    f = wd_ref.shape[0]
    e = blke_ref[0, i]
    valid = blks_ref[0, i] == i

    def fetch(expert, slot):
        pltpu.make_async_copy(wgu_hbm.at[expert], wgu32_ref.at[slot], wsem.at[slot]).start()
        pltpu.make_async_copy(wd_hbm.at[expert], wd32_ref.at[slot], wsem.at[slot]).start()

    def wait_fetch(slot):
        pltpu.make_async_copy(wgu_hbm.at[0], wgu32_ref.at[slot], wsem.at[slot]).wait()
        pltpu.make_async_copy(wd_hbm.at[0], wd32_ref.at[slot], wsem.at[slot]).wait()

    @pl.when(i == 0)
    def _():
        run_ref[0] = 0
        fetch(e, 0)

    @pl.when(jnp.logical_and(valid, jnp.logical_or(i == 0, e != blke_ref[0, jnp.maximum(i - 1, 0)])))
    def _():
        slot = run_ref[0] % 2
        wait_fetch(slot)
        nxt = i + (cnt_ref[0, e] + (MOE_BLOCK - 1)) // MOE_BLOCK
        nxt_c = jnp.minimum(nxt, last)

        @pl.when(blks_ref[0, nxt_c] == nxt)
        def _():
            fetch(blke_ref[0, nxt_c], 1 - slot)

        wgu_ref[...] = wgu32_ref[slot].astype(jnp.bfloat16)
        wd_ref[...] = wd32_ref[slot].astype(jnp.bfloat16)
        run_ref[0] = run_ref[0] + 1

    @pl.when(blks_ref[0, i] != i)
    def _():
        ys_ref[...] = jnp.zeros_like(ys_ref)

    @pl.when(blks_ref[0, i] == i)
    def _():
        x = jnp.concatenate([xs_ref[pl.ds(j, MOE_BLOCK, stride=nt), :] for j in range(nt)], axis=-1)
        gu = jnp.dot(x.astype(jnp.bfloat16), wgu_ref[...], preferred_element_type=jnp.float32) + bgu_ref[0]
        g = jnp.minimum(gu[:, 0:f], SWIGLU_LIMIT)
        lin = jnp.clip(gu[:, f:2 * f], -SWIGLU_LIMIT, SWIGLU_LIMIT)
        act = g * _sigmoid(SWIGLU_ALPHA * g) * (lin + 1.0)
        y = jnp.dot(act.astype(jnp.bfloat16), wd_ref[...], preferred_element_type=jnp.float32) + bd_ref[0]
        for j in range(nt):
            ys_ref[pl.ds(j, MOE_BLOCK, stride=nt), :] = y[:, j * LANES:(j + 1) * LANES]


def _experts(blk_e, blk_s, counts, xs, w_gate_up, b_gate_up, w_down, b_down, nt):
    ne, d, f2 = w_gate_up.shape
    f = f2 // 2
    n_blocks = blk_e.shape[1]
    blk_rows = MOE_BLOCK * nt
    grid_spec = pltpu.PrefetchScalarGridSpec(
        num_scalar_prefetch=3,
        grid=(n_blocks,),
        in_specs=[pl.BlockSpec((blk_rows, LANES), lambda i, be, bs, cn: (bs[0, i], 0)),
                  pl.BlockSpec(memory_space=pl.ANY),
                  pl.BlockSpec((1, 1, f2), lambda i, be, bs, cn: (be[0, i], 0, 0)),
                  pl.BlockSpec(memory_space=pl.ANY),
                  pl.BlockSpec((1, 1, d), lambda i, be, bs, cn: (be[0, i], 0, 0))],
        out_specs=pl.BlockSpec((blk_rows, LANES), lambda i, be, bs, cn: (i, 0)),
        scratch_shapes=[pltpu.VMEM((2, d, f2), jnp.float32), pltpu.VMEM((2, f, d), jnp.float32),
                        pltpu.VMEM((d, f2), jnp.bfloat16), pltpu.VMEM((f, d), jnp.bfloat16),
                        pltpu.SemaphoreType.DMA((2,)), pltpu.SMEM((1,), jnp.int32)],
    )
    return pl.pallas_call(
        functools.partial(_expert_kernel, nt),
        grid_spec=grid_spec,
        out_shape=jax.ShapeDtypeStruct(xs.shape, jnp.float32),
        compiler_params=pltpu.CompilerParams(
            dimension_semantics=("arbitrary",), vmem_limit_bytes=VMEM_LIMIT),
        name="moe_experts",
    )(blk_e, blk_s, counts, xs, w_gate_up, b_gate_up.reshape(ne, 1, f2), w_down, b_down.reshape(ne, 1, d))


def _stage_kernel(nt, loc0_ref, loc1_ref, loc2_ref, loc3_ref, h2_ref, st_ref):
    loc_refs = (loc0_ref, loc1_ref, loc2_ref, loc3_ref)
    rows = h2_ref.shape[0] // nt

    def place(i, carry):
        for u in range(TILE_UNROLL):
            tok = i * TILE_UNROLL + u
            row = _tile(h2_ref, tok, nt)[...]
            for k in range(TOP_K):
                st_ref[pl.ds(pl.multiple_of(loc_refs[k][0, 0, 0, tok], nt), nt), :] = row
        return carry

    lax.fori_loop(0, rows // TILE_UNROLL, place, 0)


def _stage(loc, h2, nt):
    t = h2.shape[0] // nt
    rows = ROUTE_ROWS
    tiles = t // rows
    return pl.pallas_call(
        functools.partial(_stage_kernel, nt),
        grid=(tiles,),
        in_specs=[*_slot_specs(rows), pl.BlockSpec((rows * nt, LANES), lambda i: (i, 0))],
        out_specs=pl.BlockSpec((rows * TOP_K * nt, LANES), lambda i: (i, 0)),
        out_shape=jax.ShapeDtypeStruct((t * TOP_K * nt, LANES), jnp.float32),
        compiler_params=pltpu.CompilerParams(
            dimension_semantics=("arbitrary",), vmem_limit_bytes=VMEM_LIMIT),
        name="moe_stage",
    )(loc, loc, loc, loc, h2)


def _run_expert_kernel(nt, blke_ref, blks_ref, cnt_ref, poff_ref, run_ref, st_hbm, wgu_hbm, bgu_ref, wd_hbm,
                       bd_ref, ys_ref, xbuf_ref, wgu32_ref, wd32_ref, wgu_ref, wd_ref, gsem, wsem, runidx_ref,
                       tptr_ref):
    i = pl.program_id(0)
    last = pl.num_programs(0) - 1
    f = wd_ref.shape[0]
    e = blke_ref[0, i]
    valid = blks_ref[0, i] == i
    p = i % 2
    tiles = run_ref.shape[0]
    tile_rows = st_hbm.shape[0] // nt // tiles

    def block_rows(b):
        eb = blke_ref[0, b]
        r0 = b * MOE_BLOCK - poff_ref[0, eb]
        return eb, r0, jnp.minimum(MOE_BLOCK, cnt_ref[0, eb] - r0)

    def gather(b, half):
        eb, r0, _ = block_rows(b)
        fresh = jnp.logical_or(b == 0, eb != blke_ref[0, jnp.maximum(b - 1, 0)])
        t0 = jnp.where(fresh, 0, tptr_ref[0])

        def more(t):
            return jnp.logical_and(t < tiles, run_ref[jnp.minimum(t, tiles - 1), 2, eb] < r0 + MOE_BLOCK)

        def one_tile(t):
            c0 = run_ref[t, 2, eb]
            lo = jnp.maximum(c0, r0)
            hi = jnp.minimum(c0 + run_ref[t, 0, eb], r0 + MOE_BLOCK)

            @pl.when(hi > lo)
            def _():
                src = t * tile_rows + run_ref[t, 1, eb] + (lo - c0)
                pltpu.make_async_copy(
                    st_hbm.at[pl.ds(pl.multiple_of(src * nt, nt), (hi - lo) * nt), :],
                    xbuf_ref.at[half, pl.ds(pl.multiple_of((lo - r0) * nt, nt), (hi - lo) * nt), :],
                    gsem.at[half]).start()
            return t + 1

        t_end = lax.while_loop(more, one_tile, t0)
        tptr_ref[0] = jnp.maximum(t_end - 1, 0)

    def wait_gather(b, half):
        _, _, n = block_rows(b)
        pltpu.make_async_copy(st_hbm.at[pl.ds(0, n * nt), :], xbuf_ref.at[half, pl.ds(0, n * nt), :],
                              gsem.at[half]).wait()

    def fetch(expert, slot):
        pltpu.make_async_copy(wgu_hbm.at[expert], wgu32_ref.at[slot], wsem.at[slot]).start()
        pltpu.make_async_copy(wd_hbm.at[expert], wd32_ref.at[slot], wsem.at[slot]).start()

    def wait_fetch(slot):
        pltpu.make_async_copy(wgu_hbm.at[0], wgu32_ref.at[slot], wsem.at[slot]).wait()
        pltpu.make_async_copy(wd_hbm.at[0], wd32_ref.at[slot], wsem.at[slot]).wait()

    @pl.when(i == 0)
    def _():
        xbuf_ref[...] = jnp.zeros_like(xbuf_ref)
        runidx_ref[0] = 0
        tptr_ref[0] = 0
        fetch(e, 0)
        gather(0, 0)

    @pl.when(jnp.logical_and(valid, jnp.logical_or(i == 0, e != blke_ref[0, jnp.maximum(i - 1, 0)])))
    def _():
        slot = runidx_ref[0] % 2
        wait_fetch(slot)
        nxt = i + (cnt_ref[0, e] + (MOE_BLOCK - 1)) // MOE_BLOCK
        nxt_c = jnp.minimum(nxt, last)

        @pl.when(blks_ref[0, nxt_c] == nxt)
        def _():
            fetch(blke_ref[0, nxt_c], 1 - slot)

        wgu_ref[...] = wgu32_ref[slot].astype(jnp.bfloat16)
        wd_ref[...] = wd32_ref[slot].astype(jnp.bfloat16)
        runidx_ref[0] = runidx_ref[0] + 1

    @pl.when(jnp.logical_not(valid))
    def _():
        ys_ref[...] = jnp.zeros_like(ys_ref)

    @pl.when(valid)
    def _():
        nb = jnp.minimum(i + 1, last)

        @pl.when(blks_ref[0, nb] == i + 1)
        def _():
            gather(i + 1, 1 - p)

        wait_gather(i, p)
        x = jnp.concatenate([xbuf_ref[p, pl.ds(j, MOE_BLOCK, stride=nt), :] for j in range(nt)], axis=-1)
        gu = jnp.dot(x.astype(jnp.bfloat16), wgu_ref[...], preferred_element_type=jnp.float32) + bgu_ref[0]
        g = jnp.minimum(gu[:, 0:f], SWIGLU_LIMIT)
        lin = jnp.clip(gu[:, f:2 * f], -SWIGLU_LIMIT, SWIGLU_LIMIT)
        act = g * _sigmoid(SWIGLU_ALPHA * g) * (lin + 1.0)
        y = jnp.dot(act.astype(jnp.bfloat16), wd_ref[...], preferred_element_type=jnp.float32) + bd_ref[0]
        for j in range(nt):
            ys_ref[pl.ds(j, MOE_BLOCK, stride=nt), :] = y[:, j * LANES:(j + 1) * LANES]


def _run_experts(blk_e, blk_s, counts, poff, run, staged, w_gate_up, b_gate_up, w_down, b_down, nt):
    ne, d, f2 = w_gate_up.shape
    f = f2 // 2
    n_blocks = blk_e.shape[1]
    blk_rows = MOE_BLOCK * nt
    any_spec = pl.BlockSpec(memory_space=pl.ANY)

    def per_expert(i, be, bs, cn, po):
        return (be[0, i], 0, 0)

    grid_spec = pltpu.PrefetchScalarGridSpec(
        num_scalar_prefetch=4,
        grid=(n_blocks,),
        in_specs=[pl.BlockSpec(memory_space=pltpu.SMEM), any_spec, any_spec,
                  pl.BlockSpec((1, 1, f2), per_expert), any_spec, pl.BlockSpec((1, 1, d), per_expert)],
        out_specs=pl.BlockSpec((blk_rows, LANES), lambda i, be, bs, cn, po: (i, 0)),
        scratch_shapes=[pltpu.VMEM((2, blk_rows, LANES), jnp.float32),
                        pltpu.VMEM((2, d, f2), jnp.float32), pltpu.VMEM((2, f, d), jnp.float32),
                        pltpu.VMEM((d, f2), jnp.bfloat16), pltpu.VMEM((f, d), jnp.bfloat16),
                        pltpu.SemaphoreType.DMA((2,)), pltpu.SemaphoreType.DMA((2,)),
                        pltpu.SMEM((1,), jnp.int32), pltpu.SMEM((1,), jnp.int32)],
    )
    return pl.pallas_call(
        functools.partial(_run_expert_kernel, nt),
        grid_spec=grid_spec,
        out_shape=jax.ShapeDtypeStruct((n_blocks * blk_rows, LANES), jnp.float32),
        compiler_params=pltpu.CompilerParams(
            dimension_semantics=("arbitrary",), vmem_limit_bytes=VMEM_LIMIT),
        name="moe_experts",
    )(blk_e, blk_s, counts, poff, run, staged, w_gate_up, b_gate_up.reshape(ne, 1, f2), w_down,
      b_down.reshape(ne, 1, d))


def _combine_kernel(nt, loc0_ref, loc1_ref, loc2_ref, loc3_ref, g0_ref, g1_ref, g2_ref, g3_ref,
                    run_ref, runn_ref, x1_ref, gfin_ref, ys_ref, out_ref, stage_ref, acc_ref, sem):
    loc_refs = (loc0_ref, loc1_ref, loc2_ref, loc3_ref)
    gate_refs = (g0_ref, g1_ref, g2_ref, g3_ref)
    rows, d = x1_ref.shape
    n_assign = rows * TOP_K
    step = pl.program_id(0)
    par = step % 2
    unroll = TILE_UNROLL

    def fetch(r_ref, half):
        _run_copies(r_ref,
                    lambda first, base, n: ys_ref.at[pl.ds(pl.multiple_of(base * nt, nt), n * nt), :],
                    lambda first, base, n: stage_ref.at[half, pl.ds(pl.multiple_of(first * nt, nt), n * nt), :],
                    sem.at[half], nt)

    @pl.when(step == 0)
    def _():
        fetch(run_ref, 0)

    @pl.when(step + 1 < pl.num_programs(0))
    def _():
        fetch(runn_ref, 1 - par)

    pltpu.make_async_copy(ys_ref.at[pl.ds(0, n_assign * nt), :], stage_ref.at[par], sem.at[par]).wait()

    def mix_from(half):
        def mix(i, carry):
            for u in range(unroll):
                tok = i * unroll + u
                acc = None
                for k in range(TOP_K):
                    row = stage_ref[half, pl.ds(pl.multiple_of(loc_refs[k][0, 0, 0, tok], nt), nt), :]
                    term = gate_refs[k][0, 0, 0, tok] * row
                    acc = term if acc is None else acc + term
                acc_ref[pl.ds(pl.multiple_of(tok * nt, nt), nt), :] = acc
            return carry

        lax.fori_loop(0, rows // unroll, mix, 0)

    for half in range(2):
        pl.when(par == half)(functools.partial(mix_from, half))

    y = jnp.concatenate([acc_ref[pl.ds(j, rows, stride=nt), :] for j in range(nt)], axis=-1)
    x = x1_ref[...] + y
    ms = jnp.mean(x * x, axis=-1, keepdims=True)
    out_ref[...] = x * lax.rsqrt(ms + NORM_EPS) * gfin_ref[...]


def _combine(loc, gate, runs, x1, g_final, ys, nt):
    t, d = x1.shape
    rows = ROUTE_ROWS
    tiles = t // rows
    ne = runs.shape[2]
    n_assign = rows * TOP_K
    smem = pltpu.SMEM
    return pl.pallas_call(
        functools.partial(_combine_kernel, nt),
        grid=(tiles,),
        in_specs=[*_slot_specs(rows), *_slot_specs(rows),
                  pl.BlockSpec((1, SUBLANES, ne), lambda i: (i, 0, 0), memory_space=smem),
                  pl.BlockSpec((1, SUBLANES, ne), lambda i: (jnp.minimum(i + 1, tiles - 1), 0, 0),
                               memory_space=smem),
                  pl.BlockSpec((rows, d), lambda i: (i, 0)),
                  _const_spec(g_final.shape),
                  pl.BlockSpec(memory_space=pl.ANY)],
        out_specs=pl.BlockSpec((rows, d), lambda i: (i, 0)),
        out_shape=jax.ShapeDtypeStruct((t, d), jnp.float32),
        scratch_shapes=[pltpu.VMEM((2, n_assign * nt, LANES), jnp.float32),
                        pltpu.VMEM((rows * nt, LANES), jnp.float32),
                        pltpu.SemaphoreType.DMA((2,))],
        compiler_params=pltpu.CompilerParams(
            dimension_semantics=("arbitrary",), vmem_limit_bytes=VMEM_LIMIT),
        name="moe_combine",
    )(loc, loc, loc, loc, gate, gate, gate, gate, runs, runs, x1, g_final, ys)


def _block_diag(blocks):
    g, r, c = blocks.shape
    eye = jnp.eye(g, dtype=blocks.dtype)
    return (blocks[:, :, None, :] * eye[:, None, :, None]).reshape(g * r, g * c)


def kernel(x, norm_mix_g, w_in, conv_w, conv_b, conv_ln_g, conv_ln_b, w_conv_out, ssm_a_re, ssm_a_im,
           ssm_log_step, ssm_b_re, ssm_b_im, ssm_c_re, ssm_c_im, ssm_d, w_ssm_glu, b_ssm_glu, w_ssm_out,
           w_out, norm_ffn_g, w_router, b_router, w_gate_up, b_gate_up, w_down, b_down, norm_final_g):
    bsz, seq, d = x.shape
    depth = w_in.shape[0]
    bf = jnp.bfloat16
    nt = d // LANES
    n_tok = bsz * seq
    ne = w_router.shape[-1]
    n_blocks = -(-n_tok * TOP_K // MOE_BLOCK) + ne
    seg_len = MIX_ROWS // SCAN_SEGS

    def row(v):
        return v.reshape(1, -1)

    for l in range(depth):
        ar, ai, arp, aip, bbr, bbi = _ssm_disc(
            ssm_a_re[l], ssm_a_im[l], ssm_log_step[l],
            jnp.swapaxes(ssm_b_re[l], 1, 2), jnp.swapaxes(ssm_b_im[l], 1, 2), seg_len)
        bblk = jnp.concatenate([_block_diag(bbr), _block_diag(bbi)], axis=1).astype(bf)
        c_r = _block_diag(jnp.swapaxes(ssm_c_re[l], 1, 2)).astype(bf)
        c_i = _block_diag(jnp.swapaxes(ssm_c_im[l], 1, 2)).astype(bf)
        avec = jnp.stack([ar.reshape(-1), ai.reshape(-1), arp.reshape(-1), aip.reshape(-1)])

        x = _mixer(x, row(norm_mix_g[l]), w_in[l].astype(bf), conv_w[l], row(conv_b[l]),
                   row(conv_ln_g[l]), row(conv_ln_b[l]), w_conv_out[l].astype(bf), bblk, c_r, c_i, avec,
                   row(ssm_d[l]), w_ssm_glu[l].astype(bf), row(b_ssm_glu[l]), w_ssm_out[l].astype(bf),
                   w_out[l].astype(bf))

        x1 = x.reshape(n_tok, d)
        staged, loc, gate, run, counts = _route(x1, row(norm_ffn_g[l]), w_router[l], row(b_router[l]))
        poff, blk_e, blk_s = _meta(counts, n_blocks)
        runs = _runs(poff, run)
        ys = _run_experts(blk_e, blk_s, counts, poff, run, staged, w_gate_up[l], b_gate_up[l], w_down[l],
                          b_down[l], nt)
        assert depth == 1
        x = _combine(loc, gate, runs, x1, row(norm_final_g), ys, nt).reshape(bsz, seq, d)
    return x
```

```python
import functools
import math

import jax
import jax.numpy as jnp
from jax import lax
from jax.experimental import pallas as pl
from jax.experimental.pallas import tpu as pltpu

NORM_EPS = 1e-6
SWIGLU_LIMIT = 7.0
SWIGLU_ALPHA = 1.702
TOP_K = 4

LANES = 128
SUBLANES = 8

MIX_ROWS = 512
SCAN_SEGS = SUBLANES
SEG_PAD = 8
SCAN_UNROLL = 8
TILE_UNROLL = 8
CONV_HALO = 32
CONV_ROWS = 64
ROUTE_ROWS = 512
MOE_BLOCK = 512
V7X_VMEM_BYTES = 64 * 1024 * 1024
VMEM_LIMIT = V7X_VMEM_BYTES - 8 * 1024 * 1024


def _sigmoid(x):
    return jax.nn.sigmoid(x)


def _gelu_tanh(x):
    c = math.sqrt(2.0 / math.pi)
    return 0.5 * x * (1.0 + jnp.tanh(c * (x + 0.044715 * (x * x * x))))


def _const_spec(shape):
    nd = len(shape)
    return pl.BlockSpec(shape, lambda *_: (0,) * nd)


def _ssm_disc_kernel(seg_len, are_ref, aim_ref, ls_ref, bre_ref, bim_ref,
                     ar_ref, ai_ref, arp_ref, aip_ref, bbr_ref, bbi_ref):
    lr = are_ref[...]
    li = aim_ref[...]
    dt = jnp.exp(ls_ref[...])
    mag = jnp.exp(lr * dt)
    ar = mag * jnp.cos(li * dt)
    ai = mag * jnp.sin(li * dt)
    den = lr * lr + li * li
    fr = ((ar - 1.0) * lr + ai * li) / den
    fi = (ai * lr - (ar - 1.0) * li) / den
    ar_ref[...] = ar
    ai_ref[...] = ai
    magp = jnp.exp(lr * dt * seg_len)
    arp_ref[...] = magp * jnp.cos(li * dt * seg_len)
    aip_ref[...] = magp * jnp.sin(li * dt * seg_len)
    br = bre_ref[...]
    bi = bim_ref[...]
    bbr_ref[...] = fr[:, None, :] * br - fi[:, None, :] * bi
    bbi_ref[...] = fr[:, None, :] * bi + fi[:, None, :] * br


def _ssm_disc(a_re, a_im, log_step, bt_re, bt_im, seg_len):
    g, p = a_re.shape
    h = bt_re.shape[1]
    f32 = jnp.float32
    out_shape = [jax.ShapeDtypeStruct((g, p), f32)] * 4 + [jax.ShapeDtypeStruct((g, h, p), f32)] * 2
    return pl.pallas_call(
        functools.partial(_ssm_disc_kernel, float(seg_len)),
        out_shape=out_shape,
        name="ssm_disc",
    )(a_re, a_im, log_step.reshape(g, 1), bt_re, bt_im)


def _mixer_kernel(dims, x_ref, gmix_ref, win_ref, convw_ref, convb_ref, lng_ref, lnb_ref, wco_ref,
                  bblk_ref, cr_ref, ci_ref, avec_ref, dskip_ref, wglu_ref, bglu_ref, wso_ref, wout_ref,
                  x1_ref, vbuf_ref, conv_ref, scan_ref, xst_ref, cin_ref, st_ref):
    d, dc, ds, gp, cw = dims
    rows = MIX_ROWS
    seg = rows // SCAN_SEGS
    pitch = seg + SEG_PAD
    nsl = gp // LANES
    c_idx = pl.program_id(1)

    half = rows // 2
    tail = 2 * half + 1

    @pl.when(c_idx == 0)
    def _():
        for lc in range(dc // LANES):
            vbuf_ref[lc, pl.ds(0, CONV_HALO, stride=2), :] = jnp.zeros((CONV_HALO, LANES), jnp.float32)
        st_ref[...] = jnp.zeros_like(st_ref)

    @pl.when(c_idx != 0)
    def _():
        for lc in range(dc // LANES):
            vbuf_ref[lc, pl.ds(0, CONV_HALO, stride=2), :] = vbuf_ref[lc, pl.ds(tail, CONV_HALO, stride=2), :]

    x = x_ref[0]
    ms = jnp.mean(x * x, axis=-1, keepdims=True)
    h = (x * lax.rsqrt(ms + NORM_EPS) * gmix_ref[...]).astype(jnp.bfloat16)

    pa = jnp.dot(h, win_ref[:, 0:2 * dc], preferred_element_type=jnp.float32)
    vglu = pa[:, 0:dc] * _sigmoid(pa[:, dc:2 * dc])
    for lc in range(dc // LANES):
        ls = slice(lc * LANES, (lc + 1) * LANES)
        vbuf_ref[lc, pl.ds(2 * CONV_HALO, half, stride=2), :] = vglu[0:half, ls]
        vbuf_ref[lc, pl.ds(2 * CONV_HALO + 1, half, stride=2), :] = vglu[half:rows, ls]
        vbuf_ref[lc, pl.ds(1, CONV_HALO, stride=2), :] = vglu[half - CONV_HALO:half, ls]

    base = CONV_HALO - (cw - 1)
    nv = CONV_ROWS // SUBLANES

    def conv_chunk(rc, p):
        r0 = rc * CONV_ROWS
        for lc in range(dc // LANES):
            ls = slice(lc * LANES, (lc + 1) * LANES)
            wv = [jnp.broadcast_to(convw_ref[k:k + 1, ls], (SUBLANES, LANES)) for k in range(cw)]
            acc = [jnp.zeros((SUBLANES, LANES), jnp.float32) for _ in range(nv)]
            for s in range(CONV_ROWS - SUBLANES + cw):
                win = vbuf_ref[lc, pl.ds(2 * (r0 + base + s) + p, SUBLANES, stride=2), :]
                for i in range(nv):
                    k = s - SUBLANES * i
                    if 0 <= k < cw:
                        acc[i] = acc[i] + wv[k] * win
            for i in range(nv):
                conv_ref[pl.ds(p * half + r0 + SUBLANES * i, SUBLANES), ls] = acc[i]

    g0 = 2 * dc + ds
    chunks = [(rc, p) for rc in range(half // CONV_ROWS) for p in range(2)]
    gw = 2 * d // len(chunks)
    sg = []
    for n, (rc, p) in enumerate(chunks):
        sg.append(_sigmoid(jnp.dot(h, win_ref[:, g0 + n * gw:g0 + (n + 1) * gw],
                                   preferred_element_type=jnp.float32)))
        conv_chunk(rc, p)
    sg = jnp.concatenate(sg, axis=-1)
    sg_conv = sg[:, 0:d]
    sg_ssm = sg[:, d:2 * d]

    v = conv_ref[...] + convb_ref[...]
    mu = jnp.mean(v, axis=-1, keepdims=True)
    vc = v - mu
    var = jnp.mean(vc * vc, axis=-1, keepdims=True)
    v = vc * lax.rsqrt(var + NORM_EPS) * lng_ref[...] + lnb_ref[...]
    v = v * _sigmoid(v)
    y_conv = jnp.dot(v.astype(jnp.bfloat16), wco_ref[...], preferred_element_type=jnp.float32)

    u = jnp.dot(h, win_ref[:, 2 * dc:2 * dc + ds], preferred_element_type=jnp.float32)
    bu = jnp.dot(u.astype(jnp.bfloat16), bblk_ref[...], preferred_element_type=jnp.float32)
    for n in range(2 * nsl):
        for s in range(SCAN_SEGS):
            scan_ref[n, s * pitch:s * pitch + seg, :] = bu[s * seg:(s + 1) * seg, n * LANES:(n + 1) * LANES]

    def bcast(row, n):
        return jnp.broadcast_to(avec_ref[row:row + 1, n * LANES:(n + 1) * LANES], (SUBLANES, LANES))

    a_r = [bcast(0, n) for n in range(nsl)]
    a_i = [bcast(1, n) for n in range(nsl)]

    def step(j, st, store):
        out = []
        for n in range(nsl):
            sr, si = st[2 * n], st[2 * n + 1]
            br = scan_ref[n, pl.ds(j, SCAN_SEGS, stride=pitch), :]
            bi = scan_ref[nsl + n, pl.ds(j, SCAN_SEGS, stride=pitch), :]
            nr = a_r[n] * sr - a_i[n] * si + br
            ni = a_r[n] * si + a_i[n] * sr + bi
            if store:
                scan_ref[n, pl.ds(j, SCAN_SEGS, stride=pitch), :] = nr
                scan_ref[nsl + n, pl.ds(j, SCAN_SEGS, stride=pitch), :] = ni
            out += [nr, ni]
        return tuple(out)

    def steps(jj, st, store):
        for q in range(SCAN_UNROLL):
            st = step(jj * SCAN_UNROLL + q, st, store)
        return st

    zero = jnp.zeros((SUBLANES, LANES), jnp.float32)
    fin = lax.fori_loop(0, seg // SCAN_UNROLL, functools.partial(steps, store=False), (zero,) * (2 * nsl))

    for n in range(nsl):
        fr, fi = fin[2 * n], fin[2 * n + 1]
        ap_r = avec_ref[2:3, n * LANES:(n + 1) * LANES]
        ap_i = avec_ref[3:4, n * LANES:(n + 1) * LANES]
        c_r = st_ref[0:1, n * LANES:(n + 1) * LANES]
        c_i = st_ref[1:2, n * LANES:(n + 1) * LANES]
        for s in range(SCAN_SEGS):
            cin_ref[2 * n, s:s + 1, :] = c_r
            cin_ref[2 * n + 1, s:s + 1, :] = c_i
            n_r = ap_r * c_r - ap_i * c_i + fr[s:s + 1, :]
            n_i = ap_r * c_i + ap_i * c_r + fi[s:s + 1, :]
            c_r, c_i = n_r, n_i
        st_ref[0:1, n * LANES:(n + 1) * LANES] = c_r
        st_ref[1:2, n * LANES:(n + 1) * LANES] = c_i

    lax.fori_loop(0, seg // SCAN_UNROLL, functools.partial(steps, store=True),
                  tuple(cin_ref[q] for q in range(2 * nsl)))

    for n in range(2 * nsl):
        for s in range(SCAN_SEGS):
            xst_ref[s * seg:(s + 1) * seg, n * LANES:(n + 1) * LANES] = (
                scan_ref[n, s * pitch:s * pitch + seg, :].astype(jnp.bfloat16))

    y = (jnp.dot(xst_ref[:, 0:gp], cr_ref[...], preferred_element_type=jnp.float32)
         - jnp.dot(xst_ref[:, gp:2 * gp], ci_ref[...], preferred_element_type=jnp.float32))
    y = _gelu_tanh(y + dskip_ref[...] * u)
    glu = jnp.dot(y.astype(jnp.bfloat16), wglu_ref[...], preferred_element_type=jnp.float32) + bglu_ref[...]
    y = y * _sigmoid(glu)
    y_ssm = jnp.dot(y.astype(jnp.bfloat16), wso_ref[...], preferred_element_type=jnp.float32)

    m = sg_conv * y_conv + sg_ssm * y_ssm
    x1_ref[0] = x + jnp.dot(m.astype(jnp.bfloat16), wout_ref[...], preferred_element_type=jnp.float32)


def _mixer(x, gmix, w_in, conv_w, conv_b, ln_g, ln_b, w_co, bblk, c_r, c_i, avec, d_skip,
           w_glu, b_glu, w_so, w_out):
    b, s, d = x.shape
    cw, dc = conv_w.shape
    ds = d_skip.shape[-1]
    gp = c_r.shape[0]
    rows = MIX_ROWS
    seg = rows // SCAN_SEGS
    pitch = seg + SEG_PAD
    assert s % rows == 0 and cw - 1 <= CONV_HALO and gp % LANES == 0 and dc % LANES == 0
    dims = (d, dc, ds, gp, cw)
    consts = [gmix, w_in, conv_w, conv_b, ln_g, ln_b, w_co, bblk, c_r, c_i, avec, d_skip,
              w_glu, b_glu, w_so, w_out]
    return pl.pallas_call(
        functools.partial(_mixer_kernel, dims),
        grid=(b, s // rows),
        in_specs=[pl.BlockSpec((1, rows, d), lambda i, j: (i, j, 0))] + [_const_spec(c.shape) for c in consts],
        out_specs=pl.BlockSpec((1, rows, d), lambda i, j: (i, j, 0)),
        out_shape=jax.ShapeDtypeStruct((b, s, d), jnp.float32),
        scratch_shapes=[
            pltpu.VMEM((dc // LANES, 2 * CONV_HALO + rows, LANES), jnp.float32),
            pltpu.VMEM((rows, dc), jnp.float32),
            pltpu.VMEM((2 * gp // LANES, SCAN_SEGS * pitch, LANES), jnp.float32),
            pltpu.VMEM((rows, 2 * gp), jnp.bfloat16),
            pltpu.VMEM((2 * gp // LANES, SUBLANES, LANES), jnp.float32),
            pltpu.VMEM((2, gp), jnp.float32),
        ],
        compiler_params=pltpu.CompilerParams(
            dimension_semantics=("arbitrary", "arbitrary"), vmem_limit_bytes=VMEM_LIMIT),
        name="mixer",
    )(x, *consts)


def _route_kernel(x1_ref, g_ref, wr_ref, br_ref, st_ref, loc_ref, gate_ref, run_ref, cnt_ref, carry_ref,
                  hbuf_ref, locv_ref, lsem, *locs_refs):
    rows, d = x1_ref.shape
    ne = wr_ref.shape[1]
    step = pl.program_id(0)
    tiles = pl.num_programs(0) - 1
    par = step % 2
    nt = d // LANES

    @pl.when(step == 0)
    def _():
        carry_ref[...] = jnp.zeros_like(carry_ref)

    def stage_from(half):
        for k in range(TOP_K):
            pltpu.make_async_copy(locv_ref.at[half, k], locs_refs[half * TOP_K + k], lsem.at[half]).wait()

        def place(i, carry):
            for u in range(TILE_UNROLL):
                tok = i * TILE_UNROLL + u
                row = hbuf_ref[half, pl.ds(pl.multiple_of(tok * nt, nt), nt), :]
                for k in range(TOP_K):
                    st_ref[pl.ds(pl.multiple_of(locs_refs[half * TOP_K + k][0, tok], nt), nt), :] = row
            return carry

        lax.fori_loop(0, rows // TILE_UNROLL, place, 0)

    for half in range(2):
        pl.when(jnp.logical_and(step >= 1, 1 - par == half))(functools.partial(stage_from, half))

    x = x1_ref[...]
    ms = jnp.mean(x * x, axis=-1, keepdims=True)
    h2 = x * lax.rsqrt(ms + NORM_EPS) * g_ref[...]

    for j in range(nt):
        hbuf_ref[par, pl.ds(j, rows, stride=nt), :] = h2[:, j * LANES:(j + 1) * LANES]

    bf = jnp.bfloat16
    h_hi = h2.astype(bf)
    h_lo = (h2 - h_hi.astype(jnp.float32)).astype(bf)
    w = wr_ref[...]
    w_hi = w.astype(bf)
    w_lo = (w - w_hi.astype(jnp.float32)).astype(bf)
    logits = (jnp.dot(h_hi, w_hi, preferred_element_type=jnp.float32)
              + jnp.dot(h_lo, w_hi, preferred_element_type=jnp.float32)
              + jnp.dot(h_hi, w_lo, preferred_element_type=jnp.float32)) + br_ref[...]

    pad = jnp.full((rows, LANES - ne), -jnp.inf, jnp.float32)
    work = jnp.concatenate([logits, pad], axis=1).T[0:ne, :]
    eidx = lax.broadcasted_iota(jnp.int32, (ne, rows), 0).astype(jnp.float32)
    sels, vals = [], []
    for k in range(TOP_K):
        mx = jnp.max(work, axis=0, keepdims=True)
        idx = jnp.min(jnp.where(work == mx, eidx, float(ne)), axis=0, keepdims=True)
        sel = eidx == idx
        sels.append(sel)
        vals.append(mx)
        work = jnp.where(sel, -jnp.inf, work)

    exps = [jnp.exp(v - vals[0]) for v in vals]
    tot = exps[0]
    for e in exps[1:]:
        tot = tot + e
    for k in range(TOP_K):
        gate_ref[0, k] = exps[k] / tot

    cnt = jnp.zeros((ne, rows), jnp.float32)
    for sel in sels:
        cnt = cnt + sel.astype(jnp.float32)
    r_i = lax.broadcasted_iota(jnp.int32, (rows, rows), 0)
    c_i = lax.broadcasted_iota(jnp.int32, (rows, rows), 1)
    triu = (r_i < c_i).astype(bf)
    before = jnp.dot(cnt.astype(bf), triu, preferred_element_type=jnp.float32)
    cnt_col = jnp.sum(cnt, axis=1, keepdims=True)
    e_r = lax.broadcasted_iota(jnp.int32, (ne, ne), 0)
    e_c = lax.broadcasted_iota(jnp.int32, (ne, ne), 1)
    col_b = jnp.broadcast_to(cnt_col, (ne, ne))
    cnt_row = jnp.sum(jnp.where(e_r == e_c, col_b, 0.0), axis=0, keepdims=True)
    first_row = jnp.sum(jnp.where(e_r < e_c, col_b, 0.0), axis=0, keepdims=True)
    first_col = jnp.sum(jnp.where(e_c < e_r, jnp.broadcast_to(cnt_row, (ne, ne)), 0.0), axis=1, keepdims=True)
    pos = before + first_col
    for k in range(TOP_K):
        lk = jnp.sum(jnp.where(sels[k], pos, 0.0), axis=0, keepdims=True).astype(jnp.int32)
        loc_ref[0, k] = lk * nt
        locv_ref[par, k] = lk * nt

    def hand_over(half):
        for k in range(TOP_K):
            pltpu.make_async_copy(locv_ref.at[half, k], locs_refs[half * TOP_K + k], lsem.at[half]).start()

    for half in range(2):
        pl.when(jnp.logical_and(step < tiles, par == half))(functools.partial(hand_over, half))

    @pl.when(step < tiles)
    def _():
        run_ref[...] = jnp.zeros_like(run_ref)
        run_ref[0, 0:1, :] = cnt_row.astype(jnp.int32)
        run_ref[0, 1:2, :] = first_row.astype(jnp.int32)
        run_ref[0, 2:3, :] = carry_ref[...].astype(jnp.int32)
        carry_ref[...] = carry_ref[...] + cnt_row
        cnt_ref[...] = carry_ref[...].astype(jnp.int32)


def _route(x1, g_ffn, w_router, b_router):
    t, d = x1.shape
    ne = w_router.shape[1]
    rows = ROUTE_ROWS
    assert t % rows == 0 and d % LANES == 0
    assert TOP_K <= SUBLANES and ne <= LANES
    nt = d // LANES
    tiles = t // rows
    def cur(i):
        return jnp.minimum(i, tiles - 1)

    per_tile = pl.BlockSpec((1, TOP_K, 1, rows), lambda i: (cur(i), 0, 0, 0))
    return pl.pallas_call(
        _route_kernel,
        grid=(tiles + 1,),
        in_specs=[pl.BlockSpec((rows, d), lambda i: (cur(i), 0)),
                  _const_spec(g_ffn.shape), _const_spec(w_router.shape), _const_spec(b_router.shape)],
        out_specs=[pl.BlockSpec((rows * TOP_K * nt, LANES), lambda i: (jnp.maximum(i - 1, 0), 0)),
                   per_tile, per_tile,
                   pl.BlockSpec((1, SUBLANES, ne), lambda i: (cur(i), 0, 0)),
                   _const_spec((1, ne))],
        out_shape=[jax.ShapeDtypeStruct((t * TOP_K * nt, LANES), jnp.float32),
                   jax.ShapeDtypeStruct((tiles, TOP_K, 1, rows), jnp.int32),
                   jax.ShapeDtypeStruct((tiles, TOP_K, 1, rows), jnp.float32),
                   jax.ShapeDtypeStruct((tiles, SUBLANES, ne), jnp.int32),
                   jax.ShapeDtypeStruct((1, ne), jnp.int32)],
        scratch_shapes=[pltpu.VMEM((1, ne), jnp.float32),
                        pltpu.VMEM((2, rows * nt, LANES), jnp.float32),
                        pltpu.VMEM((2, TOP_K, 1, rows), jnp.int32),
                        pltpu.SemaphoreType.DMA((2,))]
                       + [pltpu.SMEM((1, rows), jnp.int32)] * (2 * TOP_K),
        compiler_params=pltpu.CompilerParams(
            dimension_semantics=("arbitrary",), vmem_limit_bytes=VMEM_LIMIT),
        name="route",
    )(x1, g_ffn, w_router, b_router)


def _meta_kernel(cnt_ref, poff_ref, blke_ref, blks_ref):
    ne = cnt_ref.shape[1]
    nb = blke_ref.shape[1]
    f32 = jnp.float32
    cnt = cnt_ref[...].astype(f32)
    nblk = jnp.floor((cnt + (MOE_BLOCK - 1)) * (1.0 / MOE_BLOCK))
    e_r = lax.broadcasted_iota(jnp.int32, (ne, ne), 0)
    e_c = lax.broadcasted_iota(jnp.int32, (ne, ne), 1)
    nb_rows = jnp.broadcast_to(nblk, (ne, ne))
    bend_col = jnp.sum(jnp.where(e_c <= e_r, nb_rows, 0.0), axis=1, keepdims=True)
    nb_col = jnp.sum(jnp.where(e_c == e_r, nb_rows, 0.0), axis=1, keepdims=True)
    boff = jnp.sum(jnp.where(e_r < e_c, jnp.broadcast_to(nb_col, (ne, ne)), 0.0),
                   axis=0, keepdims=True)
    poff_ref[...] = (boff * MOE_BLOCK).astype(jnp.int32)
    total = jnp.sum(nblk, axis=1, keepdims=True)
    blk = lax.broadcasted_iota(jnp.int32, (ne, nb), 1).astype(f32)
    done = jnp.where(jnp.broadcast_to(bend_col, (ne, nb)) <= blk, 1.0, 0.0)
    be = jnp.minimum(jnp.sum(done, axis=0, keepdims=True), ne - 1.0)
    bid = lax.broadcasted_iota(jnp.int32, (1, nb), 1).astype(f32)
    last_e = jnp.sum(jnp.where(bid == total - 1.0, be, 0.0), axis=1, keepdims=True)
    blke_ref[...] = jnp.where(bid < total, be, last_e).astype(jnp.int32)
    blks_ref[...] = jnp.minimum(bid, total - 1.0).astype(jnp.int32)


def _meta(counts, n_blocks):
    ne = counts.shape[1]
    return pl.pallas_call(
        _meta_kernel,
        out_shape=[jax.ShapeDtypeStruct((1, ne), jnp.int32),
                   jax.ShapeDtypeStruct((1, n_blocks), jnp.int32),
                   jax.ShapeDtypeStruct((1, n_blocks), jnp.int32)],
        name="moe_meta",
    )(counts)


def _runs_kernel(poff_ref, run_ref, out_ref):
    run = run_ref[...]
    row = lax.broadcasted_iota(jnp.int32, run.shape, 1)
    out_ref[...] = jnp.where(row == 2, run + poff_ref[...][None], run)


def _runs(poff, run):
    return pl.pallas_call(
        _runs_kernel,
        out_shape=jax.ShapeDtypeStruct(run.shape, jnp.int32),
        name="moe_runs",
    )(poff, run)


def _slot_specs(rows):
    assert TOP_K == 4
    return [pl.BlockSpec((1, 1, 1, rows), functools.partial(lambda k, i: (i, k, 0, 0), k), memory_space=pltpu.SMEM)
            for k in range(TOP_K)]


def _run_copies(run_ref, src_of, dst_of, sem, nt):
    ne = run_ref.shape[2]

    def per_expert(e, carry):
        n = run_ref[0, 0, e]

        @pl.when(n > 0)
        def _():
            pltpu.make_async_copy(src_of(run_ref[0, 1, e], run_ref[0, 2, e], n),
                                  dst_of(run_ref[0, 1, e], run_ref[0, 2, e], n), sem).start()
        return carry

    lax.fori_loop(0, ne, per_expert, 0)


def _run_expert_kernel(nt, blke_ref, blks_ref, cnt_ref, poff_ref, run_ref, st_hbm, wgu_hbm, bgu_ref, wd_hbm,
                       bd_ref, ys_ref, xbuf_ref, wgu32_ref, wd32_ref, wgu_ref, wd_ref, gsem, wsem, runidx_ref,
                       tptr_ref):
    i = pl.program_id(0)
    last = pl.num_programs(0) - 1
    f = wd_ref.shape[0]
    e = blke_ref[0, i]
    valid = blks_ref[0, i] == i
    p = i % 2
    tiles = run_ref.shape[0]
    tile_rows = st_hbm.shape[0] // nt // tiles

    def block_rows(b):
        eb = blke_ref[0, b]
        r0 = b * MOE_BLOCK - poff_ref[0, eb]
        return eb, r0, jnp.minimum(MOE_BLOCK, cnt_ref[0, eb] - r0)

    def gather(b, half):
        eb, r0, _ = block_rows(b)
        fresh = jnp.logical_or(b == 0, eb != blke_ref[0, jnp.maximum(b - 1, 0)])
        t0 = jnp.where(fresh, 0, tptr_ref[0])

        def more(t):
            return jnp.logical_and(t < tiles, run_ref[jnp.minimum(t, tiles - 1), 2, eb] < r0 + MOE_BLOCK)

        def one_tile(t):
            c0 = run_ref[t, 2, eb]
            lo = jnp.maximum(c0, r0)
            hi = jnp.minimum(c0 + run_ref[t, 0, eb], r0 + MOE_BLOCK)

            @pl.when(hi > lo)
            def _():
                src = t * tile_rows + run_ref[t, 1, eb] + (lo - c0)
                pltpu.make_async_copy(
                    st_hbm.at[pl.ds(pl.multiple_of(src * nt, nt), (hi - lo) * nt), :],
                    xbuf_ref.at[half, pl.ds(pl.multiple_of((lo - r0) * nt, nt), (hi - lo) * nt), :],
                    gsem.at[half]).start()
            return t + 1

        t_end = lax.while_loop(more, one_tile, t0)
        tptr_ref[0] = jnp.maximum(t_end - 1, 0)

    def wait_gather(b, half):
        _, _, n = block_rows(b)
        pltpu.make_async_copy(st_hbm.at[pl.ds(0, n * nt), :], xbuf_ref.at[half, pl.ds(0, n * nt), :],
                              gsem.at[half]).wait()

    def fetch(expert, slot):
        pltpu.make_async_copy(wgu_hbm.at[expert], wgu32_ref.at[slot], wsem.at[slot]).start()
        pltpu.make_async_copy(wd_hbm.at[expert], wd32_ref.at[slot], wsem.at[slot]).start()

    def wait_fetch(slot):
        pltpu.make_async_copy(wgu_hbm.at[0], wgu32_ref.at[slot], wsem.at[slot]).wait()
        pltpu.make_async_copy(wd_hbm.at[0], wd32_ref.at[slot], wsem.at[slot]).wait()

    @pl.when(i == 0)
    def _():
        xbuf_ref[...] = jnp.zeros_like(xbuf_ref)
        runidx_ref[0] = 0
        tptr_ref[0] = 0
        fetch(e, 0)
        gather(0, 0)

    @pl.when(jnp.logical_and(valid, jnp.logical_or(i == 0, e != blke_ref[0, jnp.maximum(i - 1, 0)])))
    def _():
        slot = runidx_ref[0] % 2
        wait_fetch(slot)
        nxt = i + (cnt_ref[0, e] + (MOE_BLOCK - 1)) // MOE_BLOCK
        nxt_c = jnp.minimum(nxt, last)

        @pl.when(blks_ref[0, nxt_c] == nxt)
        def _():
            fetch(blke_ref[0, nxt_c], 1 - slot)

        wgu_ref[...] = wgu32_ref[slot].astype(jnp.bfloat16)
        wd_ref[...] = wd32_ref[slot].astype(jnp.bfloat16)
        runidx_ref[0] = runidx_ref[0] + 1

    @pl.when(jnp.logical_not(valid))
    def _():
        ys_ref[...] = jnp.zeros_like(ys_ref)

    @pl.when(valid)
    def _():
        nb = jnp.minimum(i + 1, last)

        @pl.when(blks_ref[0, nb] == i + 1)
        def _():
            gather(i + 1, 1 - p)

        wait_gather(i, p)
        x = jnp.concatenate([xbuf_ref[p, pl.ds(j, MOE_BLOCK, stride=nt), :] for j in range(nt)], axis=-1)
        gu = jnp.dot(x.astype(jnp.bfloat16), wgu_ref[...], preferred_element_type=jnp.float32) + bgu_ref[0]
        g = jnp.minimum(gu[:, 0:f], SWIGLU_LIMIT)
        lin = jnp.clip(gu[:, f:2 * f], -SWIGLU_LIMIT, SWIGLU_LIMIT)
        act = g * _sigmoid(SWIGLU_ALPHA * g) * (lin + 1.0)
        y = jnp.dot(act.astype(jnp.bfloat16), wd_ref[...], preferred_element_type=jnp.float32) + bd_ref[0]
        for j in range(nt):
            ys_ref[pl.ds(j, MOE_BLOCK, stride=nt), :] = y[:, j * LANES:(j + 1) * LANES]


def _run_experts(blk_e, blk_s, counts, poff, run, staged, w_gate_up, b_gate_up, w_down, b_down, nt):
    ne, d, f2 = w_gate_up.shape
    f = f2 // 2
    n_blocks = blk_e.shape[1]
    blk_rows = MOE_BLOCK * nt
    any_spec = pl.BlockSpec(memory_space=pl.ANY)

    def per_expert(i, be, bs, cn, po):
        return (be[0, i], 0, 0)

    grid_spec = pltpu.PrefetchScalarGridSpec(
        num_scalar_prefetch=4,
        grid=(n_blocks,),
        in_specs=[pl.BlockSpec(memory_space=pltpu.SMEM), any_spec, any_spec,
                  pl.BlockSpec((1, 1, f2), per_expert), any_spec, pl.BlockSpec((1, 1, d), per_expert)],
        out_specs=pl.BlockSpec((blk_rows, LANES), lambda i, be, bs, cn, po: (i, 0)),
        scratch_shapes=[pltpu.VMEM((2, blk_rows, LANES), jnp.float32),
                        pltpu.VMEM((2, d, f2), jnp.float32), pltpu.VMEM((2, f, d), jnp.float32),
                        pltpu.VMEM((d, f2), jnp.bfloat16), pltpu.VMEM((f, d), jnp.bfloat16),
                        pltpu.SemaphoreType.DMA((2,)), pltpu.SemaphoreType.DMA((2,)),
                        pltpu.SMEM((1,), jnp.int32), pltpu.SMEM((1,), jnp.int32)],
    )
    return pl.pallas_call(
        functools.partial(_run_expert_kernel, nt),
        grid_spec=grid_spec,
        out_shape=jax.ShapeDtypeStruct((n_blocks * blk_rows, LANES), jnp.float32),
        compiler_params=pltpu.CompilerParams(
            dimension_semantics=("arbitrary",), vmem_limit_bytes=VMEM_LIMIT),
        name="moe_experts",
    )(blk_e, blk_s, counts, poff, run, staged, w_gate_up, b_gate_up.reshape(ne, 1, f2), w_down,
      b_down.reshape(ne, 1, d))


def _combine_kernel(nt, loc0_ref, loc1_ref, loc2_ref, loc3_ref, g0_ref, g1_ref, g2_ref, g3_ref,
                    run_ref, runn_ref, x1_ref, gfin_ref, ys_ref, out_ref, stage_ref, acc_ref, sem):
    loc_refs = (loc0_ref, loc1_ref, loc2_ref, loc3_ref)
    gate_refs = (g0_ref, g1_ref, g2_ref, g3_ref)
    rows, d = x1_ref.shape
    n_assign = rows * TOP_K
    step = pl.program_id(0)
    par = step % 2
    unroll = TILE_UNROLL

    def fetch(r_ref, half):
        _run_copies(r_ref,
                    lambda first, base, n: ys_ref.at[pl.ds(pl.multiple_of(base * nt, nt), n * nt), :],
                    lambda first, base, n: stage_ref.at[half, pl.ds(pl.multiple_of(first * nt, nt), n * nt), :],
                    sem.at[half], nt)

    @pl.when(step == 0)
    def _():
        fetch(run_ref, 0)

    @pl.when(step + 1 < pl.num_programs(0))
    def _():
        fetch(runn_ref, 1 - par)

    pltpu.make_async_copy(ys_ref.at[pl.ds(0, n_assign * nt), :], stage_ref.at[par], sem.at[par]).wait()

    def mix_from(half):
        def mix(i, carry):
            for u in range(unroll):
                tok = i * unroll + u
                acc = None
                for k in range(TOP_K):
                    row = stage_ref[half, pl.ds(pl.multiple_of(loc_refs[k][0, 0, 0, tok], nt), nt), :]
                    term = gate_refs[k][0, 0, 0, tok] * row
                    acc = term if acc is None else acc + term
                acc_ref[pl.ds(pl.multiple_of(tok * nt, nt), nt), :] = acc
            return carry

        lax.fori_loop(0, rows // unroll, mix, 0)

    for half in range(2):
        pl.when(par == half)(functools.partial(mix_from, half))

    y = jnp.concatenate([acc_ref[pl.ds(j, rows, stride=nt), :] for j in range(nt)], axis=-1)
    x = x1_ref[...] + y
    ms = jnp.mean(x * x, axis=-1, keepdims=True)
    out_ref[...] = x * lax.rsqrt(ms + NORM_EPS) * gfin_ref[...]


def _combine(loc, gate, runs, x1, g_final, ys, nt):
    t, d = x1.shape
    rows = ROUTE_ROWS
    tiles = t // rows
    ne = runs.shape[2]
    n_assign = rows * TOP_K
    smem = pltpu.SMEM
    return pl.pallas_call(
        functools.partial(_combine_kernel, nt),
        grid=(tiles,),
        in_specs=[*_slot_specs(rows), *_slot_specs(rows),
                  pl.BlockSpec((1, SUBLANES, ne), lambda i: (i, 0, 0), memory_space=smem),
                  pl.BlockSpec((1, SUBLANES, ne), lambda i: (jnp.minimum(i + 1, tiles - 1), 0, 0),
                               memory_space=smem),
                  pl.BlockSpec((rows, d), lambda i: (i, 0)),
                  _const_spec(g_final.shape),
                  pl.BlockSpec(memory_space=pl.ANY)],
        out_specs=pl.BlockSpec((rows, d), lambda i: (i, 0)),
        out_shape=jax.ShapeDtypeStruct((t, d), jnp.float32),
        scratch_shapes=[pltpu.VMEM((2, n_assign * nt, LANES), jnp.float32),
                        pltpu.VMEM((rows * nt, LANES), jnp.float32),
                        pltpu.SemaphoreType.DMA((2,))],
        compiler_params=pltpu.CompilerParams(
            dimension_semantics=("arbitrary",), vmem_limit_bytes=VMEM_LIMIT),
        name="moe_combine",
    )(loc, loc, loc, loc, gate, gate, gate, gate, runs, runs, x1, g_final, ys)


def _block_diag(blocks):
    g, r, c = blocks.shape
    eye = jnp.eye(g, dtype=blocks.dtype)
    return (blocks[:, :, None, :] * eye[:, None, :, None]).reshape(g * r, g * c)


def kernel(x, norm_mix_g, w_in, conv_w, conv_b, conv_ln_g, conv_ln_b, w_conv_out, ssm_a_re, ssm_a_im,
           ssm_log_step, ssm_b_re, ssm_b_im, ssm_c_re, ssm_c_im, ssm_d, w_ssm_glu, b_ssm_glu, w_ssm_out,
           w_out, norm_ffn_g, w_router, b_router, w_gate_up, b_gate_up, w_down, b_down, norm_final_g):
    bsz, seq, d = x.shape
    depth = w_in.shape[0]
    bf = jnp.bfloat16
    nt = d // LANES
    n_tok = bsz * seq
    ne = w_router.shape[-1]
    n_blocks = -(-n_tok * TOP_K // MOE_BLOCK) + ne
    seg_len = MIX_ROWS // SCAN_SEGS

    def row(v):
        return v.reshape(1, -1)

    for l in range(depth):
        ar, ai, arp, aip, bbr, bbi = _ssm_disc(
            ssm_a_re[l], ssm_a_im[l], ssm_log_step[l],
            jnp.swapaxes(ssm_b_re[l], 1, 2), jnp.swapaxes(ssm_b_im[l], 1, 2), seg_len)
        bblk = jnp.concatenate([_block_diag(bbr), _block_diag(bbi)], axis=1).astype(bf)
        c_r = _block_diag(jnp.swapaxes(ssm_c_re[l], 1, 2)).astype(bf)
        c_i = _block_diag(jnp.swapaxes(ssm_c_im[l], 1, 2)).astype(bf)
        avec = jnp.stack([ar.reshape(-1), ai.reshape(-1), arp.reshape(-1), aip.reshape(-1)])

        x = _mixer(x, row(norm_mix_g[l]), w_in[l].astype(bf), conv_w[l], row(conv_b[l]),
                   row(conv_ln_g[l]), row(conv_ln_b[l]), w_conv_out[l].astype(bf), bblk, c_r, c_i, avec,
                   row(ssm_d[l]), w_ssm_glu[l].astype(bf), row(b_ssm_glu[l]), w_ssm_out[l].astype(bf),
                   w_out[l].astype(bf))

        x1 = x.reshape(n_tok, d)
        staged, loc, gate, run, counts = _route(x1, row(norm_ffn_g[l]), w_router[l], row(b_router[l]))
        poff, blk_e, blk_s = _meta(counts, n_blocks)
        runs = _runs(poff, run)
        ys = _run_experts(blk_e, blk_s, counts, poff, run, staged, w_gate_up[l], b_gate_up[l], w_down[l],
                          b_down[l], nt)
        assert depth == 1
        x = _combine(loc, gate, runs, x1, row(norm_final_g), ys, nt).reshape(bsz, seq, d)
    return x
```

```python
import functools
import math

import jax
import jax.numpy as jnp
from jax import lax
from jax.experimental import pallas as pl
from jax.experimental.pallas import tpu as pltpu

NORM_EPS = 1e-6
SWIGLU_LIMIT = 7.0
SWIGLU_ALPHA = 1.702
TOP_K = 4

LANES = 128
SUBLANES = 8

MIX_ROWS = 512
SCAN_SEGS = SUBLANES
SEG_PAD = 8
SCAN_UNROLL = 16
TILE_UNROLL = 16
CONV_HALO = 32
CONV_ROWS = 64
ROUTE_ROWS = 512
MOE_BLOCK = 512
V7X_VMEM_BYTES = 64 * 1024 * 1024
VMEM_LIMIT = V7X_VMEM_BYTES - 8 * 1024 * 1024


def _sigmoid(x):
    return jax.nn.sigmoid(x)


def _gelu_tanh(x):
    c = math.sqrt(2.0 / math.pi)
    return 0.5 * x * (1.0 + jnp.tanh(c * (x + 0.044715 * (x * x * x))))


def _const_spec(shape):
    nd = len(shape)
    return pl.BlockSpec(shape, lambda *_: (0,) * nd)


def _ssm_disc_kernel(seg_len, are_ref, aim_ref, ls_ref, bre_ref, bim_ref,
                     ar_ref, ai_ref, arp_ref, aip_ref, bbr_ref, bbi_ref):
    lr = are_ref[...]
    li = aim_ref[...]
    dt = jnp.exp(ls_ref[...])
    mag = jnp.exp(lr * dt)
    ar = mag * jnp.cos(li * dt)
    ai = mag * jnp.sin(li * dt)
    den = lr * lr + li * li
    fr = ((ar - 1.0) * lr + ai * li) / den
    fi = (ai * lr - (ar - 1.0) * li) / den
    ar_ref[...] = ar
    ai_ref[...] = ai
    magp = jnp.exp(lr * dt * seg_len)
    arp_ref[...] = magp * jnp.cos(li * dt * seg_len)
    aip_ref[...] = magp * jnp.sin(li * dt * seg_len)
    br = bre_ref[...]
    bi = bim_ref[...]
    bbr_ref[...] = fr[:, None, :] * br - fi[:, None, :] * bi
    bbi_ref[...] = fr[:, None, :] * bi + fi[:, None, :] * br


def _ssm_disc(a_re, a_im, log_step, bt_re, bt_im, seg_len):
    g, p = a_re.shape
    h = bt_re.shape[1]
    f32 = jnp.float32
    out_shape = [jax.ShapeDtypeStruct((g, p), f32)] * 4 + [jax.ShapeDtypeStruct((g, h, p), f32)] * 2
    return pl.pallas_call(
        functools.partial(_ssm_disc_kernel, float(seg_len)),
        out_shape=out_shape,
        name="ssm_disc",
    )(a_re, a_im, log_step.reshape(g, 1), bt_re, bt_im)


def _mixer_kernel(dims, x_ref, gmix_ref, win_ref, convw_ref, convb_ref, lng_ref, lnb_ref, wco_ref,
                  bblk_ref, cr_ref, ci_ref, avec_ref, dskip_ref, wglu_ref, bglu_ref, wso_ref, wout_ref,
                  x1_ref, vbuf_ref, conv_ref, scan_ref, xst_ref, cin_ref, st_ref):
    d, dc, ds, gp, cw = dims
    rows = MIX_ROWS
    seg = rows // SCAN_SEGS
    pitch = seg + SEG_PAD
    nsl = gp // LANES
    c_idx = pl.program_id(1)

    half = rows // 2
    tail = 2 * half + 1

    @pl.when(c_idx == 0)
    def _():
        for lc in range(dc // LANES):
            vbuf_ref[lc, pl.ds(0, CONV_HALO, stride=2), :] = jnp.zeros((CONV_HALO, LANES), jnp.float32)
        st_ref[...] = jnp.zeros_like(st_ref)

    @pl.when(c_idx != 0)
    def _():
        for lc in range(dc // LANES):
            vbuf_ref[lc, pl.ds(0, CONV_HALO, stride=2), :] = vbuf_ref[lc, pl.ds(tail, CONV_HALO, stride=2), :]

    x = x_ref[0]
    ms = jnp.mean(x * x, axis=-1, keepdims=True)
    h = (x * lax.rsqrt(ms + NORM_EPS) * gmix_ref[...]).astype(jnp.bfloat16)

    pa = jnp.dot(h, win_ref[:, 0:2 * dc], preferred_element_type=jnp.float32)
    vglu = pa[:, 0:dc] * _sigmoid(pa[:, dc:2 * dc])
    for lc in range(dc // LANES):
        ls = slice(lc * LANES, (lc + 1) * LANES)
        vbuf_ref[lc, pl.ds(2 * CONV_HALO, half, stride=2), :] = vglu[0:half, ls]
        vbuf_ref[lc, pl.ds(2 * CONV_HALO + 1, half, stride=2), :] = vglu[half:rows, ls]
        vbuf_ref[lc, pl.ds(1, CONV_HALO, stride=2), :] = vglu[half - CONV_HALO:half, ls]

    base = CONV_HALO - (cw - 1)
    nv = CONV_ROWS // SUBLANES

    def conv_chunk(rc, p):
        r0 = rc * CONV_ROWS
        for lc in range(dc // LANES):
            ls = slice(lc * LANES, (lc + 1) * LANES)
            wv = [jnp.broadcast_to(convw_ref[k:k + 1, ls], (SUBLANES, LANES)) for k in range(cw)]
            acc = [jnp.zeros((SUBLANES, LANES), jnp.float32) for _ in range(nv)]
            for s in range(CONV_ROWS - SUBLANES + cw):
                win = vbuf_ref[lc, pl.ds(2 * (r0 + base + s) + p, SUBLANES, stride=2), :]
                for i in range(nv):
                    k = s - SUBLANES * i
                    if 0 <= k < cw:
                        acc[i] = acc[i] + wv[k] * win
            for i in range(nv):
                conv_ref[pl.ds(p * half + r0 + SUBLANES * i, SUBLANES), ls] = acc[i]

    g0 = 2 * dc + ds
    chunks = [(rc, p) for rc in range(half // CONV_ROWS) for p in range(2)]
    gw = 2 * d // len(chunks)
    sg = []
    for n, (rc, p) in enumerate(chunks):
        sg.append(_sigmoid(jnp.dot(h, win_ref[:, g0 + n * gw:g0 + (n + 1) * gw],
                                   preferred_element_type=jnp.float32)))
        conv_chunk(rc, p)
    sg = jnp.concatenate(sg, axis=-1)
    sg_conv = sg[:, 0:d]
    sg_ssm = sg[:, d:2 * d]

    v = conv_ref[...] + convb_ref[...]
    mu = jnp.mean(v, axis=-1, keepdims=True)
    vc = v - mu
    var = jnp.mean(vc * vc, axis=-1, keepdims=True)
    v = vc * lax.rsqrt(var + NORM_EPS) * lng_ref[...] + lnb_ref[...]
    v = v * _sigmoid(v)
    y_conv = jnp.dot(v.astype(jnp.bfloat16), wco_ref[...], preferred_element_type=jnp.float32)

    u = jnp.dot(h, win_ref[:, 2 * dc:2 * dc + ds], preferred_element_type=jnp.float32)
    bu = jnp.dot(u.astype(jnp.bfloat16), bblk_ref[...], preferred_element_type=jnp.float32)
    for n in range(2 * nsl):
        for s in range(SCAN_SEGS):
            scan_ref[n, s * pitch:s * pitch + seg, :] = bu[s * seg:(s + 1) * seg, n * LANES:(n + 1) * LANES]

    def bcast(row, n):
        return jnp.broadcast_to(avec_ref[row:row + 1, n * LANES:(n + 1) * LANES], (SUBLANES, LANES))

    a_r = [bcast(0, n) for n in range(nsl)]
    a_i = [bcast(1, n) for n in range(nsl)]

    def step(j, st, store):
        out = []
        for n in range(nsl):
            sr, si = st[2 * n], st[2 * n + 1]
            br = scan_ref[n, pl.ds(j, SCAN_SEGS, stride=pitch), :]
            bi = scan_ref[nsl + n, pl.ds(j, SCAN_SEGS, stride=pitch), :]
            nr = a_r[n] * sr - a_i[n] * si + br
            ni = a_r[n] * si + a_i[n] * sr + bi
            if store:
                scan_ref[n, pl.ds(j, SCAN_SEGS, stride=pitch), :] = nr
                scan_ref[nsl + n, pl.ds(j, SCAN_SEGS, stride=pitch), :] = ni
            out += [nr, ni]
        return tuple(out)

    def steps(jj, st, store):
        for q in range(SCAN_UNROLL):
            st = step(jj * SCAN_UNROLL + q, st, store)
        return st

    zero = jnp.zeros((SUBLANES, LANES), jnp.float32)
    fin = lax.fori_loop(0, seg // SCAN_UNROLL, functools.partial(steps, store=False), (zero,) * (2 * nsl))

    for n in range(nsl):
        fr, fi = fin[2 * n], fin[2 * n + 1]
        ap_r = avec_ref[2:3, n * LANES:(n + 1) * LANES]
        ap_i = avec_ref[3:4, n * LANES:(n + 1) * LANES]
        c_r = st_ref[0:1, n * LANES:(n + 1) * LANES]
        c_i = st_ref[1:2, n * LANES:(n + 1) * LANES]
        for s in range(SCAN_SEGS):
            cin_ref[2 * n, s:s + 1, :] = c_r
            cin_ref[2 * n + 1, s:s + 1, :] = c_i
            n_r = ap_r * c_r - ap_i * c_i + fr[s:s + 1, :]
            n_i = ap_r * c_i + ap_i * c_r + fi[s:s + 1, :]
            c_r, c_i = n_r, n_i
        st_ref[0:1, n * LANES:(n + 1) * LANES] = c_r
        st_ref[1:2, n * LANES:(n + 1) * LANES] = c_i

    lax.fori_loop(0, seg // SCAN_UNROLL, functools.partial(steps, store=True),
                  tuple(cin_ref[q] for q in range(2 * nsl)))

    for n in range(2 * nsl):
        for s in range(SCAN_SEGS):
            xst_ref[s * seg:(s + 1) * seg, n * LANES:(n + 1) * LANES] = (
                scan_ref[n, s * pitch:s * pitch + seg, :].astype(jnp.bfloat16))

    y = (jnp.dot(xst_ref[:, 0:gp], cr_ref[...], preferred_element_type=jnp.float32)
         - jnp.dot(xst_ref[:, gp:2 * gp], ci_ref[...], preferred_element_type=jnp.float32))
    y = _gelu_tanh(y + dskip_ref[...] * u)
    glu = jnp.dot(y.astype(jnp.bfloat16), wglu_ref[...], preferred_element_type=jnp.float32) + bglu_ref[...]
    y = y * _sigmoid(glu)
    y_ssm = jnp.dot(y.astype(jnp.bfloat16), wso_ref[...], preferred_element_type=jnp.float32)

    m = sg_conv * y_conv + sg_ssm * y_ssm
    x1_ref[0] = x + jnp.dot(m.astype(jnp.bfloat16), wout_ref[...], preferred_element_type=jnp.float32)


def _mixer(x, gmix, w_in, conv_w, conv_b, ln_g, ln_b, w_co, bblk, c_r, c_i, avec, d_skip,
           w_glu, b_glu, w_so, w_out):
    b, s, d = x.shape
    cw, dc = conv_w.shape
    ds = d_skip.shape[-1]
    gp = c_r.shape[0]
    rows = MIX_ROWS
    seg = rows // SCAN_SEGS
    pitch = seg + SEG_PAD
    assert s % rows == 0 and cw - 1 <= CONV_HALO and gp % LANES == 0 and dc % LANES == 0
    dims = (d, dc, ds, gp, cw)
    consts = [gmix, w_in, conv_w, conv_b, ln_g, ln_b, w_co, bblk, c_r, c_i, avec, d_skip,
              w_glu, b_glu, w_so, w_out]
    return pl.pallas_call(
        functools.partial(_mixer_kernel, dims),
        grid=(b, s // rows),
        in_specs=[pl.BlockSpec((1, rows, d), lambda i, j: (i, j, 0))] + [_const_spec(c.shape) for c in consts],
        out_specs=pl.BlockSpec((1, rows, d), lambda i, j: (i, j, 0)),
        out_shape=jax.ShapeDtypeStruct((b, s, d), jnp.float32),
        scratch_shapes=[
            pltpu.VMEM((dc // LANES, 2 * CONV_HALO + rows, LANES), jnp.float32),
            pltpu.VMEM((rows, dc), jnp.float32),
            pltpu.VMEM((2 * gp // LANES, SCAN_SEGS * pitch, LANES), jnp.float32),
            pltpu.VMEM((rows, 2 * gp), jnp.bfloat16),
            pltpu.VMEM((2 * gp // LANES, SUBLANES, LANES), jnp.float32),
            pltpu.VMEM((2, gp), jnp.float32),
        ],
        compiler_params=pltpu.CompilerParams(
            dimension_semantics=("arbitrary", "arbitrary"), vmem_limit_bytes=VMEM_LIMIT),
        name="mixer",
    )(x, *consts)


def _route_kernel(x1_ref, g_ref, wr_ref, br_ref, st_ref, loc_ref, gate_ref, run_ref, cnt_ref, carry_ref,
                  hbuf_ref, locv_ref, lsem, *locs_refs):
    rows, d = x1_ref.shape
    ne = wr_ref.shape[1]
    step = pl.program_id(0)
    tiles = pl.num_programs(0) - 1
    par = step % 2
    nt = d // LANES

    @pl.when(step == 0)
    def _():
        carry_ref[...] = jnp.zeros_like(carry_ref)

    def stage_from(half):
        for k in range(TOP_K):
            pltpu.make_async_copy(locv_ref.at[half, k], locs_refs[half * TOP_K + k], lsem.at[half]).wait()

        def place(i, carry):
            for u in range(TILE_UNROLL):
                tok = i * TILE_UNROLL + u
                row = hbuf_ref[half, pl.ds(pl.multiple_of(tok * nt, nt), nt), :]
                for k in range(TOP_K):
                    st_ref[pl.ds(pl.multiple_of(locs_refs[half * TOP_K + k][0, tok], nt), nt), :] = row
            return carry

        lax.fori_loop(0, rows // TILE_UNROLL, place, 0)

    for half in range(2):
        pl.when(jnp.logical_and(step >= 1, 1 - par == half))(functools.partial(stage_from, half))

    x = x1_ref[...]
    ms = jnp.mean(x * x, axis=-1, keepdims=True)
    h2 = x * lax.rsqrt(ms + NORM_EPS) * g_ref[...]

    for j in range(nt):
        hbuf_ref[par, pl.ds(j, rows, stride=nt), :] = h2[:, j * LANES:(j + 1) * LANES]

    bf = jnp.bfloat16
    h_hi = h2.astype(bf)
    h_lo = (h2 - h_hi.astype(jnp.float32)).astype(bf)
    w = wr_ref[...]
    w_hi = w.astype(bf)
    w_lo = (w - w_hi.astype(jnp.float32)).astype(bf)
    logits = (jnp.dot(h_hi, w_hi, preferred_element_type=jnp.float32)
              + jnp.dot(h_lo, w_hi, preferred_element_type=jnp.float32)
              + jnp.dot(h_hi, w_lo, preferred_element_type=jnp.float32)) + br_ref[...]

    pad = jnp.full((rows, LANES - ne), -jnp.inf, jnp.float32)
    work = jnp.concatenate([logits, pad], axis=1).T[0:ne, :]
    eidx = lax.broadcasted_iota(jnp.int32, (ne, rows), 0).astype(jnp.float32)
    sels, vals = [], []
    for k in range(TOP_K):
        mx = jnp.max(work, axis=0, keepdims=True)
        idx = jnp.min(jnp.where(work == mx, eidx, float(ne)), axis=0, keepdims=True)
        sel = eidx == idx
        sels.append(sel)
        vals.append(mx)
        work = jnp.where(sel, -jnp.inf, work)

    exps = [jnp.exp(v - vals[0]) for v in vals]
    tot = exps[0]
    for e in exps[1:]:
        tot = tot + e
    for k in range(TOP_K):
        gate_ref[0, k] = exps[k] / tot

    cnt = jnp.zeros((ne, rows), jnp.float32)
    for sel in sels:
        cnt = cnt + sel.astype(jnp.float32)
    r_i = lax.broadcasted_iota(jnp.int32, (rows, rows), 0)
    c_i = lax.broadcasted_iota(jnp.int32, (rows, rows), 1)
    triu = (r_i < c_i).astype(bf)
    before = jnp.dot(cnt.astype(bf), triu, preferred_element_type=jnp.float32)
    cnt_col = jnp.sum(cnt, axis=1, keepdims=True)
    e_r = lax.broadcasted_iota(jnp.int32, (ne, ne), 0)
    e_c = lax.broadcasted_iota(jnp.int32, (ne, ne), 1)
    col_b = jnp.broadcast_to(cnt_col, (ne, ne))
    cnt_row = jnp.sum(jnp.where(e_r == e_c, col_b, 0.0), axis=0, keepdims=True)
    first_row = jnp.sum(jnp.where(e_r < e_c, col_b, 0.0), axis=0, keepdims=True)
    first_col = jnp.sum(jnp.where(e_c < e_r, jnp.broadcast_to(cnt_row, (ne, ne)), 0.0), axis=1, keepdims=True)
    pos = before + first_col
    for k in range(TOP_K):
        lk = jnp.sum(jnp.where(sels[k], pos, 0.0), axis=0, keepdims=True).astype(jnp.int32)
        loc_ref[0, k] = lk * nt
        locv_ref[par, k] = lk * nt

    def hand_over(half):
        for k in range(TOP_K):
            pltpu.make_async_copy(locv_ref.at[half, k], locs_refs[half * TOP_K + k], lsem.at[half]).start()

    for half in range(2):
        pl.when(jnp.logical_and(step < tiles, par == half))(functools.partial(hand_over, half))

    @pl.when(step < tiles)
    def _():
        run_ref[...] = jnp.zeros_like(run_ref)
        run_ref[0, 0:1, :] = cnt_row.astype(jnp.int32)
        run_ref[0, 1:2, :] = first_row.astype(jnp.int32)
        run_ref[0, 2:3, :] = carry_ref[...].astype(jnp.int32)
        carry_ref[...] = carry_ref[...] + cnt_row
        cnt_ref[...] = carry_ref[...].astype(jnp.int32)


def _route(x1, g_ffn, w_router, b_router):
    t, d = x1.shape
    ne = w_router.shape[1]
    rows = ROUTE_ROWS
    assert t % rows == 0 and d % LANES == 0
    assert TOP_K <= SUBLANES and ne <= LANES
    nt = d // LANES
    tiles = t // rows
    def cur(i):
        return jnp.minimum(i, tiles - 1)

    per_tile = pl.BlockSpec((1, TOP_K, 1, rows), lambda i: (cur(i), 0, 0, 0))
    return pl.pallas_call(
        _route_kernel,
        grid=(tiles + 1,),
        in_specs=[pl.BlockSpec((rows, d), lambda i: (cur(i), 0)),
                  _const_spec(g_ffn.shape), _const_spec(w_router.shape), _const_spec(b_router.shape)],
        out_specs=[pl.BlockSpec((rows * TOP_K * nt, LANES), lambda i: (jnp.maximum(i - 1, 0), 0)),
                   per_tile, per_tile,
                   pl.BlockSpec((1, SUBLANES, ne), lambda i: (cur(i), 0, 0)),
                   _const_spec((1, ne))],
        out_shape=[jax.ShapeDtypeStruct((t * TOP_K * nt, LANES), jnp.float32),
                   jax.ShapeDtypeStruct((tiles, TOP_K, 1, rows), jnp.int32),
                   jax.ShapeDtypeStruct((tiles, TOP_K, 1, rows), jnp.float32),
                   jax.ShapeDtypeStruct((tiles, SUBLANES, ne), jnp.int32),
                   jax.ShapeDtypeStruct((1, ne), jnp.int32)],
        scratch_shapes=[pltpu.VMEM((1, ne), jnp.float32),
                        pltpu.VMEM((2, rows * nt, LANES), jnp.float32),
                        pltpu.VMEM((2, TOP_K, 1, rows), jnp.int32),
                        pltpu.SemaphoreType.DMA((2,))]
                       + [pltpu.SMEM((1, rows), jnp.int32)] * (2 * TOP_K),
        compiler_params=pltpu.CompilerParams(
            dimension_semantics=("arbitrary",), vmem_limit_bytes=VMEM_LIMIT),
        name="route",
    )(x1, g_ffn, w_router, b_router)


def _meta_kernel(cnt_ref, poff_ref, blke_ref, blks_ref):
    ne = cnt_ref.shape[1]
    nb = blke_ref.shape[1]
    f32 = jnp.float32
    cnt = cnt_ref[...].astype(f32)
    nblk = jnp.floor((cnt + (MOE_BLOCK - 1)) * (1.0 / MOE_BLOCK))
    e_r = lax.broadcasted_iota(jnp.int32, (ne, ne), 0)
    e_c = lax.broadcasted_iota(jnp.int32, (ne, ne), 1)
    nb_rows = jnp.broadcast_to(nblk, (ne, ne))
    bend_col = jnp.sum(jnp.where(e_c <= e_r, nb_rows, 0.0), axis=1, keepdims=True)
    nb_col = jnp.sum(jnp.where(e_c == e_r, nb_rows, 0.0), axis=1, keepdims=True)
    boff = jnp.sum(jnp.where(e_r < e_c, jnp.broadcast_to(nb_col, (ne, ne)), 0.0),
                   axis=0, keepdims=True)
    poff_ref[...] = (boff * MOE_BLOCK).astype(jnp.int32)
    total = jnp.sum(nblk, axis=1, keepdims=True)
    blk = lax.broadcasted_iota(jnp.int32, (ne, nb), 1).astype(f32)
    done = jnp.where(jnp.broadcast_to(bend_col, (ne, nb)) <= blk, 1.0, 0.0)
    be = jnp.minimum(jnp.sum(done, axis=0, keepdims=True), ne - 1.0)
    bid = lax.broadcasted_iota(jnp.int32, (1, nb), 1).astype(f32)
    last_e = jnp.sum(jnp.where(bid == total - 1.0, be, 0.0), axis=1, keepdims=True)
    blke_ref[...] = jnp.where(bid < total, be, last_e).astype(jnp.int32)
    blks_ref[...] = jnp.minimum(bid, total - 1.0).astype(jnp.int32)


def _meta(counts, n_blocks):
    ne = counts.shape[1]
    return pl.pallas_call(
        _meta_kernel,
        out_shape=[jax.ShapeDtypeStruct((1, ne), jnp.int32),
                   jax.ShapeDtypeStruct((1, n_blocks), jnp.int32),
                   jax.ShapeDtypeStruct((1, n_blocks), jnp.int32)],
        name="moe_meta",
    )(counts)


def _runs_kernel(poff_ref, run_ref, out_ref):
    run = run_ref[...]
    row = lax.broadcasted_iota(jnp.int32, run.shape, 1)
    out_ref[...] = jnp.where(row == 2, run + poff_ref[...][None], run)


def _runs(poff, run):
    return pl.pallas_call(
        _runs_kernel,
        out_shape=jax.ShapeDtypeStruct(run.shape, jnp.int32),
        name="moe_runs",
    )(poff, run)


def _slot_specs(rows):
    assert TOP_K == 4
    return [pl.BlockSpec((1, 1, 1, rows), functools.partial(lambda k, i: (i, k, 0, 0), k), memory_space=pltpu.SMEM)
            for k in range(TOP_K)]


def _run_copies(run_ref, src_of, dst_of, sem, nt):
    ne = run_ref.shape[2]

    def per_expert(e, carry):
        n = run_ref[0, 0, e]

        @pl.when(n > 0)
        def _():
            pltpu.make_async_copy(src_of(run_ref[0, 1, e], run_ref[0, 2, e], n),
                                  dst_of(run_ref[0, 1, e], run_ref[0, 2, e], n), sem).start()
        return carry

    lax.fori_loop(0, ne, per_expert, 0)


def _run_expert_kernel(nt, blke_ref, blks_ref, cnt_ref, poff_ref, run_ref, st_hbm, wgu_hbm, bgu_ref, wd_hbm,
                       bd_ref, ys_ref, xbuf_ref, wgu32_ref, wd32_ref, wgu_ref, wd_ref, gsem, wsem, runidx_ref,
                       tptr_ref):
    i = pl.program_id(0)
    last = pl.num_programs(0) - 1
    f = wd_ref.shape[0]
    e = blke_ref[0, i]
    valid = blks_ref[0, i] == i
    p = i % 2
    tiles = run_ref.shape[0]
    tile_rows = st_hbm.shape[0] // nt // tiles

    def block_rows(b):
        eb = blke_ref[0, b]
        r0 = b * MOE_BLOCK - poff_ref[0, eb]
        return eb, r0, jnp.minimum(MOE_BLOCK, cnt_ref[0, eb] - r0)

    def gather(b, half):
        eb, r0, _ = block_rows(b)
        fresh = jnp.logical_or(b == 0, eb != blke_ref[0, jnp.maximum(b - 1, 0)])
        t0 = jnp.where(fresh, 0, tptr_ref[0])

        def more(t):
            return jnp.logical_and(t < tiles, run_ref[jnp.minimum(t, tiles - 1), 2, eb] < r0 + MOE_BLOCK)

        def one_tile(t):
            c0 = run_ref[t, 2, eb]
            lo = jnp.maximum(c0, r0)
            hi = jnp.minimum(c0 + run_ref[t, 0, eb], r0 + MOE_BLOCK)

            @pl.when(hi > lo)
            def _():
                src = t * tile_rows + run_ref[t, 1, eb] + (lo - c0)
                pltpu.make_async_copy(
                    st_hbm.at[pl.ds(pl.multiple_of(src * nt, nt), (hi - lo) * nt), :],
                    xbuf_ref.at[half, pl.ds(pl.multiple_of((lo - r0) * nt, nt), (hi - lo) * nt), :],
                    gsem.at[half]).start()
            return t + 1

        t_end = lax.while_loop(more, one_tile, t0)
        tptr_ref[0] = jnp.maximum(t_end - 1, 0)

    def wait_gather(b, half):
        _, _, n = block_rows(b)
        pltpu.make_async_copy(st_hbm.at[pl.ds(0, n * nt), :], xbuf_ref.at[half, pl.ds(0, n * nt), :],
                              gsem.at[half]).wait()

    def fetch(expert, slot):
        pltpu.make_async_copy(wgu_hbm.at[expert], wgu32_ref.at[slot], wsem.at[slot]).start()
        pltpu.make_async_copy(wd_hbm.at[expert], wd32_ref.at[slot], wsem.at[slot]).start()

    def wait_fetch(slot):
        pltpu.make_async_copy(wgu_hbm.at[0], wgu32_ref.at[slot], wsem.at[slot]).wait()
        pltpu.make_async_copy(wd_hbm.at[0], wd32_ref.at[slot], wsem.at[slot]).wait()

    @pl.when(i == 0)
    def _():
        xbuf_ref[...] = jnp.zeros_like(xbuf_ref)
        runidx_ref[0] = 0
        tptr_ref[0] = 0
        fetch(e, 0)
        gather(0, 0)

    @pl.when(jnp.logical_and(valid, jnp.logical_or(i == 0, e != blke_ref[0, jnp.maximum(i - 1, 0)])))
    def _():
        slot = runidx_ref[0] % 2
        wait_fetch(slot)
        nxt = i + (cnt_ref[0, e] + (MOE_BLOCK - 1)) // MOE_BLOCK
        nxt_c = jnp.minimum(nxt, last)

        @pl.when(blks_ref[0, nxt_c] == nxt)
        def _():
            fetch(blke_ref[0, nxt_c], 1 - slot)

        wgu_ref[...] = wgu32_ref[slot].astype(jnp.bfloat16)
        wd_ref[...] = wd32_ref[slot].astype(jnp.bfloat16)
        runidx_ref[0] = runidx_ref[0] + 1

    @pl.when(jnp.logical_not(valid))
    def _():
        ys_ref[...] = jnp.zeros_like(ys_ref)

    @pl.when(valid)
    def _():
        nb = jnp.minimum(i + 1, last)

        @pl.when(blks_ref[0, nb] == i + 1)
        def _():
            gather(i + 1, 1 - p)

        wait_gather(i, p)
        x = jnp.concatenate([xbuf_ref[p, pl.ds(j, MOE_BLOCK, stride=nt), :] for j in range(nt)], axis=-1)
        gu = jnp.dot(x.astype(jnp.bfloat16), wgu_ref[...], preferred_element_type=jnp.float32) + bgu_ref[0]
        g = jnp.minimum(gu[:, 0:f], SWIGLU_LIMIT)
        lin = jnp.clip(gu[:, f:2 * f], -SWIGLU_LIMIT, SWIGLU_LIMIT)
        act = g * _sigmoid(SWIGLU_ALPHA * g) * (lin + 1.0)
        y = jnp.dot(act.astype(jnp.bfloat16), wd_ref[...], preferred_element_type=jnp.float32) + bd_ref[0]
        for j in range(nt):
            ys_ref[pl.ds(j, MOE_BLOCK, stride=nt), :] = y[:, j * LANES:(j + 1) * LANES]


def _run_experts(blk_e, blk_s, counts, poff, run, staged, w_gate_up, b_gate_up, w_down, b_down, nt):
    ne, d, f2 = w_gate_up.shape
    f = f2 // 2
    n_blocks = blk_e.shape[1]
    blk_rows = MOE_BLOCK * nt
    any_spec = pl.BlockSpec(memory_space=pl.ANY)

    def per_expert(i, be, bs, cn, po):
        return (be[0, i], 0, 0)

    grid_spec = pltpu.PrefetchScalarGridSpec(
        num_scalar_prefetch=4,
        grid=(n_blocks,),
        in_specs=[pl.BlockSpec(memory_space=pltpu.SMEM), any_spec, any_spec,
                  pl.BlockSpec((1, 1, f2), per_expert), any_spec, pl.BlockSpec((1, 1, d), per_expert)],
        out_specs=pl.BlockSpec((blk_rows, LANES), lambda i, be, bs, cn, po: (i, 0)),
        scratch_shapes=[pltpu.VMEM((2, blk_rows, LANES), jnp.float32),
                        pltpu.VMEM((2, d, f2), jnp.float32), pltpu.VMEM((2, f, d), jnp.float32),
                        pltpu.VMEM((d, f2), jnp.bfloat16), pltpu.VMEM((f, d), jnp.bfloat16),
                        pltpu.SemaphoreType.DMA((2,)), pltpu.SemaphoreType.DMA((2,)),
                        pltpu.SMEM((1,), jnp.int32), pltpu.SMEM((1,), jnp.int32)],
    )
    return pl.pallas_call(
        functools.partial(_run_expert_kernel, nt),
        grid_spec=grid_spec,
        out_shape=jax.ShapeDtypeStruct((n_blocks * blk_rows, LANES), jnp.float32),
        compiler_params=pltpu.CompilerParams(
            dimension_semantics=("arbitrary",), vmem_limit_bytes=VMEM_LIMIT),
        name="moe_experts",
    )(blk_e, blk_s, counts, poff, run, staged, w_gate_up, b_gate_up.reshape(ne, 1, f2), w_down,
      b_down.reshape(ne, 1, d))


def _combine_kernel(nt, loc0_ref, loc1_ref, loc2_ref, loc3_ref, g0_ref, g1_ref, g2_ref, g3_ref,
                    run_ref, runn_ref, x1_ref, gfin_ref, ys_ref, out_ref, stage_ref, acc_ref, sem):
    loc_refs = (loc0_ref, loc1_ref, loc2_ref, loc3_ref)
    gate_refs = (g0_ref, g1_ref, g2_ref, g3_ref)
    rows, d = x1_ref.shape
    n_assign = rows * TOP_K
    step = pl.program_id(0)
    par = step % 2
    unroll = TILE_UNROLL

    def fetch(r_ref, half):
        _run_copies(r_ref,
                    lambda first, base, n: ys_ref.at[pl.ds(pl.multiple_of(base * nt, nt), n * nt), :],
                    lambda first, base, n: stage_ref.at[half, pl.ds(pl.multiple_of(first * nt, nt), n * nt), :],
                    sem.at[half], nt)

    @pl.when(step == 0)
    def _():
        fetch(run_ref, 0)

    @pl.when(step + 1 < pl.num_programs(0))
    def _():
        fetch(runn_ref, 1 - par)

    pltpu.make_async_copy(ys_ref.at[pl.ds(0, n_assign * nt), :], stage_ref.at[par], sem.at[par]).wait()

    def mix_from(half):
        def mix(i, carry):
            for u in range(unroll):
                tok = i * unroll + u
                acc = None
                for k in range(TOP_K):
                    row = stage_ref[half, pl.ds(pl.multiple_of(loc_refs[k][0, 0, 0, tok], nt), nt), :]
                    term = gate_refs[k][0, 0, 0, tok] * row
                    acc = term if acc is None else acc + term
                acc_ref[pl.ds(pl.multiple_of(tok * nt, nt), nt), :] = acc
            return carry

        lax.fori_loop(0, rows // unroll, mix, 0)

    for half in range(2):
        pl.when(par == half)(functools.partial(mix_from, half))

    y = jnp.concatenate([acc_ref[pl.ds(j, rows, stride=nt), :] for j in range(nt)], axis=-1)
    x = x1_ref[...] + y
    ms = jnp.mean(x * x, axis=-1, keepdims=True)
    out_ref[...] = x * lax.rsqrt(ms + NORM_EPS) * gfin_ref[...]


def _combine(loc, gate, runs, x1, g_final, ys, nt):
    t, d = x1.shape
    rows = ROUTE_ROWS
    tiles = t // rows
    ne = runs.shape[2]
    n_assign = rows * TOP_K
    smem = pltpu.SMEM
    return pl.pallas_call(
        functools.partial(_combine_kernel, nt),
        grid=(tiles,),
        in_specs=[*_slot_specs(rows), *_slot_specs(rows),
                  pl.BlockSpec((1, SUBLANES, ne), lambda i: (i, 0, 0), memory_space=smem),
                  pl.BlockSpec((1, SUBLANES, ne), lambda i: (jnp.minimum(i + 1, tiles - 1), 0, 0),
                               memory_space=smem),
                  pl.BlockSpec((rows, d), lambda i: (i, 0)),
                  _const_spec(g_final.shape),
                  pl.BlockSpec(memory_space=pl.ANY)],
        out_specs=pl.BlockSpec((rows, d), lambda i: (i, 0)),
        out_shape=jax.ShapeDtypeStruct((t, d), jnp.float32),
        scratch_shapes=[pltpu.VMEM((2, n_assign * nt, LANES), jnp.float32),
                        pltpu.VMEM((rows * nt, LANES), jnp.float32),
                        pltpu.SemaphoreType.DMA((2,))],
        compiler_params=pltpu.CompilerParams(
            dimension_semantics=("arbitrary",), vmem_limit_bytes=VMEM_LIMIT),
        name="moe_combine",
    )(loc, loc, loc, loc, gate, gate, gate, gate, runs, runs, x1, g_final, ys)


def _block_diag(blocks):
    g, r, c = blocks.shape
    eye = jnp.eye(g, dtype=blocks.dtype)
    return (blocks[:, :, None, :] * eye[:, None, :, None]).reshape(g * r, g * c)


def kernel(x, norm_mix_g, w_in, conv_w, conv_b, conv_ln_g, conv_ln_b, w_conv_out, ssm_a_re, ssm_a_im,
           ssm_log_step, ssm_b_re, ssm_b_im, ssm_c_re, ssm_c_im, ssm_d, w_ssm_glu, b_ssm_glu, w_ssm_out,
           w_out, norm_ffn_g, w_router, b_router, w_gate_up, b_gate_up, w_down, b_down, norm_final_g):
    bsz, seq, d = x.shape
    depth = w_in.shape[0]
    bf = jnp.bfloat16
    nt = d // LANES
    n_tok = bsz * seq
    ne = w_router.shape[-1]
    n_blocks = -(-n_tok * TOP_K // MOE_BLOCK) + ne
    seg_len = MIX_ROWS // SCAN_SEGS

    def row(v):
        return v.reshape(1, -1)

    for l in range(depth):
        ar, ai, arp, aip, bbr, bbi = _ssm_disc(
            ssm_a_re[l], ssm_a_im[l], ssm_log_step[l],
            jnp.swapaxes(ssm_b_re[l], 1, 2), jnp.swapaxes(ssm_b_im[l], 1, 2), seg_len)
        bblk = jnp.concatenate([_block_diag(bbr), _block_diag(bbi)], axis=1).astype(bf)
        c_r = _block_diag(jnp.swapaxes(ssm_c_re[l], 1, 2)).astype(bf)
        c_i = _block_diag(jnp.swapaxes(ssm_c_im[l], 1, 2)).astype(bf)
        avec = jnp.stack([ar.reshape(-1), ai.reshape(-1), arp.reshape(-1), aip.reshape(-1)])

        x = _mixer(x, row(norm_mix_g[l]), w_in[l].astype(bf), conv_w[l], row(conv_b[l]),
                   row(conv_ln_g[l]), row(conv_ln_b[l]), w_conv_out[l].astype(bf), bblk, c_r, c_i, avec,
                   row(ssm_d[l]), w_ssm_glu[l].astype(bf), row(b_ssm_glu[l]), w_ssm_out[l].astype(bf),
                   w_out[l].astype(bf))

        x1 = x.reshape(n_tok, d)
        staged, loc, gate, run, counts = _route(x1, row(norm_ffn_g[l]), w_router[l], row(b_router[l]))
        poff, blk_e, blk_s = _meta(counts, n_blocks)
        runs = _runs(poff, run)
        ys = _run_experts(blk_e, blk_s, counts, poff, run, staged, w_gate_up[l], b_gate_up[l], w_down[l],
                          b_down[l], nt)
        assert depth == 1
        x = _combine(loc, gate, runs, x1, row(norm_final_g), ys, nt).reshape(bsz, seq, d)
    return x
```

```python
import functools
import math

import jax
import jax.numpy as jnp
from jax import lax
from jax.experimental import pallas as pl
from jax.experimental.pallas import tpu as pltpu

NORM_EPS = 1e-6
SWIGLU_LIMIT = 7.0
SWIGLU_ALPHA = 1.702
TOP_K = 4

LANES = 128
SUBLANES = 8

MIX_ROWS = 512
SCAN_SEGS = SUBLANES
SEG_PAD = 8
SCAN_UNROLL = 16
TILE_UNROLL = 16
CONV_HALO = 32
CONV_ROWS = 64
ROUTE_ROWS = 512
MOE_BLOCK = 512
V7X_VMEM_BYTES = 64 * 1024 * 1024
VMEM_LIMIT = V7X_VMEM_BYTES - 8 * 1024 * 1024


def _sigmoid(x):
    return jax.nn.sigmoid(x)


def _gelu_tanh(x):
    c = math.sqrt(2.0 / math.pi)
    return 0.5 * x * (1.0 + jnp.tanh(c * (x + 0.044715 * (x * x * x))))


def _const_spec(shape):
    nd = len(shape)
    return pl.BlockSpec(shape, lambda *_: (0,) * nd)


def _ssm_disc_kernel(seg_len, are_ref, aim_ref, ls_ref, bre_ref, bim_ref,
                     ar_ref, ai_ref, arp_ref, aip_ref, bbr_ref, bbi_ref):
    lr = are_ref[...]
    li = aim_ref[...]
    dt = jnp.exp(ls_ref[...])
    mag = jnp.exp(lr * dt)
    ar = mag * jnp.cos(li * dt)
    ai = mag * jnp.sin(li * dt)
    den = lr * lr + li * li
    fr = ((ar - 1.0) * lr + ai * li) / den
    fi = (ai * lr - (ar - 1.0) * li) / den
    ar_ref[...] = ar
    ai_ref[...] = ai
    magp = jnp.exp(lr * dt * seg_len)
    arp_ref[...] = magp * jnp.cos(li * dt * seg_len)
    aip_ref[...] = magp * jnp.sin(li * dt * seg_len)
    br = bre_ref[...]
    bi = bim_ref[...]
    bbr_ref[...] = fr[:, None, :] * br - fi[:, None, :] * bi
    bbi_ref[...] = fr[:, None, :] * bi + fi[:, None, :] * br


def _ssm_disc(a_re, a_im, log_step, bt_re, bt_im, seg_len):
    g, p = a_re.shape
    h = bt_re.shape[1]
    f32 = jnp.float32
    out_shape = [jax.ShapeDtypeStruct((g, p), f32)] * 4 + [jax.ShapeDtypeStruct((g, h, p), f32)] * 2
    return pl.pallas_call(
        functools.partial(_ssm_disc_kernel, float(seg_len)),
        out_shape=out_shape,
        name="ssm_disc",
    )(a_re, a_im, log_step.reshape(g, 1), bt_re, bt_im)


def _mixer_kernel(dims, x_ref, gmix_ref, win_ref, convw_ref, convb_ref, lng_ref, lnb_ref, wco_ref,
                  bblk_ref, cr_ref, ci_ref, avec_ref, dskip_ref, wglu_ref, bglu_ref, wso_ref, wout_ref,
                  x1_ref, vbuf_ref, conv_ref, scan_ref, xst_ref, cin_ref, st_ref):
    d, dc, ds, gp, cw = dims
    rows = MIX_ROWS
    seg = rows // SCAN_SEGS
    pitch = seg + SEG_PAD
    nsl = gp // LANES
    c_idx = pl.program_id(1)

    half = rows // 2
    tail = 2 * half + 1

    @pl.when(c_idx == 0)
    def _():
        for lc in range(dc // LANES):
            vbuf_ref[lc, pl.ds(0, CONV_HALO, stride=2), :] = jnp.zeros((CONV_HALO, LANES), jnp.float32)
        st_ref[...] = jnp.zeros_like(st_ref)

    @pl.when(c_idx != 0)
    def _():
        for lc in range(dc // LANES):
            vbuf_ref[lc, pl.ds(0, CONV_HALO, stride=2), :] = vbuf_ref[lc, pl.ds(tail, CONV_HALO, stride=2), :]

    x = x_ref[0]
    ms = jnp.mean(x * x, axis=-1, keepdims=True)
    h = (x * lax.rsqrt(ms + NORM_EPS) * gmix_ref[...]).astype(jnp.bfloat16)

    pa = jnp.dot(h, win_ref[:, 0:2 * dc], preferred_element_type=jnp.float32)
    vglu = pa[:, 0:dc] * _sigmoid(pa[:, dc:2 * dc])
    for lc in range(dc // LANES):
        ls = slice(lc * LANES, (lc + 1) * LANES)
        vbuf_ref[lc, pl.ds(2 * CONV_HALO, half, stride=2), :] = vglu[0:half, ls]
        vbuf_ref[lc, pl.ds(2 * CONV_HALO + 1, half, stride=2), :] = vglu[half:rows, ls]
        vbuf_ref[lc, pl.ds(1, CONV_HALO, stride=2), :] = vglu[half - CONV_HALO:half, ls]

    base = CONV_HALO - (cw - 1)
    nv = CONV_ROWS // SUBLANES

    def conv_chunk(rc, p):
        r0 = rc * CONV_ROWS
        for lc in range(dc // LANES):
            ls = slice(lc * LANES, (lc + 1) * LANES)
            wv = [jnp.broadcast_to(convw_ref[k:k + 1, ls], (SUBLANES, LANES)) for k in range(cw)]
            acc = [jnp.zeros((SUBLANES, LANES), jnp.float32) for _ in range(nv)]
            for s in range(CONV_ROWS - SUBLANES + cw):
                win = vbuf_ref[lc, pl.ds(2 * (r0 + base + s) + p, SUBLANES, stride=2), :]
                for i in range(nv):
                    k = s - SUBLANES * i
                    if 0 <= k < cw:
                        acc[i] = acc[i] + wv[k] * win
            for i in range(nv):
                conv_ref[pl.ds(p * half + r0 + SUBLANES * i, SUBLANES), ls] = acc[i]

    g0 = 2 * dc + ds
    chunks = [(rc, p) for rc in range(half // CONV_ROWS) for p in range(2)]
    gw = 2 * d // len(chunks)
    sg = []
    for n, (rc, p) in enumerate(chunks):
        sg.append(_sigmoid(jnp.dot(h, win_ref[:, g0 + n * gw:g0 + (n + 1) * gw],
                                   preferred_element_type=jnp.float32)))
        conv_chunk(rc, p)
    sg = jnp.concatenate(sg, axis=-1)
    sg_conv = sg[:, 0:d]
    sg_ssm = sg[:, d:2 * d]

    v = conv_ref[...] + convb_ref[...]
    mu = jnp.mean(v, axis=-1, keepdims=True)
    vc = v - mu
    var = jnp.mean(vc * vc, axis=-1, keepdims=True)
    v = vc * lax.rsqrt(var + NORM_EPS) * lng_ref[...] + lnb_ref[...]
    v = v * _sigmoid(v)
    y_conv = jnp.dot(v.astype(jnp.bfloat16), wco_ref[...], preferred_element_type=jnp.float32)

    u = jnp.dot(h, win_ref[:, 2 * dc:2 * dc + ds], preferred_element_type=jnp.float32)
    bu = jnp.dot(u.astype(jnp.bfloat16), bblk_ref[...], preferred_element_type=jnp.float32)
    for n in range(2 * nsl):
        for s in range(SCAN_SEGS):
            scan_ref[n, s * pitch:s * pitch + seg, :] = bu[s * seg:(s + 1) * seg, n * LANES:(n + 1) * LANES]

    def bcast(row, n):
        return jnp.broadcast_to(avec_ref[row:row + 1, n * LANES:(n + 1) * LANES], (SUBLANES, LANES))

    a_r = [bcast(0, n) for n in range(nsl)]
    a_i = [bcast(1, n) for n in range(nsl)]

    def step(j, st, store):
        out = []
        for n in range(nsl):
            sr, si = st[2 * n], st[2 * n + 1]
            br = scan_ref[n, pl.ds(j, SCAN_SEGS, stride=pitch), :]
            bi = scan_ref[nsl + n, pl.ds(j, SCAN_SEGS, stride=pitch), :]
            nr = a_r[n] * sr - a_i[n] * si + br
            ni = a_r[n] * si + a_i[n] * sr + bi
            if store:
                scan_ref[n, pl.ds(j, SCAN_SEGS, stride=pitch), :] = nr
                scan_ref[nsl + n, pl.ds(j, SCAN_SEGS, stride=pitch), :] = ni
            out += [nr, ni]
        return tuple(out)

    def steps(jj, st, store):
        for q in range(SCAN_UNROLL):
            st = step(jj * SCAN_UNROLL + q, st, store)
        return st

    zero = jnp.zeros((SUBLANES, LANES), jnp.float32)
    fin = lax.fori_loop(0, seg // SCAN_UNROLL, functools.partial(steps, store=False), (zero,) * (2 * nsl))

    for n in range(nsl):
        fr, fi = fin[2 * n], fin[2 * n + 1]
        ap_r = avec_ref[2:3, n * LANES:(n + 1) * LANES]
        ap_i = avec_ref[3:4, n * LANES:(n + 1) * LANES]
        c_r = st_ref[0:1, n * LANES:(n + 1) * LANES]
        c_i = st_ref[1:2, n * LANES:(n + 1) * LANES]
        for s in range(SCAN_SEGS):
            cin_ref[2 * n, s:s + 1, :] = c_r
            cin_ref[2 * n + 1, s:s + 1, :] = c_i
            n_r = ap_r * c_r - ap_i * c_i + fr[s:s + 1, :]
            n_i = ap_r * c_i + ap_i * c_r + fi[s:s + 1, :]
            c_r, c_i = n_r, n_i
        st_ref[0:1, n * LANES:(n + 1) * LANES] = c_r
        st_ref[1:2, n * LANES:(n + 1) * LANES] = c_i

    lax.fori_loop(0, seg // SCAN_UNROLL, functools.partial(steps, store=True),
                  tuple(cin_ref[q] for q in range(2 * nsl)))

    for n in range(2 * nsl):
        for s in range(SCAN_SEGS):
            xst_ref[s * seg:(s + 1) * seg, n * LANES:(n + 1) * LANES] = (
                scan_ref[n, s * pitch:s * pitch + seg, :].astype(jnp.bfloat16))

    y = (jnp.dot(xst_ref[:, 0:gp], cr_ref[...], preferred_element_type=jnp.float32)
         - jnp.dot(xst_ref[:, gp:2 * gp], ci_ref[...], preferred_element_type=jnp.float32))
    y = _gelu_tanh(y + dskip_ref[...] * u)
    glu = jnp.dot(y.astype(jnp.bfloat16), wglu_ref[...], preferred_element_type=jnp.float32) + bglu_ref[...]
    y = y * _sigmoid(glu)
    y_ssm = jnp.dot(y.astype(jnp.bfloat16), wso_ref[...], preferred_element_type=jnp.float32)

    m = sg_conv * y_conv + sg_ssm * y_ssm
    x1_ref[0] = x + jnp.dot(m.astype(jnp.bfloat16), wout_ref[...], preferred_element_type=jnp.float32)


def _mixer(x, gmix, w_in, conv_w, conv_b, ln_g, ln_b, w_co, bblk, c_r, c_i, avec, d_skip,
           w_glu, b_glu, w_so, w_out):
    b, s, d = x.shape
    cw, dc = conv_w.shape
    ds = d_skip.shape[-1]
    gp = c_r.shape[0]
    rows = MIX_ROWS
    seg = rows // SCAN_SEGS
    pitch = seg + SEG_PAD
    assert s % rows == 0 and cw - 1 <= CONV_HALO and gp % LANES == 0 and dc % LANES == 0
    dims = (d, dc, ds, gp, cw)
    consts = [gmix, w_in, conv_w, conv_b, ln_g, ln_b, w_co, bblk, c_r, c_i, avec, d_skip,
              w_glu, b_glu, w_so, w_out]
    return pl.pallas_call(
        functools.partial(_mixer_kernel, dims),
        grid=(b, s // rows),
        in_specs=[pl.BlockSpec((1, rows, d), lambda i, j: (i, j, 0))] + [_const_spec(c.shape) for c in consts],
        out_specs=pl.BlockSpec((1, rows, d), lambda i, j: (i, j, 0)),
        out_shape=jax.ShapeDtypeStruct((b, s, d), jnp.float32),
        scratch_shapes=[
            pltpu.VMEM((dc // LANES, 2 * CONV_HALO + rows, LANES), jnp.float32),
            pltpu.VMEM((rows, dc), jnp.float32),
            pltpu.VMEM((2 * gp // LANES, SCAN_SEGS * pitch, LANES), jnp.float32),
            pltpu.VMEM((rows, 2 * gp), jnp.bfloat16),
            pltpu.VMEM((2 * gp // LANES, SUBLANES, LANES), jnp.float32),
            pltpu.VMEM((2, gp), jnp.float32),
        ],
        compiler_params=pltpu.CompilerParams(
            dimension_semantics=("arbitrary", "arbitrary"), vmem_limit_bytes=VMEM_LIMIT),
        name="mixer",
    )(x, *consts)


def _route_kernel(x1_ref, g_ref, wr_ref, br_ref, st_ref, loc_ref, gate_ref, run_ref, cnt_ref, carry_ref,
                  hbuf_ref, locv_ref, lsem, *locs_refs):
    rows, d = x1_ref.shape
    ne = wr_ref.shape[1]
    step = pl.program_id(0)
    tiles = pl.num_programs(0) - 1
    par = step % 2
    nt = d // LANES

    @pl.when(step == 0)
    def _():
        carry_ref[...] = jnp.zeros_like(carry_ref)

    def stage_from(half):
        for k in range(TOP_K):
            pltpu.make_async_copy(locv_ref.at[half, k], locs_refs[half * TOP_K + k], lsem.at[half]).wait()

        def place(i, carry):
            for u in range(TILE_UNROLL):
                tok = i * TILE_UNROLL + u
                row = hbuf_ref[half, pl.ds(pl.multiple_of(tok * nt, nt), nt), :]
                for k in range(TOP_K):
                    st_ref[pl.ds(pl.multiple_of(locs_refs[half * TOP_K + k][0, tok], nt), nt), :] = row
            return carry

        lax.fori_loop(0, rows // TILE_UNROLL, place, 0)

    for half in range(2):
        pl.when(jnp.logical_and(step >= 1, 1 - par == half))(functools.partial(stage_from, half))

    x = x1_ref[...]
    ms = jnp.mean(x * x, axis=-1, keepdims=True)
    h2 = x * lax.rsqrt(ms + NORM_EPS) * g_ref[...]

    for j in range(nt):
        hbuf_ref[par, pl.ds(j, rows, stride=nt), :] = h2[:, j * LANES:(j + 1) * LANES]

    bf = jnp.bfloat16
    h_hi = h2.astype(bf)
    h_lo = (h2 - h_hi.astype(jnp.float32)).astype(bf)
    w = wr_ref[...]
    w_hi = w.astype(bf)
    w_lo = (w - w_hi.astype(jnp.float32)).astype(bf)
    logits = (jnp.dot(h_hi, w_hi, preferred_element_type=jnp.float32)
              + jnp.dot(h_lo, w_hi, preferred_element_type=jnp.float32)
              + jnp.dot(h_hi, w_lo, preferred_element_type=jnp.float32)) + br_ref[...]

    pad = jnp.full((rows, LANES - ne), -jnp.inf, jnp.float32)
    work = jnp.concatenate([logits, pad], axis=1).T[0:ne, :]
    eidx = lax.broadcasted_iota(jnp.int32, (ne, rows), 0).astype(jnp.float32)
    sels, vals = [], []
    for k in range(TOP_K):
        mx = jnp.max(work, axis=0, keepdims=True)
        idx = jnp.min(jnp.where(work == mx, eidx, float(ne)), axis=0, keepdims=True)
        sel = eidx == idx
        sels.append(sel)
        vals.append(mx)
        work = jnp.where(sel, -jnp.inf, work)

    exps = [jnp.exp(v - vals[0]) for v in vals]
    tot = exps[0]
    for e in exps[1:]:
        tot = tot + e
    for k in range(TOP_K):
        gate_ref[0, k] = exps[k] / tot

    cnt = jnp.zeros((ne, rows), jnp.float32)
    for sel in sels:
        cnt = cnt + sel.astype(jnp.float32)
    r_i = lax.broadcasted_iota(jnp.int32, (rows, rows), 0)
    c_i = lax.broadcasted_iota(jnp.int32, (rows, rows), 1)
    triu = (r_i < c_i).astype(bf)
    before = jnp.dot(cnt.astype(bf), triu, preferred_element_type=jnp.float32)
    cnt_col = jnp.sum(cnt, axis=1, keepdims=True)
    e_r = lax.broadcasted_iota(jnp.int32, (ne, ne), 0)
    e_c = lax.broadcasted_iota(jnp.int32, (ne, ne), 1)
    col_b = jnp.broadcast_to(cnt_col, (ne, ne))
    cnt_row = jnp.sum(jnp.where(e_r == e_c, col_b, 0.0), axis=0, keepdims=True)
    first_row = jnp.sum(jnp.where(e_r < e_c, col_b, 0.0), axis=0, keepdims=True)
    first_col = jnp.sum(jnp.where(e_c < e_r, jnp.broadcast_to(cnt_row, (ne, ne)), 0.0), axis=1, keepdims=True)
    pos = before + first_col
    for k in range(TOP_K):
        lk = jnp.sum(jnp.where(sels[k], pos, 0.0), axis=0, keepdims=True).astype(jnp.int32)
        loc_ref[0, k] = lk * nt
        locv_ref[par, k] = lk * nt

    def hand_over(half):
        for k in range(TOP_K):
            pltpu.make_async_copy(locv_ref.at[half, k], locs_refs[half * TOP_K + k], lsem.at[half]).start()

    for half in range(2):
        pl.when(jnp.logical_and(step < tiles, par == half))(functools.partial(hand_over, half))

    @pl.when(step < tiles)
    def _():
        run_ref[...] = jnp.zeros_like(run_ref)
        run_ref[0, 0:1, :] = cnt_row.astype(jnp.int32)
        run_ref[0, 1:2, :] = first_row.astype(jnp.int32)
        run_ref[0, 2:3, :] = carry_ref[...].astype(jnp.int32)
        carry_ref[...] = carry_ref[...] + cnt_row
        cnt_ref[...] = carry_ref[...].astype(jnp.int32)


def _route(x1, g_ffn, w_router, b_router):
    t, d = x1.shape
    ne = w_router.shape[1]
    rows = ROUTE_ROWS
    assert t % rows == 0 and d % LANES == 0
    assert TOP_K <= SUBLANES and ne <= LANES
    nt = d // LANES
    tiles = t // rows
    def cur(i):
        return jnp.minimum(i, tiles - 1)

    per_tile = pl.BlockSpec((1, TOP_K, 1, rows), lambda i: (cur(i), 0, 0, 0))
    return pl.pallas_call(
        _route_kernel,
        grid=(tiles + 1,),
        in_specs=[pl.BlockSpec((rows, d), lambda i: (cur(i), 0)),
                  _const_spec(g_ffn.shape), _const_spec(w_router.shape), _const_spec(b_router.shape)],
        out_specs=[pl.BlockSpec((rows * TOP_K * nt, LANES), lambda i: (jnp.maximum(i - 1, 0), 0)),
                   per_tile, per_tile,
                   pl.BlockSpec((1, SUBLANES, ne), lambda i: (cur(i), 0, 0)),
                   _const_spec((1, ne))],
        out_shape=[jax.ShapeDtypeStruct((t * TOP_K * nt, LANES), jnp.float32),
                   jax.ShapeDtypeStruct((tiles, TOP_K, 1, rows), jnp.int32),
                   jax.ShapeDtypeStruct((tiles, TOP_K, 1, rows), jnp.float32),
                   jax.ShapeDtypeStruct((tiles, SUBLANES, ne), jnp.int32),
                   jax.ShapeDtypeStruct((1, ne), jnp.int32)],
        scratch_shapes=[pltpu.VMEM((1, ne), jnp.float32),
                        pltpu.VMEM((2, rows * nt, LANES), jnp.float32),
                        pltpu.VMEM((2, TOP_K, 1, rows), jnp.int32),
                        pltpu.SemaphoreType.DMA((2,))]
                       + [pltpu.SMEM((1, rows), jnp.int32)] * (2 * TOP_K),
        compiler_params=pltpu.CompilerParams(
            dimension_semantics=("arbitrary",), vmem_limit_bytes=VMEM_LIMIT),
        name="route",
    )(x1, g_ffn, w_router, b_router)


def _meta_kernel(cnt_ref, poff_ref, blke_ref, blks_ref):
    ne = cnt_ref.shape[1]
    nb = blke_ref.shape[1]
    f32 = jnp.float32
    cnt = cnt_ref[...].astype(f32)
    nblk = jnp.floor((cnt + (MOE_BLOCK - 1)) * (1.0 / MOE_BLOCK))
    e_r = lax.broadcasted_iota(jnp.int32, (ne, ne), 0)
    e_c = lax.broadcasted_iota(jnp.int32, (ne, ne), 1)
    nb_rows = jnp.broadcast_to(nblk, (ne, ne))
    bend_col = jnp.sum(jnp.where(e_c <= e_r, nb_rows, 0.0), axis=1, keepdims=True)
    nb_col = jnp.sum(jnp.where(e_c == e_r, nb_rows, 0.0), axis=1, keepdims=True)
    boff = jnp.sum(jnp.where(e_r < e_c, jnp.broadcast_to(nb_col, (ne, ne)), 0.0),
                   axis=0, keepdims=True)
    poff_ref[...] = (boff * MOE_BLOCK).astype(jnp.int32)
    total = jnp.sum(nblk, axis=1, keepdims=True)
    blk = lax.broadcasted_iota(jnp.int32, (ne, nb), 1).astype(f32)
    done = jnp.where(jnp.broadcast_to(bend_col, (ne, nb)) <= blk, 1.0, 0.0)
    be = jnp.minimum(jnp.sum(done, axis=0, keepdims=True), ne - 1.0)
    bid = lax.broadcasted_iota(jnp.int32, (1, nb), 1).astype(f32)
    last_e = jnp.sum(jnp.where(bid == total - 1.0, be, 0.0), axis=1, keepdims=True)
    blke_ref[...] = jnp.where(bid < total, be, last_e).astype(jnp.int32)
    blks_ref[...] = jnp.minimum(bid, total - 1.0).astype(jnp.int32)


def _meta(counts, n_blocks):
    ne = counts.shape[1]
    return pl.pallas_call(
        _meta_kernel,
        out_shape=[jax.ShapeDtypeStruct((1, ne), jnp.int32),
                   jax.ShapeDtypeStruct((1, n_blocks), jnp.int32),
                   jax.ShapeDtypeStruct((1, n_blocks), jnp.int32)],
        name="moe_meta",
    )(counts)


def _runs_kernel(poff_ref, run_ref, out_ref):
    run = run_ref[...]
    row = lax.broadcasted_iota(jnp.int32, run.shape, 1)
    out_ref[...] = jnp.where(row == 2, run + poff_ref[...][None], run)


def _runs(poff, run):
    return pl.pallas_call(
        _runs_kernel,
        out_shape=jax.ShapeDtypeStruct(run.shape, jnp.int32),
        name="moe_runs",
    )(poff, run)


def _slot_specs(rows):
    assert TOP_K == 4
    return [pl.BlockSpec((1, 1, 1, rows), functools.partial(lambda k, i: (i, k, 0, 0), k), memory_space=pltpu.SMEM)
            for k in range(TOP_K)]


def _run_copies(run_ref, src_of, dst_of, sem, nt):
    ne = run_ref.shape[2]
    assert ne % 2 == 0

    def per_pair(j, carry):
        for q in range(2):
            e = 2 * j + q
            n = run_ref[0, 0, e]

            @pl.when(n > 0)
            def _(e=e, n=n, q=q):
                pltpu.make_async_copy(src_of(run_ref[0, 1, e], run_ref[0, 2, e], n),
                                      dst_of(run_ref[0, 1, e], run_ref[0, 2, e], n), sem).start(priority=q)
        return carry

    lax.fori_loop(0, ne // 2, per_pair, 0)


def _run_expert_kernel(nt, blke_ref, blks_ref, cnt_ref, poff_ref, run_ref, st_hbm, wgu_hbm, bgu_ref, wd_hbm,
                       bd_ref, ys_ref, xbuf_ref, wgu32_ref, wd32_ref, wgu_ref, wd_ref, gsem, wsem, runidx_ref,
                       tptr_ref):
    i = pl.program_id(0)
    last = pl.num_programs(0) - 1
    f = wd_ref.shape[0]
    e = blke_ref[0, i]
    valid = blks_ref[0, i] == i
    p = i % 2
    tiles = run_ref.shape[0]
    tile_rows = st_hbm.shape[0] // nt // tiles

    def block_rows(b):
        eb = blke_ref[0, b]
        r0 = b * MOE_BLOCK - poff_ref[0, eb]
        return eb, r0, jnp.minimum(MOE_BLOCK, cnt_ref[0, eb] - r0)

    def gather(b, half):
        eb, r0, _ = block_rows(b)
        fresh = jnp.logical_or(b == 0, eb != blke_ref[0, jnp.maximum(b - 1, 0)])
        t0 = jnp.where(fresh, 0, tptr_ref[0])

        def more(t):
            return jnp.logical_and(t < tiles, run_ref[jnp.minimum(t, tiles - 1), 2, eb] < r0 + MOE_BLOCK)

        def one_tile(t):
            c0 = run_ref[t, 2, eb]
            lo = jnp.maximum(c0, r0)
            hi = jnp.minimum(c0 + run_ref[t, 0, eb], r0 + MOE_BLOCK)

            @pl.when(hi > lo)
            def _():
                src = t * tile_rows + run_ref[t, 1, eb] + (lo - c0)
                pltpu.make_async_copy(
                    st_hbm.at[pl.ds(pl.multiple_of(src * nt, nt), (hi - lo) * nt), :],
                    xbuf_ref.at[half, pl.ds(pl.multiple_of((lo - r0) * nt, nt), (hi - lo) * nt), :],
                    gsem.at[half]).start()
            return t + 1

        t_end = lax.while_loop(more, one_tile, t0)
        tptr_ref[0] = jnp.maximum(t_end - 1, 0)

    def wait_gather(b, half):
        _, _, n = block_rows(b)
        pltpu.make_async_copy(st_hbm.at[pl.ds(0, n * nt), :], xbuf_ref.at[half, pl.ds(0, n * nt), :],
                              gsem.at[half]).wait()

    def fetch(expert, slot):
        pltpu.make_async_copy(wgu_hbm.at[expert], wgu32_ref.at[slot], wsem.at[slot]).start()
        pltpu.make_async_copy(wd_hbm.at[expert], wd32_ref.at[slot], wsem.at[slot]).start()

    def wait_fetch(slot):
        pltpu.make_async_copy(wgu_hbm.at[0], wgu32_ref.at[slot], wsem.at[slot]).wait()
        pltpu.make_async_copy(wd_hbm.at[0], wd32_ref.at[slot], wsem.at[slot]).wait()

    @pl.when(i == 0)
    def _():
        xbuf_ref[...] = jnp.zeros_like(xbuf_ref)
        runidx_ref[0] = 0
        tptr_ref[0] = 0
        fetch(e, 0)
        gather(0, 0)

    @pl.when(jnp.logical_and(valid, jnp.logical_or(i == 0, e != blke_ref[0, jnp.maximum(i - 1, 0)])))
    def _():
        slot = runidx_ref[0] % 2
        wait_fetch(slot)
        nxt = i + (cnt_ref[0, e] + (MOE_BLOCK - 1)) // MOE_BLOCK
        nxt_c = jnp.minimum(nxt, last)

        @pl.when(blks_ref[0, nxt_c] == nxt)
        def _():
            fetch(blke_ref[0, nxt_c], 1 - slot)

        wgu_ref[...] = wgu32_ref[slot].astype(jnp.bfloat16)
        wd_ref[...] = wd32_ref[slot].astype(jnp.bfloat16)
        runidx_ref[0] = runidx_ref[0] + 1

    @pl.when(jnp.logical_not(valid))
    def _():
        ys_ref[...] = jnp.zeros_like(ys_ref)

    @pl.when(valid)
    def _():
        nb = jnp.minimum(i + 1, last)

        @pl.when(blks_ref[0, nb] == i + 1)
        def _():
            gather(i + 1, 1 - p)

        wait_gather(i, p)
        x = jnp.concatenate([xbuf_ref[p, pl.ds(j, MOE_BLOCK, stride=nt), :] for j in range(nt)], axis=-1)
        gu = jnp.dot(x.astype(jnp.bfloat16), wgu_ref[...], preferred_element_type=jnp.float32) + bgu_ref[0]
        g = jnp.minimum(gu[:, 0:f], SWIGLU_LIMIT)
        lin = jnp.clip(gu[:, f:2 * f], -SWIGLU_LIMIT, SWIGLU_LIMIT)
        act = g * _sigmoid(SWIGLU_ALPHA * g) * (lin + 1.0)
        y = jnp.dot(act.astype(jnp.bfloat16), wd_ref[...], preferred_element_type=jnp.float32) + bd_ref[0]
        for j in range(nt):
            ys_ref[pl.ds(j, MOE_BLOCK, stride=nt), :] = y[:, j * LANES:(j + 1) * LANES]


def _run_experts(blk_e, blk_s, counts, poff, run, staged, w_gate_up, b_gate_up, w_down, b_down, nt):
    ne, d, f2 = w_gate_up.shape
    f = f2 // 2
    n_blocks = blk_e.shape[1]
    blk_rows = MOE_BLOCK * nt
    any_spec = pl.BlockSpec(memory_space=pl.ANY)

    def per_expert(i, be, bs, cn, po):
        return (be[0, i], 0, 0)

    grid_spec = pltpu.PrefetchScalarGridSpec(
        num_scalar_prefetch=4,
        grid=(n_blocks,),
        in_specs=[pl.BlockSpec(memory_space=pltpu.SMEM), any_spec, any_spec,
                  pl.BlockSpec((1, 1, f2), per_expert), any_spec, pl.BlockSpec((1, 1, d), per_expert)],
        out_specs=pl.BlockSpec((blk_rows, LANES), lambda i, be, bs, cn, po: (i, 0)),
        scratch_shapes=[pltpu.VMEM((2, blk_rows, LANES), jnp.float32),
                        pltpu.VMEM((2, d, f2), jnp.float32), pltpu.VMEM((2, f, d), jnp.float32),
                        pltpu.VMEM((d, f2), jnp.bfloat16), pltpu.VMEM((f, d), jnp.bfloat16),
                        pltpu.SemaphoreType.DMA((2,)), pltpu.SemaphoreType.DMA((2,)),
                        pltpu.SMEM((1,), jnp.int32), pltpu.SMEM((1,), jnp.int32)],
    )
    return pl.pallas_call(
        functools.partial(_run_expert_kernel, nt),
        grid_spec=grid_spec,
        out_shape=jax.ShapeDtypeStruct((n_blocks * blk_rows, LANES), jnp.float32),
        compiler_params=pltpu.CompilerParams(
            dimension_semantics=("arbitrary",), vmem_limit_bytes=VMEM_LIMIT),
        name="moe_experts",
    )(blk_e, blk_s, counts, poff, run, staged, w_gate_up, b_gate_up.reshape(ne, 1, f2), w_down,
      b_down.reshape(ne, 1, d))


def _combine_kernel(nt, loc0_ref, loc1_ref, loc2_ref, loc3_ref, g0_ref, g1_ref, g2_ref, g3_ref,
                    run_ref, runn_ref, x1_ref, gfin_ref, ys_ref, out_ref, stage_ref, acc_ref, sem):
    loc_refs = (loc0_ref, loc1_ref, loc2_ref, loc3_ref)
    gate_refs = (g0_ref, g1_ref, g2_ref, g3_ref)
    rows, d = x1_ref.shape
    n_assign = rows * TOP_K
    step = pl.program_id(0)
    par = step % 2
    unroll = TILE_UNROLL

    def fetch(r_ref, half):
        _run_copies(r_ref,
                    lambda first, base, n: ys_ref.at[pl.ds(pl.multiple_of(base * nt, nt), n * nt), :],
                    lambda first, base, n: stage_ref.at[half, pl.ds(pl.multiple_of(first * nt, nt), n * nt), :],
                    sem.at[half], nt)

    @pl.when(step == 0)
    def _():
        fetch(run_ref, 0)

    @pl.when(step + 1 < pl.num_programs(0))
    def _():
        fetch(runn_ref, 1 - par)

    pltpu.make_async_copy(ys_ref.at[pl.ds(0, n_assign * nt), :], stage_ref.at[par], sem.at[par]).wait()

    def mix_from(half):
        def mix(i, carry):
            for u in range(unroll):
                tok = i * unroll + u
                acc = None
                for k in range(TOP_K):
                    row = stage_ref[half, pl.ds(pl.multiple_of(loc_refs[k][0, 0, 0, tok], nt), nt), :]
                    term = gate_refs[k][0, 0, 0, tok] * row
                    acc = term if acc is None else acc + term
                acc_ref[pl.ds(pl.multiple_of(tok * nt, nt), nt), :] = acc
            return carry

        lax.fori_loop(0, rows // unroll, mix, 0)

    for half in range(2):
        pl.when(par == half)(functools.partial(mix_from, half))

    y = jnp.concatenate([acc_ref[pl.ds(j, rows, stride=nt), :] for j in range(nt)], axis=-1)
    x = x1_ref[...] + y
    ms = jnp.mean(x * x, axis=-1, keepdims=True)
    out_ref[...] = x * lax.rsqrt(ms + NORM_EPS) * gfin_ref[...]


def _combine(loc, gate, runs, x1, g_final, ys, nt):
    t, d = x1.shape
    rows = ROUTE_ROWS
    tiles = t // rows
    ne = runs.shape[2]
    n_assign = rows * TOP_K
    smem = pltpu.SMEM
    return pl.pallas_call(
        functools.partial(_combine_kernel, nt),
        grid=(tiles,),
        in_specs=[*_slot_specs(rows), *_slot_specs(rows),
                  pl.BlockSpec((1, SUBLANES, ne), lambda i: (i, 0, 0), memory_space=smem),
                  pl.BlockSpec((1, SUBLANES, ne), lambda i: (jnp.minimum(i + 1, tiles - 1), 0, 0),
                               memory_space=smem),
                  pl.BlockSpec((rows, d), lambda i: (i, 0)),
                  _const_spec(g_final.shape),
                  pl.BlockSpec(memory_space=pl.ANY)],
        out_specs=pl.BlockSpec((rows, d), lambda i: (i, 0)),
        out_shape=jax.ShapeDtypeStruct((t, d), jnp.float32),
        scratch_shapes=[pltpu.VMEM((2, n_assign * nt, LANES), jnp.float32),
                        pltpu.VMEM((rows * nt, LANES), jnp.float32),
                        pltpu.SemaphoreType.DMA((2,))],
        compiler_params=pltpu.CompilerParams(
            dimension_semantics=("arbitrary",), vmem_limit_bytes=VMEM_LIMIT),
        name="moe_combine",
    )(loc, loc, loc, loc, gate, gate, gate, gate, runs, runs, x1, g_final, ys)


def _block_diag(blocks):
    g, r, c = blocks.shape
    eye = jnp.eye(g, dtype=blocks.dtype)
    return (blocks[:, :, None, :] * eye[:, None, :, None]).reshape(g * r, g * c)


def kernel(x, norm_mix_g, w_in, conv_w, conv_b, conv_ln_g, conv_ln_b, w_conv_out, ssm_a_re, ssm_a_im,
           ssm_log_step, ssm_b_re, ssm_b_im, ssm_c_re, ssm_c_im, ssm_d, w_ssm_glu, b_ssm_glu, w_ssm_out,
           w_out, norm_ffn_g, w_router, b_router, w_gate_up, b_gate_up, w_down, b_down, norm_final_g):
    bsz, seq, d = x.shape
    depth = w_in.shape[0]
    bf = jnp.bfloat16
    nt = d // LANES
    n_tok = bsz * seq
    ne = w_router.shape[-1]
    n_blocks = -(-n_tok * TOP_K // MOE_BLOCK) + ne
    seg_len = MIX_ROWS // SCAN_SEGS

    def row(v):
        return v.reshape(1, -1)

    for l in range(depth):
        ar, ai, arp, aip, bbr, bbi = _ssm_disc(
            ssm_a_re[l], ssm_a_im[l], ssm_log_step[l],
            jnp.swapaxes(ssm_b_re[l], 1, 2), jnp.swapaxes(ssm_b_im[l], 1, 2), seg_len)
        bblk = jnp.concatenate([_block_diag(bbr), _block_diag(bbi)], axis=1).astype(bf)
        c_r = _block_diag(jnp.swapaxes(ssm_c_re[l], 1, 2)).astype(bf)
        c_i = _block_diag(jnp.swapaxes(ssm_c_im[l], 1, 2)).astype(bf)
        avec = jnp.stack([ar.reshape(-1), ai.reshape(-1), arp.reshape(-1), aip.reshape(-1)])

        x = _mixer(x, row(norm_mix_g[l]), w_in[l].astype(bf), conv_w[l], row(conv_b[l]),
                   row(conv_ln_g[l]), row(conv_ln_b[l]), w_conv_out[l].astype(bf), bblk, c_r, c_i, avec,
                   row(ssm_d[l]), w_ssm_glu[l].astype(bf), row(b_ssm_glu[l]), w_ssm_out[l].astype(bf),
                   w_out[l].astype(bf))

        x1 = x.reshape(n_tok, d)
        staged, loc, gate, run, counts = _route(x1, row(norm_ffn_g[l]), w_router[l], row(b_router[l]))
        poff, blk_e, blk_s = _meta(counts, n_blocks)
        runs = _runs(poff, run)
        ys = _run_experts(blk_e, blk_s, counts, poff, run, staged, w_gate_up[l], b_gate_up[l], w_down[l],
                          b_down[l], nt)
        assert depth == 1
        x = _combine(loc, gate, runs, x1, row(norm_final_g), ys, nt).reshape(bsz, seq, d)
    return x
```

```python
import functools
import math

import jax
import jax.numpy as jnp
from jax import lax
from jax.experimental import pallas as pl
from jax.experimental.pallas import tpu as pltpu

NORM_EPS = 1e-6
SWIGLU_LIMIT = 7.0
SWIGLU_ALPHA = 1.702
TOP_K = 4

LANES = 128
SUBLANES = 8

MIX_ROWS = 512
SCAN_SEGS = SUBLANES
SEG_PAD = 8
SCAN_UNROLL = 16
TILE_UNROLL = 16
CONV_HALO = 32
CONV_ROWS = 64
ROUTE_ROWS = 512
MOE_BLOCK = 512
V7X_VMEM_BYTES = 64 * 1024 * 1024
VMEM_LIMIT = V7X_VMEM_BYTES - 8 * 1024 * 1024


def _sigmoid(x):
    return jax.nn.sigmoid(x)


def _gelu_tanh(x):
    c = math.sqrt(2.0 / math.pi)
    return 0.5 * x * (1.0 + jnp.tanh(c * (x + 0.044715 * (x * x * x))))


def _const_spec(shape):
    nd = len(shape)
    return pl.BlockSpec(shape, lambda *_: (0,) * nd)


def _ssm_disc_kernel(seg_len, are_ref, aim_ref, ls_ref, bre_ref, bim_ref,
                     ar_ref, ai_ref, arp_ref, aip_ref, bbr_ref, bbi_ref):
    lr = are_ref[...]
    li = aim_ref[...]
    dt = jnp.exp(ls_ref[...])
    mag = jnp.exp(lr * dt)
    ar = mag * jnp.cos(li * dt)
    ai = mag * jnp.sin(li * dt)
    den = lr * lr + li * li
    fr = ((ar - 1.0) * lr + ai * li) / den
    fi = (ai * lr - (ar - 1.0) * li) / den
    ar_ref[...] = ar
    ai_ref[...] = ai
    magp = jnp.exp(lr * dt * seg_len)
    arp_ref[...] = magp * jnp.cos(li * dt * seg_len)
    aip_ref[...] = magp * jnp.sin(li * dt * seg_len)
    br = bre_ref[...]
    bi = bim_ref[...]
    bbr_ref[...] = fr[:, None, :] * br - fi[:, None, :] * bi
    bbi_ref[...] = fr[:, None, :] * bi + fi[:, None, :] * br


def _ssm_disc(a_re, a_im, log_step, bt_re, bt_im, seg_len):
    g, p = a_re.shape
    h = bt_re.shape[1]
    f32 = jnp.float32
    out_shape = [jax.ShapeDtypeStruct((g, p), f32)] * 4 + [jax.ShapeDtypeStruct((g, h, p), f32)] * 2
    return pl.pallas_call(
        functools.partial(_ssm_disc_kernel, float(seg_len)),
        out_shape=out_shape,
        name="ssm_disc",
    )(a_re, a_im, log_step.reshape(g, 1), bt_re, bt_im)


def _mixer_kernel(dims, x_ref, gmix_ref, win_ref, convw_ref, convb_ref, lng_ref, lnb_ref, wco_ref,
                  bblk_ref, cr_ref, ci_ref, avec_ref, dskip_ref, wglu_ref, bglu_ref, wso_ref, wout_ref,
                  x1_ref, vbuf_ref, conv_ref, scan_ref, xst_ref, cin_ref, st_ref):
    d, dc, ds, gp, cw = dims
    rows = MIX_ROWS
    seg = rows // SCAN_SEGS
    pitch = seg + SEG_PAD
    nsl = gp // LANES
    c_idx = pl.program_id(1)

    half = rows // 2
    tail = 2 * half + 1

    @pl.when(c_idx == 0)
    def _():
        for lc in range(dc // LANES):
            vbuf_ref[lc, pl.ds(0, CONV_HALO, stride=2), :] = jnp.zeros((CONV_HALO, LANES), jnp.float32)
        st_ref[...] = jnp.zeros_like(st_ref)

    @pl.when(c_idx != 0)
    def _():
        for lc in range(dc // LANES):
            vbuf_ref[lc, pl.ds(0, CONV_HALO, stride=2), :] = vbuf_ref[lc, pl.ds(tail, CONV_HALO, stride=2), :]

    x = x_ref[0]
    ms = jnp.mean(x * x, axis=-1, keepdims=True)
    h = (x * lax.rsqrt(ms + NORM_EPS) * gmix_ref[...]).astype(jnp.bfloat16)

    pa = jnp.dot(h, win_ref[:, 0:2 * dc], preferred_element_type=jnp.float32)
    vglu = pa[:, 0:dc] * _sigmoid(pa[:, dc:2 * dc])
    for lc in range(dc // LANES):
        ls = slice(lc * LANES, (lc + 1) * LANES)
        vbuf_ref[lc, pl.ds(2 * CONV_HALO, half, stride=2), :] = vglu[0:half, ls]
        vbuf_ref[lc, pl.ds(2 * CONV_HALO + 1, half, stride=2), :] = vglu[half:rows, ls]
        vbuf_ref[lc, pl.ds(1, CONV_HALO, stride=2), :] = vglu[half - CONV_HALO:half, ls]

    base = CONV_HALO - (cw - 1)
    nv = CONV_ROWS // SUBLANES

    def conv_chunk(rc, p):
        r0 = rc * CONV_ROWS
        for lc in range(dc // LANES):
            ls = slice(lc * LANES, (lc + 1) * LANES)
            wv = [jnp.broadcast_to(convw_ref[k:k + 1, ls], (SUBLANES, LANES)) for k in range(cw)]
            acc = [jnp.zeros((SUBLANES, LANES), jnp.float32) for _ in range(nv)]
            for s in range(CONV_ROWS - SUBLANES + cw):
                win = vbuf_ref[lc, pl.ds(2 * (r0 + base + s) + p, SUBLANES, stride=2), :]
                for i in range(nv):
                    k = s - SUBLANES * i
                    if 0 <= k < cw:
                        acc[i] = acc[i] + wv[k] * win
            for i in range(nv):
                conv_ref[pl.ds(p * half + r0 + SUBLANES * i, SUBLANES), ls] = acc[i]

    g0 = 2 * dc + ds
    chunks = [(rc, p) for rc in range(half // CONV_ROWS) for p in range(2)]
    gw = 2 * d // len(chunks)
    sg = []
    for n, (rc, p) in enumerate(chunks):
        sg.append(_sigmoid(jnp.dot(h, win_ref[:, g0 + n * gw:g0 + (n + 1) * gw],
                                   preferred_element_type=jnp.float32)))
        conv_chunk(rc, p)
    sg = jnp.concatenate(sg, axis=-1)
    sg_conv = sg[:, 0:d]
    sg_ssm = sg[:, d:2 * d]

    v = conv_ref[...] + convb_ref[...]
    mu = jnp.mean(v, axis=-1, keepdims=True)
    vc = v - mu
    var = jnp.mean(vc * vc, axis=-1, keepdims=True)
    v = vc * lax.rsqrt(var + NORM_EPS) * lng_ref[...] + lnb_ref[...]
    v = v * _sigmoid(v)
    y_conv = jnp.dot(v.astype(jnp.bfloat16), wco_ref[...], preferred_element_type=jnp.float32)

    u = jnp.dot(h, win_ref[:, 2 * dc:2 * dc + ds], preferred_element_type=jnp.float32)
    bu = jnp.dot(u.astype(jnp.bfloat16), bblk_ref[...], preferred_element_type=jnp.float32)
    for n in range(2 * nsl):
        for s in range(SCAN_SEGS):
            scan_ref[n, s * pitch:s * pitch + seg, :] = bu[s * seg:(s + 1) * seg, n * LANES:(n + 1) * LANES]

    def bcast(row, n):
        return jnp.broadcast_to(avec_ref[row:row + 1, n * LANES:(n + 1) * LANES], (SUBLANES, LANES))

    a_r = [bcast(0, n) for n in range(nsl)]
    a_i = [bcast(1, n) for n in range(nsl)]

    def step(j, st, store):
        out = []
        for n in range(nsl):
            sr, si = st[2 * n], st[2 * n + 1]
            br = scan_ref[n, pl.ds(j, SCAN_SEGS, stride=pitch), :]
            bi = scan_ref[nsl + n, pl.ds(j, SCAN_SEGS, stride=pitch), :]
            nr = a_r[n] * sr - a_i[n] * si + br
            ni = a_r[n] * si + a_i[n] * sr + bi
            if store:
                scan_ref[n, pl.ds(j, SCAN_SEGS, stride=pitch), :] = nr
                scan_ref[nsl + n, pl.ds(j, SCAN_SEGS, stride=pitch), :] = ni
            out += [nr, ni]
        return tuple(out)

    def steps(jj, st, store):
        for q in range(SCAN_UNROLL):
            st = step(jj * SCAN_UNROLL + q, st, store)
        return st

    zero = jnp.zeros((SUBLANES, LANES), jnp.float32)
    fin = lax.fori_loop(0, seg // SCAN_UNROLL, functools.partial(steps, store=False), (zero,) * (2 * nsl))

    for n in range(nsl):
        fr, fi = fin[2 * n], fin[2 * n + 1]
        ap_r = avec_ref[2:3, n * LANES:(n + 1) * LANES]
        ap_i = avec_ref[3:4, n * LANES:(n + 1) * LANES]
        c_r = st_ref[0:1, n * LANES:(n + 1) * LANES]
        c_i = st_ref[1:2, n * LANES:(n + 1) * LANES]
        for s in range(SCAN_SEGS):
            cin_ref[2 * n, s:s + 1, :] = c_r
            cin_ref[2 * n + 1, s:s + 1, :] = c_i
            n_r = ap_r * c_r - ap_i * c_i + fr[s:s + 1, :]
            n_i = ap_r * c_i + ap_i * c_r + fi[s:s + 1, :]
            c_r, c_i = n_r, n_i
        st_ref[0:1, n * LANES:(n + 1) * LANES] = c_r
        st_ref[1:2, n * LANES:(n + 1) * LANES] = c_i

    lax.fori_loop(0, seg // SCAN_UNROLL, functools.partial(steps, store=True),
                  tuple(cin_ref[q] for q in range(2 * nsl)))

    for n in range(2 * nsl):
        for s in range(SCAN_SEGS):
            xst_ref[s * seg:(s + 1) * seg, n * LANES:(n + 1) * LANES] = (
                scan_ref[n, s * pitch:s * pitch + seg, :].astype(jnp.bfloat16))

    y = (jnp.dot(xst_ref[:, 0:gp], cr_ref[...], preferred_element_type=jnp.float32)
         - jnp.dot(xst_ref[:, gp:2 * gp], ci_ref[...], preferred_element_type=jnp.float32))
    y = _gelu_tanh(y + dskip_ref[...] * u)
    glu = jnp.dot(y.astype(jnp.bfloat16), wglu_ref[...], preferred_element_type=jnp.float32) + bglu_ref[...]
    y = y * _sigmoid(glu)
    y_ssm = jnp.dot(y.astype(jnp.bfloat16), wso_ref[...], preferred_element_type=jnp.float32)

    m = sg_conv * y_conv + sg_ssm * y_ssm
    x1_ref[0] = x + jnp.dot(m.astype(jnp.bfloat16), wout_ref[...], preferred_element_type=jnp.float32)


def _mixer(x, gmix, w_in, conv_w, conv_b, ln_g, ln_b, w_co, bblk, c_r, c_i, avec, d_skip,
           w_glu, b_glu, w_so, w_out):
    b, s, d = x.shape
    cw, dc = conv_w.shape
    ds = d_skip.shape[-1]
    gp = c_r.shape[0]
    rows = MIX_ROWS
    seg = rows // SCAN_SEGS
    pitch = seg + SEG_PAD
    assert s % rows == 0 and cw - 1 <= CONV_HALO and gp % LANES == 0 and dc % LANES == 0
    dims = (d, dc, ds, gp, cw)
    consts = [gmix, w_in, conv_w, conv_b, ln_g, ln_b, w_co, bblk, c_r, c_i, avec, d_skip,
              w_glu, b_glu, w_so, w_out]
    return pl.pallas_call(
        functools.partial(_mixer_kernel, dims),
        grid=(b, s // rows),
        in_specs=[pl.BlockSpec((1, rows, d), lambda i, j: (i, j, 0))] + [_const_spec(c.shape) for c in consts],
        out_specs=pl.BlockSpec((1, rows, d), lambda i, j: (i, j, 0)),
        out_shape=jax.ShapeDtypeStruct((b, s, d), jnp.float32),
        scratch_shapes=[
            pltpu.VMEM((dc // LANES, 2 * CONV_HALO + rows, LANES), jnp.float32),
            pltpu.VMEM((rows, dc), jnp.float32),
            pltpu.VMEM((2 * gp // LANES, SCAN_SEGS * pitch, LANES), jnp.float32),
            pltpu.VMEM((rows, 2 * gp), jnp.bfloat16),
            pltpu.VMEM((2 * gp // LANES, SUBLANES, LANES), jnp.float32),
            pltpu.VMEM((2, gp), jnp.float32),
        ],
        compiler_params=pltpu.CompilerParams(
            dimension_semantics=("arbitrary", "arbitrary"), vmem_limit_bytes=VMEM_LIMIT),
        name="mixer",
    )(x, *consts)


def _route_kernel(x1_ref, g_ref, wr_ref, br_ref, st_ref, loc_ref, gate_ref, run_ref, cnt_ref, carry_ref,
                  hbuf_ref, locv_ref, lsem, *locs_refs):
    rows, d = x1_ref.shape
    ne = wr_ref.shape[1]
    step = pl.program_id(0)
    tiles = pl.num_programs(0) - 1
    par = step % 2
    nt = d // LANES

    @pl.when(step == 0)
    def _():
        carry_ref[...] = jnp.zeros_like(carry_ref)

    def stage_from(half):
        for k in range(TOP_K):
            pltpu.make_async_copy(locv_ref.at[half, k], locs_refs[half * TOP_K + k], lsem.at[half]).wait()

        def place(i, carry):
            for u in range(TILE_UNROLL):
                tok = i * TILE_UNROLL + u
                row = hbuf_ref[half, pl.ds(pl.multiple_of(tok * nt, nt), nt), :]
                for k in range(TOP_K):
                    st_ref[pl.ds(pl.multiple_of(locs_refs[half * TOP_K + k][0, tok], nt), nt), :] = row
            return carry

        lax.fori_loop(0, rows // TILE_UNROLL, place, 0)

    for half in range(2):
        pl.when(jnp.logical_and(step >= 1, 1 - par == half))(functools.partial(stage_from, half))

    x = x1_ref[...]
    ms = jnp.mean(x * x, axis=-1, keepdims=True)
    h2 = x * lax.rsqrt(ms + NORM_EPS) * g_ref[...]

    for j in range(nt):
        hbuf_ref[par, pl.ds(j, rows, stride=nt), :] = h2[:, j * LANES:(j + 1) * LANES]

    bf = jnp.bfloat16
    h_hi = h2.astype(bf)
    h_lo = (h2 - h_hi.astype(jnp.float32)).astype(bf)
    w = wr_ref[...]
    w_hi = w.astype(bf)
    w_lo = (w - w_hi.astype(jnp.float32)).astype(bf)
    logits = (jnp.dot(h_hi, w_hi, preferred_element_type=jnp.float32)
              + jnp.dot(h_lo, w_hi, preferred_element_type=jnp.float32)
              + jnp.dot(h_hi, w_lo, preferred_element_type=jnp.float32)) + br_ref[...]

    pad = jnp.full((rows, LANES - ne), -jnp.inf, jnp.float32)
    work = jnp.concatenate([logits, pad], axis=1).T[0:ne, :]
    eidx = lax.broadcasted_iota(jnp.int32, (ne, rows), 0).astype(jnp.float32)
    sels, vals = [], []
    for k in range(TOP_K):
        mx = jnp.max(work, axis=0, keepdims=True)
        idx = jnp.min(jnp.where(work == mx, eidx, float(ne)), axis=0, keepdims=True)
        sel = eidx == idx
        sels.append(sel)
        vals.append(mx)
        work = jnp.where(sel, -jnp.inf, work)

    exps = [jnp.exp(v - vals[0]) for v in vals]
    tot = exps[0]
    for e in exps[1:]:
        tot = tot + e
    for k in range(TOP_K):
        gate_ref[0, k] = exps[k] / tot

    cnt = jnp.zeros((ne, rows), jnp.float32)
    for sel in sels:
        cnt = cnt + sel.astype(jnp.float32)
    r_i = lax.broadcasted_iota(jnp.int32, (rows, rows), 0)
    c_i = lax.broadcasted_iota(jnp.int32, (rows, rows), 1)
    triu = (r_i < c_i).astype(bf)
    before = jnp.dot(cnt.astype(bf), triu, preferred_element_type=jnp.float32)
    cnt_col = jnp.sum(cnt, axis=1, keepdims=True)
    e_r = lax.broadcasted_iota(jnp.int32, (ne, ne), 0)
    e_c = lax.broadcasted_iota(jnp.int32, (ne, ne), 1)
    col_b = jnp.broadcast_to(cnt_col, (ne, ne))
    cnt_row = jnp.sum(jnp.where(e_r == e_c, col_b, 0.0), axis=0, keepdims=True)
    first_row = jnp.sum(jnp.where(e_r < e_c, col_b, 0.0), axis=0, keepdims=True)
    first_col = jnp.sum(jnp.where(e_c < e_r, jnp.broadcast_to(cnt_row, (ne, ne)), 0.0), axis=1, keepdims=True)
    pos = before + first_col
    for k in range(TOP_K):
        lk = jnp.sum(jnp.where(sels[k], pos, 0.0), axis=0, keepdims=True).astype(jnp.int32)
        loc_ref[0, k] = lk * nt
        locv_ref[par, k] = lk * nt

    def hand_over(half):
        for k in range(TOP_K):
            pltpu.make_async_copy(locv_ref.at[half, k], locs_refs[half * TOP_K + k], lsem.at[half]).start()

    for half in range(2):
        pl.when(jnp.logical_and(step < tiles, par == half))(functools.partial(hand_over, half))

    @pl.when(step < tiles)
    def _():
        run_ref[...] = jnp.zeros_like(run_ref)
        run_ref[0, 0:1, :] = cnt_row.astype(jnp.int32)
        run_ref[0, 1:2, :] = first_row.astype(jnp.int32)
        run_ref[0, 2:3, :] = carry_ref[...].astype(jnp.int32)
        carry_ref[...] = carry_ref[...] + cnt_row
        cnt_ref[...] = carry_ref[...].astype(jnp.int32)


def _route(x1, g_ffn, w_router, b_router):
    t, d = x1.shape
    ne = w_router.shape[1]
    rows = ROUTE_ROWS
    assert t % rows == 0 and d % LANES == 0
    assert TOP_K <= SUBLANES and ne <= LANES
    nt = d // LANES
    tiles = t // rows
    def cur(i):
        return jnp.minimum(i, tiles - 1)

    per_tile = pl.BlockSpec((1, TOP_K, 1, rows), lambda i: (cur(i), 0, 0, 0))
    return pl.pallas_call(
        _route_kernel,
        grid=(tiles + 1,),
        in_specs=[pl.BlockSpec((rows, d), lambda i: (cur(i), 0)),
                  _const_spec(g_ffn.shape), _const_spec(w_router.shape), _const_spec(b_router.shape)],
        out_specs=[pl.BlockSpec((rows * TOP_K * nt, LANES), lambda i: (jnp.maximum(i - 1, 0), 0)),
                   per_tile, per_tile,
                   pl.BlockSpec((1, SUBLANES, ne), lambda i: (cur(i), 0, 0)),
                   _const_spec((1, ne))],
        out_shape=[jax.ShapeDtypeStruct((t * TOP_K * nt, LANES), jnp.float32),
                   jax.ShapeDtypeStruct((tiles, TOP_K, 1, rows), jnp.int32),
                   jax.ShapeDtypeStruct((tiles, TOP_K, 1, rows), jnp.float32),
                   jax.ShapeDtypeStruct((tiles, SUBLANES, ne), jnp.int32),
                   jax.ShapeDtypeStruct((1, ne), jnp.int32)],
        scratch_shapes=[pltpu.VMEM((1, ne), jnp.float32),
                        pltpu.VMEM((2, rows * nt, LANES), jnp.float32),
                        pltpu.VMEM((2, TOP_K, 1, rows), jnp.int32),
                        pltpu.SemaphoreType.DMA((2,))]
                       + [pltpu.SMEM((1, rows), jnp.int32)] * (2 * TOP_K),
        compiler_params=pltpu.CompilerParams(
            dimension_semantics=("arbitrary",), vmem_limit_bytes=VMEM_LIMIT),
        name="route",
    )(x1, g_ffn, w_router, b_router)


def _meta_kernel(cnt_ref, poff_ref, blke_ref, blks_ref):
    ne = cnt_ref.shape[1]
    nb = blke_ref.shape[1]
    f32 = jnp.float32
    cnt = cnt_ref[...].astype(f32)
    nblk = jnp.floor((cnt + (MOE_BLOCK - 1)) * (1.0 / MOE_BLOCK))
    e_r = lax.broadcasted_iota(jnp.int32, (ne, ne), 0)
    e_c = lax.broadcasted_iota(jnp.int32, (ne, ne), 1)
    nb_rows = jnp.broadcast_to(nblk, (ne, ne))
    bend_col = jnp.sum(jnp.where(e_c <= e_r, nb_rows, 0.0), axis=1, keepdims=True)
    nb_col = jnp.sum(jnp.where(e_c == e_r, nb_rows, 0.0), axis=1, keepdims=True)
    boff = jnp.sum(jnp.where(e_r < e_c, jnp.broadcast_to(nb_col, (ne, ne)), 0.0),
                   axis=0, keepdims=True)
    poff_ref[...] = (boff * MOE_BLOCK).astype(jnp.int32)
    total = jnp.sum(nblk, axis=1, keepdims=True)
    blk = lax.broadcasted_iota(jnp.int32, (ne, nb), 1).astype(f32)
    done = jnp.where(jnp.broadcast_to(bend_col, (ne, nb)) <= blk, 1.0, 0.0)
    be = jnp.minimum(jnp.sum(done, axis=0, keepdims=True), ne - 1.0)
    bid = lax.broadcasted_iota(jnp.int32, (1, nb), 1).astype(f32)
    last_e = jnp.sum(jnp.where(bid == total - 1.0, be, 0.0), axis=1, keepdims=True)
    blke_ref[...] = jnp.where(bid < total, be, last_e).astype(jnp.int32)
    blks_ref[...] = jnp.minimum(bid, total - 1.0).astype(jnp.int32)


def _meta(counts, n_blocks):
    ne = counts.shape[1]
    return pl.pallas_call(
        _meta_kernel,
        out_shape=[jax.ShapeDtypeStruct((1, ne), jnp.int32),
                   jax.ShapeDtypeStruct((1, n_blocks), jnp.int32),
                   jax.ShapeDtypeStruct((1, n_blocks), jnp.int32)],
        name="moe_meta",
    )(counts)


def _runs_kernel(poff_ref, run_ref, out_ref):
    run = run_ref[...]
    row = lax.broadcasted_iota(jnp.int32, run.shape, 1)
    out_ref[...] = jnp.where(row == 2, run + poff_ref[...][None], run)


def _runs(poff, run):
    return pl.pallas_call(
        _runs_kernel,
        out_shape=jax.ShapeDtypeStruct(run.shape, jnp.int32),
        name="moe_runs",
    )(poff, run)


def _slot_specs(rows):
    assert TOP_K == 4
    return [pl.BlockSpec((1, 1, 1, rows), functools.partial(lambda k, i: (i, k, 0, 0), k), memory_space=pltpu.SMEM)
            for k in range(TOP_K)]


def _run_copies(run_ref, src_of, dst_of, sem, nt):
    ne = run_ref.shape[2]

    def per_expert(e, carry):
        n = run_ref[0, 0, e]

        @pl.when(n > 0)
        def _():
            pltpu.make_async_copy(src_of(run_ref[0, 1, e], run_ref[0, 2, e], n),
                                  dst_of(run_ref[0, 1, e], run_ref[0, 2, e], n), sem).start()
        return carry

    lax.fori_loop(0, ne, per_expert, 0)


def _run_expert_kernel(nt, blke_ref, blks_ref, cnt_ref, poff_ref, run_ref, st_hbm, wgu_hbm, bgu_ref, wd_hbm,
                       bd_ref, ys_ref, xbuf_ref, wgu32_ref, wd32_ref, wgu_ref, wd_ref, gsem, wsem, runidx_ref,
                       tptr_ref):
    i = pl.program_id(0)
    last = pl.num_programs(0) - 1
    f = wd_ref.shape[0]
    e = blke_ref[0, i]
    valid = blks_ref[0, i] == i
    p = i % 2
    tiles = run_ref.shape[2]
    tile_rows = st_hbm.shape[0] // nt // tiles

    def block_rows(b):
        eb = blke_ref[0, b]
        r0 = b * MOE_BLOCK - poff_ref[0, eb]
        return eb, r0, jnp.minimum(MOE_BLOCK, cnt_ref[0, eb] - r0)

    def gather(b, half):
        eb, r0, _ = block_rows(b)
        fresh = jnp.logical_or(b == 0, eb != blke_ref[0, jnp.maximum(b - 1, 0)])
        t0 = jnp.where(fresh, 0, tptr_ref[0])

        def more(t):
            return jnp.logical_and(t < tiles, run_ref[eb, 2, jnp.minimum(t, tiles - 1)] < r0 + MOE_BLOCK)

        def one_tile(t):
            c0 = run_ref[eb, 2, t]
            lo = jnp.maximum(c0, r0)
            hi = jnp.minimum(c0 + run_ref[eb, 0, t], r0 + MOE_BLOCK)

            @pl.when(hi > lo)
            def _():
                src = t * tile_rows + run_ref[eb, 1, t] + (lo - c0)
                pltpu.make_async_copy(
                    st_hbm.at[pl.ds(pl.multiple_of(src * nt, nt), (hi - lo) * nt), :],
                    xbuf_ref.at[half, pl.ds(pl.multiple_of((lo - r0) * nt, nt), (hi - lo) * nt), :],
                    gsem.at[half]).start()
            return t + 1

        t_end = lax.while_loop(more, one_tile, t0)
        tptr_ref[0] = jnp.maximum(t_end - 1, 0)

    def wait_gather(b, half):
        _, _, n = block_rows(b)
        pltpu.make_async_copy(st_hbm.at[pl.ds(0, n * nt), :], xbuf_ref.at[half, pl.ds(0, n * nt), :],
                              gsem.at[half]).wait()

    def fetch(expert, slot):
        pltpu.make_async_copy(wgu_hbm.at[expert], wgu32_ref.at[slot], wsem.at[slot]).start()
        pltpu.make_async_copy(wd_hbm.at[expert], wd32_ref.at[slot], wsem.at[slot]).start()

    def wait_fetch(slot):
        pltpu.make_async_copy(wgu_hbm.at[0], wgu32_ref.at[slot], wsem.at[slot]).wait()
        pltpu.make_async_copy(wd_hbm.at[0], wd32_ref.at[slot], wsem.at[slot]).wait()

    @pl.when(i == 0)
    def _():
        xbuf_ref[...] = jnp.zeros_like(xbuf_ref)
        runidx_ref[0] = 0
        tptr_ref[0] = 0
        fetch(e, 0)
        gather(0, 0)

    @pl.when(jnp.logical_and(valid, jnp.logical_or(i == 0, e != blke_ref[0, jnp.maximum(i - 1, 0)])))
    def _():
        slot = runidx_ref[0] % 2
        wait_fetch(slot)
        nxt = i + (cnt_ref[0, e] + (MOE_BLOCK - 1)) // MOE_BLOCK
        nxt_c = jnp.minimum(nxt, last)

        @pl.when(blks_ref[0, nxt_c] == nxt)
        def _():
            fetch(blke_ref[0, nxt_c], 1 - slot)

        wgu_ref[...] = wgu32_ref[slot].astype(jnp.bfloat16)
        wd_ref[...] = wd32_ref[slot].astype(jnp.bfloat16)
        runidx_ref[0] = runidx_ref[0] + 1

    @pl.when(jnp.logical_not(valid))
    def _():
        ys_ref[...] = jnp.zeros_like(ys_ref)

    @pl.when(valid)
    def _():
        nb = jnp.minimum(i + 1, last)

        @pl.when(blks_ref[0, nb] == i + 1)
        def _():
            gather(i + 1, 1 - p)

        wait_gather(i, p)
        x = jnp.concatenate([xbuf_ref[p, pl.ds(j, MOE_BLOCK, stride=nt), :] for j in range(nt)], axis=-1)
        gu = jnp.dot(x.astype(jnp.bfloat16), wgu_ref[...], preferred_element_type=jnp.float32) + bgu_ref[0]
        g = jnp.minimum(gu[:, 0:f], SWIGLU_LIMIT)
        lin = jnp.clip(gu[:, f:2 * f], -SWIGLU_LIMIT, SWIGLU_LIMIT)
        act = g * _sigmoid(SWIGLU_ALPHA * g) * (lin + 1.0)
        y = jnp.dot(act.astype(jnp.bfloat16), wd_ref[...], preferred_element_type=jnp.float32) + bd_ref[0]
        for j in range(nt):
            ys_ref[pl.ds(j, MOE_BLOCK, stride=nt), :] = y[:, j * LANES:(j + 1) * LANES]


def _run_experts(blk_e, blk_s, counts, poff, run, staged, w_gate_up, b_gate_up, w_down, b_down, nt):
    ne, d, f2 = w_gate_up.shape
    f = f2 // 2
    n_blocks = blk_e.shape[1]
    blk_rows = MOE_BLOCK * nt
    any_spec = pl.BlockSpec(memory_space=pl.ANY)

    def per_expert(i, be, bs, cn, po):
        return (be[0, i], 0, 0)

    grid_spec = pltpu.PrefetchScalarGridSpec(
        num_scalar_prefetch=4,
        grid=(n_blocks,),
        in_specs=[pl.BlockSpec(memory_space=pltpu.SMEM), any_spec, any_spec,
                  pl.BlockSpec((1, 1, f2), per_expert), any_spec, pl.BlockSpec((1, 1, d), per_expert)],
        out_specs=pl.BlockSpec((blk_rows, LANES), lambda i, be, bs, cn, po: (i, 0)),
        scratch_shapes=[pltpu.VMEM((2, blk_rows, LANES), jnp.float32),
                        pltpu.VMEM((2, d, f2), jnp.float32), pltpu.VMEM((2, f, d), jnp.float32),
                        pltpu.VMEM((d, f2), jnp.bfloat16), pltpu.VMEM((f, d), jnp.bfloat16),
                        pltpu.SemaphoreType.DMA((2,)), pltpu.SemaphoreType.DMA((2,)),
                        pltpu.SMEM((1,), jnp.int32), pltpu.SMEM((1,), jnp.int32)],
    )
    return pl.pallas_call(
        functools.partial(_run_expert_kernel, nt),
        grid_spec=grid_spec,
        out_shape=jax.ShapeDtypeStruct((n_blocks * blk_rows, LANES), jnp.float32),
        compiler_params=pltpu.CompilerParams(
            dimension_semantics=("arbitrary",), vmem_limit_bytes=VMEM_LIMIT),
        name="moe_experts",
    )(blk_e, blk_s, counts, poff, jnp.transpose(run, (2, 1, 0)), staged, w_gate_up, b_gate_up.reshape(ne, 1, f2), w_down,
      b_down.reshape(ne, 1, d))


def _combine_kernel(nt, loc0_ref, loc1_ref, loc2_ref, loc3_ref, g0_ref, g1_ref, g2_ref, g3_ref,
                    run_ref, runn_ref, x1_ref, gfin_ref, ys_ref, out_ref, stage_ref, acc_ref, sem):
    loc_refs = (loc0_ref, loc1_ref, loc2_ref, loc3_ref)
    gate_refs = (g0_ref, g1_ref, g2_ref, g3_ref)
    rows, d = x1_ref.shape
    n_assign = rows * TOP_K
    step = pl.program_id(0)
    par = step % 2
    unroll = TILE_UNROLL

    def fetch(r_ref, half):
        _run_copies(r_ref,
                    lambda first, base, n: ys_ref.at[pl.ds(pl.multiple_of(base * nt, nt), n * nt), :],
                    lambda first, base, n: stage_ref.at[half, pl.ds(pl.multiple_of(first * nt, nt), n * nt), :],
                    sem.at[half], nt)

    @pl.when(step == 0)
    def _():
        fetch(run_ref, 0)

    @pl.when(step + 1 < pl.num_programs(0))
    def _():
        fetch(runn_ref, 1 - par)

    pltpu.make_async_copy(ys_ref.at[pl.ds(0, n_assign * nt), :], stage_ref.at[par], sem.at[par]).wait()

    def mix_from(half):
        def mix(i, carry):
            for u in range(unroll):
                tok = i * unroll + u
                acc = None
                for k in range(TOP_K):
                    row = stage_ref[half, pl.ds(pl.multiple_of(loc_refs[k][0, 0, 0, tok], nt), nt), :]
                    term = gate_refs[k][0, 0, 0, tok] * row
                    acc = term if acc is None else acc + term
                acc_ref[pl.ds(pl.multiple_of(tok * nt, nt), nt), :] = acc
            return carry

        lax.fori_loop(0, rows // unroll, mix, 0)

    for half in range(2):
        pl.when(par == half)(functools.partial(mix_from, half))

    y = jnp.concatenate([acc_ref[pl.ds(j, rows, stride=nt), :] for j in range(nt)], axis=-1)
    x = x1_ref[...] + y
    ms = jnp.mean(x * x, axis=-1, keepdims=True)
    out_ref[...] = x * lax.rsqrt(ms + NORM_EPS) * gfin_ref[...]


def _combine(loc, gate, runs, x1, g_final, ys, nt):
    t, d = x1.shape
    rows = ROUTE_ROWS
    tiles = t // rows
    ne = runs.shape[2]
    n_assign = rows * TOP_K
    smem = pltpu.SMEM
    return pl.pallas_call(
        functools.partial(_combine_kernel, nt),
        grid=(tiles,),
        in_specs=[*_slot_specs(rows), *_slot_specs(rows),
                  pl.BlockSpec((1, SUBLANES, ne), lambda i: (i, 0, 0), memory_space=smem),
                  pl.BlockSpec((1, SUBLANES, ne), lambda i: (jnp.minimum(i + 1, tiles - 1), 0, 0),
                               memory_space=smem),
                  pl.BlockSpec((rows, d), lambda i: (i, 0)),
                  _const_spec(g_final.shape),
                  pl.BlockSpec(memory_space=pl.ANY)],
        out_specs=pl.BlockSpec((rows, d), lambda i: (i, 0)),
        out_shape=jax.ShapeDtypeStruct((t, d), jnp.float32),
        scratch_shapes=[pltpu.VMEM((2, n_assign * nt, LANES), jnp.float32),
                        pltpu.VMEM((rows * nt, LANES), jnp.float32),
                        pltpu.SemaphoreType.DMA((2,))],
        compiler_params=pltpu.CompilerParams(
            dimension_semantics=("arbitrary",), vmem_limit_bytes=VMEM_LIMIT),
        name="moe_combine",
    )(loc, loc, loc, loc, gate, gate, gate, gate, runs, runs, x1, g_final, ys)


def _block_diag(blocks):
    g, r, c = blocks.shape
    eye = jnp.eye(g, dtype=blocks.dtype)
    return (blocks[:, :, None, :] * eye[:, None, :, None]).reshape(g * r, g * c)


def kernel(x, norm_mix_g, w_in, conv_w, conv_b, conv_ln_g, conv_ln_b, w_conv_out, ssm_a_re, ssm_a_im,
           ssm_log_step, ssm_b_re, ssm_b_im, ssm_c_re, ssm_c_im, ssm_d, w_ssm_glu, b_ssm_glu, w_ssm_out,
           w_out, norm_ffn_g, w_router, b_router, w_gate_up, b_gate_up, w_down, b_down, norm_final_g):
    bsz, seq, d = x.shape
    depth = w_in.shape[0]
    bf = jnp.bfloat16
    nt = d // LANES
    n_tok = bsz * seq
    ne = w_router.shape[-1]
    n_blocks = -(-n_tok * TOP_K // MOE_BLOCK) + ne
    seg_len = MIX_ROWS // SCAN_SEGS

    def row(v):
        return v.reshape(1, -1)

    for l in range(depth):
        ar, ai, arp, aip, bbr, bbi = _ssm_disc(
            ssm_a_re[l], ssm_a_im[l], ssm_log_step[l],
            jnp.swapaxes(ssm_b_re[l], 1, 2), jnp.swapaxes(ssm_b_im[l], 1, 2), seg_len)
        bblk = jnp.concatenate([_block_diag(bbr), _block_diag(bbi)], axis=1).astype(bf)
        c_r = _block_diag(jnp.swapaxes(ssm_c_re[l], 1, 2)).astype(bf)
        c_i = _block_diag(jnp.swapaxes(ssm_c_im[l], 1, 2)).astype(bf)
        avec = jnp.stack([ar.reshape(-1), ai.reshape(-1), arp.reshape(-1), aip.reshape(-1)])

        x = _mixer(x, row(norm_mix_g[l]), w_in[l].astype(bf), conv_w[l], row(conv_b[l]),
                   row(conv_ln_g[l]), row(conv_ln_b[l]), w_conv_out[l].astype(bf), bblk, c_r, c_i, avec,
                   row(ssm_d[l]), w_ssm_glu[l].astype(bf), row(b_ssm_glu[l]), w_ssm_out[l].astype(bf),
                   w_out[l].astype(bf))

        x1 = x.reshape(n_tok, d)
        staged, loc, gate, run, counts = _route(x1, row(norm_ffn_g[l]), w_router[l], row(b_router[l]))
        poff, blk_e, blk_s = _meta(counts, n_blocks)
        runs = _runs(poff, run)
        ys = _run_experts(blk_e, blk_s, counts, poff, run, staged, w_gate_up[l], b_gate_up[l], w_down[l],
                          b_down[l], nt)
        assert depth == 1
        x = _combine(loc, gate, runs, x1, row(norm_final_g), ys, nt).reshape(bsz, seq, d)
    return x
```

```python
import functools
import math

import jax
import jax.numpy as jnp
from jax import lax
from jax.experimental import pallas as pl
from jax.experimental.pallas import tpu as pltpu

NORM_EPS = 1e-6
SWIGLU_LIMIT = 7.0
SWIGLU_ALPHA = 1.702
TOP_K = 4

LANES = 128
SUBLANES = 8

MIX_ROWS = 512
SCAN_SEGS = SUBLANES
SEG_PAD = 8
SCAN_UNROLL = 16
TILE_UNROLL = 32
CONV_HALO = 32
CONV_ROWS = 64
ROUTE_ROWS = 512
MOE_BLOCK = 512
V7X_VMEM_BYTES = 64 * 1024 * 1024
VMEM_LIMIT = V7X_VMEM_BYTES - 8 * 1024 * 1024


def _sigmoid(x):
    return jax.nn.sigmoid(x)


def _gelu_tanh(x):
    c = math.sqrt(2.0 / math.pi)
    return 0.5 * x * (1.0 + jnp.tanh(c * (x + 0.044715 * (x * x * x))))


def _const_spec(shape):
    nd = len(shape)
    return pl.BlockSpec(shape, lambda *_: (0,) * nd)


def _ssm_disc_kernel(seg_len, are_ref, aim_ref, ls_ref, bre_ref, bim_ref,
                     ar_ref, ai_ref, arp_ref, aip_ref, bbr_ref, bbi_ref):
    lr = are_ref[...]
    li = aim_ref[...]
    dt = jnp.exp(ls_ref[...])
    mag = jnp.exp(lr * dt)
    ar = mag * jnp.cos(li * dt)
    ai = mag * jnp.sin(li * dt)
    den = lr * lr + li * li
    fr = ((ar - 1.0) * lr + ai * li) / den
    fi = (ai * lr - (ar - 1.0) * li) / den
    ar_ref[...] = ar
    ai_ref[...] = ai
    magp = jnp.exp(lr * dt * seg_len)
    arp_ref[...] = magp * jnp.cos(li * dt * seg_len)
    aip_ref[...] = magp * jnp.sin(li * dt * seg_len)
    br = bre_ref[...]
    bi = bim_ref[...]
    bbr_ref[...] = fr[:, None, :] * br - fi[:, None, :] * bi
    bbi_ref[...] = fr[:, None, :] * bi + fi[:, None, :] * br


def _ssm_disc(a_re, a_im, log_step, bt_re, bt_im, seg_len):
    g, p = a_re.shape
    h = bt_re.shape[1]
    f32 = jnp.float32
    out_shape = [jax.ShapeDtypeStruct((g, p), f32)] * 4 + [jax.ShapeDtypeStruct((g, h, p), f32)] * 2
    return pl.pallas_call(
        functools.partial(_ssm_disc_kernel, float(seg_len)),
        out_shape=out_shape,
        name="ssm_disc",
    )(a_re, a_im, log_step.reshape(g, 1), bt_re, bt_im)


def _mixer_kernel(dims, x_ref, gmix_ref, win_ref, convw_ref, convb_ref, lng_ref, lnb_ref, wco_ref,
                  bblk_ref, cr_ref, ci_ref, avec_ref, dskip_ref, wglu_ref, bglu_ref, wso_ref, wout_ref,
                  x1_ref, vbuf_ref, conv_ref, scan_ref, xst_ref, cin_ref, st_ref):
    d, dc, ds, gp, cw = dims
    rows = MIX_ROWS
    seg = rows // SCAN_SEGS
    pitch = seg + SEG_PAD
    nsl = gp // LANES
    c_idx = pl.program_id(1)

    half = rows // 2
    tail = 2 * half + 1

    @pl.when(c_idx == 0)
    def _():
        for lc in range(dc // LANES):
            vbuf_ref[lc, pl.ds(0, CONV_HALO, stride=2), :] = jnp.zeros((CONV_HALO, LANES), jnp.float32)
        st_ref[...] = jnp.zeros_like(st_ref)

    @pl.when(c_idx != 0)
    def _():
        for lc in range(dc // LANES):
            vbuf_ref[lc, pl.ds(0, CONV_HALO, stride=2), :] = vbuf_ref[lc, pl.ds(tail, CONV_HALO, stride=2), :]

    x = x_ref[0]
    ms = jnp.mean(x * x, axis=-1, keepdims=True)
    h = (x * lax.rsqrt(ms + NORM_EPS) * gmix_ref[...]).astype(jnp.bfloat16)

    pa = jnp.dot(h, win_ref[:, 0:2 * dc], preferred_element_type=jnp.float32)
    vglu = pa[:, 0:dc] * _sigmoid(pa[:, dc:2 * dc])
    for lc in range(dc // LANES):
        ls = slice(lc * LANES, (lc + 1) * LANES)
        vbuf_ref[lc, pl.ds(2 * CONV_HALO, half, stride=2), :] = vglu[0:half, ls]
        vbuf_ref[lc, pl.ds(2 * CONV_HALO + 1, half, stride=2), :] = vglu[half:rows, ls]
        vbuf_ref[lc, pl.ds(1, CONV_HALO, stride=2), :] = vglu[half - CONV_HALO:half, ls]

    base = CONV_HALO - (cw - 1)
    nv = CONV_ROWS // SUBLANES

    def conv_chunk(rc, p):
        r0 = rc * CONV_ROWS
        for lc in range(dc // LANES):
            ls = slice(lc * LANES, (lc + 1) * LANES)
            wv = [jnp.broadcast_to(convw_ref[k:k + 1, ls], (SUBLANES, LANES)) for k in range(cw)]
            acc = [jnp.zeros((SUBLANES, LANES), jnp.float32) for _ in range(nv)]
            for s in range(CONV_ROWS - SUBLANES + cw):
                win = vbuf_ref[lc, pl.ds(2 * (r0 + base + s) + p, SUBLANES, stride=2), :]
                for i in range(nv):
                    k = s - SUBLANES * i
                    if 0 <= k < cw:
                        acc[i] = acc[i] + wv[k] * win
            for i in range(nv):
                conv_ref[pl.ds(p * half + r0 + SUBLANES * i, SUBLANES), ls] = acc[i]

    g0 = 2 * dc + ds
    chunks = [(rc, p) for rc in range(half // CONV_ROWS) for p in range(2)]
    gw = 2 * d // len(chunks)
    sg = []
    for n, (rc, p) in enumerate(chunks):
        sg.append(_sigmoid(jnp.dot(h, win_ref[:, g0 + n * gw:g0 + (n + 1) * gw],
                                   preferred_element_type=jnp.float32)))
        conv_chunk(rc, p)
    sg = jnp.concatenate(sg, axis=-1)
    sg_conv = sg[:, 0:d]
    sg_ssm = sg[:, d:2 * d]

    v = conv_ref[...] + convb_ref[...]
    mu = jnp.mean(v, axis=-1, keepdims=True)
    vc = v - mu
    var = jnp.mean(vc * vc, axis=-1, keepdims=True)
    v = vc * lax.rsqrt(var + NORM_EPS) * lng_ref[...] + lnb_ref[...]
    v = v * _sigmoid(v)
    y_conv = jnp.dot(v.astype(jnp.bfloat16), wco_ref[...], preferred_element_type=jnp.float32)

    u = jnp.dot(h, win_ref[:, 2 * dc:2 * dc + ds], preferred_element_type=jnp.float32)
    bu = jnp.dot(u.astype(jnp.bfloat16), bblk_ref[...], preferred_element_type=jnp.float32)
    for n in range(2 * nsl):
        for s in range(SCAN_SEGS):
            scan_ref[n, s * pitch:s * pitch + seg, :] = bu[s * seg:(s + 1) * seg, n * LANES:(n + 1) * LANES]

    def bcast(row, n):
        return jnp.broadcast_to(avec_ref[row:row + 1, n * LANES:(n + 1) * LANES], (SUBLANES, LANES))

    a_r = [bcast(0, n) for n in range(nsl)]
    a_i = [bcast(1, n) for n in range(nsl)]

    def step(j, st, store):
        out = []
        for n in range(nsl):
            sr, si = st[2 * n], st[2 * n + 1]
            br = scan_ref[n, pl.ds(j, SCAN_SEGS, stride=pitch), :]
            bi = scan_ref[nsl + n, pl.ds(j, SCAN_SEGS, stride=pitch), :]
            nr = a_r[n] * sr - a_i[n] * si + br
            ni = a_r[n] * si + a_i[n] * sr + bi
            if store:
                scan_ref[n, pl.ds(j, SCAN_SEGS, stride=pitch), :] = nr
                scan_ref[nsl + n, pl.ds(j, SCAN_SEGS, stride=pitch), :] = ni
            out += [nr, ni]
        return tuple(out)

    def steps(jj, st, store):
        for q in range(SCAN_UNROLL):
            st = step(jj * SCAN_UNROLL + q, st, store)
        return st

    zero = jnp.zeros((SUBLANES, LANES), jnp.float32)
    fin = lax.fori_loop(0, seg // SCAN_UNROLL, functools.partial(steps, store=False), (zero,) * (2 * nsl))

    for n in range(nsl):
        fr, fi = fin[2 * n], fin[2 * n + 1]
        ap_r = avec_ref[2:3, n * LANES:(n + 1) * LANES]
        ap_i = avec_ref[3:4, n * LANES:(n + 1) * LANES]
        c_r = st_ref[0:1, n * LANES:(n + 1) * LANES]
        c_i = st_ref[1:2, n * LANES:(n + 1) * LANES]
        for s in range(SCAN_SEGS):
            cin_ref[2 * n, s:s + 1, :] = c_r
            cin_ref[2 * n + 1, s:s + 1, :] = c_i
            n_r = ap_r * c_r - ap_i * c_i + fr[s:s + 1, :]
            n_i = ap_r * c_i + ap_i * c_r + fi[s:s + 1, :]
            c_r, c_i = n_r, n_i
        st_ref[0:1, n * LANES:(n + 1) * LANES] = c_r
        st_ref[1:2, n * LANES:(n + 1) * LANES] = c_i

    lax.fori_loop(0, seg // SCAN_UNROLL, functools.partial(steps, store=True),
                  tuple(cin_ref[q] for q in range(2 * nsl)))

    for n in range(2 * nsl):
        for s in range(SCAN_SEGS):
            xst_ref[s * seg:(s + 1) * seg, n * LANES:(n + 1) * LANES] = (
                scan_ref[n, s * pitch:s * pitch + seg, :].astype(jnp.bfloat16))

    y = (jnp.dot(xst_ref[:, 0:gp], cr_ref[...], preferred_element_type=jnp.float32)
         - jnp.dot(xst_ref[:, gp:2 * gp], ci_ref[...], preferred_element_type=jnp.float32))
    y = _gelu_tanh(y + dskip_ref[...] * u)
    glu = jnp.dot(y.astype(jnp.bfloat16), wglu_ref[...], preferred_element_type=jnp.float32) + bglu_ref[...]
    y = y * _sigmoid(glu)
    y_ssm = jnp.dot(y.astype(jnp.bfloat16), wso_ref[...], preferred_element_type=jnp.float32)

    m = sg_conv * y_conv + sg_ssm * y_ssm
    x1_ref[0] = x + jnp.dot(m.astype(jnp.bfloat16), wout_ref[...], preferred_element_type=jnp.float32)


def _mixer(x, gmix, w_in, conv_w, conv_b, ln_g, ln_b, w_co, bblk, c_r, c_i, avec, d_skip,
           w_glu, b_glu, w_so, w_out):
    b, s, d = x.shape
    cw, dc = conv_w.shape
    ds = d_skip.shape[-1]
    gp = c_r.shape[0]
    rows = MIX_ROWS
    seg = rows // SCAN_SEGS
    pitch = seg + SEG_PAD
    assert s % rows == 0 and cw - 1 <= CONV_HALO and gp % LANES == 0 and dc % LANES == 0
    dims = (d, dc, ds, gp, cw)
    consts = [gmix, w_in, conv_w, conv_b, ln_g, ln_b, w_co, bblk, c_r, c_i, avec, d_skip,
              w_glu, b_glu, w_so, w_out]
    return pl.pallas_call(
        functools.partial(_mixer_kernel, dims),
        grid=(b, s // rows),
        in_specs=[pl.BlockSpec((1, rows, d), lambda i, j: (i, j, 0))] + [_const_spec(c.shape) for c in consts],
        out_specs=pl.BlockSpec((1, rows, d), lambda i, j: (i, j, 0)),
        out_shape=jax.ShapeDtypeStruct((b, s, d), jnp.float32),
        scratch_shapes=[
            pltpu.VMEM((dc // LANES, 2 * CONV_HALO + rows, LANES), jnp.float32),
            pltpu.VMEM((rows, dc), jnp.float32),
            pltpu.VMEM((2 * gp // LANES, SCAN_SEGS * pitch, LANES), jnp.float32),
            pltpu.VMEM((rows, 2 * gp), jnp.bfloat16),
            pltpu.VMEM((2 * gp // LANES, SUBLANES, LANES), jnp.float32),
            pltpu.VMEM((2, gp), jnp.float32),
        ],
        compiler_params=pltpu.CompilerParams(
            dimension_semantics=("arbitrary", "arbitrary"), vmem_limit_bytes=VMEM_LIMIT),
        name="mixer",
    )(x, *consts)


def _route_kernel(x1_ref, g_ref, wr_ref, br_ref, st_ref, loc_ref, gate_ref, run_ref, cnt_ref, carry_ref,
                  hbuf_ref, locv_ref, lsem, *locs_refs):
    rows, d = x1_ref.shape
    ne = wr_ref.shape[1]
    step = pl.program_id(0)
    tiles = pl.num_programs(0) - 1
    par = step % 2
    nt = d // LANES

    @pl.when(step == 0)
    def _():
        carry_ref[...] = jnp.zeros_like(carry_ref)

    def stage_from(half):
        for k in range(TOP_K):
            pltpu.make_async_copy(locv_ref.at[half, k], locs_refs[half * TOP_K + k], lsem.at[half]).wait()

        def place(i, carry):
            for u in range(TILE_UNROLL):
                tok = i * TILE_UNROLL + u
                row = hbuf_ref[half, pl.ds(pl.multiple_of(tok * nt, nt), nt), :]
                for k in range(TOP_K):
                    st_ref[pl.ds(pl.multiple_of(locs_refs[half * TOP_K + k][0, tok], nt), nt), :] = row
            return carry

        lax.fori_loop(0, rows // TILE_UNROLL, place, 0)

    for half in range(2):
        pl.when(jnp.logical_and(step >= 1, 1 - par == half))(functools.partial(stage_from, half))

    x = x1_ref[...]
    ms = jnp.mean(x * x, axis=-1, keepdims=True)
    h2 = x * lax.rsqrt(ms + NORM_EPS) * g_ref[...]

    for j in range(nt):
        hbuf_ref[par, pl.ds(j, rows, stride=nt), :] = h2[:, j * LANES:(j + 1) * LANES]

    bf = jnp.bfloat16
    h_hi = h2.astype(bf)
    h_lo = (h2 - h_hi.astype(jnp.float32)).astype(bf)
    w = wr_ref[...]
    w_hi = w.astype(bf)
    w_lo = (w - w_hi.astype(jnp.float32)).astype(bf)
    logits = (jnp.dot(h_hi, w_hi, preferred_element_type=jnp.float32)
              + jnp.dot(h_lo, w_hi, preferred_element_type=jnp.float32)
              + jnp.dot(h_hi, w_lo, preferred_element_type=jnp.float32)) + br_ref[...]

    pad = jnp.full((rows, LANES - ne), -jnp.inf, jnp.float32)
    work = jnp.concatenate([logits, pad], axis=1).T[0:ne, :]
    eidx = lax.broadcasted_iota(jnp.int32, (ne, rows), 0).astype(jnp.float32)
    sels, vals = [], []
    for k in range(TOP_K):
        mx = jnp.max(work, axis=0, keepdims=True)
        idx = jnp.min(jnp.where(work == mx, eidx, float(ne)), axis=0, keepdims=True)
        sel = eidx == idx
        sels.append(sel)
        vals.append(mx)
        work = jnp.where(sel, -jnp.inf, work)

    exps = [jnp.exp(v - vals[0]) for v in vals]
    tot = exps[0]
    for e in exps[1:]:
        tot = tot + e
    for k in range(TOP_K):
        gate_ref[0, k] = exps[k] / tot

    cnt = jnp.zeros((ne, rows), jnp.float32)
    for sel in sels:
        cnt = cnt + sel.astype(jnp.float32)
    r_i = lax.broadcasted_iota(jnp.int32, (rows, rows), 0)
    c_i = lax.broadcasted_iota(jnp.int32, (rows, rows), 1)
    triu = (r_i < c_i).astype(bf)
    before = jnp.dot(cnt.astype(bf), triu, preferred_element_type=jnp.float32)
    cnt_col = jnp.sum(cnt, axis=1, keepdims=True)
    e_r = lax.broadcasted_iota(jnp.int32, (ne, ne), 0)
    e_c = lax.broadcasted_iota(jnp.int32, (ne, ne), 1)
    col_b = jnp.broadcast_to(cnt_col, (ne, ne))
    cnt_row = jnp.sum(jnp.where(e_r == e_c, col_b, 0.0), axis=0, keepdims=True)
    first_row = jnp.sum(jnp.where(e_r < e_c, col_b, 0.0), axis=0, keepdims=True)
    first_col = jnp.sum(jnp.where(e_c < e_r, jnp.broadcast_to(cnt_row, (ne, ne)), 0.0), axis=1, keepdims=True)
    pos = before + first_col
    for k in range(TOP_K):
        lk = jnp.sum(jnp.where(sels[k], pos, 0.0), axis=0, keepdims=True).astype(jnp.int32)
        loc_ref[0, k] = lk * nt
        locv_ref[par, k] = lk * nt

    def hand_over(half):
        for k in range(TOP_K):
            pltpu.make_async_copy(locv_ref.at[half, k], locs_refs[half * TOP_K + k], lsem.at[half]).start()

    for half in range(2):
        pl.when(jnp.logical_and(step < tiles, par == half))(functools.partial(hand_over, half))

    @pl.when(step < tiles)
    def _():
        run_ref[...] = jnp.zeros_like(run_ref)
        run_ref[0, 0:1, :] = cnt_row.astype(jnp.int32)
        run_ref[0, 1:2, :] = first_row.astype(jnp.int32)
        run_ref[0, 2:3, :] = carry_ref[...].astype(jnp.int32)
        carry_ref[...] = carry_ref[...] + cnt_row
        cnt_ref[...] = carry_ref[...].astype(jnp.int32)


def _route(x1, g_ffn, w_router, b_router):
    t, d = x1.shape
    ne = w_router.shape[1]
    rows = ROUTE_ROWS
    assert t % rows == 0 and d % LANES == 0
    assert TOP_K <= SUBLANES and ne <= LANES
    nt = d // LANES
    tiles = t // rows
    def cur(i):
        return jnp.minimum(i, tiles - 1)

    per_tile = pl.BlockSpec((1, TOP_K, 1, rows), lambda i: (cur(i), 0, 0, 0))
    return pl.pallas_call(
        _route_kernel,
        grid=(tiles + 1,),
        in_specs=[pl.BlockSpec((rows, d), lambda i: (cur(i), 0)),
                  _const_spec(g_ffn.shape), _const_spec(w_router.shape), _const_spec(b_router.shape)],
        out_specs=[pl.BlockSpec((rows * TOP_K * nt, LANES), lambda i: (jnp.maximum(i - 1, 0), 0)),
                   per_tile, per_tile,
                   pl.BlockSpec((1, SUBLANES, ne), lambda i: (cur(i), 0, 0)),
                   _const_spec((1, ne))],
        out_shape=[jax.ShapeDtypeStruct((t * TOP_K * nt, LANES), jnp.float32),
                   jax.ShapeDtypeStruct((tiles, TOP_K, 1, rows), jnp.int32),
                   jax.ShapeDtypeStruct((tiles, TOP_K, 1, rows), jnp.float32),
                   jax.ShapeDtypeStruct((tiles, SUBLANES, ne), jnp.int32),
                   jax.ShapeDtypeStruct((1, ne), jnp.int32)],
        scratch_shapes=[pltpu.VMEM((1, ne), jnp.float32),
                        pltpu.VMEM((2, rows * nt, LANES), jnp.float32),
                        pltpu.VMEM((2, TOP_K, 1, rows), jnp.int32),
                        pltpu.SemaphoreType.DMA((2,))]
                       + [pltpu.SMEM((1, rows), jnp.int32)] * (2 * TOP_K),
        compiler_params=pltpu.CompilerParams(
            dimension_semantics=("arbitrary",), vmem_limit_bytes=VMEM_LIMIT),
        name="route",
    )(x1, g_ffn, w_router, b_router)


def _meta_kernel(cnt_ref, poff_ref, blke_ref, blks_ref):
    ne = cnt_ref.shape[1]
    nb = blke_ref.shape[1]
    f32 = jnp.float32
    cnt = cnt_ref[...].astype(f32)
    nblk = jnp.floor((cnt + (MOE_BLOCK - 1)) * (1.0 / MOE_BLOCK))
    e_r = lax.broadcasted_iota(jnp.int32, (ne, ne), 0)
    e_c = lax.broadcasted_iota(jnp.int32, (ne, ne), 1)
    nb_rows = jnp.broadcast_to(nblk, (ne, ne))
    bend_col = jnp.sum(jnp.where(e_c <= e_r, nb_rows, 0.0), axis=1, keepdims=True)
    nb_col = jnp.sum(jnp.where(e_c == e_r, nb_rows, 0.0), axis=1, keepdims=True)
    boff = jnp.sum(jnp.where(e_r < e_c, jnp.broadcast_to(nb_col, (ne, ne)), 0.0),
                   axis=0, keepdims=True)
    poff_ref[...] = (boff * MOE_BLOCK).astype(jnp.int32)
    total = jnp.sum(nblk, axis=1, keepdims=True)
    blk = lax.broadcasted_iota(jnp.int32, (ne, nb), 1).astype(f32)
    done = jnp.where(jnp.broadcast_to(bend_col, (ne, nb)) <= blk, 1.0, 0.0)
    be = jnp.minimum(jnp.sum(done, axis=0, keepdims=True), ne - 1.0)
    bid = lax.broadcasted_iota(jnp.int32, (1, nb), 1).astype(f32)
    last_e = jnp.sum(jnp.where(bid == total - 1.0, be, 0.0), axis=1, keepdims=True)
    blke_ref[...] = jnp.where(bid < total, be, last_e).astype(jnp.int32)
    blks_ref[...] = jnp.minimum(bid, total - 1.0).astype(jnp.int32)


def _meta(counts, n_blocks):
    ne = counts.shape[1]
    return pl.pallas_call(
        _meta_kernel,
        out_shape=[jax.ShapeDtypeStruct((1, ne), jnp.int32),
                   jax.ShapeDtypeStruct((1, n_blocks), jnp.int32),
                   jax.ShapeDtypeStruct((1, n_blocks), jnp.int32)],
        name="moe_meta",
    )(counts)


def _runs_kernel(poff_ref, run_ref, out_ref):
    run = run_ref[...]
    row = lax.broadcasted_iota(jnp.int32, run.shape, 1)
    out_ref[...] = jnp.where(row == 2, run + poff_ref[...][None], run)


def _runs(poff, run):
    return pl.pallas_call(
        _runs_kernel,
        out_shape=jax.ShapeDtypeStruct(run.shape, jnp.int32),
        name="moe_runs",
    )(poff, run)


def _slot_specs(rows):
    assert TOP_K == 4
    return [pl.BlockSpec((1, 1, 1, rows), functools.partial(lambda k, i: (i, k, 0, 0), k), memory_space=pltpu.SMEM)
            for k in range(TOP_K)]


def _run_copies(run_ref, src_of, dst_of, sem, nt):
    ne = run_ref.shape[2]

    def per_expert(e, carry):
        n = run_ref[0, 0, e]

        @pl.when(n > 0)
        def _():
            pltpu.make_async_copy(src_of(run_ref[0, 1, e], run_ref[0, 2, e], n),
                                  dst_of(run_ref[0, 1, e], run_ref[0, 2, e], n), sem).start()
        return carry

    lax.fori_loop(0, ne, per_expert, 0)


def _run_expert_kernel(nt, blke_ref, blks_ref, cnt_ref, poff_ref, run_ref, st_hbm, wgu_hbm, bgu_ref, wd_hbm,
                       bd_ref, ys_ref, xbuf_ref, wgu32_ref, wd32_ref, wgu_ref, wd_ref, gsem, wsem, runidx_ref,
                       tptr_ref):
    i = pl.program_id(0)
    last = pl.num_programs(0) - 1
    f = wd_ref.shape[0]
    e = blke_ref[0, i]
    valid = blks_ref[0, i] == i
    p = i % 2
    tiles = run_ref.shape[0]
    tile_rows = st_hbm.shape[0] // nt // tiles

    def block_rows(b):
        eb = blke_ref[0, b]
        r0 = b * MOE_BLOCK - poff_ref[0, eb]
        return eb, r0, jnp.minimum(MOE_BLOCK, cnt_ref[0, eb] - r0)

    def gather(b, half):
        eb, r0, _ = block_rows(b)
        fresh = jnp.logical_or(b == 0, eb != blke_ref[0, jnp.maximum(b - 1, 0)])
        t0 = jnp.where(fresh, 0, tptr_ref[0])

        def more(t):
            return jnp.logical_and(t < tiles, run_ref[jnp.minimum(t, tiles - 1), 2, eb] < r0 + MOE_BLOCK)

        def one_tile(t):
            c0 = run_ref[t, 2, eb]
            lo = jnp.maximum(c0, r0)
            hi = jnp.minimum(c0 + run_ref[t, 0, eb], r0 + MOE_BLOCK)

            @pl.when(hi > lo)
            def _():
                src = t * tile_rows + run_ref[t, 1, eb] + (lo - c0)
                pltpu.make_async_copy(
                    st_hbm.at[pl.ds(pl.multiple_of(src * nt, nt), (hi - lo) * nt), :],
                    xbuf_ref.at[half, pl.ds(pl.multiple_of((lo - r0) * nt, nt), (hi - lo) * nt), :],
                    gsem.at[half]).start()
            return t + 1

        t_end = lax.while_loop(more, one_tile, t0)
        tptr_ref[0] = jnp.maximum(t_end - 1, 0)

    def wait_gather(b, half):
        _, _, n = block_rows(b)
        pltpu.make_async_copy(st_hbm.at[pl.ds(0, n * nt), :], xbuf_ref.at[half, pl.ds(0, n * nt), :],
                              gsem.at[half]).wait()

    def fetch(expert, slot):
        pltpu.make_async_copy(wgu_hbm.at[expert], wgu32_ref.at[slot], wsem.at[slot]).start()
        pltpu.make_async_copy(wd_hbm.at[expert], wd32_ref.at[slot], wsem.at[slot]).start()

    def wait_fetch(slot):
        pltpu.make_async_copy(wgu_hbm.at[0], wgu32_ref.at[slot], wsem.at[slot]).wait()
        pltpu.make_async_copy(wd_hbm.at[0], wd32_ref.at[slot], wsem.at[slot]).wait()

    @pl.when(i == 0)
    def _():
        xbuf_ref[...] = jnp.zeros_like(xbuf_ref)
        runidx_ref[0] = 0
        tptr_ref[0] = 0
        fetch(e, 0)
        gather(0, 0)

    @pl.when(jnp.logical_and(valid, jnp.logical_or(i == 0, e != blke_ref[0, jnp.maximum(i - 1, 0)])))
    def _():
        slot = runidx_ref[0] % 2
        wait_fetch(slot)
        nxt = i + (cnt_ref[0, e] + (MOE_BLOCK - 1)) // MOE_BLOCK
        nxt_c = jnp.minimum(nxt, last)

        @pl.when(blks_ref[0, nxt_c] == nxt)
        def _():
            fetch(blke_ref[0, nxt_c], 1 - slot)

        wgu_ref[...] = wgu32_ref[slot].astype(jnp.bfloat16)
        wd_ref[...] = wd32_ref[slot].astype(jnp.bfloat16)
        runidx_ref[0] = runidx_ref[0] + 1

    @pl.when(jnp.logical_not(valid))
    def _():
        ys_ref[...] = jnp.zeros_like(ys_ref)

    @pl.when(valid)
    def _():
        nb = jnp.minimum(i + 1, last)

        @pl.when(blks_ref[0, nb] == i + 1)
        def _():
            gather(i + 1, 1 - p)

        wait_gather(i, p)
        x = jnp.concatenate([xbuf_ref[p, pl.ds(j, MOE_BLOCK, stride=nt), :] for j in range(nt)], axis=-1)
        gu = jnp.dot(x.astype(jnp.bfloat16), wgu_ref[...], preferred_element_type=jnp.float32) + bgu_ref[0]
        g = jnp.minimum(gu[:, 0:f], SWIGLU_LIMIT)
        lin = jnp.clip(gu[:, f:2 * f], -SWIGLU_LIMIT, SWIGLU_LIMIT)
        act = g * _sigmoid(SWIGLU_ALPHA * g) * (lin + 1.0)
        y = jnp.dot(act.astype(jnp.bfloat16), wd_ref[...], preferred_element_type=jnp.float32) + bd_ref[0]
        for j in range(nt):
            ys_ref[pl.ds(j, MOE_BLOCK, stride=nt), :] = y[:, j * LANES:(j + 1) * LANES]


def _run_experts(blk_e, blk_s, counts, poff, run, staged, w_gate_up, b_gate_up, w_down, b_down, nt):
    ne, d, f2 = w_gate_up.shape
    f = f2 // 2
    n_blocks = blk_e.shape[1]
    blk_rows = MOE_BLOCK * nt
    any_spec = pl.BlockSpec(memory_space=pl.ANY)

    def per_expert(i, be, bs, cn, po):
        return (be[0, i], 0, 0)

    grid_spec = pltpu.PrefetchScalarGridSpec(
        num_scalar_prefetch=4,
        grid=(n_blocks,),
        in_specs=[pl.BlockSpec(memory_space=pltpu.SMEM), any_spec, any_spec,
                  pl.BlockSpec((1, 1, f2), per_expert), any_spec, pl.BlockSpec((1, 1, d), per_expert)],
        out_specs=pl.BlockSpec((blk_rows, LANES), lambda i, be, bs, cn, po: (i, 0)),
        scratch_shapes=[pltpu.VMEM((2, blk_rows, LANES), jnp.float32),
                        pltpu.VMEM((2, d, f2), jnp.float32), pltpu.VMEM((2, f, d), jnp.float32),
                        pltpu.VMEM((d, f2), jnp.bfloat16), pltpu.VMEM((f, d), jnp.bfloat16),
                        pltpu.SemaphoreType.DMA((2,)), pltpu.SemaphoreType.DMA((2,)),
                        pltpu.SMEM((1,), jnp.int32), pltpu.SMEM((1,), jnp.int32)],
    )
    return pl.pallas_call(
        functools.partial(_run_expert_kernel, nt),
        grid_spec=grid_spec,
        out_shape=jax.ShapeDtypeStruct((n_blocks * blk_rows, LANES), jnp.float32),
        compiler_params=pltpu.CompilerParams(
            dimension_semantics=("arbitrary",), vmem_limit_bytes=VMEM_LIMIT),
        name="moe_experts",
    )(blk_e, blk_s, counts, poff, run, staged, w_gate_up, b_gate_up.reshape(ne, 1, f2), w_down,
      b_down.reshape(ne, 1, d))


def _combine_kernel(nt, loc0_ref, loc1_ref, loc2_ref, loc3_ref, g0_ref, g1_ref, g2_ref, g3_ref,
                    run_ref, runn_ref, x1_ref, gfin_ref, ys_ref, out_ref, stage_ref, acc_ref, sem):
    loc_refs = (loc0_ref, loc1_ref, loc2_ref, loc3_ref)
    gate_refs = (g0_ref, g1_ref, g2_ref, g3_ref)
    rows, d = x1_ref.shape
    n_assign = rows * TOP_K
    step = pl.program_id(0)
    par = step % 2
    unroll = TILE_UNROLL

    def fetch(r_ref, half):
        _run_copies(r_ref,
                    lambda first, base, n: ys_ref.at[pl.ds(pl.multiple_of(base * nt, nt), n * nt), :],
                    lambda first, base, n: stage_ref.at[half, pl.ds(pl.multiple_of(first * nt, nt), n * nt), :],
                    sem.at[half], nt)

    @pl.when(step == 0)
    def _():
        fetch(run_ref, 0)

    @pl.when(step + 1 < pl.num_programs(0))
    def _():
        fetch(runn_ref, 1 - par)

    pltpu.make_async_copy(ys_ref.at[pl.ds(0, n_assign * nt), :], stage_ref.at[par], sem.at[par]).wait()

    def mix_from(half):
        def mix(i, carry):
            for u in range(unroll):
                tok = i * unroll + u
                acc = None
                for k in range(TOP_K):
                    row = stage_ref[half, pl.ds(pl.multiple_of(loc_refs[k][0, 0, 0, tok], nt), nt), :]
                    term = gate_refs[k][0, 0, 0, tok] * row
                    acc = term if acc is None else acc + term
                acc_ref[pl.ds(pl.multiple_of(tok * nt, nt), nt), :] = acc
            return carry

        lax.fori_loop(0, rows // unroll, mix, 0)

    for half in range(2):
        pl.when(par == half)(functools.partial(mix_from, half))

    y = jnp.concatenate([acc_ref[pl.ds(j, rows, stride=nt), :] for j in range(nt)], axis=-1)
    x = x1_ref[...] + y
    ms = jnp.mean(x * x, axis=-1, keepdims=True)
    out_ref[...] = x * lax.rsqrt(ms + NORM_EPS) * gfin_ref[...]


def _combine(loc, gate, runs, x1, g_final, ys, nt):
    t, d = x1.shape
    rows = ROUTE_ROWS
    tiles = t // rows
    ne = runs.shape[2]
    n_assign = rows * TOP_K
    smem = pltpu.SMEM
    return pl.pallas_call(
        functools.partial(_combine_kernel, nt),
        grid=(tiles,),
        in_specs=[*_slot_specs(rows), *_slot_specs(rows),
                  pl.BlockSpec((1, SUBLANES, ne), lambda i: (i, 0, 0), memory_space=smem),
                  pl.BlockSpec((1, SUBLANES, ne), lambda i: (jnp.minimum(i + 1, tiles - 1), 0, 0),
                               memory_space=smem),
                  pl.BlockSpec((rows, d), lambda i: (i, 0)),
                  _const_spec(g_final.shape),
                  pl.BlockSpec(memory_space=pl.ANY)],
        out_specs=pl.BlockSpec((rows, d), lambda i: (i, 0)),
        out_shape=jax.ShapeDtypeStruct((t, d), jnp.float32),
        scratch_shapes=[pltpu.VMEM((2, n_assign * nt, LANES), jnp.float32),
                        pltpu.VMEM((rows * nt, LANES), jnp.float32),
                        pltpu.SemaphoreType.DMA((2,))],
        compiler_params=pltpu.CompilerParams(
            dimension_semantics=("arbitrary",), vmem_limit_bytes=VMEM_LIMIT),
        name="moe_combine",
    )(loc, loc, loc, loc, gate, gate, gate, gate, runs, runs, x1, g_final, ys)


def _block_diag(blocks):
    g, r, c = blocks.shape
    eye = jnp.eye(g, dtype=blocks.dtype)
    return (blocks[:, :, None, :] * eye[:, None, :, None]).reshape(g * r, g * c)


def kernel(x, norm_mix_g, w_in, conv_w, conv_b, conv_ln_g, conv_ln_b, w_conv_out, ssm_a_re, ssm_a_im,
           ssm_log_step, ssm_b_re, ssm_b_im, ssm_c_re, ssm_c_im, ssm_d, w_ssm_glu, b_ssm_glu, w_ssm_out,
           w_out, norm_ffn_g, w_router, b_router, w_gate_up, b_gate_up, w_down, b_down, norm_final_g):
    bsz, seq, d = x.shape
    depth = w_in.shape[0]
    bf = jnp.bfloat16
    nt = d // LANES
    n_tok = bsz * seq
    ne = w_router.shape[-1]
    n_blocks = -(-n_tok * TOP_K // MOE_BLOCK) + ne
    seg_len = MIX_ROWS // SCAN_SEGS

    def row(v):
        return v.reshape(1, -1)

    for l in range(depth):
        ar, ai, arp, aip, bbr, bbi = _ssm_disc(
            ssm_a_re[l], ssm_a_im[l], ssm_log_step[l],
            jnp.swapaxes(ssm_b_re[l], 1, 2), jnp.swapaxes(ssm_b_im[l], 1, 2), seg_len)
        bblk = jnp.concatenate([_block_diag(bbr), _block_diag(bbi)], axis=1).astype(bf)
        c_r = _block_diag(jnp.swapaxes(ssm_c_re[l], 1, 2)).astype(bf)
        c_i = _block_diag(jnp.swapaxes(ssm_c_im[l], 1, 2)).astype(bf)
        avec = jnp.stack([ar.reshape(-1), ai.reshape(-1), arp.reshape(-1), aip.reshape(-1)])

        x = _mixer(x, row(norm_mix_g[l]), w_in[l].astype(bf), conv_w[l], row(conv_b[l]),
                   row(conv_ln_g[l]), row(conv_ln_b[l]), w_conv_out[l].astype(bf), bblk, c_r, c_i, avec,
                   row(ssm_d[l]), w_ssm_glu[l].astype(bf), row(b_ssm_glu[l]), w_ssm_out[l].astype(bf),
                   w_out[l].astype(bf))

        x1 = x.reshape(n_tok, d)
        staged, loc, gate, run, counts = _route(x1, row(norm_ffn_g[l]), w_router[l], row(b_router[l]))
        poff, blk_e, blk_s = _meta(counts, n_blocks)
        runs = _runs(poff, run)
        ys = _run_experts(blk_e, blk_s, counts, poff, run, staged, w_gate_up[l], b_gate_up[l], w_down[l],
                          b_down[l], nt)
        assert depth == 1
        x = _combine(loc, gate, runs, x1, row(norm_final_g), ys, nt).reshape(bsz, seq, d)
    return x
```
